```python
import math
import jax, jax.numpy as jnp
from jax import lax
import numpy as np

D_MODEL = 1024
BATCH = 16
SEQ = 4096
DEPTH = 1
DEC_BATCH = 16
DEC_SEQ = 16
PAST_LEN = 2048

CHUNK = 64
A_HEAD_DIM = 64
A_HEADS = (D_MODEL // 2) // A_HEAD_DIM
A_KV_HEADS = A_HEADS // 2
A_GROUP = A_HEADS // A_KV_HEADS
A_WIDTH = A_HEADS * A_HEAD_DIM
IDX_HEADS = 8
IDX_DIM = 64
TOPK_MAX = 256
QUERY_BLOCK = 128
REL_BUCKETS = 32
REL_MAX_DIST = 128
G_KEY_DIM = 128
G_VAL_DIM = 128
G_HEADS = (D_MODEL // 2) // G_VAL_DIM
G_WIDTH = G_HEADS * G_VAL_DIM
CONV_W = 4
CONV_CH = 2 * G_HEADS * G_KEY_DIM + G_HEADS * G_VAL_DIM
MIX_WIDTH = A_WIDTH + G_WIDTH
IN_SPLITS = (A_HEADS * A_HEAD_DIM, A_KV_HEADS * A_HEAD_DIM, A_KV_HEADS * A_HEAD_DIM,
             IDX_HEADS * IDX_DIM, IDX_DIM, IDX_HEADS,
             G_HEADS * G_KEY_DIM, G_HEADS * G_KEY_DIM, G_HEADS * G_VAL_DIM, G_HEADS * G_VAL_DIM,
             G_HEADS, G_HEADS)
IN_WIDTH = sum(IN_SPLITS)
D_FF = 2816
N_MOD = 9
EPS = 1e-6

kernel_name = 'hybrid_dsa_gdn_macaron_stream_step'


def rms_norm(x, gain):
    xf = x.astype(jnp.float32)
    y = xf * lax.rsqrt(jnp.mean(xf * xf, axis=-1, keepdims=True) + EPS)
    return (y * gain.astype(jnp.float32)).astype(x.dtype)


def modulate(h, shift, scale):
    return h * (1.0 + scale[:, None, :]) + shift[:, None, :]


def swiglu(h, w_gate, w_up, w_down):
    return (jax.nn.silu(h @ w_gate) * (h @ w_up)) @ w_down


def l2_normalize(x):
    xf = x.astype(jnp.float32)
    return xf * lax.rsqrt(jnp.sum(xf * xf, axis=-1, keepdims=True) + 1e-6)


def rel_bucket(rel):
    half = REL_BUCKETS // 2
    max_exact = half // 2
    n = jnp.abs(rel)
    n_f = jnp.maximum(n, 1).astype(jnp.float32)
    large = max_exact + (jnp.log(n_f / max_exact) / math.log(REL_MAX_DIST / max_exact)
                         * (half - max_exact)).astype(jnp.int32)
    large = jnp.minimum(large, half - 1)
    return jnp.where(rel > 0, half, 0) + jnp.where(n < max_exact, n, large)


def causal_conv(x, hist, w):
    xp = jnp.concatenate([hist.astype(x.dtype), x], axis=1)
    y = lax.conv_general_dilated(xp, w[:, None, :].astype(x.dtype), window_strides=(1,), padding='VALID',
                                 dimension_numbers=('NWC', 'WIO', 'NWC'), feature_group_count=x.shape[-1])
    return jax.nn.silu(y), xp[:, -(CONV_W - 1):]


def sparse_attention_block(q, iq, iw, qpos, k_all, v_all, ik_all, rel_bias, n_top):
    B, L = k_all.shape[0], k_all.shape[1]
    limit = jnp.minimum((qpos // CHUNK + 1) * CHUNK, L)
    dots = jnp.einsum('bqhd,bkd->bqhk', iq, ik_all).astype(jnp.float32) * IDX_DIM ** -0.5
    score = jnp.einsum('bqh,bqhk->bqk', iw.astype(jnp.float32), jax.nn.relu(dots))
    score = jnp.where(jnp.arange(L)[None, None, :] < limit[None, :, None], score, -jnp.inf)
    _, idx = lax.top_k(score, n_top)
    bidx = jnp.arange(B)[:, None, None]
    k_sel = k_all[bidx, idx]
    v_sel = v_all[bidx, idx]
    logits = jnp.einsum('bqgrd,bqkgd->bqgrk', q, k_sel).astype(jnp.float32) * A_HEAD_DIM ** -0.5
    bias = rel_bias[rel_bucket(idx - qpos[None, :, None])]
    bias = jnp.moveaxis(bias.reshape(*idx.shape, A_KV_HEADS, A_GROUP), 2, -1)
    valid = (idx < limit[None, :, None])[:, :, None, None, :]
    p = jax.nn.softmax(jnp.where(valid, logits + bias.astype(jnp.float32), -jnp.inf), axis=-1)
    return jnp.einsum('bqgrk,bqkgd->bqgrd', p.astype(v_sel.dtype), v_sel)


def gated_delta_chunked(q, k, v, g, beta, s0):
    B, T, H, DK = q.shape
    DV = v.shape[-1]
    C = min(CHUNK, T)
    N = T // C

    def chunks(a):
        return jnp.transpose(a.reshape(B, N, C, H, -1), (0, 1, 3, 2, 4))

    q, k, v = chunks(q), chunks(k), chunks(v)
    g = jnp.transpose(g.reshape(B, N, C, H), (0, 1, 3, 2))
    beta = jnp.transpose(beta.reshape(B, N, C, H), (0, 1, 3, 2))
    gc = jnp.cumsum(g, axis=-1)
    tri = jnp.tril(jnp.ones((C, C), bool))
    strict = jnp.tril(jnp.ones((C, C), bool), -1)
    decay = jnp.exp(jnp.where(tri, gc[..., :, None] - gc[..., None, :], -jnp.inf))
    kb = k * beta[..., None]
    m = jnp.where(strict, jnp.einsum('bnhid,bnhjd->bnhij', kb, k) * decay, 0.0)
    a = jnp.eye(C, dtype=jnp.float32) + m
    rhs = jnp.concatenate([v * beta[..., None], kb * jnp.exp(gc)[..., None]], axis=-1)
    sol = lax.linalg.triangular_solve(a, rhs, left_side=True, lower=True, unit_diagonal=True)
    u, w = sol[..., :DV], sol[..., DV:]
    qk = jnp.where(tri, jnp.einsum('bnhid,bnhjd->bnhij', q, k) * decay, 0.0)

    def step(S, xs):
        q_c, k_c, u_c, w_c, qk_c, gc_c = xs
        v_new = u_c - jnp.einsum('bhcd,bhde->bhce', w_c, S)
        o = (jnp.einsum('bhcd,bhde->bhce', q_c * jnp.exp(gc_c)[..., None], S)
             + jnp.einsum('bhij,bhje->bhie', qk_c, v_new))
        g_last = gc_c[..., -1]
        S = (S * jnp.exp(g_last)[..., None, None]
             + jnp.einsum('bhcd,bhce->bhde', k_c * jnp.exp(g_last[..., None] - gc_c)[..., None], v_new))
        return S, o

    xs = tuple(jnp.moveaxis(t, 1, 0) for t in (q, k, u, w, qk, gc))
    S, o = lax.scan(step, s0, xs)
    o = jnp.transpose(o, (1, 0, 3, 2, 4)).reshape(B, T, H, DV)
    return o, S


def hybrid_mixer(h, past, w_in, w_out, rel_bias, conv_w, a_log, dt_bias, gnorm_w):
    B, T, _ = h.shape
    offs = [int(o) for o in np.cumsum(IN_SPLITS)[:-1]]
    aq, ak, av, iq, ik, iw, gq, gk, gv, gz, gb, ga = jnp.split(h @ w_in, offs, axis=-1)
    aq = aq.reshape(B, T, A_KV_HEADS, A_GROUP, A_HEAD_DIM)
    ak = ak.reshape(B, T, A_KV_HEADS, A_HEAD_DIM)
    av = av.reshape(B, T, A_KV_HEADS, A_HEAD_DIM)
    iq = iq.reshape(B, T, IDX_HEADS, IDX_DIM)
    iw = iw * IDX_HEADS ** -0.5
    if past is None:
        offset = 0
        k_all, v_all, ik_all = ak, av, ik
        conv_hist = jnp.zeros((B, CONV_W - 1, CONV_CH), h.dtype)
        s0 = jnp.zeros((B, G_HEADS, G_KEY_DIM, G_VAL_DIM), jnp.float32)
    else:
        k_hist, v_hist, ik_hist, conv_hist, s0 = past
        offset = k_hist.shape[1]
        k_all = jnp.concatenate([k_hist.astype(ak.dtype), ak], axis=1)
        v_all = jnp.concatenate([v_hist.astype(av.dtype), av], axis=1)
        ik_all = jnp.concatenate([ik_hist.astype(ik.dtype), ik], axis=1)
        s0 = s0.astype(jnp.float32)

    L = k_all.shape[1]
    n_top = min(TOPK_MAX, L // 4)
    qb = min(QUERY_BLOCK, T)
    nb = T // qb
    qpos = (offset + jnp.arange(T, dtype=jnp.int32)).reshape(nb, qb)

    def blocks(t):
        return jnp.moveaxis(t.reshape(B, nb, qb, *t.shape[2:]), 1, 0)

    attn = lax.map(lambda xs: sparse_attention_block(*xs, k_all, v_all, ik_all, rel_bias, n_top),
                   (blocks(aq), blocks(iq), blocks(iw), qpos))
    attn = jnp.moveaxis(attn, 0, 1).reshape(B, T, A_WIDTH)

    qkv, conv_new = causal_conv(jnp.concatenate([gq, gk, gv], axis=-1), conv_hist, conv_w)
    gq, gk, gv = jnp.split(qkv, [G_HEADS * G_KEY_DIM, 2 * G_HEADS * G_KEY_DIM], axis=-1)
    gq = l2_normalize(gq.reshape(B, T, G_HEADS, G_KEY_DIM)) * G_KEY_DIM ** -0.5
    gk = l2_normalize(gk.reshape(B, T, G_HEADS, G_KEY_DIM))
    gv = gv.reshape(B, T, G_HEADS, G_VAL_DIM).astype(jnp.float32)
    beta = jax.nn.sigmoid(gb.astype(jnp.float32))
    g = -jnp.exp(a_log.astype(jnp.float32)) * jax.nn.softplus(ga.astype(jnp.float32) + dt_bias.astype(jnp.float32))
    o, s_new = gated_delta_chunked(gq, gk, gv, g, beta, s0)
    o = rms_norm(o, gnorm_w) * jax.nn.silu(gz.reshape(B, T, G_HEADS, G_VAL_DIM).astype(jnp.float32))

    mixed = jnp.concatenate([attn, o.reshape(B, T, G_WIDTH).astype(h.dtype)], axis=-1)
    return mixed @ w_out, (ak, av, ik, conv_new, s_new)


def trunk_layer(x, c, past, w_mod, b_mod, norm_ffn1, norm_mix, norm_ffn2,
                ffn1_w_gate, ffn1_w_up, ffn1_w_down, ffn2_w_gate, ffn2_w_up, ffn2_w_down,
                w_in, w_out, rel_bias, conv_w, a_log, dt_bias, gnorm_w):
    mod = jax.nn.silu(c) @ w_mod + b_mod
    sh1, sc1, gt1, sh2, sc2, gt2, sh3, sc3, gt3 = jnp.split(mod, N_MOD, axis=-1)
    x = x + 0.5 * gt1[:, None, :] * swiglu(modulate(rms_norm(x, norm_ffn1), sh1, sc1),
                                            ffn1_w_gate, ffn1_w_up, ffn1_w_down)
    m, new_state = hybrid_mixer(modulate(rms_norm(x, norm_mix), sh2, sc2), past,
                                w_in, w_out, rel_bias, conv_w, a_log, dt_bias, gnorm_w)
    x = x + gt2[:, None, :] * m
    x = x + 0.5 * gt3[:, None, :] * swiglu(modulate(rms_norm(x, norm_ffn2), sh3, sc3),
                                            ffn2_w_gate, ffn2_w_up, ffn2_w_down)
    return x, new_state


def setup_inputs(seed: int = 0) -> dict:
    key = jax.random.key(seed)
    ks = jax.random.split(key, 32)
    f32 = jnp.float32
    D = D_MODEL

    def nrm(k, shape, scale):
        return jax.random.normal(k, shape, f32) * scale

    dt = jnp.exp(jax.random.uniform(ks[27], (DEPTH, G_HEADS), f32, math.log(1e-3), math.log(1e-1)))
    return {
        'x_prompt': nrm(ks[0], (BATCH, SEQ, D), 1.0),
        'x_sample': nrm(ks[1], (DEC_BATCH, DEC_SEQ, D), 1.0),
        'cache_k': nrm(ks[2], (DEPTH, DEC_BATCH, PAST_LEN, A_KV_HEADS, A_HEAD_DIM), 1.0),
        'cache_v': nrm(ks[3], (DEPTH, DEC_BATCH, PAST_LEN, A_KV_HEADS, A_HEAD_DIM), 1.0),
        'cache_idx_k': nrm(ks[4], (DEPTH, DEC_BATCH, PAST_LEN, IDX_DIM), 1.0),
        'state_conv': nrm(ks[5], (DEPTH, DEC_BATCH, CONV_W - 1, CONV_CH), 1.0),
        'state_delta': nrm(ks[6], (DEPTH, DEC_BATCH, G_HEADS, G_KEY_DIM, G_VAL_DIM), G_KEY_DIM ** -0.5),
        'c_prompt': nrm(ks[7], (BATCH, D), 1.0),
        'c_sample': nrm(ks[8], (DEC_BATCH, D), 1.0),
        'w_mod': nrm(ks[9], (DEPTH, D, N_MOD * D), 0.5 * D ** -0.5),
        'b_mod': nrm(ks[10], (DEPTH, N_MOD * D), 0.02),
        'norm_ffn1': 1.0 + nrm(ks[11], (DEPTH, D), 0.05),
        'norm_mix': 1.0 + nrm(ks[12], (DEPTH, D), 0.05),
        'norm_ffn2': 1.0 + nrm(ks[13], (DEPTH, D), 0.05),
        'ffn1_w_gate': nrm(ks[14], (DEPTH, D, D_FF), D ** -0.5),
        'ffn1_w_up': nrm(ks[15], (DEPTH, D, D_FF), D ** -0.5),
        'ffn1_w_down': nrm(ks[16], (DEPTH, D_FF, D), D_FF ** -0.5),
        'ffn2_w_gate': nrm(ks[17], (DEPTH, D, D_FF), D ** -0.5),
        'ffn2_w_up': nrm(ks[18], (DEPTH, D, D_FF), D ** -0.5),
        'ffn2_w_down': nrm(ks[19], (DEPTH, D_FF, D), D_FF ** -0.5),
        'w_in': nrm(ks[20], (DEPTH, D, IN_WIDTH), D ** -0.5),
        'w_out': nrm(ks[21], (DEPTH, MIX_WIDTH, D), MIX_WIDTH ** -0.5),
        'rel_bias': nrm(ks[22], (REL_BUCKETS, A_HEADS), 0.5),
        'conv_w': nrm(ks[23], (DEPTH, CONV_W, CONV_CH), CONV_W ** -0.5),
        'a_log': jnp.log(jax.random.uniform(ks[24], (DEPTH, G_HEADS), f32, 1.0, 16.0)),
        'dt_bias': jnp.log(jnp.expm1(dt)),
        'gnorm_w': 1.0 + nrm(ks[25], (DEPTH, G_VAL_DIM), 0.05),
        'norm_final': 1.0 + nrm(ks[26], (D,), 0.05),
    }


def reference(x_prompt, x_sample, cache_k, cache_v, cache_idx_k, state_conv, state_delta,
              c_prompt, c_sample, w_mod, b_mod, norm_ffn1, norm_mix, norm_ffn2,
              ffn1_w_gate, ffn1_w_up, ffn1_w_down, ffn2_w_gate, ffn2_w_up, ffn2_w_down,
              w_in, w_out, rel_bias, conv_w, a_log, dt_bias, gnorm_w, norm_final):
    def run(x, c, with_past):
        per_layer = []
        for l in range(DEPTH):
            past = (cache_k[l], cache_v[l], cache_idx_k[l], state_conv[l], state_delta[l]) if with_past else None
            x, st = trunk_layer(x, c, past, w_mod[l], b_mod[l], norm_ffn1[l], norm_mix[l], norm_ffn2[l],
                                ffn1_w_gate[l], ffn1_w_up[l], ffn1_w_down[l],
                                ffn2_w_gate[l], ffn2_w_up[l], ffn2_w_down[l],
                                w_in[l], w_out[l], rel_bias, conv_w[l], a_log[l], dt_bias[l], gnorm_w[l])
            per_layer.append(st)
        stacked = [jnp.stack(s, axis=0) for s in zip(*per_layer)]
        return rms_norm(x, norm_final), stacked

    y_prompt, (k_p, v_p, ik_p, conv_p, delta_p) = run(x_prompt, c_prompt, False)
    y_sample, (k_s, v_s, ik_s, conv_s, delta_s) = run(x_sample, c_sample, True)
    return (y_prompt, y_sample, k_p, v_p, ik_p, conv_p, delta_p, k_s, v_s, ik_s, conv_s, delta_s)
```

```python
import functools

import jax
import jax.numpy as jnp
import numpy as np
from jax import lax
from jax.experimental import pallas as pl
from jax.experimental.pallas import tpu as pltpu

F32 = jnp.float32
BF16 = jnp.bfloat16

D_MODEL = 1024
CHUNK = 64
A_HEAD_DIM = 64
A_HEADS = 8
A_KV_HEADS = 4
A_WIDTH = A_HEADS * A_HEAD_DIM
IDX_HEADS = 8
IDX_DIM = 64
TOPK_MAX = 256
REL_BUCKETS = 32
G_KEY_DIM = 128
G_VAL_DIM = 128
G_HEADS = 4
G_WIDTH = G_HEADS * G_VAL_DIM
CONV_W = 4
CONV_CH = 2 * G_HEADS * G_KEY_DIM + G_HEADS * G_VAL_DIM
D_FF = 2816
N_MOD = 9
EPS = 1e-6

LANES = 128
MXU_DIM = 256
VMEM_LIMIT_BYTES = 56 * 1024 * 1024

FF_CHUNK = MXU_DIM
N_FF_CHUNKS = D_FF // FF_CHUNK
TOKEN_TILE = 512
ATT_TQ = 256
ATT_KB = 256
GDN_BLOCK = 256

QX_W = A_HEADS * LANES
OFF_QX = 0
OFF_K = OFF_QX + QX_W
OFF_V = OFF_K + A_KV_HEADS * A_HEAD_DIM
OFF_IQ = OFF_V + A_KV_HEADS * A_HEAD_DIM
OFF_IKX = OFF_IQ + IDX_HEADS * IDX_DIM
OFF_MISC = OFF_IKX + 2 * LANES
OFF_GQKV = OFF_MISC + LANES
OFF_GZ = OFF_GQKV + CONV_CH
IN_PACKED = OFF_GZ + G_WIDTH
MISC_IW = IDX_DIM
MISC_GB = MISC_IW + IDX_HEADS
MISC_GA = MISC_GB + G_HEADS

NEG_BIG = -1e30
INT_MIN = -2 ** 31


def _cparams(n_axes):
    return pltpu.CompilerParams(dimension_semantics=("arbitrary",) * n_axes,
                                vmem_limit_bytes=VMEM_LIMIT_BYTES)


def _resident(shape):
    nd = len(shape)
    return pl.BlockSpec(shape, lambda *_: (0,) * nd, pipeline_mode=pl.Buffered(1))


def _dot_nt(a, b):
    return lax.dot_general(a, b, (((1,), (1,)), ((), ())), preferred_element_type=F32)


def _rms_mod(x, gain, shift, scale):
    ms = jnp.mean(x * x, axis=-1, keepdims=True)
    y = x * lax.rsqrt(ms + EPS) * gain
    return y * (1.0 + scale) + shift


def _silu(x):
    return x * jax.nn.sigmoid(x)


def _mod_kernel(c_ref, w_ref, b_ref, o_ref):
    s = _silu(c_ref[...]).astype(BF16)
    o_ref[...] = jnp.dot(s, w_ref[...].astype(BF16), preferred_element_type=F32) + b_ref[...]


def _mod_call(c, w_mod, b_mod):
    rows, d = c.shape
    n = w_mod.shape[1]
    tn = D_MODEL
    return pl.pallas_call(
        _mod_kernel,
        out_shape=jax.ShapeDtypeStruct((rows, n), F32),
        grid=(n // tn,),
        in_specs=[pl.BlockSpec((rows, d), lambda j: (0, 0)),
                  pl.BlockSpec((d, tn), lambda j: (0, j)),
                  pl.BlockSpec((1, tn), lambda j: (0, j))],
        out_specs=pl.BlockSpec((rows, tn), lambda j: (0, j)),
        compiler_params=_cparams(1),
        name="mod",
    )(c, w_mod, b_mod.reshape(1, n))


def _mod_specs(per_token, tm, tiles_per_batch):
    if per_token:
        return pl.BlockSpec((tm, D_MODEL), lambda i: (i, 0))
    return pl.BlockSpec((None, 1, D_MODEL), lambda i: (i // tiles_per_batch, 0, 0))


def _ffn_kernel(x_ref, sh_ref, sc_ref, gt_ref, gain_ref, wgu_ref, wd_ref, *rest, final_norm):
    if final_norm:
        nf_ref, o_ref, acc_ref = rest
    else:
        o_ref, acc_ref = rest
    x = x_ref[...]
    h = _rms_mod(x, gain_ref[...], sh_ref[...], sc_ref[...]).astype(BF16)
    for j in range(N_FF_CHUNKS):
        ab = jnp.dot(h, wgu_ref[j], preferred_element_type=F32)
        g = (_silu(ab[:, :FF_CHUNK]) * ab[:, FF_CHUNK:]).astype(BF16)
        d = jnp.dot(g, wd_ref[j], preferred_element_type=F32)
        if j == 0:
            acc_ref[...] = d
        else:
            acc_ref[...] += d
    y = x + 0.5 * gt_ref[...] * acc_ref[...]
    if final_norm:
        ms = jnp.mean(y * y, axis=-1, keepdims=True)
        y = y * lax.rsqrt(ms + EPS) * nf_ref[...]
    o_ref[...] = y


def _ffn_call(x2, sh, sc, gt, gain, wgu, wd, norm_final, *, tm, tiles_per_batch, per_token):
    n = x2.shape[0]
    mspec = _mod_specs(per_token, tm, tiles_per_batch)
    in_specs = [pl.BlockSpec((tm, D_MODEL), lambda i: (i, 0)), mspec, mspec, mspec,
                _resident((1, D_MODEL)), _resident(wgu.shape), _resident(wd.shape)]
    args = [x2, sh, sc, gt, gain, wgu, wd]
    final_norm = norm_final is not None
    if final_norm:
        in_specs.append(_resident((1, D_MODEL)))
        args.append(norm_final)
    return pl.pallas_call(
        functools.partial(_ffn_kernel, final_norm=final_norm),
        out_shape=jax.ShapeDtypeStruct((n, D_MODEL), F32),
        grid=(n // tm,),
        in_specs=in_specs,
        out_specs=pl.BlockSpec((tm, D_MODEL), lambda i: (i, 0)),
        scratch_shapes=[pltpu.VMEM((tm, D_MODEL), F32)],
        compiler_params=_cparams(1),
        name="ffn_final" if final_norm else "ffn",
    )(*args)


def _inproj_kernel(x_ref, sh_ref, sc_ref, gain_ref, w_ref,
                   qx_o, k_o, kb_o, v_o, vb_o, iq_o, ikx_o, misc_o, gqkv_o, gz_o):
    h = _rms_mod(x_ref[...], gain_ref[...], sh_ref[...], sc_ref[...]).astype(BF16)

    def mm(off, width):
        return jnp.dot(h, w_ref[:, off:off + width], preferred_element_type=F32)

    qx_o[...] = mm(OFF_QX, QX_W).astype(BF16)
    k = mm(OFF_K, OFF_V - OFF_K)
    k_o[...] = k
    kb_o[...] = k.astype(BF16)
    v = mm(OFF_V, OFF_IQ - OFF_V)
    v_o[...] = v
    vb_o[...] = v.astype(BF16)
    iq_o[...] = mm(OFF_IQ, OFF_IKX - OFF_IQ).astype(BF16)
    ikx_o[...] = mm(OFF_IKX, OFF_MISC - OFF_IKX).astype(BF16)
    misc_o[...] = mm(OFF_MISC, LANES)
    gqkv_o[...] = mm(OFF_GQKV, CONV_CH)
    gz_o[...] = mm(OFF_GZ, G_WIDTH)


def _inproj_call(x2, sh, sc, gain, w_packed, *, tm, tiles_per_batch, per_token):
    n = x2.shape[0]
    mspec = _mod_specs(per_token, tm, tiles_per_batch)
    widths = [(QX_W, BF16), (256, F32), (256, BF16), (256, F32), (256, BF16), (512, BF16),
              (2 * LANES, BF16), (LANES, F32), (CONV_CH, F32), (G_WIDTH, F32)]
    return pl.pallas_call(
        _inproj_kernel,
        out_shape=[jax.ShapeDtypeStruct((n, w), dt) for w, dt in widths],
        grid=(n // tm,),
        in_specs=[pl.BlockSpec((tm, D_MODEL), lambda i: (i, 0)), mspec, mspec,
                  _resident((1, D_MODEL)), _resident(w_packed.shape)],
        out_specs=[pl.BlockSpec((tm, w), lambda i: (i, 0)) for w, _ in widths],
        compiler_params=_cparams(1),
        name="inproj",
    )(x2, sh, sc, gain, w_packed)


def _outproj_kernel(x_ref, gt_ref, a_ref, g_ref, wa_ref, wg_ref, o_ref):
    m = (jnp.dot(a_ref[...], wa_ref[...], preferred_element_type=F32)
         + jnp.dot(g_ref[...], wg_ref[...], preferred_element_type=F32))
    o_ref[...] = x_ref[...] + gt_ref[...] * m


def _outproj_call(x2, gt, attn, gdn, wa, wg, *, tm, tiles_per_batch, per_token):
    n = x2.shape[0]
    mspec = _mod_specs(per_token, tm, tiles_per_batch)
    return pl.pallas_call(
        _outproj_kernel,
        out_shape=jax.ShapeDtypeStruct((n, D_MODEL), F32),
        grid=(n // tm,),
        in_specs=[pl.BlockSpec((tm, D_MODEL), lambda i: (i, 0)), mspec,
                  pl.BlockSpec((tm, A_WIDTH), lambda i: (i, 0)),
                  pl.BlockSpec((tm, G_WIDTH), lambda i: (i, 0)),
                  _resident(wa.shape), _resident(wg.shape)],
        out_specs=pl.BlockSpec((tm, D_MODEL), lambda i: (i, 0)),
        compiler_params=_cparams(1),
        name="outproj",
    )(x2, gt, attn, gdn, wa, wg)


def _rel_bucket_int(rel):
    n = jnp.abs(rel)
    large = jnp.full(rel.shape, 8, jnp.int32)
    for th in (12, 16, 23, 32, 46, 64, 91):
        large = large + jnp.where(n >= th, 1, 0)
    return jnp.where(rel > 0, REL_BUCKETS // 2, 0) + jnp.where(n < 8, n, large)


def _key_to_float(t):
    bits = jnp.where(t >= 0, t, t ^ jnp.int32(0x7FFFFFFF))
    return lax.bitcast_convert_type(bits, F32)


def _attn_kernel(relb_ref, qx_ref, iq_ref, misc_ref, k_ref, vt_ref, ikx_ref, o_ref,
                 s_ref, bias_ref, m_ref, l_ref, acc_ref, *, tq, l_true, offset, n_top):
    kb_sz = ATT_KB
    first = (pl.program_id(0) == 0) & (pl.program_id(1) == 0)
    q0 = offset + pl.program_id(1) * tq
    kmax = jnp.minimum(q0 + tq, l_true)
    nkb = (kmax + kb_sz - 1) // kb_sz

    key_off = lax.broadcasted_iota(jnp.int32, (kb_sz, tq), 0)
    qry_off = lax.broadcasted_iota(jnp.int32, (kb_sz, tq), 1)

    @pl.when(first)
    def _():
        for d in range(3):
            bucket = _rel_bucket_int(key_off - qry_off - d * kb_sz)
            for h in range(A_HEADS):
                t = jnp.zeros((kb_sz, tq), F32)
                for b in range(REL_BUCKETS):
                    t = jnp.where(bucket == b, relb_ref[b, h], t)
                bias_ref[d, h] = t

    qpos = q0 + lax.broadcasted_iota(jnp.int32, (1, tq), 1)
    limit = jnp.minimum((qpos // CHUNK + 1) * CHUNK, l_true)

    iw_t = misc_ref[0].T[MISC_IW:MISC_IW + IDX_HEADS, :] * (IDX_HEADS ** -0.5)

    def score_body(kb, carry):
        base = pl.multiple_of(kb * kb_sz, kb_sz)
        ik2 = ikx_ref[0, pl.ds(base, kb_sz), :]
        ik_lo = ik2[:, :LANES]
        ik_hi = ik2[:, LANES:]
        s = jnp.zeros((kb_sz, tq), F32)
        for j in range(IDX_HEADS // 2):
            slab = iq_ref[0, :, j * LANES:(j + 1) * LANES]
            d0 = _dot_nt(ik_lo, slab)
            d1 = _dot_nt(ik_hi, slab)
            s = s + iw_t[2 * j:2 * j + 1, :] * jnp.maximum(d0, 0.0)
            s = s + iw_t[2 * j + 1:2 * j + 2, :] * jnp.maximum(d1, 0.0)
        s_ref[pl.ds(base, kb_sz), :] = jnp.where(base + key_off < limit, s, -jnp.inf)
        return carry

    lax.fori_loop(0, nkb, score_body, 0)

    def count(pred):
        def body(kb, acc):
            base = pl.multiple_of(kb * kb_sz, kb_sz)
            hit = pred(s_ref[pl.ds(base, kb_sz), :], base + key_off)
            return acc + jnp.sum(hit.reshape(kb_sz // 8, 8, tq), axis=0)
        acc = lax.fori_loop(0, nkb, body, jnp.zeros((8, tq), F32))
        return jnp.sum(acc, axis=0, keepdims=True)

    def bisect_body(i, t):
        cand = t + lax.shift_left(jnp.int32(1), 31 - i)
        thr_c = _key_to_float(cand)
        c = count(lambda blk, _: jnp.where(blk >= thr_c, 1.0, 0.0))
        return jnp.where(c >= n_top, cand, t)

    t_key = lax.fori_loop(0, 32, bisect_body, jnp.full((1, tq), INT_MIN, jnp.int32))
    thr = _key_to_float(t_key)

    need = n_top - count(lambda blk, _: jnp.where(blk > thr, 1.0, 0.0))
    n_eq = count(lambda blk, _: jnp.where(blk == thr, 1.0, 0.0))
    take_all = limit <= n_top
    excess = jnp.where(take_all, 0.0, jnp.where(n_eq > need, 1.0, 0.0))
    any_excess = jnp.max(excess) > 0.0
    idx_bits = 14

    def tie_body(i, c):
        cand = c + lax.shift_left(jnp.int32(1), idx_bits - 1 - i)
        f = count(lambda blk, kidx: jnp.where(blk == thr, jnp.where(kidx < cand, 1.0, 0.0), 0.0))
        return jnp.where(f <= need, cand, c)

    cut0 = jnp.where(any_excess, jnp.zeros((1, tq), jnp.int32),
                     jnp.full((1, tq), 2 ** idx_bits, jnp.int32))
    cut = lax.fori_loop(0, jnp.where(any_excess, idx_bits, 0), tie_body, cut0)
    thr = jnp.where(take_all, -jnp.inf, thr)
    cut = jnp.where(take_all, limit, cut)

    def mask_body(kb, carry):
        base = pl.multiple_of(kb * kb_sz, kb_sz)
        blk = s_ref[pl.ds(base, kb_sz), :]
        tie = jnp.where(base + key_off < cut, 0.0, NEG_BIG)
        s_ref[pl.ds(base, kb_sz), :] = jnp.where(blk > thr, 0.0, jnp.where(blk == thr, tie, NEG_BIG))
        return carry

    lax.fori_loop(0, nkb, mask_body, 0)

    m_ref[...] = jnp.full(m_ref.shape, NEG_BIG, F32)
    l_ref[...] = jnp.zeros(l_ref.shape, F32)
    acc_ref[...] = jnp.zeros(acc_ref.shape, F32)

    def att_body(kb, carry):
        base = pl.multiple_of(kb * kb_sz, kb_sz)
        dsel = jnp.clip((q0 - base) // kb_sz, 0, 2)
        kblk = k_ref[0, pl.ds(base, kb_sz), :]
        vblk = vt_ref[0, kb]
        sel = s_ref[pl.ds(base, kb_sz), :]
        for h in range(A_HEADS):
            sl = h // 4
            lg = _dot_nt(kblk[:, sl * LANES:(sl + 1) * LANES], qx_ref[0, :, h * LANES:(h + 1) * LANES])
            lg = lg + sel + bias_ref[dsel, h]
            m_old = m_ref[h:h + 1, :]
            m_new = jnp.maximum(m_old, jnp.max(lg, axis=0, keepdims=True))
            alpha = jnp.exp(m_old - m_new)
            p = jnp.exp(lg - m_new)
            l_ref[h:h + 1, :] = alpha * l_ref[h:h + 1, :] + jnp.sum(p, axis=0, keepdims=True)
            pv = jnp.dot(vblk[sl * LANES:(sl + 1) * LANES, :], p.astype(BF16), preferred_element_type=F32)
            acc_ref[h] = alpha * acc_ref[h] + pv
            m_ref[h:h + 1, :] = m_new
        return carry

    lax.fori_loop(0, nkb, att_body, 0)

    outs = []
    for h in range(A_HEADS):
        pos = (h // 2) % 2
        outs.append(acc_ref[h, pos * A_HEAD_DIM:(pos + 1) * A_HEAD_DIM, :] / l_ref[h:h + 1, :])
    o_ref[0] = jnp.concatenate(outs, axis=0).T.astype(BF16)


def _attn_call(rel_bias, qx, iq, misc, kb, vt, ikx, *, tq, l_true, offset, n_top):
    b, tq_total, _ = qx.shape
    lp = kb.shape[1]
    nq = tq_total // tq
    assert offset % ATT_KB == 0 and tq % CHUNK == 0 and ATT_KB % tq == 0 and lp % ATT_KB == 0
    assert lp < 2 ** 14 and l_true <= lp
    kern = functools.partial(_attn_kernel, tq=tq, l_true=l_true, offset=offset, n_top=n_top)
    return pl.pallas_call(
        kern,
        out_shape=jax.ShapeDtypeStruct((b, tq_total, A_WIDTH), BF16),
        grid=(b, nq),
        in_specs=[pl.BlockSpec(memory_space=pltpu.SMEM),
                  pl.BlockSpec((1, tq, QX_W), lambda i, j: (i, j, 0)),
                  pl.BlockSpec((1, tq, IDX_HEADS * IDX_DIM), lambda i, j: (i, j, 0)),
                  pl.BlockSpec((1, tq, LANES), lambda i, j: (i, j, 0)),
                  pl.BlockSpec((1, lp, 256), lambda i, j: (i, 0, 0)),
                  pl.BlockSpec((1, lp // ATT_KB, 256, ATT_KB), lambda i, j: (i, 0, 0, 0)),
                  pl.BlockSpec((1, lp, 2 * LANES), lambda i, j: (i, 0, 0))],
        out_specs=pl.BlockSpec((1, tq, A_WIDTH), lambda i, j: (i, j, 0)),
        scratch_shapes=[pltpu.VMEM((lp, tq), F32),
                        pltpu.VMEM((3, A_HEADS, ATT_KB, tq), F32),
                        pltpu.VMEM((A_HEADS, tq), F32),
                        pltpu.VMEM((A_HEADS, tq), F32),
                        pltpu.VMEM((A_HEADS, LANES, tq), F32)],
        compiler_params=_cparams(2),
        name="sparse_attn",
    )(rel_bias, qx, iq, misc, kb, vt, ikx)


def _split_bf16(a, n):
    parts = []
    r = a
    for i in range(n):
        p = r.astype(BF16)
        parts.append(p)
        if i + 1 < n:
            r = r - p.astype(F32)
    return parts


def _mm(a, b, dims, pieces):
    pa = _split_bf16(a, pieces)
    pb = _split_bf16(b, pieces)
    out = None
    for i in range(pieces):
        for j in range(pieces - i):
            t = lax.dot_general(pa[i], pb[j], dims, preferred_element_type=F32)
            out = t if out is None else out + t
    return out


_NN = (((1,), (0,)), ((), ()))
_NT = (((1,), (1,)), ((), ()))
_TN = (((0,), (0,)), ((), ()))

GDN_PIECES_SOLVE = 2
GDN_PIECES_STATE = 2
GDN_PIECES_LOCAL = 2


def _gdn_kernel(x_ref, misc_ref, gz_ref, convw_ref, hist_ref, s0_ref, gnw_ref, alane_ref, dlane_ref,
                o_ref, conv_o, s_o, xp_ref, y_ref, st_ref, *, cb, t_true, nblk):
    c = CHUNK
    blk = pl.program_id(1)

    @pl.when(blk == 0)
    def _():
        xp_ref[8 - (CONV_W - 1):8, :] = hist_ref[0]
        st_ref[...] = s0_ref[0]

    xp_ref[8:8 + cb, :] = x_ref[0]
    y = convw_ref[CONV_W - 1:CONV_W, :] * xp_ref[8:8 + cb, :]
    for j in range(CONV_W - 1):
        y = y + convw_ref[j:j + 1, :] * xp_ref[5 + j:5 + j + cb, :]
    y_ref[...] = _silu(y)

    last_row = (t_true - 1) % cb

    @pl.when(blk == nblk - 1)
    def _():
        conv_o[0] = xp_ref[8 + last_row - (CONV_W - 2):8 + last_row + 1, :]

    xp_ref[8 - (CONV_W - 1):8, :] = xp_ref[8 + cb - (CONV_W - 1):8 + cb, :]

    ri = lax.broadcasted_iota(jnp.int32, (c, c), 0)
    ci = lax.broadcasted_iota(jnp.int32, (c, c), 1)
    tri = ri >= ci
    strict = ri > ci
    eye = ri == ci
    tril_bf = jnp.where(tri, 1.0, 0.0).astype(BF16)
    eye_f = jnp.where(eye, 1.0, 0.0)

    def chunk_body(ck, carry):
        r0 = pl.multiple_of(ck * c, c)
        ms = misc_ref[0, pl.ds(r0, c), :]
        tok = blk * cb + r0 + lax.broadcasted_iota(jnp.int32, (c, 1), 0)
        live = tok < t_true
        beta_s = jnp.where(live, jax.nn.sigmoid(ms), 0.0)
        z = ms + dlane_ref[...]
        softplus = jnp.maximum(z, 0.0) + jnp.log(1.0 + jnp.exp(-jnp.abs(z)))
        g_s = jnp.where(live, -jnp.exp(alane_ref[...]) * softplus, 0.0)
        gc_s = None
        for piece in _split_bf16(g_s, 3):
            t = jnp.dot(tril_bf, piece, preferred_element_type=F32)
            gc_s = t if gc_s is None else gc_s + t

        for h in range(G_HEADS):
            q = y_ref[pl.ds(r0, c), h * G_KEY_DIM:(h + 1) * G_KEY_DIM]
            k = y_ref[pl.ds(r0, c), G_WIDTH + h * G_KEY_DIM:G_WIDTH + (h + 1) * G_KEY_DIM]
            v = y_ref[pl.ds(r0, c), 2 * G_WIDTH + h * G_VAL_DIM:2 * G_WIDTH + (h + 1) * G_VAL_DIM]
            q = q * lax.rsqrt(jnp.sum(q * q, axis=-1, keepdims=True) + 1e-6) * (G_KEY_DIM ** -0.5)
            k = k * lax.rsqrt(jnp.sum(k * k, axis=-1, keepdims=True) + 1e-6)
            beta = beta_s[:, MISC_GB + h:MISC_GB + h + 1]
            gc = gc_s[:, MISC_GA + h:MISC_GA + h + 1]
            gc_b = jnp.broadcast_to(gc, (c, c))
            gc_row = jnp.sum(jnp.where(eye, gc_b, 0.0), axis=0, keepdims=True)
            decay = jnp.exp(jnp.where(tri, gc_b - gc_row, NEG_BIG))
            kb = k * beta
            m = jnp.where(strict, _mm(kb, k, _NT, GDN_PIECES_LOCAL) * decay, 0.0)
            eg = jnp.exp(gc)
            rhs = jnp.concatenate([v * beta, kb * eg], axis=1)
            pw = -m
            inv = eye_f + pw
            for _ in range(5):
                pw = _mm(pw, pw, _NN, GDN_PIECES_SOLVE)
                inv = inv + _mm(inv, pw, _NN, GDN_PIECES_SOLVE)
            sol = _mm(inv, rhs, _NN, GDN_PIECES_SOLVE)
            u = sol[:, :G_VAL_DIM]
            w = sol[:, G_VAL_DIM:]
            qk = jnp.where(tri, _mm(q, k, _NT, GDN_PIECES_LOCAL) * decay, 0.0)
            s_prev = st_ref[h]
            v_new = u - _mm(w, s_prev, _NN, GDN_PIECES_STATE)
            o = _mm(q * eg, s_prev, _NN, GDN_PIECES_STATE) + _mm(qk, v_new, _NN, GDN_PIECES_LOCAL)
            g_last = gc[c - 1:c, :]
            st_ref[h] = (s_prev * jnp.exp(g_last)
                         + _mm(k * jnp.exp(g_last - gc), v_new, _TN, GDN_PIECES_STATE))
            on = o * lax.rsqrt(jnp.mean(o * o, axis=-1, keepdims=True) + EPS) * gnw_ref[...]
            gate = gz_ref[0, pl.ds(r0, c), h * G_VAL_DIM:(h + 1) * G_VAL_DIM]
            o_ref[0, pl.ds(r0, c), h * G_VAL_DIM:(h + 1) * G_VAL_DIM] = (on * _silu(gate)).astype(BF16)
        return carry

    lax.fori_loop(0, cb // c, chunk_body, 0)

    @pl.when(blk == nblk - 1)
    def _():
        s_o[0] = st_ref[...]


def _gdn_call(gqkv, misc, gz, conv_w, hist, s0, gnorm_w, a_lane, d_lane, *, cb, t_true):
    b, t_pad, _ = gqkv.shape
    nblk = t_pad // cb
    assert t_true >= CONV_W - 1 and (nblk - 1) * cb < t_true <= t_pad
    kern = functools.partial(_gdn_kernel, cb=cb, t_true=t_true, nblk=nblk)
    return pl.pallas_call(
        kern,
        out_shape=[jax.ShapeDtypeStruct((b, t_pad, G_WIDTH), BF16),
                   jax.ShapeDtypeStruct((b, CONV_W - 1, CONV_CH), F32),
                   jax.ShapeDtypeStruct((b, G_HEADS, G_KEY_DIM, G_VAL_DIM), F32)],
        grid=(b, nblk),
        in_specs=[pl.BlockSpec((1, cb, CONV_CH), lambda i, j: (i, j, 0)),
                  pl.BlockSpec((1, cb, LANES), lambda i, j: (i, j, 0)),
                  pl.BlockSpec((1, cb, G_WIDTH), lambda i, j: (i, j, 0)),
                  pl.BlockSpec((CONV_W, CONV_CH), lambda i, j: (0, 0)),
                  pl.BlockSpec((1, CONV_W - 1, CONV_CH), lambda i, j: (i, 0, 0)),
                  pl.BlockSpec((1, G_HEADS, G_KEY_DIM, G_VAL_DIM), lambda i, j: (i, 0, 0, 0)),
                  pl.BlockSpec((1, G_VAL_DIM), lambda i, j: (0, 0)),
                  pl.BlockSpec((1, LANES), lambda i, j: (0, 0)),
                  pl.BlockSpec((1, LANES), lambda i, j: (0, 0))],
        out_specs=[pl.BlockSpec((1, cb, G_WIDTH), lambda i, j: (i, j, 0)),
                   pl.BlockSpec((1, CONV_W - 1, CONV_CH), lambda i, j: (i, 0, 0)),
                   pl.BlockSpec((1, G_HEADS, G_KEY_DIM, G_VAL_DIM), lambda i, j: (i, 0, 0, 0))],
        scratch_shapes=[pltpu.VMEM((cb + 8, CONV_CH), F32),
                        pltpu.VMEM((cb, CONV_CH), F32),
                        pltpu.VMEM((G_HEADS, G_KEY_DIM, G_VAL_DIM), F32)],
        compiler_params=_cparams(2),
        name="gated_delta",
    )(gqkv, misc, gz, conv_w, hist, s0, gnorm_w, a_lane, d_lane)


def _pack_w_in(w_in):
    d = w_in.shape[0]
    splits = (512, 256, 256, 512, 64, 8, 512, 512, 512, 512, 4, 4)
    offs = np.concatenate([[0], np.cumsum(splits)])
    aq, ak, av, iq, ik, iw, gq, gk, gv, gz, gb, ga = [w_in[:, offs[i]:offs[i + 1]] for i in range(12)]
    zeros64 = jnp.zeros((d, A_HEAD_DIM), w_in.dtype)
    qx = []
    for h in range(A_HEADS):
        qh = aq[:, h * A_HEAD_DIM:(h + 1) * A_HEAD_DIM] * (A_HEAD_DIM ** -0.5)
        qx += [qh, zeros64] if (h // 2) % 2 == 0 else [zeros64, qh]
    ikx = [ik, zeros64, zeros64, ik]
    misc = [ik, iw, gb, ga, jnp.zeros((d, LANES - MISC_GA - G_HEADS), w_in.dtype)]
    cols = qx + [ak, av, iq * (IDX_DIM ** -0.5)] + ikx + misc + [gq, gk, gv, gz]
    packed = jnp.concatenate(cols, axis=1).astype(BF16)
    assert packed.shape[1] == IN_PACKED
    return packed


def _pack_ffn(w_gate, w_up, w_down):
    d = w_gate.shape[0]
    wg = w_gate.reshape(d, N_FF_CHUNKS, FF_CHUNK)
    wu = w_up.reshape(d, N_FF_CHUNKS, FF_CHUNK)
    wgu = jnp.transpose(jnp.concatenate([wg, wu], axis=2), (1, 0, 2)).astype(BF16)
    wd = w_down.reshape(N_FF_CHUNKS, FF_CHUNK, d).astype(BF16)
    return wgu, wd


def _ikx_layout(ik):
    z = jnp.zeros_like(ik)
    return jnp.concatenate([ik, z, z, ik], axis=-1)


def _lane_vec(vals, lane0):
    return jnp.zeros((1, LANES), F32).at[0, lane0:lane0 + vals.shape[0]].set(vals.astype(F32))


def _run(x, mod, past, layers, rel_bias, norm_final):
    b, t, d = x.shape
    n = b * t
    per_token = t < TOKEN_TILE
    tm = n if per_token else TOKEN_TILE
    tiles_per_batch = None if per_token else t // tm
    tok = dict(tm=tm, tiles_per_batch=tiles_per_batch, per_token=per_token)

    x2 = x.reshape(n, d)
    states = []
    for li, lw in enumerate(layers):
        m = mod[li]

        def mvec(kidx):
            row = m[:, kidx]
            return jnp.repeat(row, t, axis=0) if per_token else row[:, None, :]

        sh1, sc1, gt1, sh2, sc2, gt2, sh3, sc3, gt3 = [mvec(i) for i in range(N_MOD)]
        x2 = _ffn_call(x2, sh1, sc1, gt1, lw["norm_ffn1"], lw["wgu1"], lw["wd1"], None, **tok)
        qx, k, kb, v, vb, iq, ikx, misc, gqkv, gz = _inproj_call(
            x2, sh2, sc2, lw["norm_mix"], lw["w_in"], **tok)

        if past is None:
            offset, l_true, tq = 0, t, ATT_TQ
            k_all, v_all, ikx_all = (a.reshape(b, t, -1) for a in (kb, vb, ikx))
            qx3, iq3, misc3 = (a.reshape(b, t, -1) for a in (qx, iq, misc))
            conv_hist = jnp.zeros((b, CONV_W - 1, CONV_CH), F32)
            s0 = jnp.zeros((b, G_HEADS, G_KEY_DIM, G_VAL_DIM), F32)
        else:
            k_hist, v_hist, ik_hist, conv_hist, s0 = (p[li] for p in past)
            offset = k_hist.shape[1]
            l_true = offset + t
            tq = LANES
            lp = -(-l_true // ATT_KB) * ATT_KB
            pad_k = lambda a: jnp.pad(a, ((0, 0), (0, lp - l_true), (0, 0)))
            k_all = pad_k(jnp.concatenate([k_hist.reshape(b, offset, -1).astype(BF16),
                                           kb.reshape(b, t, -1)], axis=1))
            v_all = pad_k(jnp.concatenate([v_hist.reshape(b, offset, -1).astype(BF16),
                                           vb.reshape(b, t, -1)], axis=1))
            ikx_all = pad_k(jnp.concatenate([_ikx_layout(ik_hist.astype(BF16)),
                                             ikx.reshape(b, t, -1)], axis=1))
            pad_q = lambda a: jnp.pad(a.reshape(b, t, -1), ((0, 0), (0, tq - t), (0, 0)))
            qx3, iq3, misc3 = pad_q(qx), pad_q(iq), pad_q(misc)
            s0 = s0.astype(F32)
            conv_hist = conv_hist.astype(F32)
        lp = k_all.shape[1]
        n_top = min(TOPK_MAX, l_true // 4)
        vt = jnp.transpose(v_all.reshape(b, lp // ATT_KB, ATT_KB, -1), (0, 1, 3, 2))
        attn = _attn_call(rel_bias, qx3, iq3, misc3, k_all, vt, ikx_all,
                          tq=tq, l_true=l_true, offset=offset, n_top=n_top)
        attn = attn[:, :t].reshape(n, A_WIDTH)

        cb = GDN_BLOCK if t % GDN_BLOCK == 0 else CHUNK
        t_pad = -(-t // cb) * cb
        pad_t = lambda a: jnp.pad(a.reshape(b, t, -1), ((0, 0), (0, t_pad - t), (0, 0)))
        gdn, conv_new, s_new = _gdn_call(pad_t(gqkv), pad_t(misc), pad_t(gz), lw["conv_w"], conv_hist, s0,
                                         lw["gnorm_w"], lw["a_lane"], lw["d_lane"], cb=cb, t_true=t)
        gdn = gdn[:, :t].reshape(n, G_WIDTH)

        x2 = _outproj_call(x2, gt2, attn, gdn, lw["wo_a"], lw["wo_g"], **tok)
        nf = norm_final if li == len(layers) - 1 else None
        x2 = _ffn_call(x2, sh3, sc3, gt3, lw["norm_ffn2"], lw["wgu2"], lw["wd2"], nf, **tok)

        states.append((k.reshape(b, t, A_KV_HEADS, A_HEAD_DIM), v.reshape(b, t, A_KV_HEADS, A_HEAD_DIM),
                       misc.reshape(b, t, LANES)[..., :IDX_DIM], conv_new, s_new))
    stacked = [jnp.stack(s, axis=0) for s in zip(*states)]
    return x2.reshape(b, t, d), stacked


def kernel(x_prompt, x_sample, cache_k, cache_v, cache_idx_k, state_conv, state_delta, c_prompt, c_sample,
           w_mod, b_mod, norm_ffn1, norm_mix, norm_ffn2, ffn1_w_gate, ffn1_w_up, ffn1_w_down,
           ffn2_w_gate, ffn2_w_up, ffn2_w_down, w_in, w_out, rel_bias, conv_w, a_log, dt_bias, gnorm_w,
           norm_final):
    depth = w_mod.shape[0]
    bp = c_prompt.shape[0]
    c_all = jnp.concatenate([c_prompt, c_sample], axis=0)
    layers, mods_p, mods_s = [], [], []
    for l in range(depth):
        mod = _mod_call(c_all, w_mod[l], b_mod[l]).reshape(c_all.shape[0], N_MOD, D_MODEL)
        mods_p.append(mod[:bp])
        mods_s.append(mod[bp:])
        wgu1, wd1 = _pack_ffn(ffn1_w_gate[l], ffn1_w_up[l], ffn1_w_down[l])
        wgu2, wd2 = _pack_ffn(ffn2_w_gate[l], ffn2_w_up[l], ffn2_w_down[l])
        wo = w_out[l].astype(BF16)
        layers.append(dict(
            norm_ffn1=norm_ffn1[l].reshape(1, -1), norm_mix=norm_mix[l].reshape(1, -1),
            norm_ffn2=norm_ffn2[l].reshape(1, -1), wgu1=wgu1, wd1=wd1, wgu2=wgu2, wd2=wd2,
            w_in=_pack_w_in(w_in[l]), wo_a=wo[:A_WIDTH], wo_g=wo[A_WIDTH:],
            conv_w=conv_w[l], gnorm_w=gnorm_w[l].reshape(1, -1),
            a_lane=_lane_vec(a_log[l], MISC_GA), d_lane=_lane_vec(dt_bias[l], MISC_GA)))
    nf = norm_final.reshape(1, -1)
    y_p, (k_p, v_p, ik_p, conv_p, delta_p) = _run(x_prompt, mods_p, None, layers, rel_bias, nf)
    past = (cache_k, cache_v, cache_idx_k, state_conv, state_delta)
    y_s, (k_s, v_s, ik_s, conv_s, delta_s) = _run(x_sample, mods_s, past, layers, rel_bias, nf)
    return (y_p, y_s, k_p, v_p, ik_p, conv_p, delta_p, k_s, v_s, ik_s, conv_s, delta_s)
```

```python
import functools

import jax
import jax.numpy as jnp
import numpy as np
from jax import lax
from jax.experimental import pallas as pl
from jax.experimental.pallas import tpu as pltpu

F32 = jnp.float32
BF16 = jnp.bfloat16

D_MODEL = 1024
CHUNK = 64
A_HEAD_DIM = 64
A_HEADS = 8
A_KV_HEADS = 4
A_WIDTH = A_HEADS * A_HEAD_DIM
IDX_HEADS = 8
IDX_DIM = 64
TOPK_MAX = 256
REL_BUCKETS = 32
G_KEY_DIM = 128
G_VAL_DIM = 128
G_HEADS = 4
G_WIDTH = G_HEADS * G_VAL_DIM
CONV_W = 4
CONV_CH = 2 * G_HEADS * G_KEY_DIM + G_HEADS * G_VAL_DIM
D_FF = 2816
N_MOD = 9
EPS = 1e-6

LANES = 128
MXU_DIM = 256
VMEM_LIMIT_BYTES = 56 * 1024 * 1024

FF_CHUNK = MXU_DIM
N_FF_CHUNKS = D_FF // FF_CHUNK
TOKEN_TILE = 512
ATT_TQ = 256
ATT_KB = 256
GDN_BLOCK = 256

QX_W = A_HEADS * LANES
OFF_QX = 0
OFF_K = OFF_QX + QX_W
OFF_V = OFF_K + A_KV_HEADS * A_HEAD_DIM
OFF_IQ = OFF_V + A_KV_HEADS * A_HEAD_DIM
OFF_IKX = OFF_IQ + IDX_HEADS * IDX_DIM
OFF_MISC = OFF_IKX + 2 * LANES
OFF_GQKV = OFF_MISC + LANES
OFF_GZ = OFF_GQKV + CONV_CH
IN_PACKED = OFF_GZ + G_WIDTH
MISC_IW = IDX_DIM
MISC_GB = MISC_IW + IDX_HEADS
MISC_GA = MISC_GB + G_HEADS

NEG_BIG = -1e30
INT_MIN = -2 ** 31
LOG2E = 1.4426950408889634
V_ROWS = LANES + 16


def _cparams(n_axes):
    return pltpu.CompilerParams(dimension_semantics=("arbitrary",) * n_axes,
                                vmem_limit_bytes=VMEM_LIMIT_BYTES)


def _resident(shape):
    nd = len(shape)
    return pl.BlockSpec(shape, lambda *_: (0,) * nd, pipeline_mode=pl.Buffered(1))


def _dot_nt(a, b):
    return lax.dot_general(a, b, (((1,), (1,)), ((), ())), preferred_element_type=F32)


def _rms_mod(x, gain, shift, scale):
    ms = jnp.mean(x * x, axis=-1, keepdims=True)
    y = x * lax.rsqrt(ms + EPS) * gain
    return y * (1.0 + scale) + shift


def _silu(x):
    return x * jax.nn.sigmoid(x)


def _tree_sum(parts):
    while len(parts) > 1:
        parts = [a + b for a, b in zip(parts[0::2], parts[1::2])] + ([parts[-1]] if len(parts) % 2 else [])
    return parts[0]


def _mod_kernel(c_ref, w_ref, b_ref, o_ref):
    s = _silu(c_ref[...]).astype(BF16)
    o_ref[...] = jnp.dot(s, w_ref[...].astype(BF16), preferred_element_type=F32) + b_ref[...]


def _mod_call(c, w_mod, b_mod):
    rows, d = c.shape
    n = w_mod.shape[1]
    tn = D_MODEL
    return pl.pallas_call(
        _mod_kernel,
        out_shape=jax.ShapeDtypeStruct((rows, n), F32),
        grid=(n // tn,),
        in_specs=[pl.BlockSpec((rows, d), lambda j: (0, 0)),
                  pl.BlockSpec((d, tn), lambda j: (0, j)),
                  pl.BlockSpec((1, tn), lambda j: (0, j))],
        out_specs=pl.BlockSpec((rows, tn), lambda j: (0, j)),
        compiler_params=_cparams(1),
        name="mod",
    )(c, w_mod, b_mod.reshape(1, n))


def _mod_specs(per_token, tm, tiles_per_batch):
    if per_token:
        return pl.BlockSpec((tm, D_MODEL), lambda i: (i, 0))
    return pl.BlockSpec((None, 1, D_MODEL), lambda i: (i // tiles_per_batch, 0, 0))


def _ffn_kernel(x_ref, sh_ref, sc_ref, gt_ref, gain_ref, wgu_ref, wd_ref, *rest, final_norm):
    if final_norm:
        nf_ref, o_ref, acc_ref = rest
    else:
        o_ref, acc_ref = rest
    x = x_ref[...]
    h = _rms_mod(x, gain_ref[...], sh_ref[...], sc_ref[...]).astype(BF16)
    for j in range(N_FF_CHUNKS):
        ab = jnp.dot(h, wgu_ref[j], preferred_element_type=F32)
        g = (_silu(ab[:, :FF_CHUNK]) * ab[:, FF_CHUNK:]).astype(BF16)
        d = jnp.dot(g, wd_ref[j], preferred_element_type=F32)
        if j == 0:
            acc_ref[...] = d
        else:
            acc_ref[...] += d
    y = x + 0.5 * gt_ref[...] * acc_ref[...]
    if final_norm:
        ms = jnp.mean(y * y, axis=-1, keepdims=True)
        y = y * lax.rsqrt(ms + EPS) * nf_ref[...]
    o_ref[...] = y


def _ffn_call(x2, sh, sc, gt, gain, wgu, wd, norm_final, *, tm, tiles_per_batch, per_token):
    n = x2.shape[0]
    mspec = _mod_specs(per_token, tm, tiles_per_batch)
    in_specs = [pl.BlockSpec((tm, D_MODEL), lambda i: (i, 0)), mspec, mspec, mspec,
                _resident((1, D_MODEL)), _resident(wgu.shape), _resident(wd.shape)]
    args = [x2, sh, sc, gt, gain, wgu, wd]
    final_norm = norm_final is not None
    if final_norm:
        in_specs.append(_resident((1, D_MODEL)))
        args.append(norm_final)
    return pl.pallas_call(
        functools.partial(_ffn_kernel, final_norm=final_norm),
        out_shape=jax.ShapeDtypeStruct((n, D_MODEL), F32),
        grid=(n // tm,),
        in_specs=in_specs,
        out_specs=pl.BlockSpec((tm, D_MODEL), lambda i: (i, 0)),
        scratch_shapes=[pltpu.VMEM((tm, D_MODEL), F32)],
        compiler_params=_cparams(1),
        name="ffn_final" if final_norm else "ffn",
    )(*args)


def _inproj_kernel(x_ref, sh_ref, sc_ref, gain_ref, w_ref,
                   qx_o, k_o, kb_o, v_o, vb_o, iq_o, ikx_o, misc_o, gqkv_o, gz_o):
    h = _rms_mod(x_ref[...], gain_ref[...], sh_ref[...], sc_ref[...]).astype(BF16)

    def mm(off, width):
        return jnp.dot(h, w_ref[:, off:off + width], preferred_element_type=F32)

    qx_o[...] = mm(OFF_QX, QX_W).astype(BF16)
    k = mm(OFF_K, OFF_V - OFF_K)
    k_o[...] = k
    kb_o[...] = k.astype(BF16)
    v = mm(OFF_V, OFF_IQ - OFF_V)
    v_o[...] = v
    vb_o[...] = v.astype(BF16)
    iq_o[...] = mm(OFF_IQ, OFF_IKX - OFF_IQ).astype(BF16)
    ikx_o[...] = mm(OFF_IKX, OFF_MISC - OFF_IKX).astype(BF16)
    misc_o[...] = mm(OFF_MISC, LANES)
    gqkv_o[...] = mm(OFF_GQKV, CONV_CH)
    gz_o[...] = mm(OFF_GZ, G_WIDTH)


def _inproj_call(x2, sh, sc, gain, w_packed, *, tm, tiles_per_batch, per_token):
    n = x2.shape[0]
    mspec = _mod_specs(per_token, tm, tiles_per_batch)
    widths = [(QX_W, BF16), (256, F32), (256, BF16), (256, F32), (256, BF16), (512, BF16),
              (2 * LANES, BF16), (LANES, F32), (CONV_CH, F32), (G_WIDTH, F32)]
    return pl.pallas_call(
        _inproj_kernel,
        out_shape=[jax.ShapeDtypeStruct((n, w), dt) for w, dt in widths],
        grid=(n // tm,),
        in_specs=[pl.BlockSpec((tm, D_MODEL), lambda i: (i, 0)), mspec, mspec,
                  _resident((1, D_MODEL)), _resident(w_packed.shape)],
        out_specs=[pl.BlockSpec((tm, w), lambda i: (i, 0)) for w, _ in widths],
        compiler_params=_cparams(1),
        name="inproj",
    )(x2, sh, sc, gain, w_packed)


def _outproj_kernel(x_ref, gt_ref, a_ref, g_ref, wa_ref, wg_ref, o_ref):
    m = (jnp.dot(a_ref[...], wa_ref[...], preferred_element_type=F32)
         + jnp.dot(g_ref[...], wg_ref[...], preferred_element_type=F32))
    o_ref[...] = x_ref[...] + gt_ref[...] * m


def _outproj_call(x2, gt, attn, gdn, wa, wg, *, tm, tiles_per_batch, per_token):
    n = x2.shape[0]
    mspec = _mod_specs(per_token, tm, tiles_per_batch)
    return pl.pallas_call(
        _outproj_kernel,
        out_shape=jax.ShapeDtypeStruct((n, D_MODEL), F32),
        grid=(n // tm,),
        in_specs=[pl.BlockSpec((tm, D_MODEL), lambda i: (i, 0)), mspec,
                  pl.BlockSpec((tm, A_WIDTH), lambda i: (i, 0)),
                  pl.BlockSpec((tm, G_WIDTH), lambda i: (i, 0)),
                  _resident(wa.shape), _resident(wg.shape)],
        out_specs=pl.BlockSpec((tm, D_MODEL), lambda i: (i, 0)),
        compiler_params=_cparams(1),
        name="outproj",
    )(x2, gt, attn, gdn, wa, wg)


def _rel_bucket_int(rel):
    n = jnp.abs(rel)
    large = jnp.full(rel.shape, 8, jnp.int32)
    for th in (12, 16, 23, 32, 46, 64, 91):
        large = large + jnp.where(n >= th, 1, 0)
    return jnp.where(rel > 0, REL_BUCKETS // 2, 0) + jnp.where(n < 8, n, large)


def _key_to_float(t):
    bits = jnp.where(t >= 0, t, t ^ jnp.int32(0x7FFFFFFF))
    return lax.bitcast_convert_type(bits, F32)


def _attn_kernel(relb_ref, qx_ref, iq_ref, misc_ref, k_ref, vt_ref, ikx_ref, o_ref,
                 s_ref, sel_ref, bias_ref, lg_ref, cm_ref, acc_ref, *, tq, l_true, offset, n_top):
    kb_sz = ATT_KB
    first = (pl.program_id(0) == 0) & (pl.program_id(1) == 0)
    q0 = offset + pl.program_id(1) * tq
    kmax = jnp.minimum(q0 + tq, l_true)
    nkb = (kmax + kb_sz - 1) // kb_sz

    key_off = lax.broadcasted_iota(jnp.int32, (kb_sz, tq), 0)
    qry_off = lax.broadcasted_iota(jnp.int32, (kb_sz, tq), 1)

    @pl.when(first)
    def _():
        for d in range(3):
            bucket = _rel_bucket_int(key_off - qry_off - d * kb_sz)
            for h in range(A_HEADS):
                t = jnp.zeros((kb_sz, tq), F32)
                for b in range(REL_BUCKETS):
                    t = jnp.where(bucket == b, relb_ref[b, h], t)
                bias_ref[d, h] = (t * LOG2E).astype(BF16)

    qpos = q0 + lax.broadcasted_iota(jnp.int32, (1, tq), 1)
    limit = jnp.minimum((qpos // CHUNK + 1) * CHUNK, l_true)

    iw_t = misc_ref[0].T[MISC_IW:MISC_IW + IDX_HEADS, :] * (IDX_HEADS ** -0.5)

    def score_body(kb, carry):
        base = pl.multiple_of(kb * kb_sz, kb_sz)
        ik2 = ikx_ref[0, pl.ds(base, kb_sz), :]
        ik_lo = ik2[:, :LANES]
        ik_hi = ik2[:, LANES:]
        s = jnp.zeros((kb_sz, tq), F32)
        for j in range(IDX_HEADS // 2):
            slab = iq_ref[0, :, j * LANES:(j + 1) * LANES]
            d0 = _dot_nt(ik_lo, slab)
            d1 = _dot_nt(ik_hi, slab)
            s = s + iw_t[2 * j:2 * j + 1, :] * jnp.maximum(d0, 0.0)
            s = s + iw_t[2 * j + 1:2 * j + 2, :] * jnp.maximum(d1, 0.0)
        s_ref[pl.ds(base, kb_sz), :] = jnp.where(base + key_off < limit, s, -jnp.inf)
        return carry

    lax.fori_loop(0, nkb, score_body, 0)

    def count(pred):
        def body(kb, acc):
            base = pl.multiple_of(kb * kb_sz, kb_sz)
            hit = pred(s_ref[pl.ds(base, kb_sz), :], base + key_off)
            return acc + _tree_sum([hit[r:r + 8] for r in range(0, kb_sz, 8)])
        acc = lax.fori_loop(0, nkb, body, jnp.zeros((8, tq), F32))
        return jnp.sum(acc, axis=0, keepdims=True)

    def bisect_body(i, t):
        cand = t + lax.shift_left(jnp.int32(1), 31 - i)
        thr_c = _key_to_float(cand)
        c = count(lambda blk, _: jnp.where(blk >= thr_c, 1.0, 0.0))
        return jnp.where(c >= n_top, cand, t)

    t_key = lax.fori_loop(0, 32, bisect_body, jnp.full((1, tq), INT_MIN, jnp.int32))
    thr = _key_to_float(t_key)

    need = n_top - count(lambda blk, _: jnp.where(blk > thr, 1.0, 0.0))
    n_eq = count(lambda blk, _: jnp.where(blk == thr, 1.0, 0.0))
    take_all = limit <= n_top
    excess = jnp.where(take_all, 0.0, jnp.where(n_eq > need, 1.0, 0.0))
    any_excess = jnp.max(excess) > 0.0
    idx_bits = 14

    def tie_body(i, c):
        cand = c + lax.shift_left(jnp.int32(1), idx_bits - 1 - i)
        f = count(lambda blk, kidx: jnp.where(blk == thr, jnp.where(kidx < cand, 1.0, 0.0), 0.0))
        return jnp.where(f <= need, cand, c)

    cut0 = jnp.where(any_excess, jnp.zeros((1, tq), jnp.int32),
                     jnp.full((1, tq), 2 ** idx_bits, jnp.int32))
    cut = lax.fori_loop(0, jnp.where(any_excess, idx_bits, 0), tie_body, cut0)
    thr = jnp.where(take_all, -jnp.inf, thr)
    cut = jnp.where(take_all, limit, cut)

    def mask_body(kb, carry):
        base = pl.multiple_of(kb * kb_sz, kb_sz)
        blk = s_ref[pl.ds(base, kb_sz), :]
        tie = jnp.where(base + key_off < cut, 0.0, NEG_BIG)
        sel = jnp.where(blk > thr, 0.0, jnp.where(blk == thr, tie, NEG_BIG))
        sel_ref[pl.ds(base, kb_sz), :] = sel.astype(BF16)
        return carry

    lax.fori_loop(0, nkb, mask_body, 0)

    cm_ref[...] = jnp.full(cm_ref.shape, NEG_BIG, BF16)

    def logit_body(kb, carry):
        base = pl.multiple_of(kb * kb_sz, kb_sz)
        dsel = jnp.clip((q0 - base) // kb_sz, 0, 2)
        kblk = k_ref[0, pl.ds(base, kb_sz), :]
        sel = sel_ref[pl.ds(base, kb_sz), :]
        for h in range(A_HEADS):
            sl = h // 4
            lg = _dot_nt(kblk[:, sl * LANES:(sl + 1) * LANES], qx_ref[0, :, h * LANES:(h + 1) * LANES])
            lg = lg.astype(BF16) + sel + bias_ref[dsel, h]
            lg_ref[h, pl.ds(base, kb_sz), :] = lg
            cm_ref[h] = jnp.maximum(cm_ref[h], lg)
        return carry

    lax.fori_loop(0, nkb, logit_body, 0)

    acc_ref[...] = jnp.zeros(acc_ref.shape, F32)
    m_rows = [jnp.max(cm_ref[h].astype(F32), axis=0, keepdims=True).astype(BF16) for h in range(A_HEADS)]

    def pv_body(kb, carry):
        base = pl.multiple_of(kb * kb_sz, kb_sz)
        for h in range(A_HEADS):
            p = jnp.exp2(lg_ref[h, pl.ds(base, kb_sz), :] - m_rows[h])
            acc_ref[h] += jnp.dot(vt_ref[0, kb, h // 4], p, preferred_element_type=F32)
        return carry

    lax.fori_loop(0, nkb, pv_body, 0)

    outs = []
    for h in range(A_HEADS):
        pos = (h // 2) % 2
        outs.append(acc_ref[h, pos * A_HEAD_DIM:(pos + 1) * A_HEAD_DIM, :] / acc_ref[h, LANES:LANES + 1, :])
    o_ref[0] = jnp.concatenate(outs, axis=0).T.astype(BF16)


def _attn_call(rel_bias, qx, iq, misc, kb, vt, ikx, *, tq, l_true, offset, n_top):
    b, tq_total, _ = qx.shape
    lp = kb.shape[1]
    nq = tq_total // tq
    assert offset % ATT_KB == 0 and tq % CHUNK == 0 and ATT_KB % tq == 0 and lp % ATT_KB == 0
    assert lp < 2 ** 14 and l_true <= lp
    kern = functools.partial(_attn_kernel, tq=tq, l_true=l_true, offset=offset, n_top=n_top)
    return pl.pallas_call(
        kern,
        out_shape=jax.ShapeDtypeStruct((b, tq_total, A_WIDTH), BF16),
        grid=(b, nq),
        in_specs=[pl.BlockSpec(memory_space=pltpu.SMEM),
                  pl.BlockSpec((1, tq, QX_W), lambda i, j: (i, j, 0)),
                  pl.BlockSpec((1, tq, IDX_HEADS * IDX_DIM), lambda i, j: (i, j, 0)),
                  pl.BlockSpec((1, tq, LANES), lambda i, j: (i, j, 0)),
                  pl.BlockSpec((1, lp, 256), lambda i, j: (i, 0, 0)),
                  pl.BlockSpec((1, lp // ATT_KB, 2, V_ROWS, ATT_KB), lambda i, j: (i, 0, 0, 0, 0)),
                  pl.BlockSpec((1, lp, 2 * LANES), lambda i, j: (i, 0, 0))],
        out_specs=pl.BlockSpec((1, tq, A_WIDTH), lambda i, j: (i, j, 0)),
        scratch_shapes=[pltpu.VMEM((lp, tq), F32),
                        pltpu.VMEM((lp, tq), BF16),
                        pltpu.VMEM((3, A_HEADS, ATT_KB, tq), BF16),
                        pltpu.VMEM((A_HEADS, lp, tq), BF16),
                        pltpu.VMEM((A_HEADS, ATT_KB, tq), BF16),
                        pltpu.VMEM((A_HEADS, V_ROWS, tq), F32)],
        compiler_params=_cparams(2),
        name="sparse_attn",
    )(rel_bias, qx, iq, misc, kb, vt, ikx)


def _split_bf16(a, n):
    parts = []
    r = a
    for i in range(n):
        p = r.astype(BF16)
        parts.append(p)
        if i + 1 < n:
            r = r - p.astype(F32)
    return parts


def _bdot(a, b, dims=(((1,), (0,)), ((), ()))):
    return lax.dot_general(a.astype(BF16), b.astype(BF16), dims, preferred_element_type=F32)


_NT = (((1,), (1,)), ((), ()))
_TN = (((0,), (0,)), ((), ()))


def _gdn_kernel(x_ref, misc_ref, gz_ref, convw_ref, hist_ref, s0_ref, gnw_ref, alane_ref, dlane_ref,
                o_ref, conv_o, s_o, xp_ref, y_ref, st_ref, vn_ref, *, cb, t_true, nblk):
    c = CHUNK
    blk = pl.program_id(1)

    @pl.when(blk == 0)
    def _():
        xp_ref[8 - (CONV_W - 1):8, :] = hist_ref[0]
        st_ref[...] = s0_ref[0]

    xp_ref[8:8 + cb, :] = x_ref[0]
    y = convw_ref[CONV_W - 1:CONV_W, :] * xp_ref[8:8 + cb, :]
    for j in range(CONV_W - 1):
        y = y + convw_ref[j:j + 1, :] * xp_ref[5 + j:5 + j + cb, :]
    y_ref[...] = _silu(y)

    last_row = (t_true - 1) % cb

    @pl.when(blk == nblk - 1)
    def _():
        conv_o[0] = xp_ref[8 + last_row - (CONV_W - 2):8 + last_row + 1, :]

    xp_ref[8 - (CONV_W - 1):8, :] = xp_ref[8 + cb - (CONV_W - 1):8 + cb, :]

    ri = lax.broadcasted_iota(jnp.int32, (cb, cb), 0)
    ci = lax.broadcasted_iota(jnp.int32, (cb, cb), 1)
    lag = jnp.where((ri // c) == (ci // c), ri - ci, -1)
    tri = lag >= 0
    strict = lag > 0
    eye = ri == ci
    tril_bf = jnp.where(tri, 1.0, 0.0).astype(BF16)
    eye_f = jnp.where(eye, 1.0, 0.0)

    ms = misc_ref[0]
    tok = blk * cb + lax.broadcasted_iota(jnp.int32, (cb, 1), 0)
    live = tok < t_true
    beta_s = jnp.where(live, jax.nn.sigmoid(ms), 0.0)
    z = ms + dlane_ref[...]
    softplus = jnp.maximum(z, 0.0) + jnp.log(1.0 + jnp.exp(-jnp.abs(z)))
    g_s = jnp.where(live, -jnp.exp(alane_ref[...]) * softplus, 0.0)
    gc_s = None
    for piece in _split_bf16(g_s, 3):
        t = jnp.dot(tril_bf, piece, preferred_element_type=F32)
        gc_s = t if gc_s is None else gc_s + t

    heads = []
    for h in range(G_HEADS):
        q = y_ref[:, h * G_KEY_DIM:(h + 1) * G_KEY_DIM]
        k = y_ref[:, G_WIDTH + h * G_KEY_DIM:G_WIDTH + (h + 1) * G_KEY_DIM]
        v = y_ref[:, 2 * G_WIDTH + h * G_VAL_DIM:2 * G_WIDTH + (h + 1) * G_VAL_DIM]
        q = q * lax.rsqrt(jnp.sum(q * q, axis=-1, keepdims=True) + 1e-6) * (G_KEY_DIM ** -0.5)
        k = k * lax.rsqrt(jnp.sum(k * k, axis=-1, keepdims=True) + 1e-6)
        beta = beta_s[:, MISC_GB + h:MISC_GB + h + 1]
        gc = gc_s[:, MISC_GA + h:MISC_GA + h + 1]
        gc_b = jnp.broadcast_to(gc, (cb, cb))
        gc_row = jnp.sum(jnp.where(eye, gc_b, 0.0), axis=0, keepdims=True)
        decay = jnp.exp(jnp.where(tri, gc_b - gc_row, NEG_BIG))
        kb = k * beta
        m = jnp.where(strict, _bdot(kb, k, _NT) * decay, 0.0)
        eg = jnp.exp(gc)
        rhs = jnp.concatenate([v * beta, kb * eg], axis=1)
        pw = -m
        inv = eye_f + pw
        for _ in range(5):
            pw = _bdot(pw, pw)
            inv = inv + _bdot(inv, pw)
        sol = _bdot(inv, rhs)
        m_hi, m_lo = _split_bf16(m, 2)
        s_hi, s_lo = _split_bf16(sol, 2)
        m_sol = _bdot(m_hi, s_hi) + _bdot(m_hi, s_lo) + _bdot(m_lo, s_hi)
        sol = sol + _bdot(inv, rhs - sol - m_sol)
        qk = jnp.where(tri, _bdot(q, k, _NT) * decay, 0.0)
        heads.append(dict(u=sol[:, :G_VAL_DIM], w=sol[:, G_VAL_DIM:], qk=qk, qe=q * eg, k=k, gc=gc))

    vn_ref[...] = jnp.zeros(vn_ref.shape, F32)
    for ck in range(cb // c):
        r0, r1 = ck * c, (ck + 1) * c
        for h in range(G_HEADS):
            hd = heads[h]
            s_prev = st_ref[h]
            v_new = hd["u"][r0:r1] - _bdot(hd["w"][r0:r1], s_prev)
            vn_ref[h, r0:r1, :] = v_new
            o = _bdot(hd["qe"][r0:r1], s_prev) + _bdot(hd["qk"][r0:r1], vn_ref[h])
            gc = hd["gc"][r0:r1]
            g_last = gc[c - 1:c, :]
            st_ref[h] = s_prev * jnp.exp(g_last) + _bdot(hd["k"][r0:r1] * jnp.exp(g_last - gc), v_new, _TN)
            on = o * lax.rsqrt(jnp.mean(o * o, axis=-1, keepdims=True) + EPS) * gnw_ref[...]
            gate = gz_ref[0, r0:r1, h * G_VAL_DIM:(h + 1) * G_VAL_DIM]
            o_ref[0, r0:r1, h * G_VAL_DIM:(h + 1) * G_VAL_DIM] = (on * _silu(gate)).astype(BF16)

    @pl.when(blk == nblk - 1)
    def _():
        s_o[0] = st_ref[...]


def _gdn_call(gqkv, misc, gz, conv_w, hist, s0, gnorm_w, a_lane, d_lane, *, cb, t_true):
    b, t_pad, _ = gqkv.shape
    nblk = t_pad // cb
    assert t_true >= CONV_W - 1 and (nblk - 1) * cb < t_true <= t_pad
    kern = functools.partial(_gdn_kernel, cb=cb, t_true=t_true, nblk=nblk)
    return pl.pallas_call(
        kern,
        out_shape=[jax.ShapeDtypeStruct((b, t_pad, G_WIDTH), BF16),
                   jax.ShapeDtypeStruct((b, CONV_W - 1, CONV_CH), F32),
                   jax.ShapeDtypeStruct((b, G_HEADS, G_KEY_DIM, G_VAL_DIM), F32)],
        grid=(b, nblk),
        in_specs=[pl.BlockSpec((1, cb, CONV_CH), lambda i, j: (i, j, 0)),
                  pl.BlockSpec((1, cb, LANES), lambda i, j: (i, j, 0)),
                  pl.BlockSpec((1, cb, G_WIDTH), lambda i, j: (i, j, 0)),
                  pl.BlockSpec((CONV_W, CONV_CH), lambda i, j: (0, 0)),
                  pl.BlockSpec((1, CONV_W - 1, CONV_CH), lambda i, j: (i, 0, 0)),
                  pl.BlockSpec((1, G_HEADS, G_KEY_DIM, G_VAL_DIM), lambda i, j: (i, 0, 0, 0)),
                  pl.BlockSpec((1, G_VAL_DIM), lambda i, j: (0, 0)),
                  pl.BlockSpec((1, LANES), lambda i, j: (0, 0)),
                  pl.BlockSpec((1, LANES), lambda i, j: (0, 0))],
        out_specs=[pl.BlockSpec((1, cb, G_WIDTH), lambda i, j: (i, j, 0)),
                   pl.BlockSpec((1, CONV_W - 1, CONV_CH), lambda i, j: (i, 0, 0)),
                   pl.BlockSpec((1, G_HEADS, G_KEY_DIM, G_VAL_DIM), lambda i, j: (i, 0, 0, 0))],
        scratch_shapes=[pltpu.VMEM((cb + 8, CONV_CH), F32),
                        pltpu.VMEM((cb, CONV_CH), F32),
                        pltpu.VMEM((G_HEADS, G_KEY_DIM, G_VAL_DIM), F32),
                        pltpu.VMEM((G_HEADS, cb, G_VAL_DIM), F32)],
        compiler_params=_cparams(2),
        name="gated_delta",
    )(gqkv, misc, gz, conv_w, hist, s0, gnorm_w, a_lane, d_lane)


def _pack_w_in(w_in):
    d = w_in.shape[0]
    splits = (512, 256, 256, 512, 64, 8, 512, 512, 512, 512, 4, 4)
    offs = np.concatenate([[0], np.cumsum(splits)])
    aq, ak, av, iq, ik, iw, gq, gk, gv, gz, gb, ga = [w_in[:, offs[i]:offs[i + 1]] for i in range(12)]
    zeros64 = jnp.zeros((d, A_HEAD_DIM), w_in.dtype)
    qx = []
    for h in range(A_HEADS):
        qh = aq[:, h * A_HEAD_DIM:(h + 1) * A_HEAD_DIM] * (A_HEAD_DIM ** -0.5 * LOG2E)
        qx += [qh, zeros64] if (h // 2) % 2 == 0 else [zeros64, qh]
    ikx = [ik, zeros64, zeros64, ik]
    misc = [ik, iw, gb, ga, jnp.zeros((d, LANES - MISC_GA - G_HEADS), w_in.dtype)]
    cols = qx + [ak, av, iq * (IDX_DIM ** -0.5)] + ikx + misc + [gq, gk, gv, gz]
    packed = jnp.concatenate(cols, axis=1).astype(BF16)
    assert packed.shape[1] == IN_PACKED
    return packed


def _pack_ffn(w_gate, w_up, w_down):
    d = w_gate.shape[0]
    wg = w_gate.reshape(d, N_FF_CHUNKS, FF_CHUNK)
    wu = w_up.reshape(d, N_FF_CHUNKS, FF_CHUNK)
    wgu = jnp.transpose(jnp.concatenate([wg, wu], axis=2), (1, 0, 2)).astype(BF16)
    wd = w_down.reshape(N_FF_CHUNKS, FF_CHUNK, d).astype(BF16)
    return wgu, wd


def _ikx_layout(ik):
    z = jnp.zeros_like(ik)
    return jnp.concatenate([ik, z, z, ik], axis=-1)


def _lane_vec(vals, lane0):
    return jnp.zeros((1, LANES), F32).at[0, lane0:lane0 + vals.shape[0]].set(vals.astype(F32))


def _run(x, mod, past, layers, rel_bias, norm_final):
    b, t, d = x.shape
    n = b * t
    per_token = t < TOKEN_TILE
    tm = n if per_token else TOKEN_TILE
    tiles_per_batch = None if per_token else t // tm
    tok = dict(tm=tm, tiles_per_batch=tiles_per_batch, per_token=per_token)

    x2 = x.reshape(n, d)
    states = []
    for li, lw in enumerate(layers):
        m = mod[li]

        def mvec(kidx):
            row = m[:, kidx]
            return jnp.repeat(row, t, axis=0) if per_token else row[:, None, :]

        sh1, sc1, gt1, sh2, sc2, gt2, sh3, sc3, gt3 = [mvec(i) for i in range(N_MOD)]
        x2 = _ffn_call(x2, sh1, sc1, gt1, lw["norm_ffn1"], lw["wgu1"], lw["wd1"], None, **tok)
        qx, k, kb, v, vb, iq, ikx, misc, gqkv, gz = _inproj_call(
            x2, sh2, sc2, lw["norm_mix"], lw["w_in"], **tok)

        if past is None:
            offset, l_true, tq = 0, t, ATT_TQ
            k_all, v_all, ikx_all = (a.reshape(b, t, -1) for a in (kb, vb, ikx))
            qx3, iq3, misc3 = (a.reshape(b, t, -1) for a in (qx, iq, misc))
            conv_hist = jnp.zeros((b, CONV_W - 1, CONV_CH), F32)
            s0 = jnp.zeros((b, G_HEADS, G_KEY_DIM, G_VAL_DIM), F32)
        else:
            k_hist, v_hist, ik_hist, conv_hist, s0 = (p[li] for p in past)
            offset = k_hist.shape[1]
            l_true = offset + t
            tq = LANES
            lp = -(-l_true // ATT_KB) * ATT_KB
            pad_k = lambda a: jnp.pad(a, ((0, 0), (0, lp - l_true), (0, 0)))
            k_all = pad_k(jnp.concatenate([k_hist.reshape(b, offset, -1).astype(BF16),
                                           kb.reshape(b, t, -1)], axis=1))
            v_all = pad_k(jnp.concatenate([v_hist.reshape(b, offset, -1).astype(BF16),
                                           vb.reshape(b, t, -1)], axis=1))
            ikx_all = pad_k(jnp.concatenate([_ikx_layout(ik_hist.astype(BF16)),
                                             ikx.reshape(b, t, -1)], axis=1))
            pad_q = lambda a: jnp.pad(a.reshape(b, t, -1), ((0, 0), (0, tq - t), (0, 0)))
            qx3, iq3, misc3 = pad_q(qx), pad_q(iq), pad_q(misc)
            s0 = s0.astype(F32)
            conv_hist = conv_hist.astype(F32)
        lp = k_all.shape[1]
        n_top = min(TOPK_MAX, l_true // 4)
        vt = jnp.transpose(v_all.reshape(b, lp // ATT_KB, ATT_KB, 2, LANES), (0, 1, 3, 4, 2))
        vt = jnp.concatenate([vt, jnp.ones((b, lp // ATT_KB, 2, V_ROWS - LANES, ATT_KB), BF16)], axis=3)
        attn = _attn_call(rel_bias, qx3, iq3, misc3, k_all, vt, ikx_all,
                          tq=tq, l_true=l_true, offset=offset, n_top=n_top)
        attn = attn[:, :t].reshape(n, A_WIDTH)

        cb = GDN_BLOCK if t % GDN_BLOCK == 0 else CHUNK
        t_pad = -(-t // cb) * cb
        pad_t = lambda a: jnp.pad(a.reshape(b, t, -1), ((0, 0), (0, t_pad - t), (0, 0)))
        gdn, conv_new, s_new = _gdn_call(pad_t(gqkv), pad_t(misc), pad_t(gz), lw["conv_w"], conv_hist, s0,
                                         lw["gnorm_w"], lw["a_lane"], lw["d_lane"], cb=cb, t_true=t)
        gdn = gdn[:, :t].reshape(n, G_WIDTH)

        x2 = _outproj_call(x2, gt2, attn, gdn, lw["wo_a"], lw["wo_g"], **tok)
        nf = norm_final if li == len(layers) - 1 else None
        x2 = _ffn_call(x2, sh3, sc3, gt3, lw["norm_ffn2"], lw["wgu2"], lw["wd2"], nf, **tok)

        states.append((k.reshape(b, t, A_KV_HEADS, A_HEAD_DIM), v.reshape(b, t, A_KV_HEADS, A_HEAD_DIM),
                       misc.reshape(b, t, LANES)[..., :IDX_DIM], conv_new, s_new))
    stacked = [jnp.stack(s, axis=0) for s in zip(*states)]
    return x2.reshape(b, t, d), stacked


def kernel(x_prompt, x_sample, cache_k, cache_v, cache_idx_k, state_conv, state_delta, c_prompt, c_sample,
           w_mod, b_mod, norm_ffn1, norm_mix, norm_ffn2, ffn1_w_gate, ffn1_w_up, ffn1_w_down,
           ffn2_w_gate, ffn2_w_up, ffn2_w_down, w_in, w_out, rel_bias, conv_w, a_log, dt_bias, gnorm_w,
           norm_final):
    depth = w_mod.shape[0]
    bp = c_prompt.shape[0]
    c_all = jnp.concatenate([c_prompt, c_sample], axis=0)
    layers, mods_p, mods_s = [], [], []
    for l in range(depth):
        mod = _mod_call(c_all, w_mod[l], b_mod[l]).reshape(c_all.shape[0], N_MOD, D_MODEL)
        mods_p.append(mod[:bp])
        mods_s.append(mod[bp:])
        wgu1, wd1 = _pack_ffn(ffn1_w_gate[l], ffn1_w_up[l], ffn1_w_down[l])
        wgu2, wd2 = _pack_ffn(ffn2_w_gate[l], ffn2_w_up[l], ffn2_w_down[l])
        wo = w_out[l].astype(BF16)
        layers.append(dict(
            norm_ffn1=norm_ffn1[l].reshape(1, -1), norm_mix=norm_mix[l].reshape(1, -1),
            norm_ffn2=norm_ffn2[l].reshape(1, -1), wgu1=wgu1, wd1=wd1, wgu2=wgu2, wd2=wd2,
            w_in=_pack_w_in(w_in[l]), wo_a=wo[:A_WIDTH], wo_g=wo[A_WIDTH:],
            conv_w=conv_w[l], gnorm_w=gnorm_w[l].reshape(1, -1),
            a_lane=_lane_vec(a_log[l], MISC_GA), d_lane=_lane_vec(dt_bias[l], MISC_GA)))
    nf = norm_final.reshape(1, -1)
    y_p, (k_p, v_p, ik_p, conv_p, delta_p) = _run(x_prompt, mods_p, None, layers, rel_bias, nf)
    past = (cache_k, cache_v, cache_idx_k, state_conv, state_delta)
    y_s, (k_s, v_s, ik_s, conv_s, delta_s) = _run(x_sample, mods_s, past, layers, rel_bias, nf)
    return (y_p, y_s, k_p, v_p, ik_p, conv_p, delta_p, k_s, v_s, ik_s, conv_s, delta_s)
```

```python
import functools

import jax
import jax.numpy as jnp
import numpy as np
from jax import lax
from jax.experimental import pallas as pl
from jax.experimental.pallas import tpu as pltpu

F32 = jnp.float32
BF16 = jnp.bfloat16

D_MODEL = 1024
CHUNK = 64
A_HEAD_DIM = 64
A_HEADS = 8
A_KV_HEADS = 4
A_WIDTH = A_HEADS * A_HEAD_DIM
IDX_HEADS = 8
IDX_DIM = 64
TOPK_MAX = 256
REL_BUCKETS = 32
G_KEY_DIM = 128
G_VAL_DIM = 128
G_HEADS = 4
G_WIDTH = G_HEADS * G_VAL_DIM
CONV_W = 4
CONV_CH = 2 * G_HEADS * G_KEY_DIM + G_HEADS * G_VAL_DIM
D_FF = 2816
N_MOD = 9
EPS = 1e-6

LANES = 128
MXU_DIM = 256
VMEM_LIMIT_BYTES = 56 * 1024 * 1024

FF_CHUNK = MXU_DIM
N_FF_CHUNKS = D_FF // FF_CHUNK
TOKEN_TILE = 512
ATT_TQ = 256
ATT_KB = 256
GDN_BLOCK = 256

QX_W = A_HEADS * LANES
OFF_QX = 0
OFF_K = OFF_QX + QX_W
OFF_V = OFF_K + A_KV_HEADS * A_HEAD_DIM
OFF_IQ = OFF_V + A_KV_HEADS * A_HEAD_DIM
OFF_IKX = OFF_IQ + IDX_HEADS * IDX_DIM
OFF_MISC = OFF_IKX + 2 * LANES
OFF_GQKV = OFF_MISC + LANES
OFF_GZ = OFF_GQKV + CONV_CH
IN_PACKED = OFF_GZ + G_WIDTH
MISC_IW = IDX_DIM
MISC_GB = MISC_IW + IDX_HEADS
MISC_GA = MISC_GB + G_HEADS

NEG_BIG = -1e30
INT_MIN = -2 ** 31
LOG2E = 1.4426950408889634
V_ROWS = LANES + 16


def _cparams(n_axes):
    return pltpu.CompilerParams(dimension_semantics=("arbitrary",) * n_axes,
                                vmem_limit_bytes=VMEM_LIMIT_BYTES)


def _resident(shape):
    nd = len(shape)
    return pl.BlockSpec(shape, lambda *_: (0,) * nd, pipeline_mode=pl.Buffered(1))


def _dot_nt(a, b):
    return lax.dot_general(a, b, (((1,), (1,)), ((), ())), preferred_element_type=F32)


def _rms_mod(x, gain, shift, scale):
    ms = jnp.mean(x * x, axis=-1, keepdims=True)
    y = x * lax.rsqrt(ms + EPS) * gain
    return y * (1.0 + scale) + shift


def _silu(x):
    return x * jax.nn.sigmoid(x)


def _tree_sum(parts):
    while len(parts) > 1:
        parts = [a + b for a, b in zip(parts[0::2], parts[1::2])] + ([parts[-1]] if len(parts) % 2 else [])
    return parts[0]


def _mod_kernel(c_ref, w_ref, b_ref, o_ref):
    s = _silu(c_ref[...]).astype(BF16)
    o_ref[...] = jnp.dot(s, w_ref[...].astype(BF16), preferred_element_type=F32) + b_ref[...]


def _mod_call(c, w_mod, b_mod):
    rows, d = c.shape
    n = w_mod.shape[1]
    tn = D_MODEL
    return pl.pallas_call(
        _mod_kernel,
        out_shape=jax.ShapeDtypeStruct((rows, n), F32),
        grid=(n // tn,),
        in_specs=[pl.BlockSpec((rows, d), lambda j: (0, 0)),
                  pl.BlockSpec((d, tn), lambda j: (0, j)),
                  pl.BlockSpec((1, tn), lambda j: (0, j))],
        out_specs=pl.BlockSpec((rows, tn), lambda j: (0, j)),
        compiler_params=_cparams(1),
        name="mod",
    )(c, w_mod, b_mod.reshape(1, n))


def _mod_specs(per_token, tm, tiles_per_batch):
    if per_token:
        return pl.BlockSpec((tm, D_MODEL), lambda i: (i, 0))
    return pl.BlockSpec((None, 1, D_MODEL), lambda i: (i // tiles_per_batch, 0, 0))


def _ffn_kernel(x_ref, sh_ref, sc_ref, gt_ref, gain_ref, wgu_ref, wd_ref, *rest, final_norm):
    if final_norm:
        nf_ref, o_ref, acc_ref = rest
    else:
        o_ref, acc_ref = rest
    x = x_ref[...]
    h = _rms_mod(x, gain_ref[...], sh_ref[...], sc_ref[...]).astype(BF16)
    for j in range(N_FF_CHUNKS):
        ab = jnp.dot(h, wgu_ref[j], preferred_element_type=F32)
        g = (_silu(ab[:, :FF_CHUNK]) * ab[:, FF_CHUNK:]).astype(BF16)
        d = jnp.dot(g, wd_ref[j], preferred_element_type=F32)
        if j == 0:
            acc_ref[...] = d
        else:
            acc_ref[...] += d
    y = x + 0.5 * gt_ref[...] * acc_ref[...]
    if final_norm:
        ms = jnp.mean(y * y, axis=-1, keepdims=True)
        y = y * lax.rsqrt(ms + EPS) * nf_ref[...]
    o_ref[...] = y


def _ffn_call(x2, sh, sc, gt, gain, wgu, wd, norm_final, *, tm, tiles_per_batch, per_token):
    n = x2.shape[0]
    mspec = _mod_specs(per_token, tm, tiles_per_batch)
    in_specs = [pl.BlockSpec((tm, D_MODEL), lambda i: (i, 0)), mspec, mspec, mspec,
                _resident((1, D_MODEL)), _resident(wgu.shape), _resident(wd.shape)]
    args = [x2, sh, sc, gt, gain, wgu, wd]
    final_norm = norm_final is not None
    if final_norm:
        in_specs.append(_resident((1, D_MODEL)))
        args.append(norm_final)
    return pl.pallas_call(
        functools.partial(_ffn_kernel, final_norm=final_norm),
        out_shape=jax.ShapeDtypeStruct((n, D_MODEL), F32),
        grid=(n // tm,),
        in_specs=in_specs,
        out_specs=pl.BlockSpec((tm, D_MODEL), lambda i: (i, 0)),
        scratch_shapes=[pltpu.VMEM((tm, D_MODEL), F32)],
        compiler_params=_cparams(1),
        name="ffn_final" if final_norm else "ffn",
    )(*args)


def _inproj_kernel(x_ref, sh_ref, sc_ref, gain_ref, w_ref,
                   qx_o, k_o, kb_o, v_o, vb_o, iq_o, ikx_o, misc_o, gqkv_o, gz_o):
    h = _rms_mod(x_ref[...], gain_ref[...], sh_ref[...], sc_ref[...]).astype(BF16)

    def mm(off, width):
        return jnp.dot(h, w_ref[:, off:off + width], preferred_element_type=F32)

    qx_o[...] = mm(OFF_QX, QX_W).astype(BF16)
    k = mm(OFF_K, OFF_V - OFF_K)
    k_o[...] = k
    kb_o[...] = k.astype(BF16)
    v = mm(OFF_V, OFF_IQ - OFF_V)
    v_o[...] = v
    vb_o[...] = v.astype(BF16)
    iq_o[...] = mm(OFF_IQ, OFF_IKX - OFF_IQ).astype(BF16)
    ikx_o[...] = mm(OFF_IKX, OFF_MISC - OFF_IKX).astype(BF16)
    misc_o[...] = mm(OFF_MISC, LANES)
    gqkv_o[...] = mm(OFF_GQKV, CONV_CH)
    gz_o[...] = mm(OFF_GZ, G_WIDTH)


def _inproj_call(x2, sh, sc, gain, w_packed, *, tm, tiles_per_batch, per_token):
    n = x2.shape[0]
    mspec = _mod_specs(per_token, tm, tiles_per_batch)
    widths = [(QX_W, BF16), (256, F32), (256, BF16), (256, F32), (256, BF16), (512, BF16),
              (2 * LANES, BF16), (LANES, F32), (CONV_CH, F32), (G_WIDTH, F32)]
    return pl.pallas_call(
        _inproj_kernel,
        out_shape=[jax.ShapeDtypeStruct((n, w), dt) for w, dt in widths],
        grid=(n // tm,),
        in_specs=[pl.BlockSpec((tm, D_MODEL), lambda i: (i, 0)), mspec, mspec,
                  _resident((1, D_MODEL)), _resident(w_packed.shape)],
        out_specs=[pl.BlockSpec((tm, w), lambda i: (i, 0)) for w, _ in widths],
        compiler_params=_cparams(1),
        name="inproj",
    )(x2, sh, sc, gain, w_packed)


def _outproj_kernel(x_ref, gt_ref, a_ref, g_ref, wa_ref, wg_ref, o_ref):
    m = (jnp.dot(a_ref[...], wa_ref[...], preferred_element_type=F32)
         + jnp.dot(g_ref[...], wg_ref[...], preferred_element_type=F32))
    o_ref[...] = x_ref[...] + gt_ref[...] * m


def _outproj_call(x2, gt, attn, gdn, wa, wg, *, tm, tiles_per_batch, per_token):
    n = x2.shape[0]
    mspec = _mod_specs(per_token, tm, tiles_per_batch)
    return pl.pallas_call(
        _outproj_kernel,
        out_shape=jax.ShapeDtypeStruct((n, D_MODEL), F32),
        grid=(n // tm,),
        in_specs=[pl.BlockSpec((tm, D_MODEL), lambda i: (i, 0)), mspec,
                  pl.BlockSpec((tm, A_WIDTH), lambda i: (i, 0)),
                  pl.BlockSpec((tm, G_WIDTH), lambda i: (i, 0)),
                  _resident(wa.shape), _resident(wg.shape)],
        out_specs=pl.BlockSpec((tm, D_MODEL), lambda i: (i, 0)),
        compiler_params=_cparams(1),
        name="outproj",
    )(x2, gt, attn, gdn, wa, wg)


def _rel_bucket_int(rel):
    n = jnp.abs(rel)
    large = jnp.full(rel.shape, 8, jnp.int32)
    for th in (12, 16, 23, 32, 46, 64, 91):
        large = large + jnp.where(n >= th, 1, 0)
    return jnp.where(rel > 0, REL_BUCKETS // 2, 0) + jnp.where(n < 8, n, large)


def _key_to_float(t):
    bits = jnp.where(t >= 0, t, t ^ jnp.int32(0x7FFFFFFF))
    return lax.bitcast_convert_type(bits, F32)


def _attn_kernel(relb_ref, qx_ref, iq_ref, misc_ref, k_ref, vt_ref, ikx_ref, o_ref,
                 s_ref, sel_ref, bias_ref, lg_ref, cm_ref, acc_ref, *, tq, l_true, offset, n_top):
    kb_sz = ATT_KB
    first = (pl.program_id(0) == 0) & (pl.program_id(1) == 0)
    q0 = offset + pl.program_id(1) * tq
    kmax = jnp.minimum(q0 + tq, l_true)
    nkb = (kmax + kb_sz - 1) // kb_sz

    key_off = lax.broadcasted_iota(jnp.int32, (kb_sz, tq), 0)
    qry_off = lax.broadcasted_iota(jnp.int32, (kb_sz, tq), 1)

    @pl.when(first)
    def _():
        for d in range(3):
            bucket = _rel_bucket_int(key_off - qry_off - d * kb_sz)
            for h in range(A_HEADS):
                t = jnp.zeros((kb_sz, tq), F32)
                for b in range(REL_BUCKETS):
                    t = jnp.where(bucket == b, relb_ref[b, h], t)
                bias_ref[d, h] = (t * LOG2E).astype(BF16)

    qpos = q0 + lax.broadcasted_iota(jnp.int32, (1, tq), 1)
    limit = jnp.minimum((qpos // CHUNK + 1) * CHUNK, l_true)

    iw_t = misc_ref[0].T[MISC_IW:MISC_IW + IDX_HEADS, :] * (IDX_HEADS ** -0.5)

    def score_body(kb, carry):
        base = pl.multiple_of(kb * kb_sz, kb_sz)
        ik2 = ikx_ref[0, pl.ds(base, kb_sz), :]
        ik_lo = ik2[:, :LANES]
        ik_hi = ik2[:, LANES:]
        s = jnp.zeros((kb_sz, tq), F32)
        for j in range(IDX_HEADS // 2):
            slab = iq_ref[0, :, j * LANES:(j + 1) * LANES]
            d0 = _dot_nt(ik_lo, slab)
            d1 = _dot_nt(ik_hi, slab)
            s = s + iw_t[2 * j:2 * j + 1, :] * jnp.maximum(d0, 0.0)
            s = s + iw_t[2 * j + 1:2 * j + 2, :] * jnp.maximum(d1, 0.0)
        s_ref[pl.ds(base, kb_sz), :] = jnp.where(base + key_off < limit, s, -jnp.inf)
        return carry

    lax.fori_loop(0, nkb, score_body, 0)

    def count(pred):
        def body(kb, acc):
            base = pl.multiple_of(kb * kb_sz, kb_sz)
            hit = pred(s_ref[pl.ds(base, kb_sz), :], base + key_off)
            return acc + _tree_sum([hit[r:r + 8] for r in range(0, kb_sz, 8)])
        acc = lax.fori_loop(0, nkb, body, jnp.zeros((8, tq), F32))
        return jnp.sum(acc, axis=0, keepdims=True)

    def bisect_body(i, t):
        cand = t + lax.shift_left(jnp.int32(1), 31 - i)
        thr_c = _key_to_float(cand)
        c = count(lambda blk, _: jnp.where(blk >= thr_c, 1.0, 0.0))
        return jnp.where(c >= n_top, cand, t)

    t_key = lax.fori_loop(0, 32, bisect_body, jnp.full((1, tq), INT_MIN, jnp.int32))
    thr = _key_to_float(t_key)

    need = n_top - count(lambda blk, _: jnp.where(blk > thr, 1.0, 0.0))
    n_eq = count(lambda blk, _: jnp.where(blk == thr, 1.0, 0.0))
    take_all = limit <= n_top
    excess = jnp.where(take_all, 0.0, jnp.where(n_eq > need, 1.0, 0.0))
    any_excess = jnp.max(excess) > 0.0
    idx_bits = 14

    def tie_body(i, c):
        cand = c + lax.shift_left(jnp.int32(1), idx_bits - 1 - i)
        f = count(lambda blk, kidx: jnp.where(blk == thr, jnp.where(kidx < cand, 1.0, 0.0), 0.0))
        return jnp.where(f <= need, cand, c)

    cut0 = jnp.where(any_excess, jnp.zeros((1, tq), jnp.int32),
                     jnp.full((1, tq), 2 ** idx_bits, jnp.int32))
    cut = lax.fori_loop(0, jnp.where(any_excess, idx_bits, 0), tie_body, cut0)
    thr = jnp.where(take_all, -jnp.inf, thr)
    cut = jnp.where(take_all, limit, cut)

    def mask_body(kb, carry):
        base = pl.multiple_of(kb * kb_sz, kb_sz)
        blk = s_ref[pl.ds(base, kb_sz), :]
        tie = jnp.where(base + key_off < cut, 0.0, NEG_BIG)
        sel = jnp.where(blk > thr, 0.0, jnp.where(blk == thr, tie, NEG_BIG))
        sel_ref[pl.ds(base, kb_sz), :] = sel.astype(BF16)
        return carry

    lax.fori_loop(0, nkb, mask_body, 0)

    cm_ref[...] = jnp.full(cm_ref.shape, NEG_BIG, BF16)

    def logit_body(kb, carry):
        base = pl.multiple_of(kb * kb_sz, kb_sz)
        dsel = jnp.clip((q0 - base) // kb_sz, 0, 2)
        kblk = k_ref[0, pl.ds(base, kb_sz), :]
        sel = sel_ref[pl.ds(base, kb_sz), :]
        for h in range(A_HEADS):
            sl = h // 4
            lg = _dot_nt(kblk[:, sl * LANES:(sl + 1) * LANES], qx_ref[0, :, h * LANES:(h + 1) * LANES])
            lg = lg.astype(BF16) + sel + bias_ref[dsel, h]
            lg_ref[h, pl.ds(base, kb_sz), :] = lg
            cm_ref[h] = jnp.maximum(cm_ref[h], lg)
        return carry

    lax.fori_loop(0, nkb, logit_body, 0)

    acc_ref[...] = jnp.zeros(acc_ref.shape, F32)
    m_rows = [jnp.max(cm_ref[h].astype(F32), axis=0, keepdims=True).astype(BF16) for h in range(A_HEADS)]

    def pv_body(kb, carry):
        base = pl.multiple_of(kb * kb_sz, kb_sz)
        for h in range(A_HEADS):
            p = jnp.exp2(lg_ref[h, pl.ds(base, kb_sz), :] - m_rows[h])
            acc_ref[h] += jnp.dot(vt_ref[0, kb, h // 4], p, preferred_element_type=F32)
        return carry

    lax.fori_loop(0, nkb, pv_body, 0)

    outs = []
    for h in range(A_HEADS):
        pos = (h // 2) % 2
        outs.append(acc_ref[h, pos * A_HEAD_DIM:(pos + 1) * A_HEAD_DIM, :] / acc_ref[h, LANES:LANES + 1, :])
    o_ref[0] = jnp.concatenate(outs, axis=0).T.astype(BF16)


def _attn_call(rel_bias, qx, iq, misc, kb, vt, ikx, *, tq, l_true, offset, n_top):
    b, tq_total, _ = qx.shape
    lp = kb.shape[1]
    nq = tq_total // tq
    assert offset % ATT_KB == 0 and tq % CHUNK == 0 and ATT_KB % tq == 0 and lp % ATT_KB == 0
    assert lp < 2 ** 14 and l_true <= lp
    kern = functools.partial(_attn_kernel, tq=tq, l_true=l_true, offset=offset, n_top=n_top)
    return pl.pallas_call(
        kern,
        out_shape=jax.ShapeDtypeStruct((b, tq_total, A_WIDTH), BF16),
        grid=(b, nq),
        in_specs=[pl.BlockSpec(memory_space=pltpu.SMEM),
                  pl.BlockSpec((1, tq, QX_W), lambda i, j: (i, j, 0)),
                  pl.BlockSpec((1, tq, IDX_HEADS * IDX_DIM), lambda i, j: (i, j, 0)),
                  pl.BlockSpec((1, tq, LANES), lambda i, j: (i, j, 0)),
                  pl.BlockSpec((1, lp, 256), lambda i, j: (i, 0, 0)),
                  pl.BlockSpec((1, lp // ATT_KB, 2, V_ROWS, ATT_KB), lambda i, j: (i, 0, 0, 0, 0)),
                  pl.BlockSpec((1, lp, 2 * LANES), lambda i, j: (i, 0, 0))],
        out_specs=pl.BlockSpec((1, tq, A_WIDTH), lambda i, j: (i, j, 0)),
        scratch_shapes=[pltpu.VMEM((lp, tq), F32),
                        pltpu.VMEM((lp, tq), BF16),
                        pltpu.VMEM((3, A_HEADS, ATT_KB, tq), BF16),
                        pltpu.VMEM((A_HEADS, lp, tq), BF16),
                        pltpu.VMEM((A_HEADS, ATT_KB, tq), BF16),
                        pltpu.VMEM((A_HEADS, V_ROWS, tq), F32)],
        compiler_params=_cparams(2),
        name="sparse_attn",
    )(rel_bias, qx, iq, misc, kb, vt, ikx)


def _split_bf16(a, n):
    parts = []
    r = a
    for i in range(n):
        p = r.astype(BF16)
        parts.append(p)
        if i + 1 < n:
            r = r - p.astype(F32)
    return parts


def _bdot(a, b, dims=(((1,), (0,)), ((), ()))):
    return lax.dot_general(a.astype(BF16), b.astype(BF16), dims, preferred_element_type=F32)


_NT = (((1,), (1,)), ((), ()))
_TN = (((0,), (0,)), ((), ()))


def _gdn_kernel(x_ref, misc_ref, gz_ref, convw_ref, hist_ref, s0_ref, gnw_ref, alane_ref, dlane_ref,
                o_ref, conv_o, s_o, xp_ref, y_ref, st_ref, vn_ref,
                u_ref, w_ref, qk_ref, qe_ref, kd_ref, el_ref, gate_ref, *, cb, t_true, nblk):
    c = CHUNK
    step = pl.program_id(1)
    blk = jnp.minimum(step, nblk - 1)
    carried = (u_ref, w_ref, qk_ref, qe_ref, kd_ref, el_ref, gate_ref)
    u_nx, w_nx, qk_nx, qe_nx, kd_nx, el_nx, gate_nx = (r.at[1] for r in carried)
    u_ref, w_ref, qk_ref, qe_ref, kd_ref, el_ref, gate_ref = (r.at[0] for r in carried)

    @pl.when(step == 0)
    def _():
        xp_ref[8 - (CONV_W - 1):8, :] = hist_ref[0]
        st_ref[...] = s0_ref[0]
        for r in carried:
            r[...] = jnp.zeros(r.shape, r.dtype)

    heads = range(G_HEADS)
    for r in carried:
        r[0] = r[1]

    def scan_stages():
        vn_ref[...] = jnp.zeros(vn_ref.shape, F32)
        for ck in range(cb // c):
            r0, r1 = ck * c, (ck + 1) * c
            s_prev = [st_ref[h] for h in heads]
            v_new = [u_ref[h, r0:r1, :] - _bdot(w_ref[h, r0:r1, :], s_prev[h]) for h in heads]
            o_state = [_bdot(qe_ref[h, r0:r1, :], s_prev[h]) for h in heads]
            yield
            for h in heads:
                vn_ref[h, r0:r1, :] = v_new[h]
            o = [o_state[h] + _bdot(qk_ref[h, r0:r1, :], vn_ref[h]) for h in heads]
            for h in heads:
                s_new = s_prev[h] * el_ref[h, ck, 0:1, :] + _bdot(kd_ref[h, r0:r1, :], v_new[h], _TN)
                st_ref[h] = jnp.where(step > 0, s_new, s_prev[h])
            yield
            for h in heads:
                on = o[h] * lax.rsqrt(jnp.mean(o[h] * o[h], axis=-1, keepdims=True) + EPS) * gnw_ref[...]
                gate = gate_ref[r0:r1, h * G_VAL_DIM:(h + 1) * G_VAL_DIM]
                o_ref[0, r0:r1, h * G_VAL_DIM:(h + 1) * G_VAL_DIM] = (on * gate).astype(BF16)
            yield
        s_o[0] = st_ref[...]

    scan = scan_stages()

    def tick():
        next(scan, None)

    tick()

    xp_ref[8:8 + cb, :] = x_ref[0]
    y = convw_ref[CONV_W - 1:CONV_W, :] * xp_ref[8:8 + cb, :]
    for j in range(CONV_W - 1):
        y = y + convw_ref[j:j + 1, :] * xp_ref[5 + j:5 + j + cb, :]
    y_ref[...] = _silu(y)

    last_row = (t_true - 1) % cb
    conv_o[0] = xp_ref[8 + last_row - (CONV_W - 2):8 + last_row + 1, :]

    xp_ref[8 - (CONV_W - 1):8, :] = xp_ref[8 + cb - (CONV_W - 1):8 + cb, :]

    ri = lax.broadcasted_iota(jnp.int32, (cb, cb), 0)
    ci = lax.broadcasted_iota(jnp.int32, (cb, cb), 1)
    lag = jnp.where((ri // c) == (ci // c), ri - ci, -1)
    tri = lag >= 0
    strict = lag > 0
    eye = ri == ci
    tril_bf = jnp.where(tri, 1.0, 0.0).astype(BF16)
    eye_f = jnp.where(eye, 1.0, 0.0)

    ms = misc_ref[0]
    tok = blk * cb + lax.broadcasted_iota(jnp.int32, (cb, 1), 0)
    live = tok < t_true
    beta_s = jnp.where(live, jax.nn.sigmoid(ms), 0.0)
    z = ms + dlane_ref[...]
    softplus = jnp.maximum(z, 0.0) + jnp.log(1.0 + jnp.exp(-jnp.abs(z)))
    g_s = jnp.where(live, -jnp.exp(alane_ref[...]) * softplus, 0.0)
    gc_s = None
    for piece in _split_bf16(g_s, 3):
        t = jnp.dot(tril_bf, piece, preferred_element_type=F32)
        gc_s = t if gc_s is None else gc_s + t

    tick()

    q, k, v, beta, gc, decay, kb, eg = ([None] * G_HEADS for _ in range(8))
    for h in heads:
        qh = y_ref[:, h * G_KEY_DIM:(h + 1) * G_KEY_DIM]
        kh = y_ref[:, G_WIDTH + h * G_KEY_DIM:G_WIDTH + (h + 1) * G_KEY_DIM]
        v[h] = y_ref[:, 2 * G_WIDTH + h * G_VAL_DIM:2 * G_WIDTH + (h + 1) * G_VAL_DIM]
        q[h] = qh * lax.rsqrt(jnp.sum(qh * qh, axis=-1, keepdims=True) + 1e-6) * (G_KEY_DIM ** -0.5)
        k[h] = kh * lax.rsqrt(jnp.sum(kh * kh, axis=-1, keepdims=True) + 1e-6)
        beta[h] = beta_s[:, MISC_GB + h:MISC_GB + h + 1]
        gc[h] = gc_s[:, MISC_GA + h:MISC_GA + h + 1]
        gc_b = jnp.broadcast_to(gc[h], (cb, cb))
        gc_row = jnp.sum(jnp.where(eye, gc_b, 0.0), axis=0, keepdims=True)
        decay[h] = jnp.exp(jnp.where(tri, gc_b - gc_row, NEG_BIG))
        kb[h] = k[h] * beta[h]
        eg[h] = jnp.exp(gc[h])
    tick()
    m = [jnp.where(strict, _bdot(kb[h], k[h], _NT) * decay[h], 0.0) for h in heads]
    rhs = [jnp.concatenate([v[h] * beta[h], kb[h] * eg[h]], axis=1) for h in heads]
    qk = [jnp.where(tri, _bdot(q[h], k[h], _NT) * decay[h], 0.0) for h in heads]
    tick()
    pw = [-m[h] for h in heads]
    inv = [eye_f + pw[h] for h in heads]
    for _ in range(5):
        pw = [_bdot(pw[h], pw[h]) for h in heads]
        tick()
        inv = [inv[h] + _bdot(inv[h], pw[h]) for h in heads]
    tick()
    sol = [_bdot(inv[h], rhs[h]) for h in heads]
    tick()
    m_sol = []
    for h in heads:
        m_hi, m_lo = _split_bf16(m[h], 2)
        s_hi, s_lo = _split_bf16(sol[h], 2)
        m_sol.append(_bdot(m_hi, s_hi) + _bdot(m_hi, s_lo) + _bdot(m_lo, s_hi))
    tick()
    sol = [sol[h] + _bdot(inv[h], rhs[h] - sol[h] - m_sol[h]) for h in heads]
    for _ in scan:
        pass
    gate_nx[...] = _silu(gz_ref[0])
    for h in heads:
        u_nx[h] = sol[h][:, :G_VAL_DIM]
        w_nx[h] = sol[h][:, G_VAL_DIM:].astype(BF16)
        qk_nx[h] = qk[h].astype(BF16)
        qe_nx[h] = (q[h] * eg[h]).astype(BF16)
        for ck in range(cb // c):
            r0, r1 = ck * c, (ck + 1) * c
            g_last = gc[h][r1 - 1:r1, :]
            kd_nx[h, r0:r1, :] = (k[h][r0:r1] * jnp.exp(g_last - gc[h][r0:r1])).astype(BF16)
            el_nx[h, ck] = jnp.broadcast_to(jnp.exp(g_last), (8, LANES))


def _gdn_call(gqkv, misc, gz, conv_w, hist, s0, gnorm_w, a_lane, d_lane, *, cb, t_true):
    b, t_pad, _ = gqkv.shape
    nblk = t_pad // cb
    assert (nblk - 1) * cb < t_true <= t_pad and (t_true - 1) % cb >= CONV_W - 2
    kern = functools.partial(_gdn_kernel, cb=cb, t_true=t_true, nblk=nblk)
    prep = lambda i, j: (i, jnp.minimum(j, nblk - 1), 0)
    scan = lambda i, j: (i, jnp.maximum(j - 1, 0), 0)
    nch = cb // CHUNK
    return pl.pallas_call(
        kern,
        out_shape=[jax.ShapeDtypeStruct((b, t_pad, G_WIDTH), BF16),
                   jax.ShapeDtypeStruct((b, CONV_W - 1, CONV_CH), F32),
                   jax.ShapeDtypeStruct((b, G_HEADS, G_KEY_DIM, G_VAL_DIM), F32)],
        grid=(b, nblk + 1),
        in_specs=[pl.BlockSpec((1, cb, CONV_CH), prep),
                  pl.BlockSpec((1, cb, LANES), prep),
                  pl.BlockSpec((1, cb, G_WIDTH), prep),
                  pl.BlockSpec((CONV_W, CONV_CH), lambda i, j: (0, 0)),
                  pl.BlockSpec((1, CONV_W - 1, CONV_CH), lambda i, j: (i, 0, 0)),
                  pl.BlockSpec((1, G_HEADS, G_KEY_DIM, G_VAL_DIM), lambda i, j: (i, 0, 0, 0)),
                  pl.BlockSpec((1, G_VAL_DIM), lambda i, j: (0, 0)),
                  pl.BlockSpec((1, LANES), lambda i, j: (0, 0)),
                  pl.BlockSpec((1, LANES), lambda i, j: (0, 0))],
        out_specs=[pl.BlockSpec((1, cb, G_WIDTH), scan),
                   pl.BlockSpec((1, CONV_W - 1, CONV_CH), lambda i, j: (i, 0, 0)),
                   pl.BlockSpec((1, G_HEADS, G_KEY_DIM, G_VAL_DIM), lambda i, j: (i, 0, 0, 0))],
        scratch_shapes=[pltpu.VMEM((cb + 8, CONV_CH), F32),
                        pltpu.VMEM((cb, CONV_CH), F32),
                        pltpu.VMEM((G_HEADS, G_KEY_DIM, G_VAL_DIM), F32),
                        pltpu.VMEM((G_HEADS, cb, G_VAL_DIM), F32),
                        pltpu.VMEM((2, G_HEADS, cb, G_VAL_DIM), F32),
                        pltpu.VMEM((2, G_HEADS, cb, G_KEY_DIM), BF16),
                        pltpu.VMEM((2, G_HEADS, cb, cb), BF16),
                        pltpu.VMEM((2, G_HEADS, cb, G_KEY_DIM), BF16),
                        pltpu.VMEM((2, G_HEADS, cb, G_KEY_DIM), BF16),
                        pltpu.VMEM((2, G_HEADS, nch, 8, LANES), F32),
                        pltpu.VMEM((2, cb, G_WIDTH), F32)],
        compiler_params=_cparams(2),
        name="gated_delta",
    )(gqkv, misc, gz, conv_w, hist, s0, gnorm_w, a_lane, d_lane)


def _pack_w_in(w_in):
    d = w_in.shape[0]
    splits = (512, 256, 256, 512, 64, 8, 512, 512, 512, 512, 4, 4)
    offs = np.concatenate([[0], np.cumsum(splits)])
    aq, ak, av, iq, ik, iw, gq, gk, gv, gz, gb, ga = [w_in[:, offs[i]:offs[i + 1]] for i in range(12)]
    zeros64 = jnp.zeros((d, A_HEAD_DIM), w_in.dtype)
    qx = []
    for h in range(A_HEADS):
        qh = aq[:, h * A_HEAD_DIM:(h + 1) * A_HEAD_DIM] * (A_HEAD_DIM ** -0.5 * LOG2E)
        qx += [qh, zeros64] if (h // 2) % 2 == 0 else [zeros64, qh]
    ikx = [ik, zeros64, zeros64, ik]
    misc = [ik, iw, gb, ga, jnp.zeros((d, LANES - MISC_GA - G_HEADS), w_in.dtype)]
    cols = qx + [ak, av, iq * (IDX_DIM ** -0.5)] + ikx + misc + [gq, gk, gv, gz]
    packed = jnp.concatenate(cols, axis=1).astype(BF16)
    assert packed.shape[1] == IN_PACKED
    return packed


def _pack_ffn(w_gate, w_up, w_down):
    d = w_gate.shape[0]
    wg = w_gate.reshape(d, N_FF_CHUNKS, FF_CHUNK)
    wu = w_up.reshape(d, N_FF_CHUNKS, FF_CHUNK)
    wgu = jnp.transpose(jnp.concatenate([wg, wu], axis=2), (1, 0, 2)).astype(BF16)
    wd = w_down.reshape(N_FF_CHUNKS, FF_CHUNK, d).astype(BF16)
    return wgu, wd


def _ikx_layout(ik):
    z = jnp.zeros_like(ik)
    return jnp.concatenate([ik, z, z, ik], axis=-1)


def _lane_vec(vals, lane0):
    return jnp.zeros((1, LANES), F32).at[0, lane0:lane0 + vals.shape[0]].set(vals.astype(F32))


def _run(x, mod, past, layers, rel_bias, norm_final):
    b, t, d = x.shape
    n = b * t
    per_token = t < TOKEN_TILE
    tm = n if per_token else TOKEN_TILE
    tiles_per_batch = None if per_token else t // tm
    tok = dict(tm=tm, tiles_per_batch=tiles_per_batch, per_token=per_token)

    x2 = x.reshape(n, d)
    states = []
    for li, lw in enumerate(layers):
        m = mod[li]

        def mvec(kidx):
            row = m[:, kidx]
            return jnp.repeat(row, t, axis=0) if per_token else row[:, None, :]

        sh1, sc1, gt1, sh2, sc2, gt2, sh3, sc3, gt3 = [mvec(i) for i in range(N_MOD)]
        x2 = _ffn_call(x2, sh1, sc1, gt1, lw["norm_ffn1"], lw["wgu1"], lw["wd1"], None, **tok)
        qx, k, kb, v, vb, iq, ikx, misc, gqkv, gz = _inproj_call(
            x2, sh2, sc2, lw["norm_mix"], lw["w_in"], **tok)

        if past is None:
            offset, l_true, tq = 0, t, ATT_TQ
            k_all, v_all, ikx_all = (a.reshape(b, t, -1) for a in (kb, vb, ikx))
            qx3, iq3, misc3 = (a.reshape(b, t, -1) for a in (qx, iq, misc))
            conv_hist = jnp.zeros((b, CONV_W - 1, CONV_CH), F32)
            s0 = jnp.zeros((b, G_HEADS, G_KEY_DIM, G_VAL_DIM), F32)
        else:
            k_hist, v_hist, ik_hist, conv_hist, s0 = (p[li] for p in past)
            offset = k_hist.shape[1]
            l_true = offset + t
            tq = LANES
            lp = -(-l_true // ATT_KB) * ATT_KB
            pad_k = lambda a: jnp.pad(a, ((0, 0), (0, lp - l_true), (0, 0)))
            k_all = pad_k(jnp.concatenate([k_hist.reshape(b, offset, -1).astype(BF16),
                                           kb.reshape(b, t, -1)], axis=1))
            v_all = pad_k(jnp.concatenate([v_hist.reshape(b, offset, -1).astype(BF16),
                                           vb.reshape(b, t, -1)], axis=1))
            ikx_all = pad_k(jnp.concatenate([_ikx_layout(ik_hist.astype(BF16)),
                                             ikx.reshape(b, t, -1)], axis=1))
            pad_q = lambda a: jnp.pad(a.reshape(b, t, -1), ((0, 0), (0, tq - t), (0, 0)))
            qx3, iq3, misc3 = pad_q(qx), pad_q(iq), pad_q(misc)
            s0 = s0.astype(F32)
            conv_hist = conv_hist.astype(F32)
        lp = k_all.shape[1]
        n_top = min(TOPK_MAX, l_true // 4)
        vt = jnp.transpose(v_all.reshape(b, lp // ATT_KB, ATT_KB, 2, LANES), (0, 1, 3, 4, 2))
        vt = jnp.concatenate([vt, jnp.ones((b, lp // ATT_KB, 2, V_ROWS - LANES, ATT_KB), BF16)], axis=3)
        attn = _attn_call(rel_bias, qx3, iq3, misc3, k_all, vt, ikx_all,
                          tq=tq, l_true=l_true, offset=offset, n_top=n_top)
        attn = attn[:, :t].reshape(n, A_WIDTH)

        cb = GDN_BLOCK if t % GDN_BLOCK == 0 else CHUNK
        t_pad = -(-t // cb) * cb
        pad_t = lambda a: jnp.pad(a.reshape(b, t, -1), ((0, 0), (0, t_pad - t), (0, 0)))
        gdn, conv_new, s_new = _gdn_call(pad_t(gqkv), pad_t(misc), pad_t(gz), lw["conv_w"], conv_hist, s0,
                                         lw["gnorm_w"], lw["a_lane"], lw["d_lane"], cb=cb, t_true=t)
        gdn = gdn[:, :t].reshape(n, G_WIDTH)

        x2 = _outproj_call(x2, gt2, attn, gdn, lw["wo_a"], lw["wo_g"], **tok)
        nf = norm_final if li == len(layers) - 1 else None
        x2 = _ffn_call(x2, sh3, sc3, gt3, lw["norm_ffn2"], lw["wgu2"], lw["wd2"], nf, **tok)

        states.append((k.reshape(b, t, A_KV_HEADS, A_HEAD_DIM), v.reshape(b, t, A_KV_HEADS, A_HEAD_DIM),
                       misc.reshape(b, t, LANES)[..., :IDX_DIM], conv_new, s_new))
    stacked = [jnp.stack(s, axis=0) for s in zip(*states)]
    return x2.reshape(b, t, d), stacked


def kernel(x_prompt, x_sample, cache_k, cache_v, cache_idx_k, state_conv, state_delta, c_prompt, c_sample,
           w_mod, b_mod, norm_ffn1, norm_mix, norm_ffn2, ffn1_w_gate, ffn1_w_up, ffn1_w_down,
           ffn2_w_gate, ffn2_w_up, ffn2_w_down, w_in, w_out, rel_bias, conv_w, a_log, dt_bias, gnorm_w,
           norm_final):
    depth = w_mod.shape[0]
    bp = c_prompt.shape[0]
    c_all = jnp.concatenate([c_prompt, c_sample], axis=0)
    layers, mods_p, mods_s = [], [], []
    for l in range(depth):
        mod = _mod_call(c_all, w_mod[l], b_mod[l]).reshape(c_all.shape[0], N_MOD, D_MODEL)
        mods_p.append(mod[:bp])
        mods_s.append(mod[bp:])
        wgu1, wd1 = _pack_ffn(ffn1_w_gate[l], ffn1_w_up[l], ffn1_w_down[l])
        wgu2, wd2 = _pack_ffn(ffn2_w_gate[l], ffn2_w_up[l], ffn2_w_down[l])
        wo = w_out[l].astype(BF16)
        layers.append(dict(
            norm_ffn1=norm_ffn1[l].reshape(1, -1), norm_mix=norm_mix[l].reshape(1, -1),
            norm_ffn2=norm_ffn2[l].reshape(1, -1), wgu1=wgu1, wd1=wd1, wgu2=wgu2, wd2=wd2,
            w_in=_pack_w_in(w_in[l]), wo_a=wo[:A_WIDTH], wo_g=wo[A_WIDTH:],
            conv_w=conv_w[l], gnorm_w=gnorm_w[l].reshape(1, -1),
            a_lane=_lane_vec(a_log[l], MISC_GA), d_lane=_lane_vec(dt_bias[l], MISC_GA)))
    nf = norm_final.reshape(1, -1)
    y_p, (k_p, v_p, ik_p, conv_p, delta_p) = _run(x_prompt, mods_p, None, layers, rel_bias, nf)
    past = (cache_k, cache_v, cache_idx_k, state_conv, state_delta)
    y_s, (k_s, v_s, ik_s, conv_s, delta_s) = _run(x_sample, mods_s, past, layers, rel_bias, nf)
    return (y_p, y_s, k_p, v_p, ik_p, conv_p, delta_p, k_s, v_s, ik_s, conv_s, delta_s)
```

```python
import functools

import jax
import jax.numpy as jnp
import numpy as np
from jax import lax
from jax.experimental import pallas as pl
from jax.experimental.pallas import tpu as pltpu

F32 = jnp.float32
BF16 = jnp.bfloat16

D_MODEL = 1024
CHUNK = 64
A_HEAD_DIM = 64
A_HEADS = 8
A_KV_HEADS = 4
A_WIDTH = A_HEADS * A_HEAD_DIM
IDX_HEADS = 8
IDX_DIM = 64
TOPK_MAX = 256
REL_BUCKETS = 32
G_KEY_DIM = 128
G_VAL_DIM = 128
G_HEADS = 4
G_WIDTH = G_HEADS * G_VAL_DIM
CONV_W = 4
CONV_CH = 2 * G_HEADS * G_KEY_DIM + G_HEADS * G_VAL_DIM
D_FF = 2816
N_MOD = 9
EPS = 1e-6

LANES = 128
MXU_DIM = 256
VMEM_LIMIT_BYTES = 56 * 1024 * 1024

FF_CHUNK = MXU_DIM
N_FF_CHUNKS = D_FF // FF_CHUNK
TOKEN_TILE = 512
ATT_TQ = 256
ATT_KB = 256
GDN_BLOCK = 256

QX_W = A_HEADS * LANES
OFF_QX = 0
OFF_K = OFF_QX + QX_W
OFF_V = OFF_K + A_KV_HEADS * A_HEAD_DIM
OFF_IQ = OFF_V + A_KV_HEADS * A_HEAD_DIM
OFF_IKX = OFF_IQ + IDX_HEADS * IDX_DIM
OFF_MISC = OFF_IKX + 2 * LANES
OFF_GQKV = OFF_MISC + LANES
OFF_GZ = OFF_GQKV + CONV_CH
IN_PACKED = OFF_GZ + G_WIDTH
MISC_IW = IDX_DIM
MISC_GB = MISC_IW + IDX_HEADS
MISC_GA = MISC_GB + G_HEADS

NEG_BIG = -1e30
INT_MIN = -2 ** 31
LOG2E = 1.4426950408889634
V_ROWS = LANES + 16


def _cparams(n_axes):
    return pltpu.CompilerParams(dimension_semantics=("arbitrary",) * n_axes,
                                vmem_limit_bytes=VMEM_LIMIT_BYTES)


def _resident(shape):
    nd = len(shape)
    return pl.BlockSpec(shape, lambda *_: (0,) * nd, pipeline_mode=pl.Buffered(1))


def _dot_nt(a, b):
    return lax.dot_general(a, b, (((1,), (1,)), ((), ())), preferred_element_type=F32)


def _rms_mod(x, gain, shift, scale):
    ms = jnp.mean(x * x, axis=-1, keepdims=True)
    y = x * lax.rsqrt(ms + EPS) * gain
    return y * (1.0 + scale) + shift


def _silu(x):
    return x * jax.nn.sigmoid(x)


def _tree_sum(parts):
    while len(parts) > 1:
        parts = [a + b for a, b in zip(parts[0::2], parts[1::2])] + ([parts[-1]] if len(parts) % 2 else [])
    return parts[0]


def _mod_kernel(c_ref, w_ref, b_ref, o_ref):
    s = _silu(c_ref[...]).astype(BF16)
    o_ref[...] = jnp.dot(s, w_ref[...].astype(BF16), preferred_element_type=F32) + b_ref[...]


def _mod_call(c, w_mod, b_mod):
    rows, d = c.shape
    n = w_mod.shape[1]
    tn = D_MODEL
    return pl.pallas_call(
        _mod_kernel,
        out_shape=jax.ShapeDtypeStruct((rows, n), F32),
        grid=(n // tn,),
        in_specs=[pl.BlockSpec((rows, d), lambda j: (0, 0)),
                  pl.BlockSpec((d, tn), lambda j: (0, j)),
                  pl.BlockSpec((1, tn), lambda j: (0, j))],
        out_specs=pl.BlockSpec((rows, tn), lambda j: (0, j)),
        compiler_params=_cparams(1),
        name="mod",
    )(c, w_mod, b_mod.reshape(1, n))


def _mod_specs(per_token, tm, tiles_per_batch):
    if per_token:
        return pl.BlockSpec((tm, D_MODEL), lambda i: (i, 0))
    return pl.BlockSpec((None, 1, D_MODEL), lambda i: (i // tiles_per_batch, 0, 0))


def _ffn_kernel(x_ref, sh_ref, sc_ref, gt_ref, gain_ref, wgu_ref, wd_ref, *rest, final_norm):
    if final_norm:
        nf_ref, o_ref, acc_ref = rest
    else:
        o_ref, acc_ref = rest
    x = x_ref[...]
    h = _rms_mod(x, gain_ref[...], sh_ref[...], sc_ref[...]).astype(BF16)
    for j in range(N_FF_CHUNKS):
        ab = jnp.dot(h, wgu_ref[j], preferred_element_type=F32)
        g = (_silu(ab[:, :FF_CHUNK]) * ab[:, FF_CHUNK:]).astype(BF16)
        d = jnp.dot(g, wd_ref[j], preferred_element_type=F32)
        if j == 0:
            acc_ref[...] = d
        else:
            acc_ref[...] += d
    y = x + 0.5 * gt_ref[...] * acc_ref[...]
    if final_norm:
        ms = jnp.mean(y * y, axis=-1, keepdims=True)
        y = y * lax.rsqrt(ms + EPS) * nf_ref[...]
    o_ref[...] = y


def _ffn_call(x2, sh, sc, gt, gain, wgu, wd, norm_final, *, tm, tiles_per_batch, per_token):
    n = x2.shape[0]
    mspec = _mod_specs(per_token, tm, tiles_per_batch)
    in_specs = [pl.BlockSpec((tm, D_MODEL), lambda i: (i, 0)), mspec, mspec, mspec,
                _resident((1, D_MODEL)), _resident(wgu.shape), _resident(wd.shape)]
    args = [x2, sh, sc, gt, gain, wgu, wd]
    final_norm = norm_final is not None
    if final_norm:
        in_specs.append(_resident((1, D_MODEL)))
        args.append(norm_final)
    return pl.pallas_call(
        functools.partial(_ffn_kernel, final_norm=final_norm),
        out_shape=jax.ShapeDtypeStruct((n, D_MODEL), F32),
        grid=(n // tm,),
        in_specs=in_specs,
        out_specs=pl.BlockSpec((tm, D_MODEL), lambda i: (i, 0)),
        scratch_shapes=[pltpu.VMEM((tm, D_MODEL), F32)],
        compiler_params=_cparams(1),
        name="ffn_final" if final_norm else "ffn",
    )(*args)


def _inproj_kernel(x_ref, sh_ref, sc_ref, gain_ref, w_ref,
                   qx_o, k_o, kb_o, v_o, vb_o, iq_o, ikx_o, misc_o, gqkv_o, gz_o):
    h = _rms_mod(x_ref[...], gain_ref[...], sh_ref[...], sc_ref[...]).astype(BF16)

    def mm(off, width):
        return jnp.dot(h, w_ref[:, off:off + width], preferred_element_type=F32)

    qx_o[...] = mm(OFF_QX, QX_W).astype(BF16)
    k = mm(OFF_K, OFF_V - OFF_K)
    k_o[...] = k
    kb_o[...] = k.astype(BF16)
    v = mm(OFF_V, OFF_IQ - OFF_V)
    v_o[...] = v
    vb_o[...] = v.astype(BF16)
    iq_o[...] = mm(OFF_IQ, OFF_IKX - OFF_IQ).astype(BF16)
    ikx_o[...] = mm(OFF_IKX, OFF_MISC - OFF_IKX).astype(BF16)
    misc_o[...] = mm(OFF_MISC, LANES)
    gqkv_o[...] = mm(OFF_GQKV, CONV_CH)
    gz_o[...] = mm(OFF_GZ, G_WIDTH)


def _inproj_call(x2, sh, sc, gain, w_packed, *, tm, tiles_per_batch, per_token):
    n = x2.shape[0]
    mspec = _mod_specs(per_token, tm, tiles_per_batch)
    widths = [(QX_W, BF16), (256, F32), (256, BF16), (256, F32), (256, BF16), (512, BF16),
              (2 * LANES, BF16), (LANES, F32), (CONV_CH, F32), (G_WIDTH, F32)]
    return pl.pallas_call(
        _inproj_kernel,
        out_shape=[jax.ShapeDtypeStruct((n, w), dt) for w, dt in widths],
        grid=(n // tm,),
        in_specs=[pl.BlockSpec((tm, D_MODEL), lambda i: (i, 0)), mspec, mspec,
                  _resident((1, D_MODEL)), _resident(w_packed.shape)],
        out_specs=[pl.BlockSpec((tm, w), lambda i: (i, 0)) for w, _ in widths],
        compiler_params=_cparams(1),
        name="inproj",
    )(x2, sh, sc, gain, w_packed)


def _outproj_kernel(x_ref, gt_ref, a_ref, g_ref, wa_ref, wg_ref, o_ref):
    m = (jnp.dot(a_ref[...], wa_ref[...], preferred_element_type=F32)
         + jnp.dot(g_ref[...], wg_ref[...], preferred_element_type=F32))
    o_ref[...] = x_ref[...] + gt_ref[...] * m


def _outproj_call(x2, gt, attn, gdn, wa, wg, *, tm, tiles_per_batch, per_token):
    n = x2.shape[0]
    mspec = _mod_specs(per_token, tm, tiles_per_batch)
    return pl.pallas_call(
        _outproj_kernel,
        out_shape=jax.ShapeDtypeStruct((n, D_MODEL), F32),
        grid=(n // tm,),
        in_specs=[pl.BlockSpec((tm, D_MODEL), lambda i: (i, 0)), mspec,
                  pl.BlockSpec((tm, A_WIDTH), lambda i: (i, 0)),
                  pl.BlockSpec((tm, G_WIDTH), lambda i: (i, 0)),
                  _resident(wa.shape), _resident(wg.shape)],
        out_specs=pl.BlockSpec((tm, D_MODEL), lambda i: (i, 0)),
        compiler_params=_cparams(1),
        name="outproj",
    )(x2, gt, attn, gdn, wa, wg)


def _rel_bucket_int(rel):
    n = jnp.abs(rel)
    large = jnp.full(rel.shape, 8, jnp.int32)
    for th in (12, 16, 23, 32, 46, 64, 91):
        large = large + jnp.where(n >= th, 1, 0)
    return jnp.where(rel > 0, REL_BUCKETS // 2, 0) + jnp.where(n < 8, n, large)


def _key_to_float(t):
    bits = jnp.where(t >= 0, t, t ^ jnp.int32(0x7FFFFFFF))
    return lax.bitcast_convert_type(bits, F32)


def _top16(x):
    bits = lax.bitcast_convert_type(x, jnp.int32) & jnp.int32(-65536)
    return lax.bitcast_convert_type(bits, F32).astype(BF16)


def _attn_kernel(relb_ref, qx_ref, iq_ref, misc_ref, k_ref, vt_ref, ikx_ref, o_ref,
                 s_ref, shi_ref, sel_ref, bias_ref, lg_ref, cm_ref, acc_ref, *, tq, l_true, offset, n_top):
    kb_sz = ATT_KB
    first = (pl.program_id(0) == 0) & (pl.program_id(1) == 0)
    q0 = offset + pl.program_id(1) * tq
    kmax = jnp.minimum(q0 + tq, l_true)
    nkb = (kmax + kb_sz - 1) // kb_sz

    key_off = lax.broadcasted_iota(jnp.int32, (kb_sz, tq), 0)
    qry_off = lax.broadcasted_iota(jnp.int32, (kb_sz, tq), 1)

    @pl.when(first)
    def _():
        for d in range(3):
            bucket = _rel_bucket_int(key_off - qry_off - d * kb_sz)
            for h in range(A_HEADS):
                t = jnp.zeros((kb_sz, tq), F32)
                for b in range(REL_BUCKETS):
                    t = jnp.where(bucket == b, relb_ref[b, h], t)
                bias_ref[d, h] = (t * LOG2E).astype(BF16)

    qpos = q0 + lax.broadcasted_iota(jnp.int32, (1, tq), 1)
    limit = jnp.minimum((qpos // CHUNK + 1) * CHUNK, l_true)

    iw_t = misc_ref[0].T[MISC_IW:MISC_IW + IDX_HEADS, :] * (IDX_HEADS ** -0.5)

    def score_body(kb, carry):
        base = pl.multiple_of(kb * kb_sz, kb_sz)
        ik2 = ikx_ref[0, pl.ds(base, kb_sz), :]
        ik_lo = ik2[:, :LANES]
        ik_hi = ik2[:, LANES:]
        s = jnp.zeros((kb_sz, tq), F32)
        for j in range(IDX_HEADS // 2):
            slab = iq_ref[0, :, j * LANES:(j + 1) * LANES]
            d0 = _dot_nt(ik_lo, slab)
            d1 = _dot_nt(ik_hi, slab)
            s = s + iw_t[2 * j:2 * j + 1, :] * jnp.maximum(d0, 0.0)
            s = s + iw_t[2 * j + 1:2 * j + 2, :] * jnp.maximum(d1, 0.0)
        s = jnp.where(base + key_off < limit, s, -jnp.inf)
        s_ref[pl.ds(base, kb_sz), :] = s
        shi_ref[pl.ds(base, kb_sz), :] = _top16(s)
        return carry

    lax.fori_loop(0, nkb, score_body, 0)

    def count(pred):
        def body(kb, acc):
            base = pl.multiple_of(kb * kb_sz, kb_sz)
            hit = pred(s_ref[pl.ds(base, kb_sz), :], base + key_off)
            return acc + _tree_sum([hit[r:r + 8] for r in range(0, kb_sz, 8)])
        acc = lax.fori_loop(0, nkb, body, jnp.zeros((8, tq), F32))
        return jnp.sum(acc, axis=0, keepdims=True)

    def count_hi(thr_hi):
        one = jnp.ones((), BF16)
        zero = jnp.zeros((), BF16)

        def body(kb, acc):
            base = pl.multiple_of(kb * kb_sz, kb_sz)
            hit = jnp.where(shi_ref[pl.ds(base, kb_sz), :] >= thr_hi, one, zero)
            return acc + _tree_sum([hit[r:r + 16] for r in range(0, kb_sz, 16)])
        acc = lax.fori_loop(0, nkb, body, jnp.zeros((16, tq), BF16))
        return jnp.sum(acc.astype(F32), axis=0, keepdims=True)

    def hi_body(i, t):
        cand = t + lax.shift_left(jnp.int32(1), 31 - i)
        c = count_hi(_top16(_key_to_float(cand)))
        return jnp.where(c >= n_top, cand, t)

    t_key = lax.fori_loop(0, 16, hi_body, jnp.full((1, tq), INT_MIN, jnp.int32))

    def lo_body(i, t):
        cand = t + lax.shift_left(jnp.int32(1), 15 - i)
        thr_c = _key_to_float(cand)
        c = count(lambda blk, _: jnp.where(blk >= thr_c, 1.0, 0.0))
        return jnp.where(c >= n_top, cand, t)

    t_key = lax.fori_loop(0, 16, lo_body, t_key)
    thr = _key_to_float(t_key)

    need = n_top - count(lambda blk, _: jnp.where(blk > thr, 1.0, 0.0))
    n_eq = count(lambda blk, _: jnp.where(blk == thr, 1.0, 0.0))
    take_all = limit <= n_top
    excess = jnp.where(take_all, 0.0, jnp.where(n_eq > need, 1.0, 0.0))
    any_excess = jnp.max(excess) > 0.0
    idx_bits = 14

    def tie_body(i, c):
        cand = c + lax.shift_left(jnp.int32(1), idx_bits - 1 - i)
        f = count(lambda blk, kidx: jnp.where(blk == thr, jnp.where(kidx < cand, 1.0, 0.0), 0.0))
        return jnp.where(f <= need, cand, c)

    cut0 = jnp.where(any_excess, jnp.zeros((1, tq), jnp.int32),
                     jnp.full((1, tq), 2 ** idx_bits, jnp.int32))
    cut = lax.fori_loop(0, jnp.where(any_excess, idx_bits, 0), tie_body, cut0)
    thr = jnp.where(take_all, -jnp.inf, thr)
    cut = jnp.where(take_all, limit, cut)

    def mask_body(kb, carry):
        base = pl.multiple_of(kb * kb_sz, kb_sz)
        blk = s_ref[pl.ds(base, kb_sz), :]
        tie = jnp.where(base + key_off < cut, 0.0, NEG_BIG)
        sel = jnp.where(blk > thr, 0.0, jnp.where(blk == thr, tie, NEG_BIG))
        sel_ref[pl.ds(base, kb_sz), :] = sel.astype(BF16)
        return carry

    lax.fori_loop(0, nkb, mask_body, 0)

    cm_ref[...] = jnp.full(cm_ref.shape, NEG_BIG, BF16)

    def logit_body(kb, carry):
        base = pl.multiple_of(kb * kb_sz, kb_sz)
        dsel = jnp.clip((q0 - base) // kb_sz, 0, 2)
        kblk = k_ref[0, pl.ds(base, kb_sz), :]
        sel = sel_ref[pl.ds(base, kb_sz), :]
        for h in range(A_HEADS):
            sl = h // 4
            lg = _dot_nt(kblk[:, sl * LANES:(sl + 1) * LANES], qx_ref[0, :, h * LANES:(h + 1) * LANES])
            lg = lg.astype(BF16) + sel + bias_ref[dsel, h]
            lg_ref[h, pl.ds(base, kb_sz), :] = lg
            cm_ref[h] = jnp.maximum(cm_ref[h], lg)
        return carry

    lax.fori_loop(0, nkb, logit_body, 0)

    acc_ref[...] = jnp.zeros(acc_ref.shape, F32)
    m_rows = [jnp.max(cm_ref[h].astype(F32), axis=0, keepdims=True).astype(BF16) for h in range(A_HEADS)]

    def pv_body(kb, carry):
        base = pl.multiple_of(kb * kb_sz, kb_sz)
        for h in range(A_HEADS):
            p = jnp.exp2(lg_ref[h, pl.ds(base, kb_sz), :] - m_rows[h])
            acc_ref[h] += jnp.dot(vt_ref[0, kb, h // 4], p, preferred_element_type=F32)
        return carry

    lax.fori_loop(0, nkb, pv_body, 0)

    outs = []
    for h in range(A_HEADS):
        pos = (h // 2) % 2
        outs.append(acc_ref[h, pos * A_HEAD_DIM:(pos + 1) * A_HEAD_DIM, :] / acc_ref[h, LANES:LANES + 1, :])
    o_ref[0] = jnp.concatenate(outs, axis=0).T.astype(BF16)


def _attn_call(rel_bias, qx, iq, misc, kb, vt, ikx, *, tq, l_true, offset, n_top):
    b, tq_total, _ = qx.shape
    lp = kb.shape[1]
    nq = tq_total // tq
    assert offset % ATT_KB == 0 and tq % CHUNK == 0 and ATT_KB % tq == 0 and lp % ATT_KB == 0
    assert lp < 2 ** 14 and l_true <= lp
    kern = functools.partial(_attn_kernel, tq=tq, l_true=l_true, offset=offset, n_top=n_top)
    return pl.pallas_call(
        kern,
        out_shape=jax.ShapeDtypeStruct((b, tq_total, A_WIDTH), BF16),
        grid=(b, nq),
        in_specs=[pl.BlockSpec(memory_space=pltpu.SMEM),
                  pl.BlockSpec((1, tq, QX_W), lambda i, j: (i, j, 0)),
                  pl.BlockSpec((1, tq, IDX_HEADS * IDX_DIM), lambda i, j: (i, j, 0)),
                  pl.BlockSpec((1, tq, LANES), lambda i, j: (i, j, 0)),
                  pl.BlockSpec((1, lp, 256), lambda i, j: (i, 0, 0)),
                  pl.BlockSpec((1, lp // ATT_KB, 2, V_ROWS, ATT_KB), lambda i, j: (i, 0, 0, 0, 0)),
                  pl.BlockSpec((1, lp, 2 * LANES), lambda i, j: (i, 0, 0))],
        out_specs=pl.BlockSpec((1, tq, A_WIDTH), lambda i, j: (i, j, 0)),
        scratch_shapes=[pltpu.VMEM((lp, tq), F32),
                        pltpu.VMEM((lp, tq), BF16),
                        pltpu.VMEM((lp, tq), BF16),
                        pltpu.VMEM((3, A_HEADS, ATT_KB, tq), BF16),
                        pltpu.VMEM((A_HEADS, lp, tq), BF16),
                        pltpu.VMEM((A_HEADS, ATT_KB, tq), BF16),
                        pltpu.VMEM((A_HEADS, V_ROWS, tq), F32)],
        compiler_params=_cparams(2),
        name="sparse_attn",
    )(rel_bias, qx, iq, misc, kb, vt, ikx)


def _split_bf16(a, n):
    parts = []
    r = a
    for i in range(n):
        p = r.astype(BF16)
        parts.append(p)
        if i + 1 < n:
            r = r - p.astype(F32)
    return parts


def _bdot(a, b, dims=(((1,), (0,)), ((), ()))):
    return lax.dot_general(a.astype(BF16), b.astype(BF16), dims, preferred_element_type=F32)


_NT = (((1,), (1,)), ((), ()))
_TN = (((0,), (0,)), ((), ()))


def _gdn_kernel(x_ref, misc_ref, gz_ref, convw_ref, hist_ref, s0_ref, gnw_ref, alane_ref, dlane_ref,
                o_ref, conv_o, s_o, xp_ref, y_ref, st_ref, vn_ref,
                u_ref, w_ref, qk_ref, qe_ref, kd_ref, el_ref, gate_ref, *, cb, t_true, nblk):
    c = CHUNK
    step = pl.program_id(1)
    blk = jnp.minimum(step, nblk - 1)
    carried = (u_ref, w_ref, qk_ref, qe_ref, kd_ref, el_ref, gate_ref)
    u_nx, w_nx, qk_nx, qe_nx, kd_nx, el_nx, gate_nx = (r.at[1] for r in carried)
    u_ref, w_ref, qk_ref, qe_ref, kd_ref, el_ref, gate_ref = (r.at[0] for r in carried)

    @pl.when(step == 0)
    def _():
        xp_ref[8 - (CONV_W - 1):8, :] = hist_ref[0]
        st_ref[...] = s0_ref[0]
        for r in carried:
            r[...] = jnp.zeros(r.shape, r.dtype)

    heads = range(G_HEADS)
    for r in carried:
        r[0] = r[1]

    def scan_stages():
        vn_ref[...] = jnp.zeros(vn_ref.shape, F32)
        for ck in range(cb // c):
            r0, r1 = ck * c, (ck + 1) * c
            s_prev = [st_ref[h] for h in heads]
            v_new = [u_ref[h, r0:r1, :] - _bdot(w_ref[h, r0:r1, :], s_prev[h]) for h in heads]
            o_state = [_bdot(qe_ref[h, r0:r1, :], s_prev[h]) for h in heads]
            yield
            for h in heads:
                vn_ref[h, r0:r1, :] = v_new[h]
            o = [o_state[h] + _bdot(qk_ref[h, r0:r1, :], vn_ref[h]) for h in heads]
            for h in heads:
                s_new = s_prev[h] * el_ref[h, ck, 0:1, :] + _bdot(kd_ref[h, r0:r1, :], v_new[h], _TN)
                st_ref[h] = jnp.where(step > 0, s_new, s_prev[h])
            yield
            for h in heads:
                on = o[h] * lax.rsqrt(jnp.mean(o[h] * o[h], axis=-1, keepdims=True) + EPS) * gnw_ref[...]
                gate = gate_ref[r0:r1, h * G_VAL_DIM:(h + 1) * G_VAL_DIM]
                o_ref[0, r0:r1, h * G_VAL_DIM:(h + 1) * G_VAL_DIM] = (on * gate).astype(BF16)
            yield
        s_o[0] = st_ref[...]

    scan = scan_stages()

    def tick():
        next(scan, None)

    tick()

    xp_ref[8:8 + cb, :] = x_ref[0]
    y = convw_ref[CONV_W - 1:CONV_W, :] * xp_ref[8:8 + cb, :]
    for j in range(CONV_W - 1):
        y = y + convw_ref[j:j + 1, :] * xp_ref[5 + j:5 + j + cb, :]
    y_ref[...] = _silu(y)

    last_row = (t_true - 1) % cb
    conv_o[0] = xp_ref[8 + last_row - (CONV_W - 2):8 + last_row + 1, :]

    xp_ref[8 - (CONV_W - 1):8, :] = xp_ref[8 + cb - (CONV_W - 1):8 + cb, :]

    ri = lax.broadcasted_iota(jnp.int32, (cb, cb), 0)
    ci = lax.broadcasted_iota(jnp.int32, (cb, cb), 1)
    lag = jnp.where((ri // c) == (ci // c), ri - ci, -1)
    tri = lag >= 0
    strict = lag > 0
    eye = ri == ci
    tril_bf = jnp.where(tri, 1.0, 0.0).astype(BF16)
    eye_f = jnp.where(eye, 1.0, 0.0)

    ms = misc_ref[0]
    tok = blk * cb + lax.broadcasted_iota(jnp.int32, (cb, 1), 0)
    live = tok < t_true
    beta_s = jnp.where(live, jax.nn.sigmoid(ms), 0.0)
    z = ms + dlane_ref[...]
    softplus = jnp.maximum(z, 0.0) + jnp.log(1.0 + jnp.exp(-jnp.abs(z)))
    g_s = jnp.where(live, -jnp.exp(alane_ref[...]) * softplus, 0.0)
    gc_s = None
    for piece in _split_bf16(g_s, 3):
        t = jnp.dot(tril_bf, piece, preferred_element_type=F32)
        gc_s = t if gc_s is None else gc_s + t

    tick()

    q, k, v, beta, gc, decay, kb, eg = ([None] * G_HEADS for _ in range(8))
    for h in heads:
        qh = y_ref[:, h * G_KEY_DIM:(h + 1) * G_KEY_DIM]
        kh = y_ref[:, G_WIDTH + h * G_KEY_DIM:G_WIDTH + (h + 1) * G_KEY_DIM]
        v[h] = y_ref[:, 2 * G_WIDTH + h * G_VAL_DIM:2 * G_WIDTH + (h + 1) * G_VAL_DIM]
        q[h] = qh * lax.rsqrt(jnp.sum(qh * qh, axis=-1, keepdims=True) + 1e-6) * (G_KEY_DIM ** -0.5)
        k[h] = kh * lax.rsqrt(jnp.sum(kh * kh, axis=-1, keepdims=True) + 1e-6)
        beta[h] = beta_s[:, MISC_GB + h:MISC_GB + h + 1]
        gc[h] = gc_s[:, MISC_GA + h:MISC_GA + h + 1]
        gc_b = jnp.broadcast_to(gc[h], (cb, cb))
        gc_row = jnp.sum(jnp.where(eye, gc_b, 0.0), axis=0, keepdims=True)
        decay[h] = jnp.exp(jnp.where(tri, gc_b - gc_row, NEG_BIG))
        kb[h] = k[h] * beta[h]
        eg[h] = jnp.exp(gc[h])
    tick()
    m = [jnp.where(strict, _bdot(kb[h], k[h], _NT) * decay[h], 0.0) for h in heads]
    rhs = [jnp.concatenate([v[h] * beta[h], kb[h] * eg[h]], axis=1) for h in heads]
    qk = [jnp.where(tri, _bdot(q[h], k[h], _NT) * decay[h], 0.0) for h in heads]
    tick()
    pw = [-m[h] for h in heads]
    inv = [eye_f + pw[h] for h in heads]
    for _ in range(5):
        pw = [_bdot(pw[h], pw[h]) for h in heads]
        tick()
        inv = [inv[h] + _bdot(inv[h], pw[h]) for h in heads]
    tick()
    sol = [_bdot(inv[h], rhs[h]) for h in heads]
    tick()
    m_sol = []
    for h in heads:
        m_hi, m_lo = _split_bf16(m[h], 2)
        s_hi, s_lo = _split_bf16(sol[h], 2)
        m_sol.append(_bdot(m_hi, s_hi) + _bdot(m_hi, s_lo) + _bdot(m_lo, s_hi))
    tick()
    sol = [sol[h] + _bdot(inv[h], rhs[h] - sol[h] - m_sol[h]) for h in heads]
    for _ in scan:
        pass
    gate_nx[...] = _silu(gz_ref[0])
    for h in heads:
        u_nx[h] = sol[h][:, :G_VAL_DIM]
        w_nx[h] = sol[h][:, G_VAL_DIM:].astype(BF16)
        qk_nx[h] = qk[h].astype(BF16)
        qe_nx[h] = (q[h] * eg[h]).astype(BF16)
        for ck in range(cb // c):
            r0, r1 = ck * c, (ck + 1) * c
            g_last = gc[h][r1 - 1:r1, :]
            kd_nx[h, r0:r1, :] = (k[h][r0:r1] * jnp.exp(g_last - gc[h][r0:r1])).astype(BF16)
            el_nx[h, ck] = jnp.broadcast_to(jnp.exp(g_last), (8, LANES))


def _gdn_call(gqkv, misc, gz, conv_w, hist, s0, gnorm_w, a_lane, d_lane, *, cb, t_true):
    b, t_pad, _ = gqkv.shape
    nblk = t_pad // cb
    assert (nblk - 1) * cb < t_true <= t_pad and (t_true - 1) % cb >= CONV_W - 2
    kern = functools.partial(_gdn_kernel, cb=cb, t_true=t_true, nblk=nblk)
    prep = lambda i, j: (i, jnp.minimum(j, nblk - 1), 0)
    scan = lambda i, j: (i, jnp.maximum(j - 1, 0), 0)
    nch = cb // CHUNK
    return pl.pallas_call(
        kern,
        out_shape=[jax.ShapeDtypeStruct((b, t_pad, G_WIDTH), BF16),
                   jax.ShapeDtypeStruct((b, CONV_W - 1, CONV_CH), F32),
                   jax.ShapeDtypeStruct((b, G_HEADS, G_KEY_DIM, G_VAL_DIM), F32)],
        grid=(b, nblk + 1),
        in_specs=[pl.BlockSpec((1, cb, CONV_CH), prep),
                  pl.BlockSpec((1, cb, LANES), prep),
                  pl.BlockSpec((1, cb, G_WIDTH), prep),
                  pl.BlockSpec((CONV_W, CONV_CH), lambda i, j: (0, 0)),
                  pl.BlockSpec((1, CONV_W - 1, CONV_CH), lambda i, j: (i, 0, 0)),
                  pl.BlockSpec((1, G_HEADS, G_KEY_DIM, G_VAL_DIM), lambda i, j: (i, 0, 0, 0)),
                  pl.BlockSpec((1, G_VAL_DIM), lambda i, j: (0, 0)),
                  pl.BlockSpec((1, LANES), lambda i, j: (0, 0)),
                  pl.BlockSpec((1, LANES), lambda i, j: (0, 0))],
        out_specs=[pl.BlockSpec((1, cb, G_WIDTH), scan),
                   pl.BlockSpec((1, CONV_W - 1, CONV_CH), lambda i, j: (i, 0, 0)),
                   pl.BlockSpec((1, G_HEADS, G_KEY_DIM, G_VAL_DIM), lambda i, j: (i, 0, 0, 0))],
        scratch_shapes=[pltpu.VMEM((cb + 8, CONV_CH), F32),
                        pltpu.VMEM((cb, CONV_CH), F32),
                        pltpu.VMEM((G_HEADS, G_KEY_DIM, G_VAL_DIM), F32),
                        pltpu.VMEM((G_HEADS, cb, G_VAL_DIM), F32),
                        pltpu.VMEM((2, G_HEADS, cb, G_VAL_DIM), F32),
                        pltpu.VMEM((2, G_HEADS, cb, G_KEY_DIM), BF16),
                        pltpu.VMEM((2, G_HEADS, cb, cb), BF16),
                        pltpu.VMEM((2, G_HEADS, cb, G_KEY_DIM), BF16),
                        pltpu.VMEM((2, G_HEADS, cb, G_KEY_DIM), BF16),
                        pltpu.VMEM((2, G_HEADS, nch, 8, LANES), F32),
                        pltpu.VMEM((2, cb, G_WIDTH), F32)],
        compiler_params=_cparams(2),
        name="gated_delta",
    )(gqkv, misc, gz, conv_w, hist, s0, gnorm_w, a_lane, d_lane)


def _pack_w_in(w_in):
    d = w_in.shape[0]
    splits = (512, 256, 256, 512, 64, 8, 512, 512, 512, 512, 4, 4)
    offs = np.concatenate([[0], np.cumsum(splits)])
    aq, ak, av, iq, ik, iw, gq, gk, gv, gz, gb, ga = [w_in[:, offs[i]:offs[i + 1]] for i in range(12)]
    zeros64 = jnp.zeros((d, A_HEAD_DIM), w_in.dtype)
    qx = []
    for h in range(A_HEADS):
        qh = aq[:, h * A_HEAD_DIM:(h + 1) * A_HEAD_DIM] * (A_HEAD_DIM ** -0.5 * LOG2E)
        qx += [qh, zeros64] if (h // 2) % 2 == 0 else [zeros64, qh]
    ikx = [ik, zeros64, zeros64, ik]
    misc = [ik, iw, gb, ga, jnp.zeros((d, LANES - MISC_GA - G_HEADS), w_in.dtype)]
    cols = qx + [ak, av, iq * (IDX_DIM ** -0.5)] + ikx + misc + [gq, gk, gv, gz]
    packed = jnp.concatenate(cols, axis=1).astype(BF16)
    assert packed.shape[1] == IN_PACKED
    return packed


def _pack_ffn(w_gate, w_up, w_down):
    d = w_gate.shape[0]
    wg = w_gate.reshape(d, N_FF_CHUNKS, FF_CHUNK)
    wu = w_up.reshape(d, N_FF_CHUNKS, FF_CHUNK)
    wgu = jnp.transpose(jnp.concatenate([wg, wu], axis=2), (1, 0, 2)).astype(BF16)
    wd = w_down.reshape(N_FF_CHUNKS, FF_CHUNK, d).astype(BF16)
    return wgu, wd


def _ikx_layout(ik):
    z = jnp.zeros_like(ik)
    return jnp.concatenate([ik, z, z, ik], axis=-1)


def _lane_vec(vals, lane0):
    return jnp.zeros((1, LANES), F32).at[0, lane0:lane0 + vals.shape[0]].set(vals.astype(F32))


def _run(x, mod, past, layers, rel_bias, norm_final):
    b, t, d = x.shape
    n = b * t
    per_token = t < TOKEN_TILE
    tm = n if per_token else TOKEN_TILE
    tiles_per_batch = None if per_token else t // tm
    tok = dict(tm=tm, tiles_per_batch=tiles_per_batch, per_token=per_token)

    x2 = x.reshape(n, d)
    states = []
    for li, lw in enumerate(layers):
        m = mod[li]

        def mvec(kidx):
            row = m[:, kidx]
            return jnp.repeat(row, t, axis=0) if per_token else row[:, None, :]

        sh1, sc1, gt1, sh2, sc2, gt2, sh3, sc3, gt3 = [mvec(i) for i in range(N_MOD)]
        x2 = _ffn_call(x2, sh1, sc1, gt1, lw["norm_ffn1"], lw["wgu1"], lw["wd1"], None, **tok)
        qx, k, kb, v, vb, iq, ikx, misc, gqkv, gz = _inproj_call(
            x2, sh2, sc2, lw["norm_mix"], lw["w_in"], **tok)

        if past is None:
            offset, l_true, tq = 0, t, ATT_TQ
            k_all, v_all, ikx_all = (a.reshape(b, t, -1) for a in (kb, vb, ikx))
            qx3, iq3, misc3 = (a.reshape(b, t, -1) for a in (qx, iq, misc))
            conv_hist = jnp.zeros((b, CONV_W - 1, CONV_CH), F32)
            s0 = jnp.zeros((b, G_HEADS, G_KEY_DIM, G_VAL_DIM), F32)
        else:
            k_hist, v_hist, ik_hist, conv_hist, s0 = (p[li] for p in past)
            offset = k_hist.shape[1]
            l_true = offset + t
            tq = LANES
            lp = -(-l_true // ATT_KB) * ATT_KB
            pad_k = lambda a: jnp.pad(a, ((0, 0), (0, lp - l_true), (0, 0)))
            k_all = pad_k(jnp.concatenate([k_hist.reshape(b, offset, -1).astype(BF16),
                                           kb.reshape(b, t, -1)], axis=1))
            v_all = pad_k(jnp.concatenate([v_hist.reshape(b, offset, -1).astype(BF16),
                                           vb.reshape(b, t, -1)], axis=1))
            ikx_all = pad_k(jnp.concatenate([_ikx_layout(ik_hist.astype(BF16)),
                                             ikx.reshape(b, t, -1)], axis=1))
            pad_q = lambda a: jnp.pad(a.reshape(b, t, -1), ((0, 0), (0, tq - t), (0, 0)))
            qx3, iq3, misc3 = pad_q(qx), pad_q(iq), pad_q(misc)
            s0 = s0.astype(F32)
            conv_hist = conv_hist.astype(F32)
        lp = k_all.shape[1]
        n_top = min(TOPK_MAX, l_true // 4)
        vt = jnp.transpose(v_all.reshape(b, lp // ATT_KB, ATT_KB, 2, LANES), (0, 1, 3, 4, 2))
        vt = jnp.concatenate([vt, jnp.ones((b, lp // ATT_KB, 2, V_ROWS - LANES, ATT_KB), BF16)], axis=3)
        attn = _attn_call(rel_bias, qx3, iq3, misc3, k_all, vt, ikx_all,
                          tq=tq, l_true=l_true, offset=offset, n_top=n_top)
        attn = attn[:, :t].reshape(n, A_WIDTH)

        cb = GDN_BLOCK if t % GDN_BLOCK == 0 else CHUNK
        t_pad = -(-t // cb) * cb
        pad_t = lambda a: jnp.pad(a.reshape(b, t, -1), ((0, 0), (0, t_pad - t), (0, 0)))
        gdn, conv_new, s_new = _gdn_call(pad_t(gqkv), pad_t(misc), pad_t(gz), lw["conv_w"], conv_hist, s0,
                                         lw["gnorm_w"], lw["a_lane"], lw["d_lane"], cb=cb, t_true=t)
        gdn = gdn[:, :t].reshape(n, G_WIDTH)

        x2 = _outproj_call(x2, gt2, attn, gdn, lw["wo_a"], lw["wo_g"], **tok)
        nf = norm_final if li == len(layers) - 1 else None
        x2 = _ffn_call(x2, sh3, sc3, gt3, lw["norm_ffn2"], lw["wgu2"], lw["wd2"], nf, **tok)

        states.append((k.reshape(b, t, A_KV_HEADS, A_HEAD_DIM), v.reshape(b, t, A_KV_HEADS, A_HEAD_DIM),
                       misc.reshape(b, t, LANES)[..., :IDX_DIM], conv_new, s_new))
    stacked = [jnp.stack(s, axis=0) for s in zip(*states)]
    return x2.reshape(b, t, d), stacked


def kernel(x_prompt, x_sample, cache_k, cache_v, cache_idx_k, state_conv, state_delta, c_prompt, c_sample,
           w_mod, b_mod, norm_ffn1, norm_mix, norm_ffn2, ffn1_w_gate, ffn1_w_up, ffn1_w_down,
           ffn2_w_gate, ffn2_w_up, ffn2_w_down, w_in, w_out, rel_bias, conv_w, a_log, dt_bias, gnorm_w,
           norm_final):
    depth = w_mod.shape[0]
    bp = c_prompt.shape[0]
    c_all = jnp.concatenate([c_prompt, c_sample], axis=0)
    layers, mods_p, mods_s = [], [], []
    for l in range(depth):
        mod = _mod_call(c_all, w_mod[l], b_mod[l]).reshape(c_all.shape[0], N_MOD, D_MODEL)
        mods_p.append(mod[:bp])
        mods_s.append(mod[bp:])
        wgu1, wd1 = _pack_ffn(ffn1_w_gate[l], ffn1_w_up[l], ffn1_w_down[l])
        wgu2, wd2 = _pack_ffn(ffn2_w_gate[l], ffn2_w_up[l], ffn2_w_down[l])
        wo = w_out[l].astype(BF16)
        layers.append(dict(
            norm_ffn1=norm_ffn1[l].reshape(1, -1), norm_mix=norm_mix[l].reshape(1, -1),
            norm_ffn2=norm_ffn2[l].reshape(1, -1), wgu1=wgu1, wd1=wd1, wgu2=wgu2, wd2=wd2,
            w_in=_pack_w_in(w_in[l]), wo_a=wo[:A_WIDTH], wo_g=wo[A_WIDTH:],
            conv_w=conv_w[l], gnorm_w=gnorm_w[l].reshape(1, -1),
            a_lane=_lane_vec(a_log[l], MISC_GA), d_lane=_lane_vec(dt_bias[l], MISC_GA)))
    nf = norm_final.reshape(1, -1)
    y_p, (k_p, v_p, ik_p, conv_p, delta_p) = _run(x_prompt, mods_p, None, layers, rel_bias, nf)
    past = (cache_k, cache_v, cache_idx_k, state_conv, state_delta)
    y_s, (k_s, v_s, ik_s, conv_s, delta_s) = _run(x_sample, mods_s, past, layers, rel_bias, nf)
    return (y_p, y_s, k_p, v_p, ik_p, conv_p, delta_p, k_s, v_s, ik_s, conv_s, delta_s)
```

```python
import functools

import jax
import jax.numpy as jnp
import numpy as np
from jax import lax
from jax.experimental import pallas as pl
from jax.experimental.pallas import tpu as pltpu

F32 = jnp.float32
BF16 = jnp.bfloat16

D_MODEL = 1024
CHUNK = 64
A_HEAD_DIM = 64
A_HEADS = 8
A_KV_HEADS = 4
A_WIDTH = A_HEADS * A_HEAD_DIM
IDX_HEADS = 8
IDX_DIM = 64
TOPK_MAX = 256
REL_BUCKETS = 32
G_KEY_DIM = 128
G_VAL_DIM = 128
G_HEADS = 4
G_WIDTH = G_HEADS * G_VAL_DIM
CONV_W = 4
CONV_CH = 2 * G_HEADS * G_KEY_DIM + G_HEADS * G_VAL_DIM
D_FF = 2816
N_MOD = 9
EPS = 1e-6

LANES = 128
MXU_DIM = 256
VMEM_LIMIT_BYTES = 56 * 1024 * 1024

FF_CHUNK = MXU_DIM
N_FF_CHUNKS = D_FF // FF_CHUNK
TOKEN_TILE = 512
ATT_TQ = 256
ATT_KB = 256
GDN_BLOCK = 256

QX_W = A_HEADS * LANES
OFF_QX = 0
OFF_K = OFF_QX + QX_W
OFF_V = OFF_K + A_KV_HEADS * A_HEAD_DIM
OFF_IQ = OFF_V + A_KV_HEADS * A_HEAD_DIM
OFF_IKX = OFF_IQ + IDX_HEADS * IDX_DIM
OFF_MISC = OFF_IKX + 2 * LANES
OFF_GQKV = OFF_MISC + LANES
OFF_GZ = OFF_GQKV + CONV_CH
IN_PACKED = OFF_GZ + G_WIDTH
MISC_IW = IDX_DIM
MISC_GB = MISC_IW + IDX_HEADS
MISC_GA = MISC_GB + G_HEADS

NEG_BIG = -1e30
INT_MIN = -2 ** 31
LOG2E = 1.4426950408889634
V_ROWS = LANES + 16


def _cparams(n_axes):
    return pltpu.CompilerParams(dimension_semantics=("arbitrary",) * n_axes,
                                vmem_limit_bytes=VMEM_LIMIT_BYTES)


def _resident(shape):
    nd = len(shape)
    return pl.BlockSpec(shape, lambda *_: (0,) * nd, pipeline_mode=pl.Buffered(1))


def _dot_nt(a, b):
    return lax.dot_general(a, b, (((1,), (1,)), ((), ())), preferred_element_type=F32)


def _rms_mod(x, gain, shift, scale):
    ms = jnp.mean(x * x, axis=-1, keepdims=True)
    y = x * lax.rsqrt(ms + EPS) * gain
    return y * (1.0 + scale) + shift


def _silu(x):
    return x * jax.nn.sigmoid(x)


def _tree_sum(parts):
    while len(parts) > 1:
        parts = [a + b for a, b in zip(parts[0::2], parts[1::2])] + ([parts[-1]] if len(parts) % 2 else [])
    return parts[0]


def _mod_kernel(c_ref, w_ref, b_ref, o_ref):
    s = _silu(c_ref[...]).astype(BF16)
    o_ref[...] = jnp.dot(s, w_ref[...].astype(BF16), preferred_element_type=F32) + b_ref[...]


def _mod_call(c, w_mod, b_mod):
    rows, d = c.shape
    n = w_mod.shape[1]
    tn = D_MODEL
    return pl.pallas_call(
        _mod_kernel,
        out_shape=jax.ShapeDtypeStruct((rows, n), F32),
        grid=(n // tn,),
        in_specs=[pl.BlockSpec((rows, d), lambda j: (0, 0)),
                  pl.BlockSpec((d, tn), lambda j: (0, j)),
                  pl.BlockSpec((1, tn), lambda j: (0, j))],
        out_specs=pl.BlockSpec((rows, tn), lambda j: (0, j)),
        compiler_params=_cparams(1),
        name="mod",
    )(c, w_mod, b_mod.reshape(1, n))


def _mod_specs(per_token, tm, tiles_per_batch):
    if per_token:
        return pl.BlockSpec((tm, D_MODEL), lambda i: (i, 0))
    return pl.BlockSpec((None, 1, D_MODEL), lambda i: (i // tiles_per_batch, 0, 0))


def _ffn_kernel(x_ref, sh_ref, sc_ref, gt_ref, gain_ref, wgu_ref, wd_ref, *rest, final_norm):
    if final_norm:
        nf_ref, o_ref, acc_ref = rest
    else:
        o_ref, acc_ref = rest
    x = x_ref[...]
    h = _rms_mod(x, gain_ref[...], sh_ref[...], sc_ref[...]).astype(BF16)
    for j in range(N_FF_CHUNKS):
        ab = jnp.dot(h, wgu_ref[j], preferred_element_type=F32)
        g = (_silu(ab[:, :FF_CHUNK]) * ab[:, FF_CHUNK:]).astype(BF16)
        d = jnp.dot(g, wd_ref[j], preferred_element_type=F32)
        if j == 0:
            acc_ref[...] = d
        else:
            acc_ref[...] += d
    y = x + 0.5 * gt_ref[...] * acc_ref[...]
    if final_norm:
        ms = jnp.mean(y * y, axis=-1, keepdims=True)
        y = y * lax.rsqrt(ms + EPS) * nf_ref[...]
    o_ref[...] = y


def _ffn_call(x2, sh, sc, gt, gain, wgu, wd, norm_final, *, tm, tiles_per_batch, per_token):
    n = x2.shape[0]
    mspec = _mod_specs(per_token, tm, tiles_per_batch)
    in_specs = [pl.BlockSpec((tm, D_MODEL), lambda i: (i, 0)), mspec, mspec, mspec,
                _resident((1, D_MODEL)), _resident(wgu.shape), _resident(wd.shape)]
    args = [x2, sh, sc, gt, gain, wgu, wd]
    final_norm = norm_final is not None
    if final_norm:
        in_specs.append(_resident((1, D_MODEL)))
        args.append(norm_final)
    return pl.pallas_call(
        functools.partial(_ffn_kernel, final_norm=final_norm),
        out_shape=jax.ShapeDtypeStruct((n, D_MODEL), F32),
        grid=(n // tm,),
        in_specs=in_specs,
        out_specs=pl.BlockSpec((tm, D_MODEL), lambda i: (i, 0)),
        scratch_shapes=[pltpu.VMEM((tm, D_MODEL), F32)],
        compiler_params=_cparams(1),
        name="ffn_final" if final_norm else "ffn",
    )(*args)


def _inproj_kernel(x_ref, sh_ref, sc_ref, gain_ref, w_ref,
                   qx_o, k_o, kb_o, v_o, vb_o, iq_o, ikx_o, misc_o, gqkv_o, gz_o):
    h = _rms_mod(x_ref[...], gain_ref[...], sh_ref[...], sc_ref[...]).astype(BF16)

    def mm(off, width):
        return jnp.dot(h, w_ref[:, off:off + width], preferred_element_type=F32)

    qx_o[...] = mm(OFF_QX, QX_W).astype(BF16)
    k = mm(OFF_K, OFF_V - OFF_K)
    k_o[...] = k
    kb_o[...] = k.astype(BF16)
    v = mm(OFF_V, OFF_IQ - OFF_V)
    v_o[...] = v
    vb_o[...] = v.astype(BF16)
    iq_o[...] = mm(OFF_IQ, OFF_IKX - OFF_IQ).astype(BF16)
    ikx_o[...] = mm(OFF_IKX, OFF_MISC - OFF_IKX).astype(BF16)
    misc_o[...] = mm(OFF_MISC, LANES)
    gqkv_o[...] = mm(OFF_GQKV, CONV_CH)
    gz_o[...] = mm(OFF_GZ, G_WIDTH)


def _inproj_call(x2, sh, sc, gain, w_packed, *, tm, tiles_per_batch, per_token):
    n = x2.shape[0]
    mspec = _mod_specs(per_token, tm, tiles_per_batch)
    widths = [(QX_W, BF16), (256, F32), (256, BF16), (256, F32), (256, BF16), (512, BF16),
              (2 * LANES, BF16), (LANES, F32), (CONV_CH, F32), (G_WIDTH, F32)]
    return pl.pallas_call(
        _inproj_kernel,
        out_shape=[jax.ShapeDtypeStruct((n, w), dt) for w, dt in widths],
        grid=(n // tm,),
        in_specs=[pl.BlockSpec((tm, D_MODEL), lambda i: (i, 0)), mspec, mspec,
                  _resident((1, D_MODEL)), _resident(w_packed.shape)],
        out_specs=[pl.BlockSpec((tm, w), lambda i: (i, 0)) for w, _ in widths],
        compiler_params=_cparams(1),
        name="inproj",
    )(x2, sh, sc, gain, w_packed)


def _outproj_kernel(x_ref, gt_ref, a_ref, g_ref, wa_ref, wg_ref, o_ref):
    m = (jnp.dot(a_ref[...], wa_ref[...], preferred_element_type=F32)
         + jnp.dot(g_ref[...], wg_ref[...], preferred_element_type=F32))
    o_ref[...] = x_ref[...] + gt_ref[...] * m


def _outproj_call(x2, gt, attn, gdn, wa, wg, *, tm, tiles_per_batch, per_token):
    n = x2.shape[0]
    mspec = _mod_specs(per_token, tm, tiles_per_batch)
    return pl.pallas_call(
        _outproj_kernel,
        out_shape=jax.ShapeDtypeStruct((n, D_MODEL), F32),
        grid=(n // tm,),
        in_specs=[pl.BlockSpec((tm, D_MODEL), lambda i: (i, 0)), mspec,
                  pl.BlockSpec((tm, A_WIDTH), lambda i: (i, 0)),
                  pl.BlockSpec((tm, G_WIDTH), lambda i: (i, 0)),
                  _resident(wa.shape), _resident(wg.shape)],
        out_specs=pl.BlockSpec((tm, D_MODEL), lambda i: (i, 0)),
        compiler_params=_cparams(1),
        name="outproj",
    )(x2, gt, attn, gdn, wa, wg)


def _rel_bucket_int(rel):
    n = jnp.abs(rel)
    large = jnp.full(rel.shape, 8, jnp.int32)
    for th in (12, 16, 23, 32, 46, 64, 91):
        large = large + jnp.where(n >= th, 1, 0)
    return jnp.where(rel > 0, REL_BUCKETS // 2, 0) + jnp.where(n < 8, n, large)


def _key_to_float(t):
    bits = jnp.where(t >= 0, t, t ^ jnp.int32(0x7FFFFFFF))
    return lax.bitcast_convert_type(bits, F32)


def _top16(x):
    bits = lax.bitcast_convert_type(x, jnp.int32) & jnp.int32(-65536)
    return lax.bitcast_convert_type(bits, F32).astype(BF16)


def _attn_kernel(relb_ref, qx_ref, iq_ref, misc_ref, k_ref, vt_ref, ikx_ref, o_ref,
                 s_ref, shi_ref, sel_ref, bias_ref, lg_ref, cm_ref, acc_ref, *, tq, l_true, offset, n_top):
    kb_sz = ATT_KB
    first = (pl.program_id(0) == 0) & (pl.program_id(1) == 0)
    q0 = offset + pl.program_id(1) * tq
    kmax = jnp.minimum(q0 + tq, l_true)
    nkb = (kmax + kb_sz - 1) // kb_sz

    key_off = lax.broadcasted_iota(jnp.int32, (kb_sz, tq), 0)
    qry_off = lax.broadcasted_iota(jnp.int32, (kb_sz, tq), 1)

    @pl.when(first)
    def _():
        for d in range(3):
            bucket = _rel_bucket_int(key_off - qry_off - d * kb_sz)
            for h in range(A_HEADS):
                t = jnp.zeros((kb_sz, tq), F32)
                for b in range(REL_BUCKETS):
                    t = jnp.where(bucket == b, relb_ref[b, h], t)
                bias_ref[d, h] = (t * LOG2E).astype(BF16)

    qpos = q0 + lax.broadcasted_iota(jnp.int32, (1, tq), 1)
    limit = jnp.minimum((qpos // CHUNK + 1) * CHUNK, l_true)

    iw_t = misc_ref[0].T[MISC_IW:MISC_IW + IDX_HEADS, :] * (IDX_HEADS ** -0.5)

    def score_body(kb, carry):
        base = pl.multiple_of(kb * kb_sz, kb_sz)
        ik2 = ikx_ref[0, pl.ds(base, kb_sz), :]
        ik_lo = ik2[:, :LANES]
        ik_hi = ik2[:, LANES:]
        s = jnp.zeros((kb_sz, tq), F32)
        for j in range(IDX_HEADS // 2):
            slab = iq_ref[0, :, j * LANES:(j + 1) * LANES]
            d0 = _dot_nt(ik_lo, slab)
            d1 = _dot_nt(ik_hi, slab)
            s = s + iw_t[2 * j:2 * j + 1, :] * jnp.maximum(d0, 0.0)
            s = s + iw_t[2 * j + 1:2 * j + 2, :] * jnp.maximum(d1, 0.0)
        s = jnp.where(base + key_off < limit, s, -jnp.inf)
        s_ref[pl.ds(base, kb_sz), :] = s
        shi_ref[pl.ds(base, kb_sz), :] = _top16(s)
        return carry

    def for_each_block(body):
        def quad(i, carry):
            for r in range(4):
                body(4 * i + r, carry)
            return carry
        lax.fori_loop(0, nkb // 4, quad, 0)
        done = (nkb // 4) * 4

        @pl.when(nkb % 4 >= 2)
        def _():
            body(done, 0)
            body(done + 1, 0)

        @pl.when(nkb % 2 == 1)
        def _():
            body(nkb - 1, 0)

    for_each_block(score_body)

    def count(pred):
        def body(kb, acc):
            base = pl.multiple_of(kb * kb_sz, kb_sz)
            hit = pred(s_ref[pl.ds(base, kb_sz), :], base + key_off)
            return acc + _tree_sum([hit[r:r + 8] for r in range(0, kb_sz, 8)])
        acc = lax.fori_loop(0, nkb, body, jnp.zeros((8, tq), F32))
        return jnp.sum(acc, axis=0, keepdims=True)

    def count_hi(thr_hi):
        one = jnp.ones((), BF16)
        zero = jnp.zeros((), BF16)

        def body(kb, acc):
            base = pl.multiple_of(kb * kb_sz, kb_sz)
            hit = jnp.where(shi_ref[pl.ds(base, kb_sz), :] >= thr_hi, one, zero)
            return acc + _tree_sum([hit[r:r + 16] for r in range(0, kb_sz, 16)])
        acc = lax.fori_loop(0, nkb, body, jnp.zeros((16, tq), BF16))
        return jnp.sum(acc.astype(F32), axis=0, keepdims=True)

    def hi_body(i, t):
        cand = t + lax.shift_left(jnp.int32(1), 31 - i)
        c = count_hi(_top16(_key_to_float(cand)))
        return jnp.where(c >= n_top, cand, t)

    t_key = lax.fori_loop(0, 16, hi_body, jnp.full((1, tq), INT_MIN, jnp.int32))

    def lo_body(i, t):
        cand = t + lax.shift_left(jnp.int32(1), 15 - i)
        thr_c = _key_to_float(cand)
        c = count(lambda blk, _: jnp.where(blk >= thr_c, 1.0, 0.0))
        return jnp.where(c >= n_top, cand, t)

    t_key = lax.fori_loop(0, 16, lo_body, t_key)
    thr = _key_to_float(t_key)

    need = n_top - count(lambda blk, _: jnp.where(blk > thr, 1.0, 0.0))
    n_eq = count(lambda blk, _: jnp.where(blk == thr, 1.0, 0.0))
    take_all = limit <= n_top
    excess = jnp.where(take_all, 0.0, jnp.where(n_eq > need, 1.0, 0.0))
    any_excess = jnp.max(excess) > 0.0
    idx_bits = 14

    def tie_body(i, c):
        cand = c + lax.shift_left(jnp.int32(1), idx_bits - 1 - i)
        f = count(lambda blk, kidx: jnp.where(blk == thr, jnp.where(kidx < cand, 1.0, 0.0), 0.0))
        return jnp.where(f <= need, cand, c)

    cut0 = jnp.where(any_excess, jnp.zeros((1, tq), jnp.int32),
                     jnp.full((1, tq), 2 ** idx_bits, jnp.int32))
    cut = lax.fori_loop(0, jnp.where(any_excess, idx_bits, 0), tie_body, cut0)
    thr = jnp.where(take_all, -jnp.inf, thr)
    cut = jnp.where(take_all, limit, cut)

    def mask_body(kb, carry):
        base = pl.multiple_of(kb * kb_sz, kb_sz)
        blk = s_ref[pl.ds(base, kb_sz), :]
        tie = jnp.where(base + key_off < cut, 0.0, NEG_BIG)
        sel = jnp.where(blk > thr, 0.0, jnp.where(blk == thr, tie, NEG_BIG))
        sel_ref[pl.ds(base, kb_sz), :] = sel.astype(BF16)
        return carry

    lax.fori_loop(0, nkb, mask_body, 0)

    cm_ref[...] = jnp.full(cm_ref.shape, NEG_BIG, BF16)

    def logit_body(kb, carry):
        base = pl.multiple_of(kb * kb_sz, kb_sz)
        dsel = jnp.clip((q0 - base) // kb_sz, 0, 2)
        kblk = k_ref[0, pl.ds(base, kb_sz), :]
        sel = sel_ref[pl.ds(base, kb_sz), :]
        for h in range(A_HEADS):
            sl = h // 4
            lg = _dot_nt(kblk[:, sl * LANES:(sl + 1) * LANES], qx_ref[0, :, h * LANES:(h + 1) * LANES])
            lg = lg.astype(BF16) + sel + bias_ref[dsel, h]
            lg_ref[h, pl.ds(base, kb_sz), :] = lg
            cm_ref[h] = jnp.maximum(cm_ref[h], lg)
        return carry

    for_each_block(logit_body)

    acc_ref[...] = jnp.zeros(acc_ref.shape, F32)
    m_rows = [jnp.max(cm_ref[h].astype(F32), axis=0, keepdims=True).astype(BF16) for h in range(A_HEADS)]

    def pv_body(kb, carry):
        base = pl.multiple_of(kb * kb_sz, kb_sz)
        for h in range(A_HEADS):
            p = jnp.exp2(lg_ref[h, pl.ds(base, kb_sz), :] - m_rows[h])
            acc_ref[h] += jnp.dot(vt_ref[0, kb, h // 4], p, preferred_element_type=F32)
        return carry

    for_each_block(pv_body)

    outs = []
    for h in range(A_HEADS):
        pos = (h // 2) % 2
        outs.append(acc_ref[h, pos * A_HEAD_DIM:(pos + 1) * A_HEAD_DIM, :] / acc_ref[h, LANES:LANES + 1, :])
    o_ref[0] = jnp.concatenate(outs, axis=0).T.astype(BF16)


def _attn_call(rel_bias, qx, iq, misc, kb, vt, ikx, *, tq, l_true, offset, n_top):
    b, tq_total, _ = qx.shape
    lp = kb.shape[1]
    nq = tq_total // tq
    assert offset % ATT_KB == 0 and tq % CHUNK == 0 and ATT_KB % tq == 0 and lp % ATT_KB == 0
    assert lp < 2 ** 14 and l_true <= lp
    kern = functools.partial(_attn_kernel, tq=tq, l_true=l_true, offset=offset, n_top=n_top)
    return pl.pallas_call(
        kern,
        out_shape=jax.ShapeDtypeStruct((b, tq_total, A_WIDTH), BF16),
        grid=(b, nq),
        in_specs=[pl.BlockSpec(memory_space=pltpu.SMEM),
                  pl.BlockSpec((1, tq, QX_W), lambda i, j: (i, j, 0)),
                  pl.BlockSpec((1, tq, IDX_HEADS * IDX_DIM), lambda i, j: (i, j, 0)),
                  pl.BlockSpec((1, tq, LANES), lambda i, j: (i, j, 0)),
                  pl.BlockSpec((1, lp, 256), lambda i, j: (i, 0, 0)),
                  pl.BlockSpec((1, lp // ATT_KB, 2, V_ROWS, ATT_KB), lambda i, j: (i, 0, 0, 0, 0)),
                  pl.BlockSpec((1, lp, 2 * LANES), lambda i, j: (i, 0, 0))],
        out_specs=pl.BlockSpec((1, tq, A_WIDTH), lambda i, j: (i, j, 0)),
        scratch_shapes=[pltpu.VMEM((lp, tq), F32),
                        pltpu.VMEM((lp, tq), BF16),
                        pltpu.VMEM((lp, tq), BF16),
                        pltpu.VMEM((3, A_HEADS, ATT_KB, tq), BF16),
                        pltpu.VMEM((A_HEADS, lp, tq), BF16),
                        pltpu.VMEM((A_HEADS, ATT_KB, tq), BF16),
                        pltpu.VMEM((A_HEADS, V_ROWS, tq), F32)],
        compiler_params=_cparams(2),
        name="sparse_attn",
    )(rel_bias, qx, iq, misc, kb, vt, ikx)


def _split_bf16(a, n):
    parts = []
    r = a
    for i in range(n):
        p = r.astype(BF16)
        parts.append(p)
        if i + 1 < n:
            r = r - p.astype(F32)
    return parts


def _bdot(a, b, dims=(((1,), (0,)), ((), ()))):
    return lax.dot_general(a.astype(BF16), b.astype(BF16), dims, preferred_element_type=F32)


_NT = (((1,), (1,)), ((), ()))
_TN = (((0,), (0,)), ((), ()))


def _gdn_kernel(x_ref, misc_ref, gz_ref, convw_ref, hist_ref, s0_ref, gnw_ref, alane_ref, dlane_ref,
                o_ref, conv_o, s_o, xp_ref, y_ref, st_ref, vn_ref,
                u_ref, w_ref, qk_ref, qe_ref, kd_ref, el_ref, gate_ref, *, cb, t_true, nblk):
    c = CHUNK
    step = pl.program_id(1)
    blk = jnp.minimum(step, nblk - 1)
    carried = (u_ref, w_ref, qk_ref, qe_ref, kd_ref, el_ref, gate_ref)
    u_nx, w_nx, qk_nx, qe_nx, kd_nx, el_nx, gate_nx = (r.at[1] for r in carried)
    u_ref, w_ref, qk_ref, qe_ref, kd_ref, el_ref, gate_ref = (r.at[0] for r in carried)

    @pl.when(step == 0)
    def _():
        xp_ref[8 - (CONV_W - 1):8, :] = hist_ref[0]
        st_ref[...] = s0_ref[0]
        for r in carried:
            r[...] = jnp.zeros(r.shape, r.dtype)

    heads = range(G_HEADS)
    for r in carried:
        r[0] = r[1]

    def scan_stages():
        vn_ref[...] = jnp.zeros(vn_ref.shape, F32)
        for ck in range(cb // c):
            r0, r1 = ck * c, (ck + 1) * c
            s_prev = [st_ref[h] for h in heads]
            v_new = [u_ref[h, r0:r1, :] - _bdot(w_ref[h, r0:r1, :], s_prev[h]) for h in heads]
            o_state = [_bdot(qe_ref[h, r0:r1, :], s_prev[h]) for h in heads]
            yield
            for h in heads:
                vn_ref[h, r0:r1, :] = v_new[h]
            o = [o_state[h] + _bdot(qk_ref[h, r0:r1, :], vn_ref[h]) for h in heads]
            for h in heads:
                s_new = s_prev[h] * el_ref[h, ck, 0:1, :] + _bdot(kd_ref[h, r0:r1, :], v_new[h], _TN)
                st_ref[h] = jnp.where(step > 0, s_new, s_prev[h])
            yield
            for h in heads:
                on = o[h] * lax.rsqrt(jnp.mean(o[h] * o[h], axis=-1, keepdims=True) + EPS) * gnw_ref[...]
                gate = gate_ref[r0:r1, h * G_VAL_DIM:(h + 1) * G_VAL_DIM]
                o_ref[0, r0:r1, h * G_VAL_DIM:(h + 1) * G_VAL_DIM] = (on * gate).astype(BF16)
            yield
        s_o[0] = st_ref[...]

    scan = scan_stages()

    def tick():
        next(scan, None)

    tick()

    xp_ref[8:8 + cb, :] = x_ref[0]
    y = convw_ref[CONV_W - 1:CONV_W, :] * xp_ref[8:8 + cb, :]
    for j in range(CONV_W - 1):
        y = y + convw_ref[j:j + 1, :] * xp_ref[5 + j:5 + j + cb, :]
    y_ref[...] = _silu(y)

    last_row = (t_true - 1) % cb
    conv_o[0] = xp_ref[8 + last_row - (CONV_W - 2):8 + last_row + 1, :]

    xp_ref[8 - (CONV_W - 1):8, :] = xp_ref[8 + cb - (CONV_W - 1):8 + cb, :]

    ri = lax.broadcasted_iota(jnp.int32, (cb, cb), 0)
    ci = lax.broadcasted_iota(jnp.int32, (cb, cb), 1)
    lag = jnp.where((ri // c) == (ci // c), ri - ci, -1)
    tri = lag >= 0
    strict = lag > 0
    eye = ri == ci
    tril_bf = jnp.where(tri, 1.0, 0.0).astype(BF16)
    eye_f = jnp.where(eye, 1.0, 0.0)

    ms = misc_ref[0]
    tok = blk * cb + lax.broadcasted_iota(jnp.int32, (cb, 1), 0)
    live = tok < t_true
    beta_s = jnp.where(live, jax.nn.sigmoid(ms), 0.0)
    z = ms + dlane_ref[...]
    softplus = jnp.maximum(z, 0.0) + jnp.log(1.0 + jnp.exp(-jnp.abs(z)))
    g_s = jnp.where(live, -jnp.exp(alane_ref[...]) * softplus, 0.0)
    gc_s = None
    for piece in _split_bf16(g_s, 3):
        t = jnp.dot(tril_bf, piece, preferred_element_type=F32)
        gc_s = t if gc_s is None else gc_s + t

    tick()

    q, k, v, beta, gc, decay, kb, eg = ([None] * G_HEADS for _ in range(8))
    for h in heads:
        qh = y_ref[:, h * G_KEY_DIM:(h + 1) * G_KEY_DIM]
        kh = y_ref[:, G_WIDTH + h * G_KEY_DIM:G_WIDTH + (h + 1) * G_KEY_DIM]
        v[h] = y_ref[:, 2 * G_WIDTH + h * G_VAL_DIM:2 * G_WIDTH + (h + 1) * G_VAL_DIM]
        q[h] = qh * lax.rsqrt(jnp.sum(qh * qh, axis=-1, keepdims=True) + 1e-6) * (G_KEY_DIM ** -0.5)
        k[h] = kh * lax.rsqrt(jnp.sum(kh * kh, axis=-1, keepdims=True) + 1e-6)
        beta[h] = beta_s[:, MISC_GB + h:MISC_GB + h + 1]
        gc[h] = gc_s[:, MISC_GA + h:MISC_GA + h + 1]
        gc_b = jnp.broadcast_to(gc[h], (cb, cb))
        gc_row = jnp.sum(jnp.where(eye, gc_b, 0.0), axis=0, keepdims=True)
        decay[h] = jnp.exp(jnp.where(tri, gc_b - gc_row, NEG_BIG))
        kb[h] = k[h] * beta[h]
        eg[h] = jnp.exp(gc[h])
    tick()
    m = [jnp.where(strict, _bdot(kb[h], k[h], _NT) * decay[h], 0.0) for h in heads]
    rhs = [jnp.concatenate([v[h] * beta[h], kb[h] * eg[h]], axis=1) for h in heads]
    qk = [jnp.where(tri, _bdot(q[h], k[h], _NT) * decay[h], 0.0) for h in heads]
    tick()
    pw = [-m[h] for h in heads]
    inv = [eye_f + pw[h] for h in heads]
    for _ in range(5):
        pw = [_bdot(pw[h], pw[h]) for h in heads]
        tick()
        inv = [inv[h] + _bdot(inv[h], pw[h]) for h in heads]
    tick()
    sol = [_bdot(inv[h], rhs[h]) for h in heads]
    tick()
    m_sol = []
    for h in heads:
        m_hi, m_lo = _split_bf16(m[h], 2)
        s_hi, s_lo = _split_bf16(sol[h], 2)
        m_sol.append(_bdot(m_hi, s_hi) + _bdot(m_hi, s_lo) + _bdot(m_lo, s_hi))
    tick()
    sol = [sol[h] + _bdot(inv[h], rhs[h] - sol[h] - m_sol[h]) for h in heads]
    for _ in scan:
        pass
    gate_nx[...] = _silu(gz_ref[0])
    for h in heads:
        u_nx[h] = sol[h][:, :G_VAL_DIM]
        w_nx[h] = sol[h][:, G_VAL_DIM:].astype(BF16)
        qk_nx[h] = qk[h].astype(BF16)
        qe_nx[h] = (q[h] * eg[h]).astype(BF16)
        for ck in range(cb // c):
            r0, r1 = ck * c, (ck + 1) * c
            g_last = gc[h][r1 - 1:r1, :]
            kd_nx[h, r0:r1, :] = (k[h][r0:r1] * jnp.exp(g_last - gc[h][r0:r1])).astype(BF16)
            el_nx[h, ck] = jnp.broadcast_to(jnp.exp(g_last), (8, LANES))


def _gdn_call(gqkv, misc, gz, conv_w, hist, s0, gnorm_w, a_lane, d_lane, *, cb, t_true):
    b, t_pad, _ = gqkv.shape
    nblk = t_pad // cb
    assert (nblk - 1) * cb < t_true <= t_pad and (t_true - 1) % cb >= CONV_W - 2
    kern = functools.partial(_gdn_kernel, cb=cb, t_true=t_true, nblk=nblk)
    prep = lambda i, j: (i, jnp.minimum(j, nblk - 1), 0)
    scan = lambda i, j: (i, jnp.maximum(j - 1, 0), 0)
    nch = cb // CHUNK
    return pl.pallas_call(
        kern,
        out_shape=[jax.ShapeDtypeStruct((b, t_pad, G_WIDTH), BF16),
                   jax.ShapeDtypeStruct((b, CONV_W - 1, CONV_CH), F32),
                   jax.ShapeDtypeStruct((b, G_HEADS, G_KEY_DIM, G_VAL_DIM), F32)],
        grid=(b, nblk + 1),
        in_specs=[pl.BlockSpec((1, cb, CONV_CH), prep),
                  pl.BlockSpec((1, cb, LANES), prep),
                  pl.BlockSpec((1, cb, G_WIDTH), prep),
                  pl.BlockSpec((CONV_W, CONV_CH), lambda i, j: (0, 0)),
                  pl.BlockSpec((1, CONV_W - 1, CONV_CH), lambda i, j: (i, 0, 0)),
                  pl.BlockSpec((1, G_HEADS, G_KEY_DIM, G_VAL_DIM), lambda i, j: (i, 0, 0, 0)),
                  pl.BlockSpec((1, G_VAL_DIM), lambda i, j: (0, 0)),
                  pl.BlockSpec((1, LANES), lambda i, j: (0, 0)),
                  pl.BlockSpec((1, LANES), lambda i, j: (0, 0))],
        out_specs=[pl.BlockSpec((1, cb, G_WIDTH), scan),
                   pl.BlockSpec((1, CONV_W - 1, CONV_CH), lambda i, j: (i, 0, 0)),
                   pl.BlockSpec((1, G_HEADS, G_KEY_DIM, G_VAL_DIM), lambda i, j: (i, 0, 0, 0))],
        scratch_shapes=[pltpu.VMEM((cb + 8, CONV_CH), F32),
                        pltpu.VMEM((cb, CONV_CH), F32),
                        pltpu.VMEM((G_HEADS, G_KEY_DIM, G_VAL_DIM), F32),
                        pltpu.VMEM((G_HEADS, cb, G_VAL_DIM), F32),
                        pltpu.VMEM((2, G_HEADS, cb, G_VAL_DIM), F32),
                        pltpu.VMEM((2, G_HEADS, cb, G_KEY_DIM), BF16),
                        pltpu.VMEM((2, G_HEADS, cb, cb), BF16),
                        pltpu.VMEM((2, G_HEADS, cb, G_KEY_DIM), BF16),
                        pltpu.VMEM((2, G_HEADS, cb, G_KEY_DIM), BF16),
                        pltpu.VMEM((2, G_HEADS, nch, 8, LANES), F32),
                        pltpu.VMEM((2, cb, G_WIDTH), F32)],
        compiler_params=_cparams(2),
        name="gated_delta",
    )(gqkv, misc, gz, conv_w, hist, s0, gnorm_w, a_lane, d_lane)


def _pack_w_in(w_in):
    d = w_in.shape[0]
    splits = (512, 256, 256, 512, 64, 8, 512, 512, 512, 512, 4, 4)
    offs = np.concatenate([[0], np.cumsum(splits)])
    aq, ak, av, iq, ik, iw, gq, gk, gv, gz, gb, ga = [w_in[:, offs[i]:offs[i + 1]] for i in range(12)]
    zeros64 = jnp.zeros((d, A_HEAD_DIM), w_in.dtype)
    qx = []
    for h in range(A_HEADS):
        qh = aq[:, h * A_HEAD_DIM:(h + 1) * A_HEAD_DIM] * (A_HEAD_DIM ** -0.5 * LOG2E)
        qx += [qh, zeros64] if (h // 2) % 2 == 0 else [zeros64, qh]
    ikx = [ik, zeros64, zeros64, ik]
    misc = [ik, iw, gb, ga, jnp.zeros((d, LANES - MISC_GA - G_HEADS), w_in.dtype)]
    cols = qx + [ak, av, iq * (IDX_DIM ** -0.5)] + ikx + misc + [gq, gk, gv, gz]
    packed = jnp.concatenate(cols, axis=1).astype(BF16)
    assert packed.shape[1] == IN_PACKED
    return packed


def _pack_ffn(w_gate, w_up, w_down):
    d = w_gate.shape[0]
    wg = w_gate.reshape(d, N_FF_CHUNKS, FF_CHUNK)
    wu = w_up.reshape(d, N_FF_CHUNKS, FF_CHUNK)
    wgu = jnp.transpose(jnp.concatenate([wg, wu], axis=2), (1, 0, 2)).astype(BF16)
    wd = w_down.reshape(N_FF_CHUNKS, FF_CHUNK, d).astype(BF16)
    return wgu, wd


def _ikx_layout(ik):
    z = jnp.zeros_like(ik)
    return jnp.concatenate([ik, z, z, ik], axis=-1)


def _lane_vec(vals, lane0):
    return jnp.zeros((1, LANES), F32).at[0, lane0:lane0 + vals.shape[0]].set(vals.astype(F32))


def _run(x, mod, past, layers, rel_bias, norm_final):
    b, t, d = x.shape
    n = b * t
    per_token = t < TOKEN_TILE
    tm = n if per_token else TOKEN_TILE
    tiles_per_batch = None if per_token else t // tm
    tok = dict(tm=tm, tiles_per_batch=tiles_per_batch, per_token=per_token)

    x2 = x.reshape(n, d)
    states = []
    for li, lw in enumerate(layers):
        m = mod[li]

        def mvec(kidx):
            row = m[:, kidx]
            return jnp.repeat(row, t, axis=0) if per_token else row[:, None, :]

        sh1, sc1, gt1, sh2, sc2, gt2, sh3, sc3, gt3 = [mvec(i) for i in range(N_MOD)]
        x2 = _ffn_call(x2, sh1, sc1, gt1, lw["norm_ffn1"], lw["wgu1"], lw["wd1"], None, **tok)
        qx, k, kb, v, vb, iq, ikx, misc, gqkv, gz = _inproj_call(
            x2, sh2, sc2, lw["norm_mix"], lw["w_in"], **tok)

        if past is None:
            offset, l_true, tq = 0, t, ATT_TQ
            k_all, v_all, ikx_all = (a.reshape(b, t, -1) for a in (kb, vb, ikx))
            qx3, iq3, misc3 = (a.reshape(b, t, -1) for a in (qx, iq, misc))
            conv_hist = jnp.zeros((b, CONV_W - 1, CONV_CH), F32)
            s0 = jnp.zeros((b, G_HEADS, G_KEY_DIM, G_VAL_DIM), F32)
        else:
            k_hist, v_hist, ik_hist, conv_hist, s0 = (p[li] for p in past)
            offset = k_hist.shape[1]
            l_true = offset + t
            tq = LANES
            lp = -(-l_true // ATT_KB) * ATT_KB
            pad_k = lambda a: jnp.pad(a, ((0, 0), (0, lp - l_true), (0, 0)))
            k_all = pad_k(jnp.concatenate([k_hist.reshape(b, offset, -1).astype(BF16),
                                           kb.reshape(b, t, -1)], axis=1))
            v_all = pad_k(jnp.concatenate([v_hist.reshape(b, offset, -1).astype(BF16),
                                           vb.reshape(b, t, -1)], axis=1))
            ikx_all = pad_k(jnp.concatenate([_ikx_layout(ik_hist.astype(BF16)),
                                             ikx.reshape(b, t, -1)], axis=1))
            pad_q = lambda a: jnp.pad(a.reshape(b, t, -1), ((0, 0), (0, tq - t), (0, 0)))
            qx3, iq3, misc3 = pad_q(qx), pad_q(iq), pad_q(misc)
            s0 = s0.astype(F32)
            conv_hist = conv_hist.astype(F32)
        lp = k_all.shape[1]
        n_top = min(TOPK_MAX, l_true // 4)
        vt = jnp.transpose(v_all.reshape(b, lp // ATT_KB, ATT_KB, 2, LANES), (0, 1, 3, 4, 2))
        vt = jnp.concatenate([vt, jnp.ones((b, lp // ATT_KB, 2, V_ROWS - LANES, ATT_KB), BF16)], axis=3)
        attn = _attn_call(rel_bias, qx3, iq3, misc3, k_all, vt, ikx_all,
                          tq=tq, l_true=l_true, offset=offset, n_top=n_top)
        attn = attn[:, :t].reshape(n, A_WIDTH)

        cb = GDN_BLOCK if t % GDN_BLOCK == 0 else CHUNK
        t_pad = -(-t // cb) * cb
        pad_t = lambda a: jnp.pad(a.reshape(b, t, -1), ((0, 0), (0, t_pad - t), (0, 0)))
        gdn, conv_new, s_new = _gdn_call(pad_t(gqkv), pad_t(misc), pad_t(gz), lw["conv_w"], conv_hist, s0,
                                         lw["gnorm_w"], lw["a_lane"], lw["d_lane"], cb=cb, t_true=t)
        gdn = gdn[:, :t].reshape(n, G_WIDTH)

        x2 = _outproj_call(x2, gt2, attn, gdn, lw["wo_a"], lw["wo_g"], **tok)
        nf = norm_final if li == len(layers) - 1 else None
        x2 = _ffn_call(x2, sh3, sc3, gt3, lw["norm_ffn2"], lw["wgu2"], lw["wd2"], nf, **tok)

        states.append((k.reshape(b, t, A_KV_HEADS, A_HEAD_DIM), v.reshape(b, t, A_KV_HEADS, A_HEAD_DIM),
                       misc.reshape(b, t, LANES)[..., :IDX_DIM], conv_new, s_new))
    stacked = [jnp.stack(s, axis=0) for s in zip(*states)]
    return x2.reshape(b, t, d), stacked


def kernel(x_prompt, x_sample, cache_k, cache_v, cache_idx_k, state_conv, state_delta, c_prompt, c_sample,
           w_mod, b_mod, norm_ffn1, norm_mix, norm_ffn2, ffn1_w_gate, ffn1_w_up, ffn1_w_down,
           ffn2_w_gate, ffn2_w_up, ffn2_w_down, w_in, w_out, rel_bias, conv_w, a_log, dt_bias, gnorm_w,
           norm_final):
    depth = w_mod.shape[0]
    bp = c_prompt.shape[0]
    c_all = jnp.concatenate([c_prompt, c_sample], axis=0)
    layers, mods_p, mods_s = [], [], []
    for l in range(depth):
        mod = _mod_call(c_all, w_mod[l], b_mod[l]).reshape(c_all.shape[0], N_MOD, D_MODEL)
        mods_p.append(mod[:bp])
        mods_s.append(mod[bp:])
        wgu1, wd1 = _pack_ffn(ffn1_w_gate[l], ffn1_w_up[l], ffn1_w_down[l])
        wgu2, wd2 = _pack_ffn(ffn2_w_gate[l], ffn2_w_up[l], ffn2_w_down[l])
        wo = w_out[l].astype(BF16)
        layers.append(dict(
            norm_ffn1=norm_ffn1[l].reshape(1, -1), norm_mix=norm_mix[l].reshape(1, -1),
            norm_ffn2=norm_ffn2[l].reshape(1, -1), wgu1=wgu1, wd1=wd1, wgu2=wgu2, wd2=wd2,
            w_in=_pack_w_in(w_in[l]), wo_a=wo[:A_WIDTH], wo_g=wo[A_WIDTH:],
            conv_w=conv_w[l], gnorm_w=gnorm_w[l].reshape(1, -1),
            a_lane=_lane_vec(a_log[l], MISC_GA), d_lane=_lane_vec(dt_bias[l], MISC_GA)))
    nf = norm_final.reshape(1, -1)
    y_p, (k_p, v_p, ik_p, conv_p, delta_p) = _run(x_prompt, mods_p, None, layers, rel_bias, nf)
    past = (cache_k, cache_v, cache_idx_k, state_conv, state_delta)
    y_s, (k_s, v_s, ik_s, conv_s, delta_s) = _run(x_sample, mods_s, past, layers, rel_bias, nf)
    return (y_p, y_s, k_p, v_p, ik_p, conv_p, delta_p, k_s, v_s, ik_s, conv_s, delta_s)
```

```python
import functools

import jax
import jax.numpy as jnp
import numpy as np
from jax import lax
from jax.experimental import pallas as pl
from jax.experimental.pallas import tpu as pltpu

F32 = jnp.float32
BF16 = jnp.bfloat16

D_MODEL = 1024
CHUNK = 64
A_HEAD_DIM = 64
A_HEADS = 8
A_KV_HEADS = 4
A_WIDTH = A_HEADS * A_HEAD_DIM
IDX_HEADS = 8
IDX_DIM = 64
TOPK_MAX = 256
REL_BUCKETS = 32
G_KEY_DIM = 128
G_VAL_DIM = 128
G_HEADS = 4
G_WIDTH = G_HEADS * G_VAL_DIM
CONV_W = 4
CONV_CH = 2 * G_HEADS * G_KEY_DIM + G_HEADS * G_VAL_DIM
D_FF = 2816
N_MOD = 9
EPS = 1e-6

LANES = 128
MXU_DIM = 256
VMEM_LIMIT_BYTES = 56 * 1024 * 1024

FF_CHUNK = MXU_DIM
N_FF_CHUNKS = D_FF // FF_CHUNK
TOKEN_TILE = 512
ATT_TQ = 256
ATT_KB = 256
GDN_BLOCK = 256

QX_W = A_HEADS * LANES
OFF_QX = 0
OFF_K = OFF_QX + QX_W
OFF_V = OFF_K + A_KV_HEADS * A_HEAD_DIM
OFF_IQ = OFF_V + A_KV_HEADS * A_HEAD_DIM
OFF_IKX = OFF_IQ + IDX_HEADS * IDX_DIM
OFF_MISC = OFF_IKX + 2 * LANES
OFF_GQKV = OFF_MISC + LANES
OFF_GZ = OFF_GQKV + CONV_CH
IN_PACKED = OFF_GZ + G_WIDTH
MISC_IW = IDX_DIM
MISC_GB = MISC_IW + IDX_HEADS
MISC_GA = MISC_GB + G_HEADS

NEG_BIG = -1e30
INT_MIN = -2 ** 31
LOG2E = 1.4426950408889634
V_ROWS = LANES + 16


def _cparams(n_axes):
    return pltpu.CompilerParams(dimension_semantics=("arbitrary",) * n_axes,
                                vmem_limit_bytes=VMEM_LIMIT_BYTES)


def _resident(shape):
    nd = len(shape)
    return pl.BlockSpec(shape, lambda *_: (0,) * nd, pipeline_mode=pl.Buffered(1))


def _dot_nt(a, b):
    return lax.dot_general(a, b, (((1,), (1,)), ((), ())), preferred_element_type=F32)


def _rms_mod(x, gain, shift, scale):
    ms = jnp.mean(x * x, axis=-1, keepdims=True)
    y = x * lax.rsqrt(ms + EPS) * gain
    return y * (1.0 + scale) + shift


def _silu(x):
    return x * jax.nn.sigmoid(x)


def _tree_sum(parts):
    while len(parts) > 1:
        parts = [a + b for a, b in zip(parts[0::2], parts[1::2])] + ([parts[-1]] if len(parts) % 2 else [])
    return parts[0]


def _mod_kernel(c_ref, w_ref, b_ref, o_ref):
    s = _silu(c_ref[...]).astype(BF16)
    o_ref[...] = jnp.dot(s, w_ref[...].astype(BF16), preferred_element_type=F32) + b_ref[...]


def _mod_call(c, w_mod, b_mod):
    rows, d = c.shape
    n = w_mod.shape[1]
    tn = D_MODEL
    return pl.pallas_call(
        _mod_kernel,
        out_shape=jax.ShapeDtypeStruct((rows, n), F32),
        grid=(n // tn,),
        in_specs=[pl.BlockSpec((rows, d), lambda j: (0, 0)),
                  pl.BlockSpec((d, tn), lambda j: (0, j)),
                  pl.BlockSpec((1, tn), lambda j: (0, j))],
        out_specs=pl.BlockSpec((rows, tn), lambda j: (0, j)),
        compiler_params=_cparams(1),
        name="mod",
    )(c, w_mod, b_mod.reshape(1, n))


def _mod_specs(per_token, tm, tiles_per_batch):
    if per_token:
        return pl.BlockSpec((tm, D_MODEL), lambda i: (i, 0))
    return pl.BlockSpec((None, 1, D_MODEL), lambda i: (i // tiles_per_batch, 0, 0))


def _ffn_kernel(x_ref, sh_ref, sc_ref, gt_ref, gain_ref, wgu_ref, wd_ref, *rest, final_norm, mixer):
    rest = list(rest)
    if mixer:
        gm_ref, a_ref, g_ref, wa_ref, wg_ref = rest[:5]
        rest = rest[5:]
    if final_norm:
        nf_ref = rest.pop(0)
    o_ref, acc_ref = rest
    x = x_ref[...]
    if mixer:
        x = x + gm_ref[...] * (jnp.dot(a_ref[...], wa_ref[...], preferred_element_type=F32)
                               + jnp.dot(g_ref[...], wg_ref[...], preferred_element_type=F32))
    h = _rms_mod(x, gain_ref[...], sh_ref[...], sc_ref[...]).astype(BF16)
    for j in range(N_FF_CHUNKS):
        ab = jnp.dot(h, wgu_ref[j], preferred_element_type=F32)
        g = (_silu(ab[:, :FF_CHUNK]) * ab[:, FF_CHUNK:]).astype(BF16)
        d = jnp.dot(g, wd_ref[j], preferred_element_type=F32)
        if j == 0:
            acc_ref[...] = d
        else:
            acc_ref[...] += d
    y = x + 0.5 * gt_ref[...] * acc_ref[...]
    if final_norm:
        ms = jnp.mean(y * y, axis=-1, keepdims=True)
        y = y * lax.rsqrt(ms + EPS) * nf_ref[...]
    o_ref[...] = y


def _ffn_call(x2, sh, sc, gt, gain, wgu, wd, norm_final, mixer=None, *, tm, tiles_per_batch, per_token):
    n = x2.shape[0]
    mspec = _mod_specs(per_token, tm, tiles_per_batch)
    in_specs = [pl.BlockSpec((tm, D_MODEL), lambda i: (i, 0)), mspec, mspec, mspec,
                _resident((1, D_MODEL)), _resident(wgu.shape), _resident(wd.shape)]
    args = [x2, sh, sc, gt, gain, wgu, wd]
    if mixer is not None:
        gm, attn, gdn, wa, wg = mixer
        in_specs += [mspec, pl.BlockSpec((tm, A_WIDTH), lambda i: (i, 0)),
                     pl.BlockSpec((tm, G_WIDTH), lambda i: (i, 0)), _resident(wa.shape), _resident(wg.shape)]
        args += [gm, attn, gdn, wa, wg]
    final_norm = norm_final is not None
    if final_norm:
        in_specs.append(_resident((1, D_MODEL)))
        args.append(norm_final)
    return pl.pallas_call(
        functools.partial(_ffn_kernel, final_norm=final_norm, mixer=mixer is not None),
        out_shape=jax.ShapeDtypeStruct((n, D_MODEL), F32),
        grid=(n // tm,),
        in_specs=in_specs,
        out_specs=pl.BlockSpec((tm, D_MODEL), lambda i: (i, 0)),
        scratch_shapes=[pltpu.VMEM((tm, D_MODEL), F32)],
        compiler_params=_cparams(1),
        name="ffn_final" if final_norm else "ffn",
    )(*args)


def _inproj_kernel(x_ref, sh_ref, sc_ref, gain_ref, w_ref,
                   qx_o, k_o, kb_o, v_o, vb_o, iq_o, ikx_o, misc_o, gqkv_o, gz_o):
    h = _rms_mod(x_ref[...], gain_ref[...], sh_ref[...], sc_ref[...]).astype(BF16)

    def mm(off, width):
        return jnp.dot(h, w_ref[:, off:off + width], preferred_element_type=F32)

    qx_o[...] = mm(OFF_QX, QX_W).astype(BF16)
    k = mm(OFF_K, OFF_V - OFF_K)
    k_o[...] = k
    kb_o[...] = k.astype(BF16)
    v = mm(OFF_V, OFF_IQ - OFF_V)
    v_o[...] = v
    vb_o[...] = v.astype(BF16)
    iq_o[...] = mm(OFF_IQ, OFF_IKX - OFF_IQ).astype(BF16)
    ikx_o[...] = mm(OFF_IKX, OFF_MISC - OFF_IKX).astype(BF16)
    misc_o[...] = mm(OFF_MISC, LANES)
    gqkv_o[...] = mm(OFF_GQKV, CONV_CH)
    gz_o[...] = mm(OFF_GZ, G_WIDTH)


def _inproj_call(x2, sh, sc, gain, w_packed, *, tm, tiles_per_batch, per_token):
    n = x2.shape[0]
    mspec = _mod_specs(per_token, tm, tiles_per_batch)
    widths = [(QX_W, BF16), (256, F32), (256, BF16), (256, F32), (256, BF16), (512, BF16),
              (2 * LANES, BF16), (LANES, F32), (CONV_CH, F32), (G_WIDTH, F32)]
    return pl.pallas_call(
        _inproj_kernel,
        out_shape=[jax.ShapeDtypeStruct((n, w), dt) for w, dt in widths],
        grid=(n // tm,),
        in_specs=[pl.BlockSpec((tm, D_MODEL), lambda i: (i, 0)), mspec, mspec,
                  _resident((1, D_MODEL)), _resident(w_packed.shape)],
        out_specs=[pl.BlockSpec((tm, w), lambda i: (i, 0)) for w, _ in widths],
        compiler_params=_cparams(1),
        name="inproj",
    )(x2, sh, sc, gain, w_packed)


def _rel_bucket_int(rel):
    n = jnp.abs(rel)
    large = jnp.full(rel.shape, 8, jnp.int32)
    for th in (12, 16, 23, 32, 46, 64, 91):
        large = large + jnp.where(n >= th, 1, 0)
    return jnp.where(rel > 0, REL_BUCKETS // 2, 0) + jnp.where(n < 8, n, large)


def _key_to_float(t):
    bits = jnp.where(t >= 0, t, t ^ jnp.int32(0x7FFFFFFF))
    return lax.bitcast_convert_type(bits, F32)


def _top16(x):
    bits = lax.bitcast_convert_type(x, jnp.int32) & jnp.int32(-65536)
    return lax.bitcast_convert_type(bits, F32).astype(BF16)


def _attn_kernel(relb_ref, qx_ref, iq_ref, misc_ref, k_ref, vt_ref, ikx_ref, o_ref,
                 s_ref, shi_ref, sel_ref, bias_ref, lg_ref, cm_ref, acc_ref, cnt_ref, cnth_ref,
                 *, tq, l_true, offset, n_top):
    kb_sz = ATT_KB
    first = (pl.program_id(0) == 0) & (pl.program_id(1) == 0)
    q0 = offset + pl.program_id(1) * tq
    kmax = jnp.minimum(q0 + tq, l_true)
    nkb = (kmax + kb_sz - 1) // kb_sz

    key_off = lax.broadcasted_iota(jnp.int32, (kb_sz, tq), 0)
    qry_off = lax.broadcasted_iota(jnp.int32, (kb_sz, tq), 1)

    @pl.when(first)
    def _():
        for d in range(3):
            bucket = _rel_bucket_int(key_off - qry_off - d * kb_sz)
            for h in range(A_HEADS):
                t = jnp.zeros((kb_sz, tq), F32)
                for b in range(REL_BUCKETS):
                    t = jnp.where(bucket == b, relb_ref[b, h], t)
                bias_ref[d, h] = (t * LOG2E).astype(BF16)

    qpos = q0 + lax.broadcasted_iota(jnp.int32, (1, tq), 1)
    limit = jnp.minimum((qpos // CHUNK + 1) * CHUNK, l_true)

    iw_t = misc_ref[0].T[MISC_IW:MISC_IW + IDX_HEADS, :] * (IDX_HEADS ** -0.5)

    def score_body(kb, carry):
        base = pl.multiple_of(kb * kb_sz, kb_sz)
        ik2 = ikx_ref[0, pl.ds(base, kb_sz), :]
        ik_lo = ik2[:, :LANES]
        ik_hi = ik2[:, LANES:]
        s = jnp.zeros((kb_sz, tq), F32)
        for j in range(IDX_HEADS // 2):
            slab = iq_ref[0, :, j * LANES:(j + 1) * LANES]
            d0 = _dot_nt(ik_lo, slab)
            d1 = _dot_nt(ik_hi, slab)
            s = s + iw_t[2 * j:2 * j + 1, :] * jnp.maximum(d0, 0.0)
            s = s + iw_t[2 * j + 1:2 * j + 2, :] * jnp.maximum(d1, 0.0)
        s = jnp.where(base + key_off < limit, s, -jnp.inf)
        s_ref[pl.ds(base, kb_sz), :] = s
        shi_ref[pl.ds(base, kb_sz), :] = _top16(s)
        return carry

    def for_each_block(body):
        def quad(i, carry):
            for r in range(4):
                body(4 * i + r, carry)
            return carry
        lax.fori_loop(0, nkb // 4, quad, 0)
        done = (nkb // 4) * 4

        @pl.when(nkb % 4 >= 2)
        def _():
            body(done, 0)
            body(done + 1, 0)

        @pl.when(nkb % 2 == 1)
        def _():
            body(nkb - 1, 0)

    for_each_block(score_body)

    def sum_blocks(contrib, acc_ref):
        acc_ref[...] = jnp.zeros(acc_ref.shape, acc_ref.dtype)

        def quad(i, carry):
            acc_ref[...] += _tree_sum([contrib(4 * i + r) for r in range(4)])
            return carry
        lax.fori_loop(0, nkb // 4, quad, 0)
        done = (nkb // 4) * 4

        @pl.when(nkb % 4 >= 2)
        def _():
            acc_ref[...] += contrib(done) + contrib(done + 1)

        @pl.when(nkb % 2 == 1)
        def _():
            acc_ref[...] += contrib(nkb - 1)
        return acc_ref[...]

    def count(pred):
        def contrib(kb):
            base = pl.multiple_of(kb * kb_sz, kb_sz)
            hit = pred(s_ref[pl.ds(base, kb_sz), :], base + key_off)
            return _tree_sum([hit[r:r + 8] for r in range(0, kb_sz, 8)])
        return jnp.sum(sum_blocks(contrib, cnt_ref), axis=0, keepdims=True)

    def count_hi(thr_hi):
        one = jnp.ones((), BF16)
        zero = jnp.zeros((), BF16)

        def contrib(kb):
            base = pl.multiple_of(kb * kb_sz, kb_sz)
            hit = jnp.where(shi_ref[pl.ds(base, kb_sz), :] >= thr_hi, one, zero)
            return _tree_sum([hit[r:r + 16] for r in range(0, kb_sz, 16)])
        return jnp.sum(sum_blocks(contrib, cnth_ref).astype(F32), axis=0, keepdims=True)

    def hi_body(i, t):
        cand = t + lax.shift_left(jnp.int32(1), 31 - i)
        c = count_hi(_top16(_key_to_float(cand)))
        return jnp.where(c >= n_top, cand, t)

    t_key = lax.fori_loop(0, 16, hi_body, jnp.full((1, tq), INT_MIN, jnp.int32))

    def lo_body(i, t):
        cand = t + lax.shift_left(jnp.int32(1), 15 - i)
        thr_c = _key_to_float(cand)
        c = count(lambda blk, _: jnp.where(blk >= thr_c, 1.0, 0.0))
        return jnp.where(c >= n_top, cand, t)

    t_key = lax.fori_loop(0, 16, lo_body, t_key)
    thr = _key_to_float(t_key)

    need = n_top - count(lambda blk, _: jnp.where(blk > thr, 1.0, 0.0))
    n_eq = count(lambda blk, _: jnp.where(blk == thr, 1.0, 0.0))
    take_all = limit <= n_top
    excess = jnp.where(take_all, 0.0, jnp.where(n_eq > need, 1.0, 0.0))
    any_excess = jnp.max(excess) > 0.0
    idx_bits = 14

    def tie_body(i, c):
        cand = c + lax.shift_left(jnp.int32(1), idx_bits - 1 - i)
        f = count(lambda blk, kidx: jnp.where(blk == thr, jnp.where(kidx < cand, 1.0, 0.0), 0.0))
        return jnp.where(f <= need, cand, c)

    cut0 = jnp.where(any_excess, jnp.zeros((1, tq), jnp.int32),
                     jnp.full((1, tq), 2 ** idx_bits, jnp.int32))
    cut = lax.fori_loop(0, jnp.where(any_excess, idx_bits, 0), tie_body, cut0)
    thr = jnp.where(take_all, -jnp.inf, thr)
    cut = jnp.where(take_all, limit, cut)

    def mask_body(kb, carry):
        base = pl.multiple_of(kb * kb_sz, kb_sz)
        blk = s_ref[pl.ds(base, kb_sz), :]
        tie = jnp.where(base + key_off < cut, 0.0, NEG_BIG)
        sel = jnp.where(blk > thr, 0.0, jnp.where(blk == thr, tie, NEG_BIG))
        sel_ref[pl.ds(base, kb_sz), :] = sel.astype(BF16)
        return carry

    lax.fori_loop(0, nkb, mask_body, 0)

    cm_ref[...] = jnp.full(cm_ref.shape, NEG_BIG, BF16)

    def logit_body(kb, carry):
        base = pl.multiple_of(kb * kb_sz, kb_sz)
        dsel = jnp.clip((q0 - base) // kb_sz, 0, 2)
        kblk = k_ref[0, pl.ds(base, kb_sz), :]
        sel = sel_ref[pl.ds(base, kb_sz), :]
        for h in range(A_HEADS):
            sl = h // 4
            lg = _dot_nt(kblk[:, sl * LANES:(sl + 1) * LANES], qx_ref[0, :, h * LANES:(h + 1) * LANES])
            lg = lg.astype(BF16) + sel + bias_ref[dsel, h]
            lg_ref[h, pl.ds(base, kb_sz), :] = lg
            cm_ref[h] = jnp.maximum(cm_ref[h], lg)
        return carry

    for_each_block(logit_body)

    acc_ref[...] = jnp.zeros(acc_ref.shape, F32)
    m_rows = [jnp.max(cm_ref[h].astype(F32), axis=0, keepdims=True).astype(BF16) for h in range(A_HEADS)]

    def pv_body(kb, carry):
        base = pl.multiple_of(kb * kb_sz, kb_sz)
        for h in range(A_HEADS):
            p = jnp.exp2(lg_ref[h, pl.ds(base, kb_sz), :] - m_rows[h])
            acc_ref[h] += jnp.dot(vt_ref[0, kb, h // 4], p, preferred_element_type=F32)
        return carry

    for_each_block(pv_body)

    outs = []
    for h in range(A_HEADS):
        pos = (h // 2) % 2
        outs.append(acc_ref[h, pos * A_HEAD_DIM:(pos + 1) * A_HEAD_DIM, :] / acc_ref[h, LANES:LANES + 1, :])
    o_ref[0] = jnp.concatenate(outs, axis=0).T.astype(BF16)


def _attn_call(rel_bias, qx, iq, misc, kb, vt, ikx, *, tq, l_true, offset, n_top):
    b, tq_total, _ = qx.shape
    lp = kb.shape[1]
    nq = tq_total // tq
    assert offset % ATT_KB == 0 and tq % CHUNK == 0 and ATT_KB % tq == 0 and lp % ATT_KB == 0
    assert lp < 2 ** 14 and l_true <= lp
    kern = functools.partial(_attn_kernel, tq=tq, l_true=l_true, offset=offset, n_top=n_top)
    return pl.pallas_call(
        kern,
        out_shape=jax.ShapeDtypeStruct((b, tq_total, A_WIDTH), BF16),
        grid=(b, nq),
        in_specs=[pl.BlockSpec(memory_space=pltpu.SMEM),
                  pl.BlockSpec((1, tq, QX_W), lambda i, j: (i, j, 0)),
                  pl.BlockSpec((1, tq, IDX_HEADS * IDX_DIM), lambda i, j: (i, j, 0)),
                  pl.BlockSpec((1, tq, LANES), lambda i, j: (i, j, 0)),
                  pl.BlockSpec((1, lp, 256), lambda i, j: (i, 0, 0)),
                  pl.BlockSpec((1, lp // ATT_KB, 2, V_ROWS, ATT_KB), lambda i, j: (i, 0, 0, 0, 0)),
                  pl.BlockSpec((1, lp, 2 * LANES), lambda i, j: (i, 0, 0))],
        out_specs=pl.BlockSpec((1, tq, A_WIDTH), lambda i, j: (i, j, 0)),
        scratch_shapes=[pltpu.VMEM((lp, tq), F32),
                        pltpu.VMEM((lp, tq), BF16),
                        pltpu.VMEM((lp, tq), BF16),
                        pltpu.VMEM((3, A_HEADS, ATT_KB, tq), BF16),
                        pltpu.VMEM((A_HEADS, lp, tq), BF16),
                        pltpu.VMEM((A_HEADS, ATT_KB, tq), BF16),
                        pltpu.VMEM((A_HEADS, V_ROWS, tq), F32),
                        pltpu.VMEM((8, tq), F32),
                        pltpu.VMEM((16, tq), BF16)],
        compiler_params=_cparams(2),
        name="sparse_attn",
    )(rel_bias, qx, iq, misc, kb, vt, ikx)


def _split_bf16(a, n):
    parts = []
    r = a
    for i in range(n):
        p = r.astype(BF16)
        parts.append(p)
        if i + 1 < n:
            r = r - p.astype(F32)
    return parts


def _bdot(a, b, dims=(((1,), (0,)), ((), ()))):
    return lax.dot_general(a.astype(BF16), b.astype(BF16), dims, preferred_element_type=F32)


_NT = (((1,), (1,)), ((), ()))
_TN = (((0,), (0,)), ((), ()))


def _gdn_kernel(x_ref, misc_ref, gz_ref, convw_ref, hist_ref, s0_ref, gnw_ref, alane_ref, dlane_ref,
                o_ref, conv_o, s_o, xp_ref, y_ref, st_ref, vn_ref,
                u_ref, w_ref, qk_ref, qe_ref, kd_ref, el_ref, gate_ref, *, cb, t_true, nblk):
    c = CHUNK
    step = pl.program_id(1)
    blk = jnp.minimum(step, nblk - 1)
    carried = (u_ref, w_ref, qk_ref, qe_ref, kd_ref, el_ref, gate_ref)
    u_nx, w_nx, qk_nx, qe_nx, kd_nx, el_nx, gate_nx = (r.at[1] for r in carried)
    u_ref, w_ref, qk_ref, qe_ref, kd_ref, el_ref, gate_ref = (r.at[0] for r in carried)

    @pl.when(step == 0)
    def _():
        xp_ref[8 - (CONV_W - 1):8, :] = hist_ref[0]
        st_ref[...] = s0_ref[0]
        for r in carried:
            r[...] = jnp.zeros(r.shape, r.dtype)

    heads = range(G_HEADS)
    for r in carried:
        r[0] = r[1]

    def scan_stages():
        vn_ref[...] = jnp.zeros(vn_ref.shape, F32)
        for ck in range(cb // c):
            r0, r1 = ck * c, (ck + 1) * c
            s_prev = [st_ref[h] for h in heads]
            v_new = [u_ref[h, r0:r1, :] - _bdot(w_ref[h, r0:r1, :], s_prev[h]) for h in heads]
            o_state = [_bdot(qe_ref[h, r0:r1, :], s_prev[h]) for h in heads]
            yield
            for h in heads:
                vn_ref[h, r0:r1, :] = v_new[h]
            o = [o_state[h] + _bdot(qk_ref[h, r0:r1, :], vn_ref[h]) for h in heads]
            for h in heads:
                s_new = s_prev[h] * el_ref[h, ck, 0:1, :] + _bdot(kd_ref[h, r0:r1, :], v_new[h], _TN)
                st_ref[h] = jnp.where(step > 0, s_new, s_prev[h])
            yield
            for h in heads:
                on = o[h] * lax.rsqrt(jnp.mean(o[h] * o[h], axis=-1, keepdims=True) + EPS) * gnw_ref[...]
                gate = gate_ref[r0:r1, h * G_VAL_DIM:(h + 1) * G_VAL_DIM]
                o_ref[0, r0:r1, h * G_VAL_DIM:(h + 1) * G_VAL_DIM] = (on * gate).astype(BF16)
            yield
        s_o[0] = st_ref[...]

    scan = scan_stages()

    def tick():
        next(scan, None)

    tick()

    xp_ref[8:8 + cb, :] = x_ref[0]
    y = convw_ref[CONV_W - 1:CONV_W, :] * xp_ref[8:8 + cb, :]
    for j in range(CONV_W - 1):
        y = y + convw_ref[j:j + 1, :] * xp_ref[5 + j:5 + j + cb, :]
    y_ref[...] = _silu(y)

    last_row = (t_true - 1) % cb
    conv_o[0] = xp_ref[8 + last_row - (CONV_W - 2):8 + last_row + 1, :]

    xp_ref[8 - (CONV_W - 1):8, :] = xp_ref[8 + cb - (CONV_W - 1):8 + cb, :]

    ri = lax.broadcasted_iota(jnp.int32, (cb, cb), 0)
    ci = lax.broadcasted_iota(jnp.int32, (cb, cb), 1)
    lag = jnp.where((ri // c) == (ci // c), ri - ci, -1)
    tri = lag >= 0
    strict = lag > 0
    eye = ri == ci
    tril_bf = jnp.where(tri, 1.0, 0.0).astype(BF16)
    eye_f = jnp.where(eye, 1.0, 0.0)

    ms = misc_ref[0]
    tok = blk * cb + lax.broadcasted_iota(jnp.int32, (cb, 1), 0)
    live = tok < t_true
    beta_s = jnp.where(live, jax.nn.sigmoid(ms), 0.0)
    z = ms + dlane_ref[...]
    softplus = jnp.maximum(z, 0.0) + jnp.log(1.0 + jnp.exp(-jnp.abs(z)))
    g_s = jnp.where(live, -jnp.exp(alane_ref[...]) * softplus, 0.0)
    gc_s = None
    for piece in _split_bf16(g_s, 3):
        t = jnp.dot(tril_bf, piece, preferred_element_type=F32)
        gc_s = t if gc_s is None else gc_s + t

    tick()

    q, k, v, beta, gc, decay, kb, eg = ([None] * G_HEADS for _ in range(8))
    for h in heads:
        qh = y_ref[:, h * G_KEY_DIM:(h + 1) * G_KEY_DIM]
        kh = y_ref[:, G_WIDTH + h * G_KEY_DIM:G_WIDTH + (h + 1) * G_KEY_DIM]
        v[h] = y_ref[:, 2 * G_WIDTH + h * G_VAL_DIM:2 * G_WIDTH + (h + 1) * G_VAL_DIM]
        q[h] = qh * lax.rsqrt(jnp.sum(qh * qh, axis=-1, keepdims=True) + 1e-6) * (G_KEY_DIM ** -0.5)
        k[h] = kh * lax.rsqrt(jnp.sum(kh * kh, axis=-1, keepdims=True) + 1e-6)
        beta[h] = beta_s[:, MISC_GB + h:MISC_GB + h + 1]
        gc[h] = gc_s[:, MISC_GA + h:MISC_GA + h + 1]
        gc_b = jnp.broadcast_to(gc[h], (cb, cb))
        gc_row = jnp.sum(jnp.where(eye, gc_b, 0.0), axis=0, keepdims=True)
        decay[h] = jnp.exp(jnp.where(tri, gc_b - gc_row, NEG_BIG))
        kb[h] = k[h] * beta[h]
        eg[h] = jnp.exp(gc[h])
    tick()
    m = [jnp.where(strict, _bdot(kb[h], k[h], _NT) * decay[h], 0.0) for h in heads]
    rhs = [jnp.concatenate([v[h] * beta[h], kb[h] * eg[h]], axis=1) for h in heads]
    qk = [jnp.where(tri, _bdot(q[h], k[h], _NT) * decay[h], 0.0) for h in heads]
    tick()
    pw = [-m[h] for h in heads]
    inv = [eye_f + pw[h] for h in heads]
    for _ in range(5):
        pw = [_bdot(pw[h], pw[h]) for h in heads]
        tick()
        inv = [inv[h] + _bdot(inv[h], pw[h]) for h in heads]
    tick()
    sol = [_bdot(inv[h], rhs[h]) for h in heads]
    tick()
    m_sol = []
    for h in heads:
        m_hi, m_lo = _split_bf16(m[h], 2)
        s_hi, s_lo = _split_bf16(sol[h], 2)
        m_sol.append(_bdot(m_hi, s_hi) + _bdot(m_hi, s_lo) + _bdot(m_lo, s_hi))
    tick()
    sol = [sol[h] + _bdot(inv[h], rhs[h] - sol[h] - m_sol[h]) for h in heads]
    for _ in scan:
        pass
    gate_nx[...] = _silu(gz_ref[0])
    for h in heads:
        u_nx[h] = sol[h][:, :G_VAL_DIM]
        w_nx[h] = sol[h][:, G_VAL_DIM:].astype(BF16)
        qk_nx[h] = qk[h].astype(BF16)
        qe_nx[h] = (q[h] * eg[h]).astype(BF16)
        for ck in range(cb // c):
            r0, r1 = ck * c, (ck + 1) * c
            g_last = gc[h][r1 - 1:r1, :]
            kd_nx[h, r0:r1, :] = (k[h][r0:r1] * jnp.exp(g_last - gc[h][r0:r1])).astype(BF16)
            el_nx[h, ck] = jnp.broadcast_to(jnp.exp(g_last), (8, LANES))


def _gdn_call(gqkv, misc, gz, conv_w, hist, s0, gnorm_w, a_lane, d_lane, *, cb, t_true):
    b, t_pad, _ = gqkv.shape
    nblk = t_pad // cb
    assert (nblk - 1) * cb < t_true <= t_pad and (t_true - 1) % cb >= CONV_W - 2
    kern = functools.partial(_gdn_kernel, cb=cb, t_true=t_true, nblk=nblk)
    prep = lambda i, j: (i, jnp.minimum(j, nblk - 1), 0)
    scan = lambda i, j: (i, jnp.maximum(j - 1, 0), 0)
    nch = cb // CHUNK
    return pl.pallas_call(
        kern,
        out_shape=[jax.ShapeDtypeStruct((b, t_pad, G_WIDTH), BF16),
                   jax.ShapeDtypeStruct((b, CONV_W - 1, CONV_CH), F32),
                   jax.ShapeDtypeStruct((b, G_HEADS, G_KEY_DIM, G_VAL_DIM), F32)],
        grid=(b, nblk + 1),
        in_specs=[pl.BlockSpec((1, cb, CONV_CH), prep),
                  pl.BlockSpec((1, cb, LANES), prep),
                  pl.BlockSpec((1, cb, G_WIDTH), prep),
                  pl.BlockSpec((CONV_W, CONV_CH), lambda i, j: (0, 0)),
                  pl.BlockSpec((1, CONV_W - 1, CONV_CH), lambda i, j: (i, 0, 0)),
                  pl.BlockSpec((1, G_HEADS, G_KEY_DIM, G_VAL_DIM), lambda i, j: (i, 0, 0, 0)),
                  pl.BlockSpec((1, G_VAL_DIM), lambda i, j: (0, 0)),
                  pl.BlockSpec((1, LANES), lambda i, j: (0, 0)),
                  pl.BlockSpec((1, LANES), lambda i, j: (0, 0))],
        out_specs=[pl.BlockSpec((1, cb, G_WIDTH), scan),
                   pl.BlockSpec((1, CONV_W - 1, CONV_CH), lambda i, j: (i, 0, 0)),
                   pl.BlockSpec((1, G_HEADS, G_KEY_DIM, G_VAL_DIM), lambda i, j: (i, 0, 0, 0))],
        scratch_shapes=[pltpu.VMEM((cb + 8, CONV_CH), F32),
                        pltpu.VMEM((cb, CONV_CH), F32),
                        pltpu.VMEM((G_HEADS, G_KEY_DIM, G_VAL_DIM), F32),
                        pltpu.VMEM((G_HEADS, cb, G_VAL_DIM), F32),
                        pltpu.VMEM((2, G_HEADS, cb, G_VAL_DIM), F32),
                        pltpu.VMEM((2, G_HEADS, cb, G_KEY_DIM), BF16),
                        pltpu.VMEM((2, G_HEADS, cb, cb), BF16),
                        pltpu.VMEM((2, G_HEADS, cb, G_KEY_DIM), BF16),
                        pltpu.VMEM((2, G_HEADS, cb, G_KEY_DIM), BF16),
                        pltpu.VMEM((2, G_HEADS, nch, 8, LANES), F32),
                        pltpu.VMEM((2, cb, G_WIDTH), F32)],
        compiler_params=_cparams(2),
        name="gated_delta",
    )(gqkv, misc, gz, conv_w, hist, s0, gnorm_w, a_lane, d_lane)


def _pack_w_in(w_in):
    d = w_in.shape[0]
    splits = (512, 256, 256, 512, 64, 8, 512, 512, 512, 512, 4, 4)
    offs = np.concatenate([[0], np.cumsum(splits)])
    aq, ak, av, iq, ik, iw, gq, gk, gv, gz, gb, ga = [w_in[:, offs[i]:offs[i + 1]] for i in range(12)]
    zeros64 = jnp.zeros((d, A_HEAD_DIM), w_in.dtype)
    qx = []
    for h in range(A_HEADS):
        qh = aq[:, h * A_HEAD_DIM:(h + 1) * A_HEAD_DIM] * (A_HEAD_DIM ** -0.5 * LOG2E)
        qx += [qh, zeros64] if (h // 2) % 2 == 0 else [zeros64, qh]
    ikx = [ik, zeros64, zeros64, ik]
    misc = [ik, iw, gb, ga, jnp.zeros((d, LANES - MISC_GA - G_HEADS), w_in.dtype)]
    cols = qx + [ak, av, iq * (IDX_DIM ** -0.5)] + ikx + misc + [gq, gk, gv, gz]
    packed = jnp.concatenate(cols, axis=1).astype(BF16)
    assert packed.shape[1] == IN_PACKED
    return packed


def _pack_ffn(w_gate, w_up, w_down):
    d = w_gate.shape[0]
    wg = w_gate.reshape(d, N_FF_CHUNKS, FF_CHUNK)
    wu = w_up.reshape(d, N_FF_CHUNKS, FF_CHUNK)
    wgu = jnp.transpose(jnp.concatenate([wg, wu], axis=2), (1, 0, 2)).astype(BF16)
    wd = w_down.reshape(N_FF_CHUNKS, FF_CHUNK, d).astype(BF16)
    return wgu, wd


def _ikx_layout(ik):
    z = jnp.zeros_like(ik)
    return jnp.concatenate([ik, z, z, ik], axis=-1)


def _lane_vec(vals, lane0):
    return jnp.zeros((1, LANES), F32).at[0, lane0:lane0 + vals.shape[0]].set(vals.astype(F32))


def _run(x, mod, past, layers, rel_bias, norm_final):
    b, t, d = x.shape
    n = b * t
    per_token = t < TOKEN_TILE
    tm = n if per_token else TOKEN_TILE
    tiles_per_batch = None if per_token else t // tm
    tok = dict(tm=tm, tiles_per_batch=tiles_per_batch, per_token=per_token)

    x2 = x.reshape(n, d)
    states = []
    for li, lw in enumerate(layers):
        m = mod[li]

        def mvec(kidx):
            row = m[:, kidx]
            return jnp.repeat(row, t, axis=0) if per_token else row[:, None, :]

        sh1, sc1, gt1, sh2, sc2, gt2, sh3, sc3, gt3 = [mvec(i) for i in range(N_MOD)]
        x2 = _ffn_call(x2, sh1, sc1, gt1, lw["norm_ffn1"], lw["wgu1"], lw["wd1"], None, **tok)
        qx, k, kb, v, vb, iq, ikx, misc, gqkv, gz = _inproj_call(
            x2, sh2, sc2, lw["norm_mix"], lw["w_in"], **tok)

        if past is None:
            offset, l_true, tq = 0, t, ATT_TQ
            k_all, v_all, ikx_all = (a.reshape(b, t, -1) for a in (kb, vb, ikx))
            qx3, iq3, misc3 = (a.reshape(b, t, -1) for a in (qx, iq, misc))
            conv_hist = jnp.zeros((b, CONV_W - 1, CONV_CH), F32)
            s0 = jnp.zeros((b, G_HEADS, G_KEY_DIM, G_VAL_DIM), F32)
        else:
            k_hist, v_hist, ik_hist, conv_hist, s0 = (p[li] for p in past)
            offset = k_hist.shape[1]
            l_true = offset + t
            tq = LANES
            lp = -(-l_true // ATT_KB) * ATT_KB
            pad_k = lambda a: jnp.pad(a, ((0, 0), (0, lp - l_true), (0, 0)))
            k_all = pad_k(jnp.concatenate([k_hist.reshape(b, offset, -1).astype(BF16),
                                           kb.reshape(b, t, -1)], axis=1))
            v_all = pad_k(jnp.concatenate([v_hist.reshape(b, offset, -1).astype(BF16),
                                           vb.reshape(b, t, -1)], axis=1))
            ikx_all = pad_k(jnp.concatenate([_ikx_layout(ik_hist.astype(BF16)),
                                             ikx.reshape(b, t, -1)], axis=1))
            pad_q = lambda a: jnp.pad(a.reshape(b, t, -1), ((0, 0), (0, tq - t), (0, 0)))
            qx3, iq3, misc3 = pad_q(qx), pad_q(iq), pad_q(misc)
            s0 = s0.astype(F32)
            conv_hist = conv_hist.astype(F32)
        lp = k_all.shape[1]
        n_top = min(TOPK_MAX, l_true // 4)
        vt = jnp.transpose(v_all.reshape(b, lp // ATT_KB, ATT_KB, 2, LANES), (0, 1, 3, 4, 2))
        vt = jnp.concatenate([vt, jnp.ones((b, lp // ATT_KB, 2, V_ROWS - LANES, ATT_KB), BF16)], axis=3)
        attn = _attn_call(rel_bias, qx3, iq3, misc3, k_all, vt, ikx_all,
                          tq=tq, l_true=l_true, offset=offset, n_top=n_top)
        attn = attn[:, :t].reshape(n, A_WIDTH)

        cb = GDN_BLOCK if t % GDN_BLOCK == 0 else CHUNK
        t_pad = -(-t // cb) * cb
        pad_t = lambda a: jnp.pad(a.reshape(b, t, -1), ((0, 0), (0, t_pad - t), (0, 0)))
        gdn, conv_new, s_new = _gdn_call(pad_t(gqkv), pad_t(misc), pad_t(gz), lw["conv_w"], conv_hist, s0,
                                         lw["gnorm_w"], lw["a_lane"], lw["d_lane"], cb=cb, t_true=t)
        gdn = gdn[:, :t].reshape(n, G_WIDTH)

        nf = norm_final if li == len(layers) - 1 else None
        x2 = _ffn_call(x2, sh3, sc3, gt3, lw["norm_ffn2"], lw["wgu2"], lw["wd2"], nf,
                       mixer=(gt2, attn, gdn, lw["wo_a"], lw["wo_g"]), **tok)

        states.append((k.reshape(b, t, A_KV_HEADS, A_HEAD_DIM), v.reshape(b, t, A_KV_HEADS, A_HEAD_DIM),
                       misc.reshape(b, t, LANES)[..., :IDX_DIM], conv_new, s_new))
    stacked = [jnp.stack(s, axis=0) for s in zip(*states)]
    return x2.reshape(b, t, d), stacked


def kernel(x_prompt, x_sample, cache_k, cache_v, cache_idx_k, state_conv, state_delta, c_prompt, c_sample,
           w_mod, b_mod, norm_ffn1, norm_mix, norm_ffn2, ffn1_w_gate, ffn1_w_up, ffn1_w_down,
           ffn2_w_gate, ffn2_w_up, ffn2_w_down, w_in, w_out, rel_bias, conv_w, a_log, dt_bias, gnorm_w,
           norm_final):
    depth = w_mod.shape[0]
    bp = c_prompt.shape[0]
    c_all = jnp.concatenate([c_prompt, c_sample], axis=0)
    layers, mods_p, mods_s = [], [], []
    for l in range(depth):
        mod = _mod_call(c_all, w_mod[l], b_mod[l]).reshape(c_all.shape[0], N_MOD, D_MODEL)
        mods_p.append(mod[:bp])
        mods_s.append(mod[bp:])
        wgu1, wd1 = _pack_ffn(ffn1_w_gate[l], ffn1_w_up[l], ffn1_w_down[l])
        wgu2, wd2 = _pack_ffn(ffn2_w_gate[l], ffn2_w_up[l], ffn2_w_down[l])
        wo = w_out[l].astype(BF16)
        layers.append(dict(
            norm_ffn1=norm_ffn1[l].reshape(1, -1), norm_mix=norm_mix[l].reshape(1, -1),
            norm_ffn2=norm_ffn2[l].reshape(1, -1), wgu1=wgu1, wd1=wd1, wgu2=wgu2, wd2=wd2,
            w_in=_pack_w_in(w_in[l]), wo_a=wo[:A_WIDTH], wo_g=wo[A_WIDTH:],
            conv_w=conv_w[l], gnorm_w=gnorm_w[l].reshape(1, -1),
            a_lane=_lane_vec(a_log[l], MISC_GA), d_lane=_lane_vec(dt_bias[l], MISC_GA)))
    nf = norm_final.reshape(1, -1)
    y_p, (k_p, v_p, ik_p, conv_p, delta_p) = _run(x_prompt, mods_p, None, layers, rel_bias, nf)
    past = (cache_k, cache_v, cache_idx_k, state_conv, state_delta)
    y_s, (k_s, v_s, ik_s, conv_s, delta_s) = _run(x_sample, mods_s, past, layers, rel_bias, nf)
    return (y_p, y_s, k_p, v_p, ik_p, conv_p, delta_p, k_s, v_s, ik_s, conv_s, delta_s)
```

```python
import functools

import jax
import jax.numpy as jnp
import numpy as np
from jax import lax
from jax.experimental import pallas as pl
from jax.experimental.pallas import tpu as pltpu

F32 = jnp.float32
BF16 = jnp.bfloat16

D_MODEL = 1024
CHUNK = 64
A_HEAD_DIM = 64
A_HEADS = 8
A_KV_HEADS = 4
A_WIDTH = A_HEADS * A_HEAD_DIM
IDX_HEADS = 8
IDX_DIM = 64
TOPK_MAX = 256
REL_BUCKETS = 32
G_KEY_DIM = 128
G_VAL_DIM = 128
G_HEADS = 4
G_WIDTH = G_HEADS * G_VAL_DIM
CONV_W = 4
CONV_CH = 2 * G_HEADS * G_KEY_DIM + G_HEADS * G_VAL_DIM
D_FF = 2816
N_MOD = 9
EPS = 1e-6

LANES = 128
MXU_DIM = 256
VMEM_LIMIT_BYTES = 56 * 1024 * 1024

FF_CHUNK = MXU_DIM
N_FF_CHUNKS = D_FF // FF_CHUNK
TOKEN_TILE = 512
ATT_TQ = 256
ATT_KB = 256
GDN_BLOCK = 256

QX_W = A_HEADS * LANES
OFF_QX = 0
OFF_K = OFF_QX + QX_W
OFF_V = OFF_K + A_KV_HEADS * A_HEAD_DIM
OFF_IQ = OFF_V + A_KV_HEADS * A_HEAD_DIM
OFF_IKX = OFF_IQ + IDX_HEADS * IDX_DIM
OFF_MISC = OFF_IKX + 2 * LANES
OFF_GQKV = OFF_MISC + LANES
OFF_GZ = OFF_GQKV + CONV_CH
IN_PACKED = OFF_GZ + G_WIDTH
MISC_IW = IDX_DIM
MISC_GB = MISC_IW + IDX_HEADS
MISC_GA = MISC_GB + G_HEADS

NEG_BIG = -1e30
INT_MIN = -2 ** 31
LOG2E = 1.4426950408889634
V_ROWS = LANES + 16


def _cparams(n_axes):
    return pltpu.CompilerParams(dimension_semantics=("arbitrary",) * n_axes,
                                vmem_limit_bytes=VMEM_LIMIT_BYTES)


def _resident(shape):
    nd = len(shape)
    return pl.BlockSpec(shape, lambda *_: (0,) * nd, pipeline_mode=pl.Buffered(1))


def _dot_nt(a, b):
    return lax.dot_general(a, b, (((1,), (1,)), ((), ())), preferred_element_type=F32)


def _rms_mod(x, gain, shift, scale):
    ms = jnp.mean(x * x, axis=-1, keepdims=True)
    y = x * lax.rsqrt(ms + EPS) * gain
    return y * (1.0 + scale) + shift


def _silu(x):
    return x * jax.nn.sigmoid(x)


def _tree_sum(parts):
    while len(parts) > 1:
        parts = [a + b for a, b in zip(parts[0::2], parts[1::2])] + ([parts[-1]] if len(parts) % 2 else [])
    return parts[0]


def _mod_kernel(c_ref, w_ref, b_ref, o_ref):
    s = _silu(c_ref[...]).astype(BF16)
    o_ref[...] = jnp.dot(s, w_ref[...].astype(BF16), preferred_element_type=F32) + b_ref[...]


def _mod_call(c, w_mod, b_mod):
    rows, d = c.shape
    n = w_mod.shape[1]
    tn = D_MODEL
    return pl.pallas_call(
        _mod_kernel,
        out_shape=jax.ShapeDtypeStruct((rows, n), F32),
        grid=(n // tn,),
        in_specs=[pl.BlockSpec((rows, d), lambda j: (0, 0)),
                  pl.BlockSpec((d, tn), lambda j: (0, j)),
                  pl.BlockSpec((1, tn), lambda j: (0, j))],
        out_specs=pl.BlockSpec((rows, tn), lambda j: (0, j)),
        compiler_params=_cparams(1),
        name="mod",
    )(c, w_mod, b_mod.reshape(1, n))


def _mod_specs(per_token, tm, tiles_per_batch):
    if per_token:
        return pl.BlockSpec((tm, D_MODEL), lambda i: (i, 0))
    return pl.BlockSpec((None, 1, D_MODEL), lambda i: (i // tiles_per_batch, 0, 0))


def _ffn_kernel(x_ref, sh_ref, sc_ref, gt_ref, gain_ref, wgu_ref, wd_ref, *rest, final_norm, mixer):
    rest = list(rest)
    if mixer:
        gm_ref, a_ref, g_ref, wa_ref, wg_ref = rest[:5]
        rest = rest[5:]
    if final_norm:
        nf_ref = rest.pop(0)
    o_ref, acc_ref = rest
    x = x_ref[...]
    if mixer:
        x = x + gm_ref[...] * (jnp.dot(a_ref[...], wa_ref[...], preferred_element_type=F32)
                               + jnp.dot(g_ref[...], wg_ref[...], preferred_element_type=F32))
    h = _rms_mod(x, gain_ref[...], sh_ref[...], sc_ref[...]).astype(BF16)
    for j in range(N_FF_CHUNKS):
        ab = jnp.dot(h, wgu_ref[j], preferred_element_type=F32)
        g = (_silu(ab[:, :FF_CHUNK]) * ab[:, FF_CHUNK:]).astype(BF16)
        d = jnp.dot(g, wd_ref[j], preferred_element_type=F32)
        if j == 0:
            acc_ref[...] = d
        else:
            acc_ref[...] += d
    y = x + 0.5 * gt_ref[...] * acc_ref[...]
    if final_norm:
        ms = jnp.mean(y * y, axis=-1, keepdims=True)
        y = y * lax.rsqrt(ms + EPS) * nf_ref[...]
    o_ref[...] = y


def _ffn_call(x2, sh, sc, gt, gain, wgu, wd, norm_final, mixer=None, *, tm, tiles_per_batch, per_token):
    n = x2.shape[0]
    mspec = _mod_specs(per_token, tm, tiles_per_batch)
    in_specs = [pl.BlockSpec((tm, D_MODEL), lambda i: (i, 0)), mspec, mspec, mspec,
                _resident((1, D_MODEL)), _resident(wgu.shape), _resident(wd.shape)]
    args = [x2, sh, sc, gt, gain, wgu, wd]
    if mixer is not None:
        gm, attn, gdn, wa, wg = mixer
        in_specs += [mspec, pl.BlockSpec((tm, A_WIDTH), lambda i: (i, 0)),
                     pl.BlockSpec((tm, G_WIDTH), lambda i: (i, 0)), _resident(wa.shape), _resident(wg.shape)]
        args += [gm, attn, gdn, wa, wg]
    final_norm = norm_final is not None
    if final_norm:
        in_specs.append(_resident((1, D_MODEL)))
        args.append(norm_final)
    return pl.pallas_call(
        functools.partial(_ffn_kernel, final_norm=final_norm, mixer=mixer is not None),
        out_shape=jax.ShapeDtypeStruct((n, D_MODEL), F32),
        grid=(n // tm,),
        in_specs=in_specs,
        out_specs=pl.BlockSpec((tm, D_MODEL), lambda i: (i, 0)),
        scratch_shapes=[pltpu.VMEM((tm, D_MODEL), F32)],
        compiler_params=_cparams(1),
        name="ffn_final" if final_norm else "ffn",
    )(*args)


def _inproj_kernel(x_ref, sh_ref, sc_ref, gain_ref, w_ref,
                   qx_o, k_o, kb_o, v_o, vb_o, iq_o, ikx_o, misc_o, gqkv_o, gz_o):
    h = _rms_mod(x_ref[...], gain_ref[...], sh_ref[...], sc_ref[...]).astype(BF16)

    def mm(off, width):
        return jnp.dot(h, w_ref[:, off:off + width], preferred_element_type=F32)

    qx_o[...] = mm(OFF_QX, QX_W).astype(BF16)
    k = mm(OFF_K, OFF_V - OFF_K)
    k_o[...] = k
    kb_o[...] = k.astype(BF16)
    v = mm(OFF_V, OFF_IQ - OFF_V)
    v_o[...] = v
    vb_o[...] = v.astype(BF16)
    iq_o[...] = mm(OFF_IQ, OFF_IKX - OFF_IQ).astype(BF16)
    ikx_o[...] = mm(OFF_IKX, OFF_MISC - OFF_IKX).astype(BF16)
    misc_o[...] = mm(OFF_MISC, LANES)
    gqkv_o[...] = mm(OFF_GQKV, CONV_CH)
    gz_o[...] = mm(OFF_GZ, G_WIDTH)


def _inproj_call(x2, sh, sc, gain, w_packed, *, tm, tiles_per_batch, per_token):
    n = x2.shape[0]
    mspec = _mod_specs(per_token, tm, tiles_per_batch)
    widths = [(QX_W, BF16), (256, F32), (256, BF16), (256, F32), (256, BF16), (512, BF16),
              (2 * LANES, BF16), (LANES, F32), (CONV_CH, F32), (G_WIDTH, F32)]
    return pl.pallas_call(
        _inproj_kernel,
        out_shape=[jax.ShapeDtypeStruct((n, w), dt) for w, dt in widths],
        grid=(n // tm,),
        in_specs=[pl.BlockSpec((tm, D_MODEL), lambda i: (i, 0)), mspec, mspec,
                  _resident((1, D_MODEL)), _resident(w_packed.shape)],
        out_specs=[pl.BlockSpec((tm, w), lambda i: (i, 0)) for w, _ in widths],
        compiler_params=_cparams(1),
        name="inproj",
    )(x2, sh, sc, gain, w_packed)


def _rel_bucket_int(rel):
    n = jnp.abs(rel)
    large = jnp.full(rel.shape, 8, jnp.int32)
    for th in (12, 16, 23, 32, 46, 64, 91):
        large = large + jnp.where(n >= th, 1, 0)
    return jnp.where(rel > 0, REL_BUCKETS // 2, 0) + jnp.where(n < 8, n, large)


def _key_to_float(t):
    bits = jnp.where(t >= 0, t, t ^ jnp.int32(0x7FFFFFFF))
    return lax.bitcast_convert_type(bits, F32)


def _top16(x):
    bits = lax.bitcast_convert_type(x, jnp.int32) & jnp.int32(-65536)
    return lax.bitcast_convert_type(bits, F32).astype(BF16)


def _attn_kernel(relb_ref, qx_ref, iq_ref, misc_ref, k_ref, vt_ref, ikx_ref, o_ref,
                 s_ref, shi_ref, sel_ref, bias_ref, lg_ref, cm_ref, acc_ref, cnt_ref, cnth_ref,
                 *, tq, l_true, offset, n_top):
    kb_sz = ATT_KB
    first = (pl.program_id(0) == 0) & (pl.program_id(1) == 0)
    q0 = offset + pl.program_id(1) * tq
    kmax = jnp.minimum(q0 + tq, l_true)
    nkb = (kmax + kb_sz - 1) // kb_sz

    key_off = lax.broadcasted_iota(jnp.int32, (kb_sz, tq), 0)
    qry_off = lax.broadcasted_iota(jnp.int32, (kb_sz, tq), 1)

    @pl.when(first)
    def _():
        for d in range(3):
            bucket = _rel_bucket_int(key_off - qry_off - d * kb_sz)
            for h in range(A_HEADS):
                t = jnp.zeros((kb_sz, tq), F32)
                for b in range(REL_BUCKETS):
                    t = jnp.where(bucket == b, relb_ref[b, h], t)
                bias_ref[d, h] = (t * LOG2E).astype(BF16)

    qpos = q0 + lax.broadcasted_iota(jnp.int32, (1, tq), 1)
    limit = jnp.minimum((qpos // CHUNK + 1) * CHUNK, l_true)

    iw_t = misc_ref[0].T[MISC_IW:MISC_IW + IDX_HEADS, :] * (IDX_HEADS ** -0.5)

    def score_body(kb, carry):
        base = pl.multiple_of(kb * kb_sz, kb_sz)
        ik2 = ikx_ref[0, pl.ds(base, kb_sz), :]
        ik_lo = ik2[:, :LANES]
        ik_hi = ik2[:, LANES:]
        s = jnp.zeros((kb_sz, tq), F32)
        for j in range(IDX_HEADS // 2):
            slab = iq_ref[0, :, j * LANES:(j + 1) * LANES]
            d0 = _dot_nt(ik_lo, slab)
            d1 = _dot_nt(ik_hi, slab)
            s = s + iw_t[2 * j:2 * j + 1, :] * jnp.maximum(d0, 0.0)
            s = s + iw_t[2 * j + 1:2 * j + 2, :] * jnp.maximum(d1, 0.0)
        s = jnp.where(base + key_off < limit, s, -jnp.inf)
        s_ref[pl.ds(base, kb_sz), :] = s
        shi_ref[pl.ds(base, kb_sz), :] = _top16(s)
        return carry

    def for_each_block(body):
        def quad(i, carry):
            for r in range(4):
                body(4 * i + r, carry)
            return carry
        lax.fori_loop(0, nkb // 4, quad, 0)
        done = (nkb // 4) * 4

        @pl.when(nkb % 4 >= 2)
        def _():
            body(done, 0)
            body(done + 1, 0)

        @pl.when(nkb % 2 == 1)
        def _():
            body(nkb - 1, 0)

    for_each_block(score_body)

    def sum_blocks(contrib, acc_ref, n=nkb):
        acc_ref[...] = jnp.zeros(acc_ref.shape, acc_ref.dtype)

        def quad(i, carry):
            acc_ref[...] += _tree_sum([contrib(4 * i + r) for r in range(4)])
            return carry
        lax.fori_loop(0, n // 4, quad, 0)
        done = (n // 4) * 4

        @pl.when(n % 4 >= 2)
        def _():
            acc_ref[...] += contrib(done) + contrib(done + 1)

        @pl.when(n % 2 == 1)
        def _():
            acc_ref[...] += contrib(n - 1)
        return acc_ref[...]

    def count(pred, n=nkb):
        def contrib(kb):
            base = pl.multiple_of(kb * kb_sz, kb_sz)
            hit = pred(s_ref[pl.ds(base, kb_sz), :], base + key_off)
            return _tree_sum([hit[r:r + 8] for r in range(0, kb_sz, 8)])
        return jnp.sum(sum_blocks(contrib, cnt_ref, n), axis=0, keepdims=True)

    def count_hi(thr_hi):
        one = jnp.ones((), BF16)
        zero = jnp.zeros((), BF16)

        def contrib(kb):
            base = pl.multiple_of(kb * kb_sz, kb_sz)
            hit = jnp.where(shi_ref[pl.ds(base, kb_sz), :] >= thr_hi, one, zero)
            return _tree_sum([hit[r:r + 16] for r in range(0, kb_sz, 16)])
        return jnp.sum(sum_blocks(contrib, cnth_ref).astype(F32), axis=0, keepdims=True)

    def hi_body(i, carry):
        t, c_t = carry
        cand = t + lax.shift_left(jnp.int32(1), 31 - i)
        c = count_hi(_top16(_key_to_float(cand)))
        return jnp.where(c >= n_top, cand, t), jnp.where(c >= n_top, c, c_t)

    def lo_body(i, carry):
        t, c_t = carry
        cand = t + lax.shift_left(jnp.int32(1), 15 - i)
        thr_c = _key_to_float(cand)
        c = count(lambda blk, _: jnp.where(blk >= thr_c, 1.0, 0.0))
        return jnp.where(c >= n_top, cand, t), jnp.where(c >= n_top, c, c_t)

    carry = (jnp.full((1, tq), INT_MIN, jnp.int32), jnp.full((1, tq), float(n_top), F32))
    carry = lax.fori_loop(0, 16, hi_body, carry)
    t_key, n_ge = lax.fori_loop(0, 16, lo_body, carry)
    thr = _key_to_float(t_key)

    take_all = limit <= n_top
    excess = jnp.where(take_all, 0.0, jnp.where(n_ge > n_top, 1.0, 0.0))
    any_excess = jnp.max(excess) > 0.0
    n_tie = jnp.where(any_excess, nkb, 0)
    need = n_top - count(lambda blk, _: jnp.where(blk > thr, 1.0, 0.0), n_tie)
    idx_bits = 14

    def tie_body(i, c):
        cand = c + lax.shift_left(jnp.int32(1), idx_bits - 1 - i)
        f = count(lambda blk, kidx: jnp.where(blk == thr, jnp.where(kidx < cand, 1.0, 0.0), 0.0))
        return jnp.where(f <= need, cand, c)

    cut0 = jnp.where(any_excess, jnp.zeros((1, tq), jnp.int32),
                     jnp.full((1, tq), 2 ** idx_bits, jnp.int32))
    cut = lax.fori_loop(0, jnp.where(any_excess, idx_bits, 0), tie_body, cut0)
    thr = jnp.where(take_all, -jnp.inf, thr)
    cut = jnp.where(take_all, limit, cut)

    def mask_body(kb, carry):
        base = pl.multiple_of(kb * kb_sz, kb_sz)
        blk = s_ref[pl.ds(base, kb_sz), :]
        tie = jnp.where(base + key_off < cut, 0.0, NEG_BIG)
        sel = jnp.where(blk > thr, 0.0, jnp.where(blk == thr, tie, NEG_BIG))
        sel_ref[pl.ds(base, kb_sz), :] = sel.astype(BF16)
        return carry

    lax.fori_loop(0, nkb, mask_body, 0)

    cm_ref[...] = jnp.full(cm_ref.shape, NEG_BIG, BF16)

    def logit_body(kb, carry):
        base = pl.multiple_of(kb * kb_sz, kb_sz)
        dsel = jnp.clip((q0 - base) // kb_sz, 0, 2)
        kblk = k_ref[0, pl.ds(base, kb_sz), :]
        sel = sel_ref[pl.ds(base, kb_sz), :]
        for h in range(A_HEADS):
            sl = h // 4
            lg = _dot_nt(kblk[:, sl * LANES:(sl + 1) * LANES], qx_ref[0, :, h * LANES:(h + 1) * LANES])
            lg = lg.astype(BF16) + sel + bias_ref[dsel, h]
            lg_ref[h, pl.ds(base, kb_sz), :] = lg
            cm_ref[h] = jnp.maximum(cm_ref[h], lg)
        return carry

    for_each_block(logit_body)

    acc_ref[...] = jnp.zeros(acc_ref.shape, F32)
    m_rows = [jnp.max(cm_ref[h].astype(F32), axis=0, keepdims=True).astype(BF16) for h in range(A_HEADS)]

    def pv_body(kb, carry):
        base = pl.multiple_of(kb * kb_sz, kb_sz)
        for h in range(A_HEADS):
            p = jnp.exp2(lg_ref[h, pl.ds(base, kb_sz), :] - m_rows[h])
            acc_ref[h] += jnp.dot(vt_ref[0, kb, h // 4], p, preferred_element_type=F32)
        return carry

    for_each_block(pv_body)

    outs = []
    for h in range(A_HEADS):
        pos = (h // 2) % 2
        outs.append(acc_ref[h, pos * A_HEAD_DIM:(pos + 1) * A_HEAD_DIM, :] / acc_ref[h, LANES:LANES + 1, :])
    o_ref[0] = jnp.concatenate(outs, axis=0).T.astype(BF16)


def _attn_call(rel_bias, qx, iq, misc, kb, vt, ikx, *, tq, l_true, offset, n_top):
    b, tq_total, _ = qx.shape
    lp = kb.shape[1]
    nq = tq_total // tq
    assert offset % ATT_KB == 0 and tq % CHUNK == 0 and ATT_KB % tq == 0 and lp % ATT_KB == 0
    assert lp < 2 ** 14 and l_true <= lp
    kern = functools.partial(_attn_kernel, tq=tq, l_true=l_true, offset=offset, n_top=n_top)
    return pl.pallas_call(
        kern,
        out_shape=jax.ShapeDtypeStruct((b, tq_total, A_WIDTH), BF16),
        grid=(b, nq),
        in_specs=[pl.BlockSpec(memory_space=pltpu.SMEM),
                  pl.BlockSpec((1, tq, QX_W), lambda i, j: (i, j, 0)),
                  pl.BlockSpec((1, tq, IDX_HEADS * IDX_DIM), lambda i, j: (i, j, 0)),
                  pl.BlockSpec((1, tq, LANES), lambda i, j: (i, j, 0)),
                  pl.BlockSpec((1, lp, 256), lambda i, j: (i, 0, 0)),
                  pl.BlockSpec((1, lp // ATT_KB, 2, V_ROWS, ATT_KB), lambda i, j: (i, 0, 0, 0, 0)),
                  pl.BlockSpec((1, lp, 2 * LANES), lambda i, j: (i, 0, 0))],
        out_specs=pl.BlockSpec((1, tq, A_WIDTH), lambda i, j: (i, j, 0)),
        scratch_shapes=[pltpu.VMEM((lp, tq), F32),
                        pltpu.VMEM((lp, tq), BF16),
                        pltpu.VMEM((lp, tq), BF16),
                        pltpu.VMEM((3, A_HEADS, ATT_KB, tq), BF16),
                        pltpu.VMEM((A_HEADS, lp, tq), BF16),
                        pltpu.VMEM((A_HEADS, ATT_KB, tq), BF16),
                        pltpu.VMEM((A_HEADS, V_ROWS, tq), F32),
                        pltpu.VMEM((8, tq), F32),
                        pltpu.VMEM((16, tq), BF16)],
        compiler_params=_cparams(2),
        name="sparse_attn",
    )(rel_bias, qx, iq, misc, kb, vt, ikx)


def _split_bf16(a, n):
    parts = []
    r = a
    for i in range(n):
        p = r.astype(BF16)
        parts.append(p)
        if i + 1 < n:
            r = r - p.astype(F32)
    return parts


def _bdot(a, b, dims=(((1,), (0,)), ((), ()))):
    return lax.dot_general(a.astype(BF16), b.astype(BF16), dims, preferred_element_type=F32)


_NT = (((1,), (1,)), ((), ()))
_TN = (((0,), (0,)), ((), ()))


def _gdn_kernel(x_ref, misc_ref, gz_ref, convw_ref, hist_ref, s0_ref, gnw_ref, alane_ref, dlane_ref,
                o_ref, conv_o, s_o, xp_ref, y_ref, st_ref, vn_ref,
                u_ref, w_ref, qk_ref, qe_ref, kd_ref, el_ref, gate_ref, *, cb, t_true, nblk):
    c = CHUNK
    step = pl.program_id(1)
    blk = jnp.minimum(step, nblk - 1)
    carried = (u_ref, w_ref, qk_ref, qe_ref, kd_ref, el_ref, gate_ref)
    u_nx, w_nx, qk_nx, qe_nx, kd_nx, el_nx, gate_nx = (r.at[1] for r in carried)
    u_ref, w_ref, qk_ref, qe_ref, kd_ref, el_ref, gate_ref = (r.at[0] for r in carried)

    @pl.when(step == 0)
    def _():
        xp_ref[8 - (CONV_W - 1):8, :] = hist_ref[0]
        st_ref[...] = s0_ref[0]
        for r in carried:
            r[...] = jnp.zeros(r.shape, r.dtype)

    heads = range(G_HEADS)
    for r in carried:
        r[0] = r[1]

    def scan_stages():
        vn_ref[...] = jnp.zeros(vn_ref.shape, F32)
        for ck in range(cb // c):
            r0, r1 = ck * c, (ck + 1) * c
            s_prev = [st_ref[h] for h in heads]
            v_new = [u_ref[h, r0:r1, :] - _bdot(w_ref[h, r0:r1, :], s_prev[h]) for h in heads]
            o_state = [_bdot(qe_ref[h, r0:r1, :], s_prev[h]) for h in heads]
            yield
            for h in heads:
                vn_ref[h, r0:r1, :] = v_new[h]
            o = [o_state[h] + _bdot(qk_ref[h, r0:r1, :], vn_ref[h]) for h in heads]
            for h in heads:
                s_new = s_prev[h] * el_ref[h, ck, 0:1, :] + _bdot(kd_ref[h, r0:r1, :], v_new[h], _TN)
                st_ref[h] = jnp.where(step > 0, s_new, s_prev[h])
            yield
            for h in heads:
                on = o[h] * lax.rsqrt(jnp.mean(o[h] * o[h], axis=-1, keepdims=True) + EPS) * gnw_ref[...]
                gate = gate_ref[r0:r1, h * G_VAL_DIM:(h + 1) * G_VAL_DIM]
                o_ref[0, r0:r1, h * G_VAL_DIM:(h + 1) * G_VAL_DIM] = (on * gate).astype(BF16)
            yield
        s_o[0] = st_ref[...]

    scan = scan_stages()

    def tick():
        next(scan, None)

    tick()

    xp_ref[8:8 + cb, :] = x_ref[0]
    y = convw_ref[CONV_W - 1:CONV_W, :] * xp_ref[8:8 + cb, :]
    for j in range(CONV_W - 1):
        y = y + convw_ref[j:j + 1, :] * xp_ref[5 + j:5 + j + cb, :]
    y_ref[...] = _silu(y)

    last_row = (t_true - 1) % cb
    conv_o[0] = xp_ref[8 + last_row - (CONV_W - 2):8 + last_row + 1, :]

    xp_ref[8 - (CONV_W - 1):8, :] = xp_ref[8 + cb - (CONV_W - 1):8 + cb, :]

    ri = lax.broadcasted_iota(jnp.int32, (cb, cb), 0)
    ci = lax.broadcasted_iota(jnp.int32, (cb, cb), 1)
    lag = jnp.where((ri // c) == (ci // c), ri - ci, -1)
    tri = lag >= 0
    strict = lag > 0
    eye = ri == ci
    tril_bf = jnp.where(tri, 1.0, 0.0).astype(BF16)
    eye_f = jnp.where(eye, 1.0, 0.0)

    ms = misc_ref[0]
    tok = blk * cb + lax.broadcasted_iota(jnp.int32, (cb, 1), 0)
    live = tok < t_true
    beta_s = jnp.where(live, jax.nn.sigmoid(ms), 0.0)
    z = ms + dlane_ref[...]
    softplus = jnp.maximum(z, 0.0) + jnp.log(1.0 + jnp.exp(-jnp.abs(z)))
    g_s = jnp.where(live, -jnp.exp(alane_ref[...]) * softplus, 0.0)
    gc_s = None
    for piece in _split_bf16(g_s, 3):
        t = jnp.dot(tril_bf, piece, preferred_element_type=F32)
        gc_s = t if gc_s is None else gc_s + t

    tick()

    q, k, v, beta, gc, decay, kb, eg = ([None] * G_HEADS for _ in range(8))
    for h in heads:
        qh = y_ref[:, h * G_KEY_DIM:(h + 1) * G_KEY_DIM]
        kh = y_ref[:, G_WIDTH + h * G_KEY_DIM:G_WIDTH + (h + 1) * G_KEY_DIM]
        v[h] = y_ref[:, 2 * G_WIDTH + h * G_VAL_DIM:2 * G_WIDTH + (h + 1) * G_VAL_DIM]
        q[h] = qh * lax.rsqrt(jnp.sum(qh * qh, axis=-1, keepdims=True) + 1e-6) * (G_KEY_DIM ** -0.5)
        k[h] = kh * lax.rsqrt(jnp.sum(kh * kh, axis=-1, keepdims=True) + 1e-6)
        beta[h] = beta_s[:, MISC_GB + h:MISC_GB + h + 1]
        gc[h] = gc_s[:, MISC_GA + h:MISC_GA + h + 1]
        gc_b = jnp.broadcast_to(gc[h], (cb, cb))
        gc_row = jnp.sum(jnp.where(eye, gc_b, 0.0), axis=0, keepdims=True)
        decay[h] = jnp.exp(jnp.where(tri, gc_b - gc_row, NEG_BIG))
        kb[h] = k[h] * beta[h]
        eg[h] = jnp.exp(gc[h])
    tick()
    m = [jnp.where(strict, _bdot(kb[h], k[h], _NT) * decay[h], 0.0) for h in heads]
    rhs = [jnp.concatenate([v[h] * beta[h], kb[h] * eg[h]], axis=1) for h in heads]
    qk = [jnp.where(tri, _bdot(q[h], k[h], _NT) * decay[h], 0.0) for h in heads]
    tick()
    pw = [-m[h] for h in heads]
    inv = [eye_f + pw[h] for h in heads]
    for _ in range(5):
        pw = [_bdot(pw[h], pw[h]) for h in heads]
        tick()
        inv = [inv[h] + _bdot(inv[h], pw[h]) for h in heads]
    tick()
    sol = [_bdot(inv[h], rhs[h]) for h in heads]
    tick()
    m_sol = []
    for h in heads:
        m_hi, m_lo = _split_bf16(m[h], 2)
        s_hi, s_lo = _split_bf16(sol[h], 2)
        m_sol.append(_bdot(m_hi, s_hi) + _bdot(m_hi, s_lo) + _bdot(m_lo, s_hi))
    tick()
    sol = [sol[h] + _bdot(inv[h], rhs[h] - sol[h] - m_sol[h]) for h in heads]
    for _ in scan:
        pass
    gate_nx[...] = _silu(gz_ref[0])
    for h in heads:
        u_nx[h] = sol[h][:, :G_VAL_DIM]
        w_nx[h] = sol[h][:, G_VAL_DIM:].astype(BF16)
        qk_nx[h] = qk[h].astype(BF16)
        qe_nx[h] = (q[h] * eg[h]).astype(BF16)
        for ck in range(cb // c):
            r0, r1 = ck * c, (ck + 1) * c
            g_last = gc[h][r1 - 1:r1, :]
            kd_nx[h, r0:r1, :] = (k[h][r0:r1] * jnp.exp(g_last - gc[h][r0:r1])).astype(BF16)
            el_nx[h, ck] = jnp.broadcast_to(jnp.exp(g_last), (8, LANES))


def _gdn_call(gqkv, misc, gz, conv_w, hist, s0, gnorm_w, a_lane, d_lane, *, cb, t_true):
    b, t_pad, _ = gqkv.shape
    nblk = t_pad // cb
    assert (nblk - 1) * cb < t_true <= t_pad and (t_true - 1) % cb >= CONV_W - 2
    kern = functools.partial(_gdn_kernel, cb=cb, t_true=t_true, nblk=nblk)
    prep = lambda i, j: (i, jnp.minimum(j, nblk - 1), 0)
    scan = lambda i, j: (i, jnp.maximum(j - 1, 0), 0)
    nch = cb // CHUNK
    return pl.pallas_call(
        kern,
        out_shape=[jax.ShapeDtypeStruct((b, t_pad, G_WIDTH), BF16),
                   jax.ShapeDtypeStruct((b, CONV_W - 1, CONV_CH), F32),
                   jax.ShapeDtypeStruct((b, G_HEADS, G_KEY_DIM, G_VAL_DIM), F32)],
        grid=(b, nblk + 1),
        in_specs=[pl.BlockSpec((1, cb, CONV_CH), prep),
                  pl.BlockSpec((1, cb, LANES), prep),
                  pl.BlockSpec((1, cb, G_WIDTH), prep),
                  pl.BlockSpec((CONV_W, CONV_CH), lambda i, j: (0, 0)),
                  pl.BlockSpec((1, CONV_W - 1, CONV_CH), lambda i, j: (i, 0, 0)),
                  pl.BlockSpec((1, G_HEADS, G_KEY_DIM, G_VAL_DIM), lambda i, j: (i, 0, 0, 0)),
                  pl.BlockSpec((1, G_VAL_DIM), lambda i, j: (0, 0)),
                  pl.BlockSpec((1, LANES), lambda i, j: (0, 0)),
                  pl.BlockSpec((1, LANES), lambda i, j: (0, 0))],
        out_specs=[pl.BlockSpec((1, cb, G_WIDTH), scan),
                   pl.BlockSpec((1, CONV_W - 1, CONV_CH), lambda i, j: (i, 0, 0)),
                   pl.BlockSpec((1, G_HEADS, G_KEY_DIM, G_VAL_DIM), lambda i, j: (i, 0, 0, 0))],
        scratch_shapes=[pltpu.VMEM((cb + 8, CONV_CH), F32),
                        pltpu.VMEM((cb, CONV_CH), F32),
                        pltpu.VMEM((G_HEADS, G_KEY_DIM, G_VAL_DIM), F32),
                        pltpu.VMEM((G_HEADS, cb, G_VAL_DIM), F32),
                        pltpu.VMEM((2, G_HEADS, cb, G_VAL_DIM), F32),
                        pltpu.VMEM((2, G_HEADS, cb, G_KEY_DIM), BF16),
                        pltpu.VMEM((2, G_HEADS, cb, cb), BF16),
                        pltpu.VMEM((2, G_HEADS, cb, G_KEY_DIM), BF16),
                        pltpu.VMEM((2, G_HEADS, cb, G_KEY_DIM), BF16),
                        pltpu.VMEM((2, G_HEADS, nch, 8, LANES), F32),
                        pltpu.VMEM((2, cb, G_WIDTH), F32)],
        compiler_params=_cparams(2),
        name="gated_delta",
    )(gqkv, misc, gz, conv_w, hist, s0, gnorm_w, a_lane, d_lane)


def _pack_w_in(w_in):
    d = w_in.shape[0]
    splits = (512, 256, 256, 512, 64, 8, 512, 512, 512, 512, 4, 4)
    offs = np.concatenate([[0], np.cumsum(splits)])
    aq, ak, av, iq, ik, iw, gq, gk, gv, gz, gb, ga = [w_in[:, offs[i]:offs[i + 1]] for i in range(12)]
    zeros64 = jnp.zeros((d, A_HEAD_DIM), w_in.dtype)
    qx = []
    for h in range(A_HEADS):
        qh = aq[:, h * A_HEAD_DIM:(h + 1) * A_HEAD_DIM] * (A_HEAD_DIM ** -0.5 * LOG2E)
        qx += [qh, zeros64] if (h // 2) % 2 == 0 else [zeros64, qh]
    ikx = [ik, zeros64, zeros64, ik]
    misc = [ik, iw, gb, ga, jnp.zeros((d, LANES - MISC_GA - G_HEADS), w_in.dtype)]
    cols = qx + [ak, av, iq * (IDX_DIM ** -0.5)] + ikx + misc + [gq, gk, gv, gz]
    packed = jnp.concatenate(cols, axis=1).astype(BF16)
    assert packed.shape[1] == IN_PACKED
    return packed


def _pack_ffn(w_gate, w_up, w_down):
    d = w_gate.shape[0]
    wg = w_gate.reshape(d, N_FF_CHUNKS, FF_CHUNK)
    wu = w_up.reshape(d, N_FF_CHUNKS, FF_CHUNK)
    wgu = jnp.transpose(jnp.concatenate([wg, wu], axis=2), (1, 0, 2)).astype(BF16)
    wd = w_down.reshape(N_FF_CHUNKS, FF_CHUNK, d).astype(BF16)
    return wgu, wd


def _ikx_layout(ik):
    z = jnp.zeros_like(ik)
    return jnp.concatenate([ik, z, z, ik], axis=-1)


def _lane_vec(vals, lane0):
    return jnp.zeros((1, LANES), F32).at[0, lane0:lane0 + vals.shape[0]].set(vals.astype(F32))


def _run(x, mod, past, layers, rel_bias, norm_final):
    b, t, d = x.shape
    n = b * t
    per_token = t < TOKEN_TILE
    tm = n if per_token else TOKEN_TILE
    tiles_per_batch = None if per_token else t // tm
    tok = dict(tm=tm, tiles_per_batch=tiles_per_batch, per_token=per_token)

    x2 = x.reshape(n, d)
    states = []
    for li, lw in enumerate(layers):
        m = mod[li]

        def mvec(kidx):
            row = m[:, kidx]
            return jnp.repeat(row, t, axis=0) if per_token else row[:, None, :]

        sh1, sc1, gt1, sh2, sc2, gt2, sh3, sc3, gt3 = [mvec(i) for i in range(N_MOD)]
        x2 = _ffn_call(x2, sh1, sc1, gt1, lw["norm_ffn1"], lw["wgu1"], lw["wd1"], None, **tok)
        qx, k, kb, v, vb, iq, ikx, misc, gqkv, gz = _inproj_call(
            x2, sh2, sc2, lw["norm_mix"], lw["w_in"], **tok)

        if past is None:
            offset, l_true, tq = 0, t, ATT_TQ
            k_all, v_all, ikx_all = (a.reshape(b, t, -1) for a in (kb, vb, ikx))
            qx3, iq3, misc3 = (a.reshape(b, t, -1) for a in (qx, iq, misc))
            conv_hist = jnp.zeros((b, CONV_W - 1, CONV_CH), F32)
            s0 = jnp.zeros((b, G_HEADS, G_KEY_DIM, G_VAL_DIM), F32)
        else:
            k_hist, v_hist, ik_hist, conv_hist, s0 = (p[li] for p in past)
            offset = k_hist.shape[1]
            l_true = offset + t
            tq = LANES
            lp = -(-l_true // ATT_KB) * ATT_KB
            pad_k = lambda a: jnp.pad(a, ((0, 0), (0, lp - l_true), (0, 0)))
            k_all = pad_k(jnp.concatenate([k_hist.reshape(b, offset, -1).astype(BF16),
                                           kb.reshape(b, t, -1)], axis=1))
            v_all = pad_k(jnp.concatenate([v_hist.reshape(b, offset, -1).astype(BF16),
                                           vb.reshape(b, t, -1)], axis=1))
            ikx_all = pad_k(jnp.concatenate([_ikx_layout(ik_hist.astype(BF16)),
                                             ikx.reshape(b, t, -1)], axis=1))
            pad_q = lambda a: jnp.pad(a.reshape(b, t, -1), ((0, 0), (0, tq - t), (0, 0)))
            qx3, iq3, misc3 = pad_q(qx), pad_q(iq), pad_q(misc)
            s0 = s0.astype(F32)
            conv_hist = conv_hist.astype(F32)
        lp = k_all.shape[1]
        n_top = min(TOPK_MAX, l_true // 4)
        vt = jnp.transpose(v_all.reshape(b, lp // ATT_KB, ATT_KB, 2, LANES), (0, 1, 3, 4, 2))
        vt = jnp.concatenate([vt, jnp.ones((b, lp // ATT_KB, 2, V_ROWS - LANES, ATT_KB), BF16)], axis=3)
        attn = _attn_call(rel_bias, qx3, iq3, misc3, k_all, vt, ikx_all,
                          tq=tq, l_true=l_true, offset=offset, n_top=n_top)
        attn = attn[:, :t].reshape(n, A_WIDTH)

        cb = GDN_BLOCK if t % GDN_BLOCK == 0 else CHUNK
        t_pad = -(-t // cb) * cb
        pad_t = lambda a: jnp.pad(a.reshape(b, t, -1), ((0, 0), (0, t_pad - t), (0, 0)))
        gdn, conv_new, s_new = _gdn_call(pad_t(gqkv), pad_t(misc), pad_t(gz), lw["conv_w"], conv_hist, s0,
                                         lw["gnorm_w"], lw["a_lane"], lw["d_lane"], cb=cb, t_true=t)
        gdn = gdn[:, :t].reshape(n, G_WIDTH)

        nf = norm_final if li == len(layers) - 1 else None
        x2 = _ffn_call(x2, sh3, sc3, gt3, lw["norm_ffn2"], lw["wgu2"], lw["wd2"], nf,
                       mixer=(gt2, attn, gdn, lw["wo_a"], lw["wo_g"]), **tok)

        states.append((k.reshape(b, t, A_KV_HEADS, A_HEAD_DIM), v.reshape(b, t, A_KV_HEADS, A_HEAD_DIM),
                       misc.reshape(b, t, LANES)[..., :IDX_DIM], conv_new, s_new))
    stacked = [jnp.stack(s, axis=0) for s in zip(*states)]
    return x2.reshape(b, t, d), stacked


def kernel(x_prompt, x_sample, cache_k, cache_v, cache_idx_k, state_conv, state_delta, c_prompt, c_sample,
           w_mod, b_mod, norm_ffn1, norm_mix, norm_ffn2, ffn1_w_gate, ffn1_w_up, ffn1_w_down,
           ffn2_w_gate, ffn2_w_up, ffn2_w_down, w_in, w_out, rel_bias, conv_w, a_log, dt_bias, gnorm_w,
           norm_final):
    depth = w_mod.shape[0]
    bp = c_prompt.shape[0]
    c_all = jnp.concatenate([c_prompt, c_sample], axis=0)
    layers, mods_p, mods_s = [], [], []
    for l in range(depth):
        mod = _mod_call(c_all, w_mod[l], b_mod[l]).reshape(c_all.shape[0], N_MOD, D_MODEL)
        mods_p.append(mod[:bp])
        mods_s.append(mod[bp:])
        wgu1, wd1 = _pack_ffn(ffn1_w_gate[l], ffn1_w_up[l], ffn1_w_down[l])
        wgu2, wd2 = _pack_ffn(ffn2_w_gate[l], ffn2_w_up[l], ffn2_w_down[l])
        wo = w_out[l].astype(BF16)
        layers.append(dict(
            norm_ffn1=norm_ffn1[l].reshape(1, -1), norm_mix=norm_mix[l].reshape(1, -1),
            norm_ffn2=norm_ffn2[l].reshape(1, -1), wgu1=wgu1, wd1=wd1, wgu2=wgu2, wd2=wd2,
            w_in=_pack_w_in(w_in[l]), wo_a=wo[:A_WIDTH], wo_g=wo[A_WIDTH:],
            conv_w=conv_w[l], gnorm_w=gnorm_w[l].reshape(1, -1),
            a_lane=_lane_vec(a_log[l], MISC_GA), d_lane=_lane_vec(dt_bias[l], MISC_GA)))
    nf = norm_final.reshape(1, -1)
    y_p, (k_p, v_p, ik_p, conv_p, delta_p) = _run(x_prompt, mods_p, None, layers, rel_bias, nf)
    past = (cache_k, cache_v, cache_idx_k, state_conv, state_delta)
    y_s, (k_s, v_s, ik_s, conv_s, delta_s) = _run(x_sample, mods_s, past, layers, rel_bias, nf)
    return (y_p, y_s, k_p, v_p, ik_p, conv_p, delta_p, k_s, v_s, ik_s, conv_s, delta_s)
```

```python
import functools

import jax
import jax.numpy as jnp
import numpy as np
from jax import lax
from jax.experimental import pallas as pl
from jax.experimental.pallas import tpu as pltpu

F32 = jnp.float32
BF16 = jnp.bfloat16

D_MODEL = 1024
CHUNK = 64
A_HEAD_DIM = 64
A_HEADS = 8
A_KV_HEADS = 4
A_WIDTH = A_HEADS * A_HEAD_DIM
IDX_HEADS = 8
IDX_DIM = 64
TOPK_MAX = 256
REL_BUCKETS = 32
G_KEY_DIM = 128
G_VAL_DIM = 128
G_HEADS = 4
G_WIDTH = G_HEADS * G_VAL_DIM
CONV_W = 4
CONV_CH = 2 * G_HEADS * G_KEY_DIM + G_HEADS * G_VAL_DIM
D_FF = 2816
N_MOD = 9
EPS = 1e-6

LANES = 128
MXU_DIM = 256
VMEM_LIMIT_BYTES = 56 * 1024 * 1024

FF_CHUNK = MXU_DIM
N_FF_CHUNKS = D_FF // FF_CHUNK
TOKEN_TILE = 512
ATT_TQ = 256
ATT_KB = 256
GDN_BLOCK = 256

QX_W = A_HEADS * LANES
OFF_QX = 0
OFF_K = OFF_QX + QX_W
OFF_V = OFF_K + A_KV_HEADS * A_HEAD_DIM
OFF_IQ = OFF_V + A_KV_HEADS * A_HEAD_DIM
OFF_IKX = OFF_IQ + IDX_HEADS * IDX_DIM
OFF_MISC = OFF_IKX + 2 * LANES
OFF_GQKV = OFF_MISC + LANES
OFF_GZ = OFF_GQKV + CONV_CH
IN_PACKED = OFF_GZ + G_WIDTH
MISC_IW = IDX_DIM
MISC_GB = MISC_IW + IDX_HEADS
MISC_GA = MISC_GB + G_HEADS

NEG_BIG = -1e30
INT_MIN = -2 ** 31
LOG2E = 1.4426950408889634
V_ROWS = LANES + 16


def _cparams(n_axes):
    return pltpu.CompilerParams(dimension_semantics=("arbitrary",) * n_axes,
                                vmem_limit_bytes=VMEM_LIMIT_BYTES)


def _resident(shape):
    nd = len(shape)
    return pl.BlockSpec(shape, lambda *_: (0,) * nd, pipeline_mode=pl.Buffered(1))


def _dot_nt(a, b):
    return lax.dot_general(a, b, (((1,), (1,)), ((), ())), preferred_element_type=F32)


def _rms_mod(x, gain, shift, scale):
    ms = jnp.mean(x * x, axis=-1, keepdims=True)
    y = x * lax.rsqrt(ms + EPS) * gain
    return y * (1.0 + scale) + shift


def _silu(x):
    return x * jax.nn.sigmoid(x)


def _tree_sum(parts):
    while len(parts) > 1:
        parts = [a + b for a, b in zip(parts[0::2], parts[1::2])] + ([parts[-1]] if len(parts) % 2 else [])
    return parts[0]


def _mod_kernel(c_ref, w_ref, b_ref, o_ref):
    s = _silu(c_ref[...]).astype(BF16)
    o_ref[...] = jnp.dot(s, w_ref[...].astype(BF16), preferred_element_type=F32) + b_ref[...]


def _mod_call(c, w_mod, b_mod):
    rows, d = c.shape
    n = w_mod.shape[1]
    tn = D_MODEL
    return pl.pallas_call(
        _mod_kernel,
        out_shape=jax.ShapeDtypeStruct((rows, n), F32),
        grid=(n // tn,),
        in_specs=[pl.BlockSpec((rows, d), lambda j: (0, 0)),
                  pl.BlockSpec((d, tn), lambda j: (0, j)),
                  pl.BlockSpec((1, tn), lambda j: (0, j))],
        out_specs=pl.BlockSpec((rows, tn), lambda j: (0, j)),
        compiler_params=_cparams(1),
        name="mod",
    )(c, w_mod, b_mod.reshape(1, n))


def _mod_specs(per_token, tm, tiles_per_batch):
    if per_token:
        return pl.BlockSpec((tm, D_MODEL), lambda i: (i, 0))
    return pl.BlockSpec((None, 1, D_MODEL), lambda i: (i // tiles_per_batch, 0, 0))


def _ffn_kernel(x_ref, sh_ref, sc_ref, gt_ref, gain_ref, wgu_ref, wd_ref, *rest, final_norm, mixer):
    rest = list(rest)
    if mixer:
        gm_ref, a_ref, g_ref, wa_ref, wg_ref = rest[:5]
        rest = rest[5:]
    if final_norm:
        nf_ref = rest.pop(0)
    o_ref, acc_ref = rest
    x = x_ref[...]
    if mixer:
        x = x + gm_ref[...] * (jnp.dot(a_ref[...], wa_ref[...], preferred_element_type=F32)
                               + jnp.dot(g_ref[...], wg_ref[...], preferred_element_type=F32))
    h = _rms_mod(x, gain_ref[...], sh_ref[...], sc_ref[...]).astype(BF16)
    for j in range(N_FF_CHUNKS):
        ab = jnp.dot(h, wgu_ref[j], preferred_element_type=F32)
        g = (_silu(ab[:, :FF_CHUNK]) * ab[:, FF_CHUNK:]).astype(BF16)
        d = jnp.dot(g, wd_ref[j], preferred_element_type=F32)
        if j == 0:
            acc_ref[...] = d
        else:
            acc_ref[...] += d
    y = x + 0.5 * gt_ref[...] * acc_ref[...]
    if final_norm:
        ms = jnp.mean(y * y, axis=-1, keepdims=True)
        y = y * lax.rsqrt(ms + EPS) * nf_ref[...]
    o_ref[...] = y


def _ffn_call(x2, sh, sc, gt, gain, wgu, wd, norm_final, mixer=None, *, tm, tiles_per_batch, per_token):
    n = x2.shape[0]
    mspec = _mod_specs(per_token, tm, tiles_per_batch)
    in_specs = [pl.BlockSpec((tm, D_MODEL), lambda i: (i, 0)), mspec, mspec, mspec,
                _resident((1, D_MODEL)), _resident(wgu.shape), _resident(wd.shape)]
    args = [x2, sh, sc, gt, gain, wgu, wd]
    if mixer is not None:
        gm, attn, gdn, wa, wg = mixer
        in_specs += [mspec, pl.BlockSpec((tm, A_WIDTH), lambda i: (i, 0)),
                     pl.BlockSpec((tm, G_WIDTH), lambda i: (i, 0)), _resident(wa.shape), _resident(wg.shape)]
        args += [gm, attn, gdn, wa, wg]
    final_norm = norm_final is not None
    if final_norm:
        in_specs.append(_resident((1, D_MODEL)))
        args.append(norm_final)
    return pl.pallas_call(
        functools.partial(_ffn_kernel, final_norm=final_norm, mixer=mixer is not None),
        out_shape=jax.ShapeDtypeStruct((n, D_MODEL), F32),
        grid=(n // tm,),
        in_specs=in_specs,
        out_specs=pl.BlockSpec((tm, D_MODEL), lambda i: (i, 0)),
        scratch_shapes=[pltpu.VMEM((tm, D_MODEL), F32)],
        compiler_params=_cparams(1),
        name="ffn_final" if final_norm else "ffn",
    )(*args)


def _inproj_kernel(x_ref, sh_ref, sc_ref, gain_ref, w_ref,
                   qx_o, k_o, kb_o, v_o, vb_o, iq_o, ikx_o, misc_o, gqkv_o, gz_o):
    h = _rms_mod(x_ref[...], gain_ref[...], sh_ref[...], sc_ref[...]).astype(BF16)

    def mm(off, width):
        return jnp.dot(h, w_ref[:, off:off + width], preferred_element_type=F32)

    qx_o[...] = mm(OFF_QX, QX_W).astype(BF16)
    k = mm(OFF_K, OFF_V - OFF_K)
    k_o[...] = k
    kb_o[...] = k.astype(BF16)
    v = mm(OFF_V, OFF_IQ - OFF_V)
    v_o[...] = v
    vb_o[...] = v.astype(BF16)
    iq_o[...] = mm(OFF_IQ, OFF_IKX - OFF_IQ).astype(BF16)
    ikx_o[...] = mm(OFF_IKX, OFF_MISC - OFF_IKX).astype(BF16)
    misc_o[...] = mm(OFF_MISC, LANES)
    gqkv_o[...] = mm(OFF_GQKV, CONV_CH)
    gz_o[...] = mm(OFF_GZ, G_WIDTH)


def _inproj_call(x2, sh, sc, gain, w_packed, *, tm, tiles_per_batch, per_token):
    n = x2.shape[0]
    mspec = _mod_specs(per_token, tm, tiles_per_batch)
    widths = [(QX_W, BF16), (256, F32), (256, BF16), (256, F32), (256, BF16), (512, BF16),
              (2 * LANES, BF16), (LANES, F32), (CONV_CH, F32), (G_WIDTH, F32)]
    return pl.pallas_call(
        _inproj_kernel,
        out_shape=[jax.ShapeDtypeStruct((n, w), dt) for w, dt in widths],
        grid=(n // tm,),
        in_specs=[pl.BlockSpec((tm, D_MODEL), lambda i: (i, 0)), mspec, mspec,
                  _resident((1, D_MODEL)), _resident(w_packed.shape)],
        out_specs=[pl.BlockSpec((tm, w), lambda i: (i, 0)) for w, _ in widths],
        compiler_params=_cparams(1),
        name="inproj",
    )(x2, sh, sc, gain, w_packed)


def _rel_bucket_int(rel):
    n = jnp.abs(rel)
    large = jnp.full(rel.shape, 8, jnp.int32)
    for th in (12, 16, 23, 32, 46, 64, 91):
        large = large + jnp.where(n >= th, 1, 0)
    return jnp.where(rel > 0, REL_BUCKETS // 2, 0) + jnp.where(n < 8, n, large)


def _key_to_float(t):
    bits = jnp.where(t >= 0, t, t ^ jnp.int32(0x7FFFFFFF))
    return lax.bitcast_convert_type(bits, F32)


def _top16(x):
    bits = lax.bitcast_convert_type(x, jnp.int32) & jnp.int32(-65536)
    return lax.bitcast_convert_type(bits, F32).astype(BF16)


def _attn_kernel(relb_ref, qx_ref, iq_ref, misc_ref, k_ref, vt_ref, ikx_ref, o_ref,
                 s_ref, shi_ref, sel_ref, bias_ref, lg_ref, cm_ref, acc_ref, cnt_ref, cnth_ref,
                 *, tq, l_true, offset, n_top):
    kb_sz = ATT_KB
    first = (pl.program_id(0) == 0) & (pl.program_id(1) == 0)
    q0 = offset + pl.program_id(1) * tq
    kmax = jnp.minimum(q0 + tq, l_true)
    nkb = (kmax + kb_sz - 1) // kb_sz

    key_off = lax.broadcasted_iota(jnp.int32, (kb_sz, tq), 0)
    qry_off = lax.broadcasted_iota(jnp.int32, (kb_sz, tq), 1)

    @pl.when(first)
    def _():
        for d in range(3):
            bucket = _rel_bucket_int(key_off - qry_off - d * kb_sz)
            for h in range(A_HEADS):
                t = jnp.zeros((kb_sz, tq), F32)
                for b in range(REL_BUCKETS):
                    t = jnp.where(bucket == b, relb_ref[b, h], t)
                bias_ref[d, h] = (t * LOG2E).astype(BF16)

    qpos = q0 + lax.broadcasted_iota(jnp.int32, (1, tq), 1)
    limit = jnp.minimum((qpos // CHUNK + 1) * CHUNK, l_true)

    iw_t = misc_ref[0].T[MISC_IW:MISC_IW + IDX_HEADS, :] * (IDX_HEADS ** -0.5)

    def score_body(kb, carry):
        base = pl.multiple_of(kb * kb_sz, kb_sz)
        ik2 = ikx_ref[0, pl.ds(base, kb_sz), :]
        ik_lo = ik2[:, :LANES]
        ik_hi = ik2[:, LANES:]
        s = jnp.zeros((kb_sz, tq), F32)
        for j in range(IDX_HEADS // 2):
            slab = iq_ref[0, :, j * LANES:(j + 1) * LANES]
            d0 = _dot_nt(ik_lo, slab)
            d1 = _dot_nt(ik_hi, slab)
            s = s + iw_t[2 * j:2 * j + 1, :] * jnp.maximum(d0, 0.0)
            s = s + iw_t[2 * j + 1:2 * j + 2, :] * jnp.maximum(d1, 0.0)
        s = jnp.where(base + key_off < limit, s, -jnp.inf)
        s_ref[pl.ds(base, kb_sz), :] = s
        shi_ref[pl.ds(base, kb_sz), :] = _top16(s)
        return carry

    def for_each_block(body):
        def quad(i, carry):
            for r in range(4):
                body(4 * i + r, carry)
            return carry
        lax.fori_loop(0, nkb // 4, quad, 0)
        done = (nkb // 4) * 4

        @pl.when(nkb % 4 >= 2)
        def _():
            body(done, 0)
            body(done + 1, 0)

        @pl.when(nkb % 2 == 1)
        def _():
            body(nkb - 1, 0)

    for_each_block(score_body)

    def sum_blocks(contrib, acc_ref, n=nkb):
        acc_ref[...] = jnp.zeros(acc_ref.shape, acc_ref.dtype)

        def quad(i, carry):
            acc_ref[...] += _tree_sum([contrib(4 * i + r) for r in range(4)])
            return carry
        lax.fori_loop(0, n // 4, quad, 0)
        done = (n // 4) * 4

        @pl.when(n % 4 >= 2)
        def _():
            acc_ref[...] += contrib(done) + contrib(done + 1)

        @pl.when(n % 2 == 1)
        def _():
            acc_ref[...] += contrib(n - 1)
        return acc_ref[...]

    def count(pred, n=nkb):
        def contrib(kb):
            base = pl.multiple_of(kb * kb_sz, kb_sz)
            hit = pred(s_ref[pl.ds(base, kb_sz), :], base + key_off)
            return _tree_sum([hit[r:r + 8] for r in range(0, kb_sz, 8)])
        return jnp.sum(sum_blocks(contrib, cnt_ref, n), axis=0, keepdims=True)

    def count_hi(thr_hi):
        one = jnp.ones((), BF16)
        zero = jnp.zeros((), BF16)

        def contrib(kb):
            base = pl.multiple_of(kb * kb_sz, kb_sz)
            hit = jnp.where(shi_ref[pl.ds(base, kb_sz), :] >= thr_hi, one, zero)
            return _tree_sum([hit[r:r + 16] for r in range(0, kb_sz, 16)])
        return jnp.sum(sum_blocks(contrib, cnth_ref).astype(F32), axis=0, keepdims=True)

    def hi_body(i, carry):
        t, c_t = carry
        cand = t + lax.shift_left(jnp.int32(1), 31 - i)
        c = count_hi(_top16(_key_to_float(cand)))
        return jnp.where(c >= n_top, cand, t), jnp.where(c >= n_top, c, c_t)

    def lo_body(i, carry):
        t, c_t = carry
        cand = t + lax.shift_left(jnp.int32(1), 15 - i)
        thr_c = _key_to_float(cand)
        c = count(lambda blk, _: jnp.where(blk >= thr_c, 1.0, 0.0))
        return jnp.where(c >= n_top, cand, t), jnp.where(c >= n_top, c, c_t)

    carry = (jnp.full((1, tq), INT_MIN, jnp.int32), jnp.full((1, tq), float(n_top), F32))
    carry = lax.fori_loop(0, 16, hi_body, carry)

    take_all = limit <= n_top

    def unsettled(c_t):
        return jnp.max(jnp.where(take_all, 0.0, jnp.where(c_t == n_top, 0.0, 1.0))).astype(jnp.int32)

    def lo_cond(state):
        i, _, _, go = state
        return jnp.logical_and(i < 16, go > 0)

    def lo_step(state):
        i, t, c_t, _ = state
        t, c_t = lo_body(i, (t, c_t))
        return i + 1, t, c_t, unsettled(c_t)

    _, t_key, n_ge, _ = lax.while_loop(lo_cond, lo_step, (jnp.int32(0),) + carry + (unsettled(carry[1]),))
    thr = _key_to_float(t_key)

    excess = jnp.where(take_all, 0.0, jnp.where(n_ge > n_top, 1.0, 0.0))
    any_excess = jnp.max(excess) > 0.0
    n_tie = jnp.where(any_excess, nkb, 0)
    need = n_top - count(lambda blk, _: jnp.where(blk > thr, 1.0, 0.0), n_tie)
    idx_bits = 14

    def tie_body(i, c):
        cand = c + lax.shift_left(jnp.int32(1), idx_bits - 1 - i)
        f = count(lambda blk, kidx: jnp.where(blk == thr, jnp.where(kidx < cand, 1.0, 0.0), 0.0))
        return jnp.where(f <= need, cand, c)

    cut0 = jnp.where(any_excess, jnp.zeros((1, tq), jnp.int32),
                     jnp.full((1, tq), 2 ** idx_bits, jnp.int32))
    cut = lax.fori_loop(0, jnp.where(any_excess, idx_bits, 0), tie_body, cut0)
    thr = jnp.where(take_all, -jnp.inf, thr)
    cut = jnp.where(take_all, limit, cut)

    def mask_body(kb, carry):
        base = pl.multiple_of(kb * kb_sz, kb_sz)
        blk = s_ref[pl.ds(base, kb_sz), :]
        tie = jnp.where(base + key_off < cut, 0.0, NEG_BIG)
        sel = jnp.where(blk > thr, 0.0, jnp.where(blk == thr, tie, NEG_BIG))
        sel_ref[pl.ds(base, kb_sz), :] = sel.astype(BF16)
        return carry

    lax.fori_loop(0, nkb, mask_body, 0)

    cm_ref[...] = jnp.full(cm_ref.shape, NEG_BIG, BF16)

    def logit_body(kb, carry):
        base = pl.multiple_of(kb * kb_sz, kb_sz)
        dsel = jnp.clip((q0 - base) // kb_sz, 0, 2)
        kblk = k_ref[0, pl.ds(base, kb_sz), :]
        sel = sel_ref[pl.ds(base, kb_sz), :]
        for h in range(A_HEADS):
            sl = h // 4
            lg = _dot_nt(kblk[:, sl * LANES:(sl + 1) * LANES], qx_ref[0, :, h * LANES:(h + 1) * LANES])
            lg = lg.astype(BF16) + sel + bias_ref[dsel, h]
            lg_ref[h, pl.ds(base, kb_sz), :] = lg
            cm_ref[h] = jnp.maximum(cm_ref[h], lg)
        return carry

    for_each_block(logit_body)

    acc_ref[...] = jnp.zeros(acc_ref.shape, F32)
    m_rows = [jnp.max(cm_ref[h].astype(F32), axis=0, keepdims=True).astype(BF16) for h in range(A_HEADS)]

    def pv_body(kb, carry):
        base = pl.multiple_of(kb * kb_sz, kb_sz)
        for h in range(A_HEADS):
            p = jnp.exp2(lg_ref[h, pl.ds(base, kb_sz), :] - m_rows[h])
            acc_ref[h] += jnp.dot(vt_ref[0, kb, h // 4], p, preferred_element_type=F32)
        return carry

    for_each_block(pv_body)

    outs = []
    for h in range(A_HEADS):
        pos = (h // 2) % 2
        outs.append(acc_ref[h, pos * A_HEAD_DIM:(pos + 1) * A_HEAD_DIM, :] / acc_ref[h, LANES:LANES + 1, :])
    o_ref[0] = jnp.concatenate(outs, axis=0).T.astype(BF16)


def _attn_call(rel_bias, qx, iq, misc, kb, vt, ikx, *, tq, l_true, offset, n_top):
    b, tq_total, _ = qx.shape
    lp = kb.shape[1]
    nq = tq_total // tq
    assert offset % ATT_KB == 0 and tq % CHUNK == 0 and ATT_KB % tq == 0 and lp % ATT_KB == 0
    assert lp < 2 ** 14 and l_true <= lp
    kern = functools.partial(_attn_kernel, tq=tq, l_true=l_true, offset=offset, n_top=n_top)
    return pl.pallas_call(
        kern,
        out_shape=jax.ShapeDtypeStruct((b, tq_total, A_WIDTH), BF16),
        grid=(b, nq),
        in_specs=[pl.BlockSpec(memory_space=pltpu.SMEM),
                  pl.BlockSpec((1, tq, QX_W), lambda i, j: (i, j, 0)),
                  pl.BlockSpec((1, tq, IDX_HEADS * IDX_DIM), lambda i, j: (i, j, 0)),
                  pl.BlockSpec((1, tq, LANES), lambda i, j: (i, j, 0)),
                  pl.BlockSpec((1, lp, 256), lambda i, j: (i, 0, 0)),
                  pl.BlockSpec((1, lp // ATT_KB, 2, V_ROWS, ATT_KB), lambda i, j: (i, 0, 0, 0, 0)),
                  pl.BlockSpec((1, lp, 2 * LANES), lambda i, j: (i, 0, 0))],
        out_specs=pl.BlockSpec((1, tq, A_WIDTH), lambda i, j: (i, j, 0)),
        scratch_shapes=[pltpu.VMEM((lp, tq), F32),
                        pltpu.VMEM((lp, tq), BF16),
                        pltpu.VMEM((lp, tq), BF16),
                        pltpu.VMEM((3, A_HEADS, ATT_KB, tq), BF16),
                        pltpu.VMEM((A_HEADS, lp, tq), BF16),
                        pltpu.VMEM((A_HEADS, ATT_KB, tq), BF16),
                        pltpu.VMEM((A_HEADS, V_ROWS, tq), F32),
                        pltpu.VMEM((8, tq), F32),
                        pltpu.VMEM((16, tq), BF16)],
        compiler_params=_cparams(2),
        name="sparse_attn",
    )(rel_bias, qx, iq, misc, kb, vt, ikx)


def _split_bf16(a, n):
    parts = []
    r = a
    for i in range(n):
        p = r.astype(BF16)
        parts.append(p)
        if i + 1 < n:
            r = r - p.astype(F32)
    return parts


def _bdot(a, b, dims=(((1,), (0,)), ((), ()))):
    return lax.dot_general(a.astype(BF16), b.astype(BF16), dims, preferred_element_type=F32)


_NT = (((1,), (1,)), ((), ()))
_TN = (((0,), (0,)), ((), ()))


def _gdn_kernel(x_ref, misc_ref, gz_ref, convw_ref, hist_ref, s0_ref, gnw_ref, alane_ref, dlane_ref,
                o_ref, conv_o, s_o, xp_ref, y_ref, st_ref, vn_ref,
                u_ref, w_ref, qk_ref, qe_ref, kd_ref, el_ref, gate_ref, *, cb, t_true, nblk):
    c = CHUNK
    step = pl.program_id(1)
    blk = jnp.minimum(step, nblk - 1)
    carried = (u_ref, w_ref, qk_ref, qe_ref, kd_ref, el_ref, gate_ref)
    u_nx, w_nx, qk_nx, qe_nx, kd_nx, el_nx, gate_nx = (r.at[1] for r in carried)
    u_ref, w_ref, qk_ref, qe_ref, kd_ref, el_ref, gate_ref = (r.at[0] for r in carried)

    @pl.when(step == 0)
    def _():
        xp_ref[8 - (CONV_W - 1):8, :] = hist_ref[0]
        st_ref[...] = s0_ref[0]
        for r in carried:
            r[...] = jnp.zeros(r.shape, r.dtype)

    heads = range(G_HEADS)
    for r in carried:
        r[0] = r[1]

    def scan_stages():
        vn_ref[...] = jnp.zeros(vn_ref.shape, F32)
        for ck in range(cb // c):
            r0, r1 = ck * c, (ck + 1) * c
            s_prev = [st_ref[h] for h in heads]
            v_new = [u_ref[h, r0:r1, :] - _bdot(w_ref[h, r0:r1, :], s_prev[h]) for h in heads]
            o_state = [_bdot(qe_ref[h, r0:r1, :], s_prev[h]) for h in heads]
            yield
            for h in heads:
                vn_ref[h, r0:r1, :] = v_new[h]
            o = [o_state[h] + _bdot(qk_ref[h, r0:r1, :], vn_ref[h]) for h in heads]
            for h in heads:
                s_new = s_prev[h] * el_ref[h, ck, 0:1, :] + _bdot(kd_ref[h, r0:r1, :], v_new[h], _TN)
                st_ref[h] = jnp.where(step > 0, s_new, s_prev[h])
            yield
            for h in heads:
                on = o[h] * lax.rsqrt(jnp.mean(o[h] * o[h], axis=-1, keepdims=True) + EPS) * gnw_ref[...]
                gate = gate_ref[r0:r1, h * G_VAL_DIM:(h + 1) * G_VAL_DIM]
                o_ref[0, r0:r1, h * G_VAL_DIM:(h + 1) * G_VAL_DIM] = (on * gate).astype(BF16)
            yield
        s_o[0] = st_ref[...]

    scan = scan_stages()

    def tick():
        next(scan, None)

    tick()

    xp_ref[8:8 + cb, :] = x_ref[0]
    y = convw_ref[CONV_W - 1:CONV_W, :] * xp_ref[8:8 + cb, :]
    for j in range(CONV_W - 1):
        y = y + convw_ref[j:j + 1, :] * xp_ref[5 + j:5 + j + cb, :]
    y_ref[...] = _silu(y)

    last_row = (t_true - 1) % cb
    conv_o[0] = xp_ref[8 + last_row - (CONV_W - 2):8 + last_row + 1, :]

    xp_ref[8 - (CONV_W - 1):8, :] = xp_ref[8 + cb - (CONV_W - 1):8 + cb, :]

    ri = lax.broadcasted_iota(jnp.int32, (cb, cb), 0)
    ci = lax.broadcasted_iota(jnp.int32, (cb, cb), 1)
    lag = jnp.where((ri // c) == (ci // c), ri - ci, -1)
    tri = lag >= 0
    strict = lag > 0
    eye = ri == ci
    tril_bf = jnp.where(tri, 1.0, 0.0).astype(BF16)
    eye_f = jnp.where(eye, 1.0, 0.0)

    ms = misc_ref[0]
    tok = blk * cb + lax.broadcasted_iota(jnp.int32, (cb, 1), 0)
    live = tok < t_true
    beta_s = jnp.where(live, jax.nn.sigmoid(ms), 0.0)
    z = ms + dlane_ref[...]
    softplus = jnp.maximum(z, 0.0) + jnp.log(1.0 + jnp.exp(-jnp.abs(z)))
    g_s = jnp.where(live, -jnp.exp(alane_ref[...]) * softplus, 0.0)
    gc_s = None
    for piece in _split_bf16(g_s, 3):
        t = jnp.dot(tril_bf, piece, preferred_element_type=F32)
        gc_s = t if gc_s is None else gc_s + t

    tick()

    q, k, v, beta, gc, decay, kb, eg = ([None] * G_HEADS for _ in range(8))
    for h in heads:
        qh = y_ref[:, h * G_KEY_DIM:(h + 1) * G_KEY_DIM]
        kh = y_ref[:, G_WIDTH + h * G_KEY_DIM:G_WIDTH + (h + 1) * G_KEY_DIM]
        v[h] = y_ref[:, 2 * G_WIDTH + h * G_VAL_DIM:2 * G_WIDTH + (h + 1) * G_VAL_DIM]
        q[h] = qh * lax.rsqrt(jnp.sum(qh * qh, axis=-1, keepdims=True) + 1e-6) * (G_KEY_DIM ** -0.5)
        k[h] = kh * lax.rsqrt(jnp.sum(kh * kh, axis=-1, keepdims=True) + 1e-6)
        beta[h] = beta_s[:, MISC_GB + h:MISC_GB + h + 1]
        gc[h] = gc_s[:, MISC_GA + h:MISC_GA + h + 1]
        gc_b = jnp.broadcast_to(gc[h], (cb, cb))
        gc_row = jnp.sum(jnp.where(eye, gc_b, 0.0), axis=0, keepdims=True)
        decay[h] = jnp.exp(jnp.where(tri, gc_b - gc_row, NEG_BIG))
        kb[h] = k[h] * beta[h]
        eg[h] = jnp.exp(gc[h])
    tick()
    m = [jnp.where(strict, _bdot(kb[h], k[h], _NT) * decay[h], 0.0) for h in heads]
    rhs = [jnp.concatenate([v[h] * beta[h], kb[h] * eg[h]], axis=1) for h in heads]
    qk = [jnp.where(tri, _bdot(q[h], k[h], _NT) * decay[h], 0.0) for h in heads]
    tick()
    pw = [-m[h] for h in heads]
    inv = [eye_f + pw[h] for h in heads]
    for _ in range(5):
        pw = [_bdot(pw[h], pw[h]) for h in heads]
        tick()
        inv = [inv[h] + _bdot(inv[h], pw[h]) for h in heads]
    tick()
    sol = [_bdot(inv[h], rhs[h]) for h in heads]
    tick()
    m_sol = []
    for h in heads:
        m_hi, m_lo = _split_bf16(m[h], 2)
        s_hi, s_lo = _split_bf16(sol[h], 2)
        m_sol.append(_bdot(m_hi, s_hi) + _bdot(m_hi, s_lo) + _bdot(m_lo, s_hi))
    tick()
    sol = [sol[h] + _bdot(inv[h], rhs[h] - sol[h] - m_sol[h]) for h in heads]
    for _ in scan:
        pass
    gate_nx[...] = _silu(gz_ref[0])
    for h in heads:
        u_nx[h] = sol[h][:, :G_VAL_DIM]
        w_nx[h] = sol[h][:, G_VAL_DIM:].astype(BF16)
        qk_nx[h] = qk[h].astype(BF16)
        qe_nx[h] = (q[h] * eg[h]).astype(BF16)
        for ck in range(cb // c):
            r0, r1 = ck * c, (ck + 1) * c
            g_last = gc[h][r1 - 1:r1, :]
            kd_nx[h, r0:r1, :] = (k[h][r0:r1] * jnp.exp(g_last - gc[h][r0:r1])).astype(BF16)
            el_nx[h, ck] = jnp.broadcast_to(jnp.exp(g_last), (8, LANES))


def _gdn_call(gqkv, misc, gz, conv_w, hist, s0, gnorm_w, a_lane, d_lane, *, cb, t_true):
    b, t_pad, _ = gqkv.shape
    nblk = t_pad // cb
    assert (nblk - 1) * cb < t_true <= t_pad and (t_true - 1) % cb >= CONV_W - 2
    kern = functools.partial(_gdn_kernel, cb=cb, t_true=t_true, nblk=nblk)
    prep = lambda i, j: (i, jnp.minimum(j, nblk - 1), 0)
    scan = lambda i, j: (i, jnp.maximum(j - 1, 0), 0)
    nch = cb // CHUNK
    return pl.pallas_call(
        kern,
        out_shape=[jax.ShapeDtypeStruct((b, t_pad, G_WIDTH), BF16),
                   jax.ShapeDtypeStruct((b, CONV_W - 1, CONV_CH), F32),
                   jax.ShapeDtypeStruct((b, G_HEADS, G_KEY_DIM, G_VAL_DIM), F32)],
        grid=(b, nblk + 1),
        in_specs=[pl.BlockSpec((1, cb, CONV_CH), prep),
                  pl.BlockSpec((1, cb, LANES), prep),
                  pl.BlockSpec((1, cb, G_WIDTH), prep),
                  pl.BlockSpec((CONV_W, CONV_CH), lambda i, j: (0, 0)),
                  pl.BlockSpec((1, CONV_W - 1, CONV_CH), lambda i, j: (i, 0, 0)),
                  pl.BlockSpec((1, G_HEADS, G_KEY_DIM, G_VAL_DIM), lambda i, j: (i, 0, 0, 0)),
                  pl.BlockSpec((1, G_VAL_DIM), lambda i, j: (0, 0)),
                  pl.BlockSpec((1, LANES), lambda i, j: (0, 0)),
                  pl.BlockSpec((1, LANES), lambda i, j: (0, 0))],
        out_specs=[pl.BlockSpec((1, cb, G_WIDTH), scan),
                   pl.BlockSpec((1, CONV_W - 1, CONV_CH), lambda i, j: (i, 0, 0)),
                   pl.BlockSpec((1, G_HEADS, G_KEY_DIM, G_VAL_DIM), lambda i, j: (i, 0, 0, 0))],
        scratch_shapes=[pltpu.VMEM((cb + 8, CONV_CH), F32),
                        pltpu.VMEM((cb, CONV_CH), F32),
                        pltpu.VMEM((G_HEADS, G_KEY_DIM, G_VAL_DIM), F32),
                        pltpu.VMEM((G_HEADS, cb, G_VAL_DIM), F32),
                        pltpu.VMEM((2, G_HEADS, cb, G_VAL_DIM), F32),
                        pltpu.VMEM((2, G_HEADS, cb, G_KEY_DIM), BF16),
                        pltpu.VMEM((2, G_HEADS, cb, cb), BF16),
                        pltpu.VMEM((2, G_HEADS, cb, G_KEY_DIM), BF16),
                        pltpu.VMEM((2, G_HEADS, cb, G_KEY_DIM), BF16),
                        pltpu.VMEM((2, G_HEADS, nch, 8, LANES), F32),
                        pltpu.VMEM((2, cb, G_WIDTH), F32)],
        compiler_params=_cparams(2),
        name="gated_delta",
    )(gqkv, misc, gz, conv_w, hist, s0, gnorm_w, a_lane, d_lane)


def _pack_w_in(w_in):
    d = w_in.shape[0]
    splits = (512, 256, 256, 512, 64, 8, 512, 512, 512, 512, 4, 4)
    offs = np.concatenate([[0], np.cumsum(splits)])
    aq, ak, av, iq, ik, iw, gq, gk, gv, gz, gb, ga = [w_in[:, offs[i]:offs[i + 1]] for i in range(12)]
    zeros64 = jnp.zeros((d, A_HEAD_DIM), w_in.dtype)
    qx = []
    for h in range(A_HEADS):
        qh = aq[:, h * A_HEAD_DIM:(h + 1) * A_HEAD_DIM] * (A_HEAD_DIM ** -0.5 * LOG2E)
        qx += [qh, zeros64] if (h // 2) % 2 == 0 else [zeros64, qh]
    ikx = [ik, zeros64, zeros64, ik]
    misc = [ik, iw, gb, ga, jnp.zeros((d, LANES - MISC_GA - G_HEADS), w_in.dtype)]
    cols = qx + [ak, av, iq * (IDX_DIM ** -0.5)] + ikx + misc + [gq, gk, gv, gz]
    packed = jnp.concatenate(cols, axis=1).astype(BF16)
    assert packed.shape[1] == IN_PACKED
    return packed


def _pack_ffn(w_gate, w_up, w_down):
    d = w_gate.shape[0]
    wg = w_gate.reshape(d, N_FF_CHUNKS, FF_CHUNK)
    wu = w_up.reshape(d, N_FF_CHUNKS, FF_CHUNK)
    wgu = jnp.transpose(jnp.concatenate([wg, wu], axis=2), (1, 0, 2)).astype(BF16)
    wd = w_down.reshape(N_FF_CHUNKS, FF_CHUNK, d).astype(BF16)
    return wgu, wd


def _ikx_layout(ik):
    z = jnp.zeros_like(ik)
    return jnp.concatenate([ik, z, z, ik], axis=-1)


def _lane_vec(vals, lane0):
    return jnp.zeros((1, LANES), F32).at[0, lane0:lane0 + vals.shape[0]].set(vals.astype(F32))


def _run(x, mod, past, layers, rel_bias, norm_final):
    b, t, d = x.shape
    n = b * t
    per_token = t < TOKEN_TILE
    tm = n if per_token else TOKEN_TILE
    tiles_per_batch = None if per_token else t // tm
    tok = dict(tm=tm, tiles_per_batch=tiles_per_batch, per_token=per_token)

    x2 = x.reshape(n, d)
    states = []
    for li, lw in enumerate(layers):
        m = mod[li]

        def mvec(kidx):
            row = m[:, kidx]
            return jnp.repeat(row, t, axis=0) if per_token else row[:, None, :]

        sh1, sc1, gt1, sh2, sc2, gt2, sh3, sc3, gt3 = [mvec(i) for i in range(N_MOD)]
        x2 = _ffn_call(x2, sh1, sc1, gt1, lw["norm_ffn1"], lw["wgu1"], lw["wd1"], None, **tok)
        qx, k, kb, v, vb, iq, ikx, misc, gqkv, gz = _inproj_call(
            x2, sh2, sc2, lw["norm_mix"], lw["w_in"], **tok)

        if past is None:
            offset, l_true, tq = 0, t, ATT_TQ
            k_all, v_all, ikx_all = (a.reshape(b, t, -1) for a in (kb, vb, ikx))
            qx3, iq3, misc3 = (a.reshape(b, t, -1) for a in (qx, iq, misc))
            conv_hist = jnp.zeros((b, CONV_W - 1, CONV_CH), F32)
            s0 = jnp.zeros((b, G_HEADS, G_KEY_DIM, G_VAL_DIM), F32)
        else:
            k_hist, v_hist, ik_hist, conv_hist, s0 = (p[li] for p in past)
            offset = k_hist.shape[1]
            l_true = offset + t
            tq = LANES
            lp = -(-l_true // ATT_KB) * ATT_KB
            pad_k = lambda a: jnp.pad(a, ((0, 0), (0, lp - l_true), (0, 0)))
            k_all = pad_k(jnp.concatenate([k_hist.reshape(b, offset, -1).astype(BF16),
                                           kb.reshape(b, t, -1)], axis=1))
            v_all = pad_k(jnp.concatenate([v_hist.reshape(b, offset, -1).astype(BF16),
                                           vb.reshape(b, t, -1)], axis=1))
            ikx_all = pad_k(jnp.concatenate([_ikx_layout(ik_hist.astype(BF16)),
                                             ikx.reshape(b, t, -1)], axis=1))
            pad_q = lambda a: jnp.pad(a.reshape(b, t, -1), ((0, 0), (0, tq - t), (0, 0)))
            qx3, iq3, misc3 = pad_q(qx), pad_q(iq), pad_q(misc)
            s0 = s0.astype(F32)
            conv_hist = conv_hist.astype(F32)
        lp = k_all.shape[1]
        n_top = min(TOPK_MAX, l_true // 4)
        vt = jnp.transpose(v_all.reshape(b, lp // ATT_KB, ATT_KB, 2, LANES), (0, 1, 3, 4, 2))
        vt = jnp.concatenate([vt, jnp.ones((b, lp // ATT_KB, 2, V_ROWS - LANES, ATT_KB), BF16)], axis=3)
        attn = _attn_call(rel_bias, qx3, iq3, misc3, k_all, vt, ikx_all,
                          tq=tq, l_true=l_true, offset=offset, n_top=n_top)
        attn = attn[:, :t].reshape(n, A_WIDTH)

        cb = GDN_BLOCK if t % GDN_BLOCK == 0 else CHUNK
        t_pad = -(-t // cb) * cb
        pad_t = lambda a: jnp.pad(a.reshape(b, t, -1), ((0, 0), (0, t_pad - t), (0, 0)))
        gdn, conv_new, s_new = _gdn_call(pad_t(gqkv), pad_t(misc), pad_t(gz), lw["conv_w"], conv_hist, s0,
                                         lw["gnorm_w"], lw["a_lane"], lw["d_lane"], cb=cb, t_true=t)
        gdn = gdn[:, :t].reshape(n, G_WIDTH)

        nf = norm_final if li == len(layers) - 1 else None
        x2 = _ffn_call(x2, sh3, sc3, gt3, lw["norm_ffn2"], lw["wgu2"], lw["wd2"], nf,
                       mixer=(gt2, attn, gdn, lw["wo_a"], lw["wo_g"]), **tok)

        states.append((k.reshape(b, t, A_KV_HEADS, A_HEAD_DIM), v.reshape(b, t, A_KV_HEADS, A_HEAD_DIM),
                       misc.reshape(b, t, LANES)[..., :IDX_DIM], conv_new, s_new))
    stacked = [jnp.stack(s, axis=0) for s in zip(*states)]
    return x2.reshape(b, t, d), stacked


def kernel(x_prompt, x_sample, cache_k, cache_v, cache_idx_k, state_conv, state_delta, c_prompt, c_sample,
           w_mod, b_mod, norm_ffn1, norm_mix, norm_ffn2, ffn1_w_gate, ffn1_w_up, ffn1_w_down,
           ffn2_w_gate, ffn2_w_up, ffn2_w_down, w_in, w_out, rel_bias, conv_w, a_log, dt_bias, gnorm_w,
           norm_final):
    depth = w_mod.shape[0]
    bp = c_prompt.shape[0]
    c_all = jnp.concatenate([c_prompt, c_sample], axis=0)
    layers, mods_p, mods_s = [], [], []
    for l in range(depth):
        mod = _mod_call(c_all, w_mod[l], b_mod[l]).reshape(c_all.shape[0], N_MOD, D_MODEL)
        mods_p.append(mod[:bp])
        mods_s.append(mod[bp:])
        wgu1, wd1 = _pack_ffn(ffn1_w_gate[l], ffn1_w_up[l], ffn1_w_down[l])
        wgu2, wd2 = _pack_ffn(ffn2_w_gate[l], ffn2_w_up[l], ffn2_w_down[l])
        wo = w_out[l].astype(BF16)
        layers.append(dict(
            norm_ffn1=norm_ffn1[l].reshape(1, -1), norm_mix=norm_mix[l].reshape(1, -1),
            norm_ffn2=norm_ffn2[l].reshape(1, -1), wgu1=wgu1, wd1=wd1, wgu2=wgu2, wd2=wd2,
            w_in=_pack_w_in(w_in[l]), wo_a=wo[:A_WIDTH], wo_g=wo[A_WIDTH:],
            conv_w=conv_w[l], gnorm_w=gnorm_w[l].reshape(1, -1),
            a_lane=_lane_vec(a_log[l], MISC_GA), d_lane=_lane_vec(dt_bias[l], MISC_GA)))
    nf = norm_final.reshape(1, -1)
    y_p, (k_p, v_p, ik_p, conv_p, delta_p) = _run(x_prompt, mods_p, None, layers, rel_bias, nf)
    past = (cache_k, cache_v, cache_idx_k, state_conv, state_delta)
    y_s, (k_s, v_s, ik_s, conv_s, delta_s) = _run(x_sample, mods_s, past, layers, rel_bias, nf)
    return (y_p, y_s, k_p, v_p, ik_p, conv_p, delta_p, k_s, v_s, ik_s, conv_s, delta_s)
```

```python
import functools

import jax
import jax.numpy as jnp
import numpy as np
from jax import lax
from jax.experimental import pallas as pl
from jax.experimental.pallas import tpu as pltpu

F32 = jnp.float32
BF16 = jnp.bfloat16

D_MODEL = 1024
CHUNK = 64
A_HEAD_DIM = 64
A_HEADS = 8
A_KV_HEADS = 4
A_WIDTH = A_HEADS * A_HEAD_DIM
IDX_HEADS = 8
IDX_DIM = 64
TOPK_MAX = 256
REL_BUCKETS = 32
G_KEY_DIM = 128
G_VAL_DIM = 128
G_HEADS = 4
G_WIDTH = G_HEADS * G_VAL_DIM
CONV_W = 4
CONV_CH = 2 * G_HEADS * G_KEY_DIM + G_HEADS * G_VAL_DIM
D_FF = 2816
N_MOD = 9
EPS = 1e-6

LANES = 128
MXU_DIM = 256
VMEM_LIMIT_BYTES = 56 * 1024 * 1024

FF_CHUNK = MXU_DIM
N_FF_CHUNKS = D_FF // FF_CHUNK
TOKEN_TILE = 512
ATT_TQ = 256
ATT_KB = 256
GDN_BLOCK = 256

QX_W = A_HEADS * LANES
Q_PAIRS = ((0, 2), (1, 3), (4, 6), (5, 7))
OFF_QX = 0
OFF_K = OFF_QX + A_WIDTH
OFF_V = OFF_K + A_KV_HEADS * A_HEAD_DIM
OFF_IQ = OFF_V + A_KV_HEADS * A_HEAD_DIM
OFF_IKX = OFF_IQ + IDX_HEADS * IDX_DIM
OFF_MISC = OFF_IKX + LANES
OFF_GQKV = OFF_MISC + LANES
OFF_GZ = OFF_GQKV + CONV_CH
IN_PACKED = OFF_GZ + G_WIDTH
MISC_IW = IDX_DIM
MISC_GB = MISC_IW + IDX_HEADS
MISC_GA = MISC_GB + G_HEADS

NEG_BIG = -1e30
INT_MIN = -2 ** 31
LOG2E = 1.4426950408889634
V_ROWS = LANES + 16


def _cparams(n_axes):
    return pltpu.CompilerParams(dimension_semantics=("arbitrary",) * n_axes,
                                vmem_limit_bytes=VMEM_LIMIT_BYTES)


def _resident(shape):
    nd = len(shape)
    return pl.BlockSpec(shape, lambda *_: (0,) * nd, pipeline_mode=pl.Buffered(1))


def _dot_nt(a, b):
    return lax.dot_general(a, b, (((1,), (1,)), ((), ())), preferred_element_type=F32)


def _rms_mod(x, gain, shift, scale):
    ms = jnp.mean(x * x, axis=-1, keepdims=True)
    y = x * lax.rsqrt(ms + EPS) * gain
    return y * (1.0 + scale) + shift


def _silu(x):
    return x * jax.nn.sigmoid(x)


def _tree_sum(parts):
    while len(parts) > 1:
        parts = [a + b for a, b in zip(parts[0::2], parts[1::2])] + ([parts[-1]] if len(parts) % 2 else [])
    return parts[0]


def _mod_kernel(c_ref, w_ref, b_ref, o_ref):
    s = _silu(c_ref[...]).astype(BF16)
    o_ref[...] = jnp.dot(s, w_ref[...].astype(BF16), preferred_element_type=F32) + b_ref[...]


def _mod_call(c, w_mod, b_mod):
    rows, d = c.shape
    n = w_mod.shape[1]
    tn = D_MODEL
    return pl.pallas_call(
        _mod_kernel,
        out_shape=jax.ShapeDtypeStruct((rows, n), F32),
        grid=(n // tn,),
        in_specs=[pl.BlockSpec((rows, d), lambda j: (0, 0)),
                  pl.BlockSpec((d, tn), lambda j: (0, j)),
                  pl.BlockSpec((1, tn), lambda j: (0, j))],
        out_specs=pl.BlockSpec((rows, tn), lambda j: (0, j)),
        compiler_params=_cparams(1),
        name="mod",
    )(c, w_mod, b_mod.reshape(1, n))


def _mod_specs(per_token, tm, tiles_per_batch):
    if per_token:
        return pl.BlockSpec((tm, D_MODEL), lambda i: (i, 0))
    return pl.BlockSpec((None, 1, D_MODEL), lambda i: (i // tiles_per_batch, 0, 0))


def _ffn_kernel(x_ref, sh_ref, sc_ref, gt_ref, gain_ref, wgu_ref, wd_ref, *rest, final_norm, mixer):
    rest = list(rest)
    if mixer:
        gm_ref, a_ref, g_ref, wa_ref, wg_ref = rest[:5]
        rest = rest[5:]
    if final_norm:
        nf_ref = rest.pop(0)
    o_ref, acc_ref = rest
    x = x_ref[...]
    if mixer:
        x = x + gm_ref[...] * (jnp.dot(a_ref[...], wa_ref[...], preferred_element_type=F32)
                               + jnp.dot(g_ref[...], wg_ref[...], preferred_element_type=F32))
    h = _rms_mod(x, gain_ref[...], sh_ref[...], sc_ref[...]).astype(BF16)
    for j in range(N_FF_CHUNKS):
        ab = jnp.dot(h, wgu_ref[j], preferred_element_type=F32)
        g = (_silu(ab[:, :FF_CHUNK]) * ab[:, FF_CHUNK:]).astype(BF16)
        d = jnp.dot(g, wd_ref[j], preferred_element_type=F32)
        if j == 0:
            acc_ref[...] = d
        else:
            acc_ref[...] += d
    y = x + 0.5 * gt_ref[...] * acc_ref[...]
    if final_norm:
        ms = jnp.mean(y * y, axis=-1, keepdims=True)
        y = y * lax.rsqrt(ms + EPS) * nf_ref[...]
    o_ref[...] = y


def _ffn_call(x2, sh, sc, gt, gain, wgu, wd, norm_final, mixer=None, *, tm, tiles_per_batch, per_token):
    n = x2.shape[0]
    mspec = _mod_specs(per_token, tm, tiles_per_batch)
    in_specs = [pl.BlockSpec((tm, D_MODEL), lambda i: (i, 0)), mspec, mspec, mspec,
                _resident((1, D_MODEL)), _resident(wgu.shape), _resident(wd.shape)]
    args = [x2, sh, sc, gt, gain, wgu, wd]
    if mixer is not None:
        gm, attn, gdn, wa, wg = mixer
        in_specs += [mspec, pl.BlockSpec((tm, A_WIDTH), lambda i: (i, 0)),
                     pl.BlockSpec((tm, G_WIDTH), lambda i: (i, 0)), _resident(wa.shape), _resident(wg.shape)]
        args += [gm, attn, gdn, wa, wg]
    final_norm = norm_final is not None
    if final_norm:
        in_specs.append(_resident((1, D_MODEL)))
        args.append(norm_final)
    return pl.pallas_call(
        functools.partial(_ffn_kernel, final_norm=final_norm, mixer=mixer is not None),
        out_shape=jax.ShapeDtypeStruct((n, D_MODEL), F32),
        grid=(n // tm,),
        in_specs=in_specs,
        out_specs=pl.BlockSpec((tm, D_MODEL), lambda i: (i, 0)),
        scratch_shapes=[pltpu.VMEM((tm, D_MODEL), F32)],
        compiler_params=_cparams(1),
        name="ffn_final" if final_norm else "ffn",
    )(*args)


def _inproj_kernel(x_ref, sh_ref, sc_ref, gain_ref, w_ref,
                   qx_o, k_o, kb_o, v_o, vb_o, iq_o, ikx_o, misc_o, gqkv_o, gz_o):
    h = _rms_mod(x_ref[...], gain_ref[...], sh_ref[...], sc_ref[...]).astype(BF16)

    def mm(off, width):
        return jnp.dot(h, w_ref[:, off:off + width], preferred_element_type=F32)

    qx_o[...] = mm(OFF_QX, A_WIDTH).astype(BF16)
    k = mm(OFF_K, OFF_V - OFF_K)
    k_o[...] = k
    kb_o[...] = k.astype(BF16)
    v = mm(OFF_V, OFF_IQ - OFF_V)
    v_o[...] = v
    vb_o[...] = v.astype(BF16)
    iq_o[...] = mm(OFF_IQ, OFF_IKX - OFF_IQ).astype(BF16)
    ikx_o[...] = mm(OFF_IKX, OFF_MISC - OFF_IKX).astype(BF16)
    misc_o[...] = mm(OFF_MISC, LANES)
    gqkv_o[...] = mm(OFF_GQKV, CONV_CH)
    gz_o[...] = mm(OFF_GZ, G_WIDTH)


def _inproj_call(x2, sh, sc, gain, w_packed, *, tm, tiles_per_batch, per_token):
    n = x2.shape[0]
    mspec = _mod_specs(per_token, tm, tiles_per_batch)
    widths = [(A_WIDTH, BF16), (256, F32), (256, BF16), (256, F32), (256, BF16), (512, BF16),
              (LANES, BF16), (LANES, F32), (CONV_CH, F32), (G_WIDTH, F32)]
    return pl.pallas_call(
        _inproj_kernel,
        out_shape=[jax.ShapeDtypeStruct((n, w), dt) for w, dt in widths],
        grid=(n // tm,),
        in_specs=[pl.BlockSpec((tm, D_MODEL), lambda i: (i, 0)), mspec, mspec,
                  _resident((1, D_MODEL)), _resident(w_packed.shape)],
        out_specs=[pl.BlockSpec((tm, w), lambda i: (i, 0)) for w, _ in widths],
        compiler_params=_cparams(1),
        name="inproj",
    )(x2, sh, sc, gain, w_packed)


def _rel_bucket_int(rel):
    n = jnp.abs(rel)
    large = jnp.full(rel.shape, 8, jnp.int32)
    for th in (12, 16, 23, 32, 46, 64, 91):
        large = large + jnp.where(n >= th, 1, 0)
    return jnp.where(rel > 0, REL_BUCKETS // 2, 0) + jnp.where(n < 8, n, large)


def _key_to_float(t):
    bits = jnp.where(t >= 0, t, t ^ jnp.int32(0x7FFFFFFF))
    return lax.bitcast_convert_type(bits, F32)


def _top16(x):
    bits = lax.bitcast_convert_type(x, jnp.int32) & jnp.int32(-65536)
    return lax.bitcast_convert_type(bits, F32).astype(BF16)


def _attn_kernel(relb_ref, q_ref, iq_ref, misc_ref, k_ref, vt_ref, ikx_ref, o_ref,
                 s_ref, shi_ref, sel_ref, bias_ref, lg_ref, cm_ref, acc_ref, cnt_ref, cnth_ref, qx_ref,
                 *, tq, l_true, offset, n_top):
    kb_sz = ATT_KB
    low_half_q = lax.broadcasted_iota(jnp.int32, (tq, LANES), 1) < A_HEAD_DIM
    low_half_k = lax.broadcasted_iota(jnp.int32, (kb_sz, LANES), 1) < A_HEAD_DIM

    for s, pair in enumerate(Q_PAIRS):
        slab = q_ref[0, :, s * LANES:(s + 1) * LANES]
        zero = jnp.zeros_like(slab)
        qx_ref[:, pair[0] * LANES:(pair[0] + 1) * LANES] = jnp.where(low_half_q, slab, zero)
        qx_ref[:, pair[1] * LANES:(pair[1] + 1) * LANES] = jnp.where(low_half_q, zero, slab)
    first = (pl.program_id(0) == 0) & (pl.program_id(1) == 0)
    q0 = offset + pl.program_id(1) * tq
    kmax = jnp.minimum(q0 + tq, l_true)
    nkb = (kmax + kb_sz - 1) // kb_sz

    key_off = lax.broadcasted_iota(jnp.int32, (kb_sz, tq), 0)
    qry_off = lax.broadcasted_iota(jnp.int32, (kb_sz, tq), 1)

    @pl.when(first)
    def _():
        for d in range(3):
            bucket = _rel_bucket_int(key_off - qry_off - d * kb_sz)
            for h in range(A_HEADS):
                t = jnp.zeros((kb_sz, tq), F32)
                for b in range(REL_BUCKETS):
                    t = jnp.where(bucket == b, relb_ref[b, h], t)
                bias_ref[d, h] = (t * LOG2E).astype(BF16)

    qpos = q0 + lax.broadcasted_iota(jnp.int32, (1, tq), 1)
    limit = jnp.minimum((qpos // CHUNK + 1) * CHUNK, l_true)

    iw_t = misc_ref[0].T[MISC_IW:MISC_IW + IDX_HEADS, :] * (IDX_HEADS ** -0.5)

    def score_body(kb, carry):
        base = pl.multiple_of(kb * kb_sz, kb_sz)
        ik2 = ikx_ref[0, pl.ds(base, kb_sz), :]
        ik_lo = jnp.where(low_half_k, ik2, jnp.zeros_like(ik2))
        ik_hi = jnp.where(low_half_k, jnp.zeros_like(ik2), ik2)
        s = jnp.zeros((kb_sz, tq), F32)
        for j in range(IDX_HEADS // 2):
            slab = iq_ref[0, :, j * LANES:(j + 1) * LANES]
            d0 = _dot_nt(ik_lo, slab)
            d1 = _dot_nt(ik_hi, slab)
            s = s + iw_t[2 * j:2 * j + 1, :] * jnp.maximum(d0, 0.0)
            s = s + iw_t[2 * j + 1:2 * j + 2, :] * jnp.maximum(d1, 0.0)
        s = jnp.where(base + key_off < limit, s, -jnp.inf)
        s_ref[pl.ds(base, kb_sz), :] = s
        shi_ref[pl.ds(base, kb_sz), :] = _top16(s)
        return carry

    def for_each_block(body):
        def quad(i, carry):
            for r in range(4):
                body(4 * i + r, carry)
            return carry
        lax.fori_loop(0, nkb // 4, quad, 0)
        done = (nkb // 4) * 4

        @pl.when(nkb % 4 >= 2)
        def _():
            body(done, 0)
            body(done + 1, 0)

        @pl.when(nkb % 2 == 1)
        def _():
            body(nkb - 1, 0)

    for_each_block(score_body)

    def sum_blocks(contrib, acc_ref, n=nkb):
        acc_ref[...] = jnp.zeros(acc_ref.shape, acc_ref.dtype)

        def quad(i, carry):
            acc_ref[...] += _tree_sum([contrib(4 * i + r) for r in range(4)])
            return carry
        lax.fori_loop(0, n // 4, quad, 0)
        done = (n // 4) * 4

        @pl.when(n % 4 >= 2)
        def _():
            acc_ref[...] += contrib(done) + contrib(done + 1)

        @pl.when(n % 2 == 1)
        def _():
            acc_ref[...] += contrib(n - 1)
        return acc_ref[...]

    def count(pred, n=nkb):
        def contrib(kb):
            base = pl.multiple_of(kb * kb_sz, kb_sz)
            hit = pred(s_ref[pl.ds(base, kb_sz), :], base + key_off)
            return _tree_sum([hit[r:r + 8] for r in range(0, kb_sz, 8)])
        return jnp.sum(sum_blocks(contrib, cnt_ref, n), axis=0, keepdims=True)

    def count_hi(thr_hi):
        one = jnp.ones((), BF16)
        zero = jnp.zeros((), BF16)

        def contrib(kb):
            base = pl.multiple_of(kb * kb_sz, kb_sz)
            hit = jnp.where(shi_ref[pl.ds(base, kb_sz), :] >= thr_hi, one, zero)
            return _tree_sum([hit[r:r + 16] for r in range(0, kb_sz, 16)])
        return jnp.sum(sum_blocks(contrib, cnth_ref).astype(F32), axis=0, keepdims=True)

    def hi_body(i, carry):
        t, c_t = carry
        cand = t + lax.shift_left(jnp.int32(1), 31 - i)
        c = count_hi(_top16(_key_to_float(cand)))
        return jnp.where(c >= n_top, cand, t), jnp.where(c >= n_top, c, c_t)

    def lo_body(i, carry):
        t, c_t = carry
        cand = t + lax.shift_left(jnp.int32(1), 15 - i)
        thr_c = _key_to_float(cand)
        c = count(lambda blk, _: jnp.where(blk >= thr_c, 1.0, 0.0))
        return jnp.where(c >= n_top, cand, t), jnp.where(c >= n_top, c, c_t)

    carry = (jnp.full((1, tq), INT_MIN, jnp.int32), jnp.full((1, tq), float(n_top), F32))
    carry = lax.fori_loop(0, 16, hi_body, carry)
    t_key, n_ge = lax.fori_loop(0, 16, lo_body, carry)
    thr = _key_to_float(t_key)

    take_all = limit <= n_top
    excess = jnp.where(take_all, 0.0, jnp.where(n_ge > n_top, 1.0, 0.0))
    any_excess = jnp.max(excess) > 0.0
    n_tie = jnp.where(any_excess, nkb, 0)
    need = n_top - count(lambda blk, _: jnp.where(blk > thr, 1.0, 0.0), n_tie)
    idx_bits = 14

    def tie_body(i, c):
        cand = c + lax.shift_left(jnp.int32(1), idx_bits - 1 - i)
        f = count(lambda blk, kidx: jnp.where(blk == thr, jnp.where(kidx < cand, 1.0, 0.0), 0.0))
        return jnp.where(f <= need, cand, c)

    cut0 = jnp.where(any_excess, jnp.zeros((1, tq), jnp.int32),
                     jnp.full((1, tq), 2 ** idx_bits, jnp.int32))
    cut = lax.fori_loop(0, jnp.where(any_excess, idx_bits, 0), tie_body, cut0)
    thr = jnp.where(take_all, -jnp.inf, thr)
    cut = jnp.where(take_all, limit, cut)

    def mask_body(kb, carry):
        base = pl.multiple_of(kb * kb_sz, kb_sz)
        blk = s_ref[pl.ds(base, kb_sz), :]
        tie = jnp.where(base + key_off < cut, 0.0, NEG_BIG)
        sel = jnp.where(blk > thr, 0.0, jnp.where(blk == thr, tie, NEG_BIG))
        sel_ref[pl.ds(base, kb_sz), :] = sel.astype(BF16)
        return carry

    lax.fori_loop(0, nkb, mask_body, 0)

    cm_ref[...] = jnp.full(cm_ref.shape, NEG_BIG, BF16)

    def logit_body(kb, carry):
        base = pl.multiple_of(kb * kb_sz, kb_sz)
        dsel = jnp.clip((q0 - base) // kb_sz, 0, 2)
        kblk = k_ref[0, pl.ds(base, kb_sz), :]
        sel = sel_ref[pl.ds(base, kb_sz), :]
        for h in range(A_HEADS):
            sl = h // 4
            lg = _dot_nt(kblk[:, sl * LANES:(sl + 1) * LANES], qx_ref[:, h * LANES:(h + 1) * LANES])
            lg = lg.astype(BF16) + sel + bias_ref[dsel, h]
            lg_ref[h, pl.ds(base, kb_sz), :] = lg
            cm_ref[h] = jnp.maximum(cm_ref[h], lg)
        return carry

    for_each_block(logit_body)

    acc_ref[...] = jnp.zeros(acc_ref.shape, F32)
    m_rows = [jnp.max(cm_ref[h].astype(F32), axis=0, keepdims=True).astype(BF16) for h in range(A_HEADS)]

    def pv_body(kb, carry):
        base = pl.multiple_of(kb * kb_sz, kb_sz)
        for h in range(A_HEADS):
            p = jnp.exp2(lg_ref[h, pl.ds(base, kb_sz), :] - m_rows[h])
            acc_ref[h] += jnp.dot(vt_ref[0, kb, h // 4], p, preferred_element_type=F32)
        return carry

    for_each_block(pv_body)

    outs = []
    for h in range(A_HEADS):
        pos = (h // 2) % 2
        outs.append(acc_ref[h, pos * A_HEAD_DIM:(pos + 1) * A_HEAD_DIM, :] / acc_ref[h, LANES:LANES + 1, :])
    o_ref[0] = jnp.concatenate(outs, axis=0).T.astype(BF16)


def _attn_call(rel_bias, qx, iq, misc, kb, vt, ikx, *, tq, l_true, offset, n_top):
    b, tq_total, _ = qx.shape
    lp = kb.shape[1]
    nq = tq_total // tq
    assert offset % ATT_KB == 0 and tq % CHUNK == 0 and ATT_KB % tq == 0 and lp % ATT_KB == 0
    assert lp < 2 ** 14 and l_true <= lp
    kern = functools.partial(_attn_kernel, tq=tq, l_true=l_true, offset=offset, n_top=n_top)
    return pl.pallas_call(
        kern,
        out_shape=jax.ShapeDtypeStruct((b, tq_total, A_WIDTH), BF16),
        grid=(b, nq),
        in_specs=[pl.BlockSpec(memory_space=pltpu.SMEM),
                  pl.BlockSpec((1, tq, A_WIDTH), lambda i, j: (i, j, 0)),
                  pl.BlockSpec((1, tq, IDX_HEADS * IDX_DIM), lambda i, j: (i, j, 0)),
                  pl.BlockSpec((1, tq, LANES), lambda i, j: (i, j, 0)),
                  pl.BlockSpec((1, lp, 256), lambda i, j: (i, 0, 0)),
                  pl.BlockSpec((1, lp // ATT_KB, 2, V_ROWS, ATT_KB), lambda i, j: (i, 0, 0, 0, 0)),
                  pl.BlockSpec((1, lp, LANES), lambda i, j: (i, 0, 0))],
        out_specs=pl.BlockSpec((1, tq, A_WIDTH), lambda i, j: (i, j, 0)),
        scratch_shapes=[pltpu.VMEM((lp, tq), F32),
                        pltpu.VMEM((lp, tq), BF16),
                        pltpu.VMEM((lp, tq), BF16),
                        pltpu.VMEM((3, A_HEADS, ATT_KB, tq), BF16),
                        pltpu.VMEM((A_HEADS, lp, tq), BF16),
                        pltpu.VMEM((A_HEADS, ATT_KB, tq), BF16),
                        pltpu.VMEM((A_HEADS, V_ROWS, tq), F32),
                        pltpu.VMEM((8, tq), F32),
                        pltpu.VMEM((16, tq), BF16),
                        pltpu.VMEM((tq, QX_W), BF16)],
        compiler_params=_cparams(2),
        name="sparse_attn",
    )(rel_bias, qx, iq, misc, kb, vt, ikx)


def _split_bf16(a, n):
    parts = []
    r = a
    for i in range(n):
        p = r.astype(BF16)
        parts.append(p)
        if i + 1 < n:
            r = r - p.astype(F32)
    return parts


def _bdot(a, b, dims=(((1,), (0,)), ((), ()))):
    return lax.dot_general(a.astype(BF16), b.astype(BF16), dims, preferred_element_type=F32)


_NT = (((1,), (1,)), ((), ()))
_TN = (((0,), (0,)), ((), ()))


def _gdn_kernel(x_ref, misc_ref, gz_ref, convw_ref, hist_ref, s0_ref, gnw_ref, alane_ref, dlane_ref,
                o_ref, conv_o, s_o, xp_ref, y_ref, st_ref, vn_ref,
                u_ref, w_ref, qk_ref, qe_ref, kd_ref, el_ref, gate_ref, *, cb, t_true, nblk):
    c = CHUNK
    step = pl.program_id(1)
    blk = jnp.minimum(step, nblk - 1)
    carried = (u_ref, w_ref, qk_ref, qe_ref, kd_ref, el_ref, gate_ref)
    u_nx, w_nx, qk_nx, qe_nx, kd_nx, el_nx, gate_nx = (r.at[1] for r in carried)
    u_ref, w_ref, qk_ref, qe_ref, kd_ref, el_ref, gate_ref = (r.at[0] for r in carried)

    @pl.when(step == 0)
    def _():
        xp_ref[8 - (CONV_W - 1):8, :] = hist_ref[0]
        st_ref[...] = s0_ref[0]
        for r in carried:
            r[...] = jnp.zeros(r.shape, r.dtype)

    heads = range(G_HEADS)
    for r in carried:
        r[0] = r[1]

    def scan_stages():
        vn_ref[...] = jnp.zeros(vn_ref.shape, F32)
        for ck in range(cb // c):
            r0, r1 = ck * c, (ck + 1) * c
            s_prev = [st_ref[h] for h in heads]
            v_new = [u_ref[h, r0:r1, :] - _bdot(w_ref[h, r0:r1, :], s_prev[h]) for h in heads]
            o_state = [_bdot(qe_ref[h, r0:r1, :], s_prev[h]) for h in heads]
            yield
            for h in heads:
                vn_ref[h, r0:r1, :] = v_new[h]
            o = [o_state[h] + _bdot(qk_ref[h, r0:r1, :], vn_ref[h]) for h in heads]
            for h in heads:
                s_new = s_prev[h] * el_ref[h, ck, 0:1, :] + _bdot(kd_ref[h, r0:r1, :], v_new[h], _TN)
                st_ref[h] = jnp.where(step > 0, s_new, s_prev[h])
            yield
            for h in heads:
                on = o[h] * lax.rsqrt(jnp.mean(o[h] * o[h], axis=-1, keepdims=True) + EPS) * gnw_ref[...]
                gate = gate_ref[r0:r1, h * G_VAL_DIM:(h + 1) * G_VAL_DIM]
                o_ref[0, r0:r1, h * G_VAL_DIM:(h + 1) * G_VAL_DIM] = (on * gate).astype(BF16)
            yield
        s_o[0] = st_ref[...]

    scan = scan_stages()

    def tick():
        next(scan, None)

    tick()

    xp_ref[8:8 + cb, :] = x_ref[0]
    y = convw_ref[CONV_W - 1:CONV_W, :] * xp_ref[8:8 + cb, :]
    for j in range(CONV_W - 1):
        y = y + convw_ref[j:j + 1, :] * xp_ref[5 + j:5 + j + cb, :]
    y_ref[...] = _silu(y)

    last_row = (t_true - 1) % cb
    conv_o[0] = xp_ref[8 + last_row - (CONV_W - 2):8 + last_row + 1, :]

    xp_ref[8 - (CONV_W - 1):8, :] = xp_ref[8 + cb - (CONV_W - 1):8 + cb, :]

    ri = lax.broadcasted_iota(jnp.int32, (cb, cb), 0)
    ci = lax.broadcasted_iota(jnp.int32, (cb, cb), 1)
    lag = jnp.where((ri // c) == (ci // c), ri - ci, -1)
    tri = lag >= 0
    strict = lag > 0
    eye = ri == ci
    tril_bf = jnp.where(tri, 1.0, 0.0).astype(BF16)
    eye_f = jnp.where(eye, 1.0, 0.0)

    ms = misc_ref[0]
    tok = blk * cb + lax.broadcasted_iota(jnp.int32, (cb, 1), 0)
    live = tok < t_true
    beta_s = jnp.where(live, jax.nn.sigmoid(ms), 0.0)
    z = ms + dlane_ref[...]
    softplus = jnp.maximum(z, 0.0) + jnp.log(1.0 + jnp.exp(-jnp.abs(z)))
    g_s = jnp.where(live, -jnp.exp(alane_ref[...]) * softplus, 0.0)
    gc_s = None
    for piece in _split_bf16(g_s, 3):
        t = jnp.dot(tril_bf, piece, preferred_element_type=F32)
        gc_s = t if gc_s is None else gc_s + t

    tick()

    q, k, v, beta, gc, decay, kb, eg = ([None] * G_HEADS for _ in range(8))
    for h in heads:
        qh = y_ref[:, h * G_KEY_DIM:(h + 1) * G_KEY_DIM]
        kh = y_ref[:, G_WIDTH + h * G_KEY_DIM:G_WIDTH + (h + 1) * G_KEY_DIM]
        v[h] = y_ref[:, 2 * G_WIDTH + h * G_VAL_DIM:2 * G_WIDTH + (h + 1) * G_VAL_DIM]
        q[h] = qh * lax.rsqrt(jnp.sum(qh * qh, axis=-1, keepdims=True) + 1e-6) * (G_KEY_DIM ** -0.5)
        k[h] = kh * lax.rsqrt(jnp.sum(kh * kh, axis=-1, keepdims=True) + 1e-6)
        beta[h] = beta_s[:, MISC_GB + h:MISC_GB + h + 1]
        gc[h] = gc_s[:, MISC_GA + h:MISC_GA + h + 1]
        gc_b = jnp.broadcast_to(gc[h], (cb, cb))
        gc_row = jnp.sum(jnp.where(eye, gc_b, 0.0), axis=0, keepdims=True)
        decay[h] = jnp.exp(jnp.where(tri, gc_b - gc_row, NEG_BIG))
        kb[h] = k[h] * beta[h]
        eg[h] = jnp.exp(gc[h])
    tick()
    m = [jnp.where(strict, _bdot(kb[h], k[h], _NT) * decay[h], 0.0) for h in heads]
    rhs = [jnp.concatenate([v[h] * beta[h], kb[h] * eg[h]], axis=1) for h in heads]
    qk = [jnp.where(tri, _bdot(q[h], k[h], _NT) * decay[h], 0.0) for h in heads]
    tick()
    pw = [-m[h] for h in heads]
    inv = [eye_f + pw[h] for h in heads]
    for _ in range(5):
        pw = [_bdot(pw[h], pw[h]) for h in heads]
        tick()
        inv = [inv[h] + _bdot(inv[h], pw[h]) for h in heads]
    tick()
    sol = [_bdot(inv[h], rhs[h]) for h in heads]
    tick()
    m_sol = []
    for h in heads:
        m_hi, m_lo = _split_bf16(m[h], 2)
        s_hi, s_lo = _split_bf16(sol[h], 2)
        m_sol.append(_bdot(m_hi, s_hi) + _bdot(m_hi, s_lo) + _bdot(m_lo, s_hi))
    tick()
    sol = [sol[h] + _bdot(inv[h], rhs[h] - sol[h] - m_sol[h]) for h in heads]
    for _ in scan:
        pass
    gate_nx[...] = _silu(gz_ref[0])
    for h in heads:
        u_nx[h] = sol[h][:, :G_VAL_DIM]
        w_nx[h] = sol[h][:, G_VAL_DIM:].astype(BF16)
        qk_nx[h] = qk[h].astype(BF16)
        qe_nx[h] = (q[h] * eg[h]).astype(BF16)
        for ck in range(cb // c):
            r0, r1 = ck * c, (ck + 1) * c
            g_last = gc[h][r1 - 1:r1, :]
            kd_nx[h, r0:r1, :] = (k[h][r0:r1] * jnp.exp(g_last - gc[h][r0:r1])).astype(BF16)
            el_nx[h, ck] = jnp.broadcast_to(jnp.exp(g_last), (8, LANES))


def _gdn_call(gqkv, misc, gz, conv_w, hist, s0, gnorm_w, a_lane, d_lane, *, cb, t_true):
    b, t_pad, _ = gqkv.shape
    nblk = t_pad // cb
    assert (nblk - 1) * cb < t_true <= t_pad and (t_true - 1) % cb >= CONV_W - 2
    kern = functools.partial(_gdn_kernel, cb=cb, t_true=t_true, nblk=nblk)
    prep = lambda i, j: (i, jnp.minimum(j, nblk - 1), 0)
    scan = lambda i, j: (i, jnp.maximum(j - 1, 0), 0)
    nch = cb // CHUNK
    return pl.pallas_call(
        kern,
        out_shape=[jax.ShapeDtypeStruct((b, t_pad, G_WIDTH), BF16),
                   jax.ShapeDtypeStruct((b, CONV_W - 1, CONV_CH), F32),
                   jax.ShapeDtypeStruct((b, G_HEADS, G_KEY_DIM, G_VAL_DIM), F32)],
        grid=(b, nblk + 1),
        in_specs=[pl.BlockSpec((1, cb, CONV_CH), prep),
                  pl.BlockSpec((1, cb, LANES), prep),
                  pl.BlockSpec((1, cb, G_WIDTH), prep),
                  pl.BlockSpec((CONV_W, CONV_CH), lambda i, j: (0, 0)),
                  pl.BlockSpec((1, CONV_W - 1, CONV_CH), lambda i, j: (i, 0, 0)),
                  pl.BlockSpec((1, G_HEADS, G_KEY_DIM, G_VAL_DIM), lambda i, j: (i, 0, 0, 0)),
                  pl.BlockSpec((1, G_VAL_DIM), lambda i, j: (0, 0)),
                  pl.BlockSpec((1, LANES), lambda i, j: (0, 0)),
                  pl.BlockSpec((1, LANES), lambda i, j: (0, 0))],
        out_specs=[pl.BlockSpec((1, cb, G_WIDTH), scan),
                   pl.BlockSpec((1, CONV_W - 1, CONV_CH), lambda i, j: (i, 0, 0)),
                   pl.BlockSpec((1, G_HEADS, G_KEY_DIM, G_VAL_DIM), lambda i, j: (i, 0, 0, 0))],
        scratch_shapes=[pltpu.VMEM((cb + 8, CONV_CH), F32),
                        pltpu.VMEM((cb, CONV_CH), F32),
                        pltpu.VMEM((G_HEADS, G_KEY_DIM, G_VAL_DIM), F32),
                        pltpu.VMEM((G_HEADS, cb, G_VAL_DIM), F32),
                        pltpu.VMEM((2, G_HEADS, cb, G_VAL_DIM), F32),
                        pltpu.VMEM((2, G_HEADS, cb, G_KEY_DIM), BF16),
                        pltpu.VMEM((2, G_HEADS, cb, cb), BF16),
                        pltpu.VMEM((2, G_HEADS, cb, G_KEY_DIM), BF16),
                        pltpu.VMEM((2, G_HEADS, cb, G_KEY_DIM), BF16),
                        pltpu.VMEM((2, G_HEADS, nch, 8, LANES), F32),
                        pltpu.VMEM((2, cb, G_WIDTH), F32)],
        compiler_params=_cparams(2),
        name="gated_delta",
    )(gqkv, misc, gz, conv_w, hist, s0, gnorm_w, a_lane, d_lane)


def _pack_w_in(w_in):
    d = w_in.shape[0]
    splits = (512, 256, 256, 512, 64, 8, 512, 512, 512, 512, 4, 4)
    offs = np.concatenate([[0], np.cumsum(splits)])
    aq, ak, av, iq, ik, iw, gq, gk, gv, gz, gb, ga = [w_in[:, offs[i]:offs[i + 1]] for i in range(12)]
    qx = [aq[:, h * A_HEAD_DIM:(h + 1) * A_HEAD_DIM] * (A_HEAD_DIM ** -0.5 * LOG2E)
          for pair in Q_PAIRS for h in pair]
    ikx = [ik, ik]
    misc = [ik, iw, gb, ga, jnp.zeros((d, LANES - MISC_GA - G_HEADS), w_in.dtype)]
    cols = qx + [ak, av, iq * (IDX_DIM ** -0.5)] + ikx + misc + [gq, gk, gv, gz]
    packed = jnp.concatenate(cols, axis=1).astype(BF16)
    assert packed.shape[1] == IN_PACKED
    return packed


def _pack_ffn(w_gate, w_up, w_down):
    d = w_gate.shape[0]
    wg = w_gate.reshape(d, N_FF_CHUNKS, FF_CHUNK)
    wu = w_up.reshape(d, N_FF_CHUNKS, FF_CHUNK)
    wgu = jnp.transpose(jnp.concatenate([wg, wu], axis=2), (1, 0, 2)).astype(BF16)
    wd = w_down.reshape(N_FF_CHUNKS, FF_CHUNK, d).astype(BF16)
    return wgu, wd


def _ikx_layout(ik):
    return jnp.concatenate([ik, ik], axis=-1)


def _lane_vec(vals, lane0):
    return jnp.zeros((1, LANES), F32).at[0, lane0:lane0 + vals.shape[0]].set(vals.astype(F32))


def _run(x, mod, past, layers, rel_bias, norm_final):
    b, t, d = x.shape
    n = b * t
    per_token = t < TOKEN_TILE
    tm = n if per_token else TOKEN_TILE
    tiles_per_batch = None if per_token else t // tm
    tok = dict(tm=tm, tiles_per_batch=tiles_per_batch, per_token=per_token)

    x2 = x.reshape(n, d)
    states = []
    for li, lw in enumerate(layers):
        m = mod[li]

        def mvec(kidx):
            row = m[:, kidx]
            return jnp.repeat(row, t, axis=0) if per_token else row[:, None, :]

        sh1, sc1, gt1, sh2, sc2, gt2, sh3, sc3, gt3 = [mvec(i) for i in range(N_MOD)]
        x2 = _ffn_call(x2, sh1, sc1, gt1, lw["norm_ffn1"], lw["wgu1"], lw["wd1"], None, **tok)
        qx, k, kb, v, vb, iq, ikx, misc, gqkv, gz = _inproj_call(
            x2, sh2, sc2, lw["norm_mix"], lw["w_in"], **tok)

        if past is None:
            offset, l_true, tq = 0, t, ATT_TQ
            k_all, v_all, ikx_all = (a.reshape(b, t, -1) for a in (kb, vb, ikx))
            qx3, iq3, misc3 = (a.reshape(b, t, -1) for a in (qx, iq, misc))
            conv_hist = jnp.zeros((b, CONV_W - 1, CONV_CH), F32)
            s0 = jnp.zeros((b, G_HEADS, G_KEY_DIM, G_VAL_DIM), F32)
        else:
            k_hist, v_hist, ik_hist, conv_hist, s0 = (p[li] for p in past)
            offset = k_hist.shape[1]
            l_true = offset + t
            tq = LANES
            lp = -(-l_true // ATT_KB) * ATT_KB
            pad_k = lambda a: jnp.pad(a, ((0, 0), (0, lp - l_true), (0, 0)))
            k_all = pad_k(jnp.concatenate([k_hist.reshape(b, offset, -1).astype(BF16),
                                           kb.reshape(b, t, -1)], axis=1))
            v_all = pad_k(jnp.concatenate([v_hist.reshape(b, offset, -1).astype(BF16),
                                           vb.reshape(b, t, -1)], axis=1))
            ikx_all = pad_k(jnp.concatenate([_ikx_layout(ik_hist.astype(BF16)),
                                             ikx.reshape(b, t, -1)], axis=1))
            pad_q = lambda a: jnp.pad(a.reshape(b, t, -1), ((0, 0), (0, tq - t), (0, 0)))
            qx3, iq3, misc3 = pad_q(qx), pad_q(iq), pad_q(misc)
            s0 = s0.astype(F32)
            conv_hist = conv_hist.astype(F32)
        lp = k_all.shape[1]
        n_top = min(TOPK_MAX, l_true // 4)
        vt = jnp.transpose(v_all.reshape(b, lp // ATT_KB, ATT_KB, 2, LANES), (0, 1, 3, 4, 2))
        vt = jnp.concatenate([vt, jnp.ones((b, lp // ATT_KB, 2, V_ROWS - LANES, ATT_KB), BF16)], axis=3)
        attn = _attn_call(rel_bias, qx3, iq3, misc3, k_all, vt, ikx_all,
                          tq=tq, l_true=l_true, offset=offset, n_top=n_top)
        attn = attn[:, :t].reshape(n, A_WIDTH)

        cb = GDN_BLOCK if t % GDN_BLOCK == 0 else CHUNK
        t_pad = -(-t // cb) * cb
        pad_t = lambda a: jnp.pad(a.reshape(b, t, -1), ((0, 0), (0, t_pad - t), (0, 0)))
        gdn, conv_new, s_new = _gdn_call(pad_t(gqkv), pad_t(misc), pad_t(gz), lw["conv_w"], conv_hist, s0,
                                         lw["gnorm_w"], lw["a_lane"], lw["d_lane"], cb=cb, t_true=t)
        gdn = gdn[:, :t].reshape(n, G_WIDTH)

        nf = norm_final if li == len(layers) - 1 else None
        x2 = _ffn_call(x2, sh3, sc3, gt3, lw["norm_ffn2"], lw["wgu2"], lw["wd2"], nf,
                       mixer=(gt2, attn, gdn, lw["wo_a"], lw["wo_g"]), **tok)

        states.append((k.reshape(b, t, A_KV_HEADS, A_HEAD_DIM), v.reshape(b, t, A_KV_HEADS, A_HEAD_DIM),
                       misc.reshape(b, t, LANES)[..., :IDX_DIM], conv_new, s_new))
    stacked = [jnp.stack(s, axis=0) for s in zip(*states)]
    return x2.reshape(b, t, d), stacked


def kernel(x_prompt, x_sample, cache_k, cache_v, cache_idx_k, state_conv, state_delta, c_prompt, c_sample,
           w_mod, b_mod, norm_ffn1, norm_mix, norm_ffn2, ffn1_w_gate, ffn1_w_up, ffn1_w_down,
           ffn2_w_gate, ffn2_w_up, ffn2_w_down, w_in, w_out, rel_bias, conv_w, a_log, dt_bias, gnorm_w,
           norm_final):
    depth = w_mod.shape[0]
    bp = c_prompt.shape[0]
    c_all = jnp.concatenate([c_prompt, c_sample], axis=0)
    layers, mods_p, mods_s = [], [], []
    for l in range(depth):
        mod = _mod_call(c_all, w_mod[l], b_mod[l]).reshape(c_all.shape[0], N_MOD, D_MODEL)
        mods_p.append(mod[:bp])
        mods_s.append(mod[bp:])
        wgu1, wd1 = _pack_ffn(ffn1_w_gate[l], ffn1_w_up[l], ffn1_w_down[l])
        wgu2, wd2 = _pack_ffn(ffn2_w_gate[l], ffn2_w_up[l], ffn2_w_down[l])
        wo = w_out[l].astype(BF16)
        layers.append(dict(
            norm_ffn1=norm_ffn1[l].reshape(1, -1), norm_mix=norm_mix[l].reshape(1, -1),
            norm_ffn2=norm_ffn2[l].reshape(1, -1), wgu1=wgu1, wd1=wd1, wgu2=wgu2, wd2=wd2,
            w_in=_pack_w_in(w_in[l]), wo_a=wo[:A_WIDTH], wo_g=wo[A_WIDTH:],
            conv_w=conv_w[l], gnorm_w=gnorm_w[l].reshape(1, -1),
            a_lane=_lane_vec(a_log[l], MISC_GA), d_lane=_lane_vec(dt_bias[l], MISC_GA)))
    nf = norm_final.reshape(1, -1)
    y_p, (k_p, v_p, ik_p, conv_p, delta_p) = _run(x_prompt, mods_p, None, layers, rel_bias, nf)
    past = (cache_k, cache_v, cache_idx_k, state_conv, state_delta)
    y_s, (k_s, v_s, ik_s, conv_s, delta_s) = _run(x_sample, mods_s, past, layers, rel_bias, nf)
    return (y_p, y_s, k_p, v_p, ik_p, conv_p, delta_p, k_s, v_s, ik_s, conv_s, delta_s)
```

```python
import functools

import jax
import jax.numpy as jnp
import numpy as np
from jax import lax
from jax.experimental import pallas as pl
from jax.experimental.pallas import tpu as pltpu

F32 = jnp.float32
BF16 = jnp.bfloat16

D_MODEL = 1024
CHUNK = 64
A_HEAD_DIM = 64
A_HEADS = 8
A_KV_HEADS = 4
A_WIDTH = A_HEADS * A_HEAD_DIM
IDX_HEADS = 8
IDX_DIM = 64
TOPK_MAX = 256
REL_BUCKETS = 32
G_KEY_DIM = 128
G_VAL_DIM = 128
G_HEADS = 4
G_WIDTH = G_HEADS * G_VAL_DIM
CONV_W = 4
CONV_CH = 2 * G_HEADS * G_KEY_DIM + G_HEADS * G_VAL_DIM
D_FF = 2816
N_MOD = 9
EPS = 1e-6

LANES = 128
MXU_DIM = 256
VMEM_LIMIT_BYTES = 56 * 1024 * 1024

FF_CHUNK = MXU_DIM
N_FF_CHUNKS = D_FF // FF_CHUNK
TOKEN_TILE = 512
ATT_TQ = 256
ATT_KB = 256
GDN_BLOCK = 256

QX_W = A_HEADS * LANES
Q_PAIRS = ((0, 2), (1, 3), (4, 6), (5, 7))
OFF_QX = 0
OFF_K = OFF_QX + A_WIDTH
OFF_V = OFF_K + A_KV_HEADS * A_HEAD_DIM
OFF_IQ = OFF_V + A_KV_HEADS * A_HEAD_DIM
OFF_IKX = OFF_IQ + IDX_HEADS * IDX_DIM
OFF_MISC = OFF_IKX + LANES
OFF_GQKV = OFF_MISC + LANES
OFF_GZ = OFF_GQKV + CONV_CH
IN_PACKED = OFF_GZ + G_WIDTH
MISC_IW = IDX_DIM
MISC_GB = MISC_IW + IDX_HEADS
MISC_GA = MISC_GB + G_HEADS

NEG_BIG = -1e30
INT_MIN = -2 ** 31
LOG2E = 1.4426950408889634
V_ROWS = LANES + 16


def _cparams(n_axes):
    return pltpu.CompilerParams(dimension_semantics=("arbitrary",) * n_axes,
                                vmem_limit_bytes=VMEM_LIMIT_BYTES)


def _resident(shape):
    nd = len(shape)
    return pl.BlockSpec(shape, lambda *_: (0,) * nd, pipeline_mode=pl.Buffered(1))


def _dot_nt(a, b):
    return lax.dot_general(a, b, (((1,), (1,)), ((), ())), preferred_element_type=F32)


def _rms_mod(x, gain, shift, scale):
    ms = jnp.mean(x * x, axis=-1, keepdims=True)
    y = x * lax.rsqrt(ms + EPS) * gain
    return y * (1.0 + scale) + shift


def _silu(x):
    return x * jax.nn.sigmoid(x)


def _tree_sum(parts):
    while len(parts) > 1:
        parts = [a + b for a, b in zip(parts[0::2], parts[1::2])] + ([parts[-1]] if len(parts) % 2 else [])
    return parts[0]


def _mod_kernel(c_ref, w_ref, b_ref, o_ref):
    s = _silu(c_ref[...]).astype(BF16)
    o_ref[...] = jnp.dot(s, w_ref[...].astype(BF16), preferred_element_type=F32) + b_ref[...]


def _mod_call(c, w_mod, b_mod):
    rows, d = c.shape
    n = w_mod.shape[1]
    tn = D_MODEL
    return pl.pallas_call(
        _mod_kernel,
        out_shape=jax.ShapeDtypeStruct((rows, n), F32),
        grid=(n // tn,),
        in_specs=[pl.BlockSpec((rows, d), lambda j: (0, 0)),
                  pl.BlockSpec((d, tn), lambda j: (0, j)),
                  pl.BlockSpec((1, tn), lambda j: (0, j))],
        out_specs=pl.BlockSpec((rows, tn), lambda j: (0, j)),
        compiler_params=_cparams(1),
        name="mod",
    )(c, w_mod, b_mod.reshape(1, n))


def _mod_specs(per_token, tm, tiles_per_batch):
    if per_token:
        return pl.BlockSpec((tm, D_MODEL), lambda i: (i, 0))
    return pl.BlockSpec((None, 1, D_MODEL), lambda i: (i // tiles_per_batch, 0, 0))


def _ffn_kernel(x_ref, sh_ref, sc_ref, gt_ref, gain_ref, wgu_ref, wd_ref, *rest, final_norm, mixer):
    rest = list(rest)
    if mixer:
        gm_ref, a_ref, g_ref, wa_ref, wg_ref = rest[:5]
        rest = rest[5:]
    if final_norm:
        nf_ref = rest.pop(0)
    o_ref, acc_ref = rest
    x = x_ref[...]
    if mixer:
        x = x + gm_ref[...] * (jnp.dot(a_ref[...], wa_ref[...], preferred_element_type=F32)
                               + jnp.dot(g_ref[...], wg_ref[...], preferred_element_type=F32))
    h = _rms_mod(x, gain_ref[...], sh_ref[...], sc_ref[...]).astype(BF16)
    for j in range(N_FF_CHUNKS):
        ab = jnp.dot(h, wgu_ref[j], preferred_element_type=F32)
        g = (_silu(ab[:, :FF_CHUNK]) * ab[:, FF_CHUNK:]).astype(BF16)
        d = jnp.dot(g, wd_ref[j], preferred_element_type=F32)
        if j == 0:
            acc_ref[...] = d
        else:
            acc_ref[...] += d
    y = x + 0.5 * gt_ref[...] * acc_ref[...]
    if final_norm:
        ms = jnp.mean(y * y, axis=-1, keepdims=True)
        y = y * lax.rsqrt(ms + EPS) * nf_ref[...]
    o_ref[...] = y


def _ffn_call(x2, sh, sc, gt, gain, wgu, wd, norm_final, mixer=None, *, tm, tiles_per_batch, per_token):
    n = x2.shape[0]
    mspec = _mod_specs(per_token, tm, tiles_per_batch)
    in_specs = [pl.BlockSpec((tm, D_MODEL), lambda i: (i, 0)), mspec, mspec, mspec,
                _resident((1, D_MODEL)), _resident(wgu.shape), _resident(wd.shape)]
    args = [x2, sh, sc, gt, gain, wgu, wd]
    if mixer is not None:
        gm, attn, gdn, wa, wg = mixer
        in_specs += [mspec, pl.BlockSpec((tm, A_WIDTH), lambda i: (i, 0)),
                     pl.BlockSpec((tm, G_WIDTH), lambda i: (i, 0)), _resident(wa.shape), _resident(wg.shape)]
        args += [gm, attn, gdn, wa, wg]
    final_norm = norm_final is not None
    if final_norm:
        in_specs.append(_resident((1, D_MODEL)))
        args.append(norm_final)
    return pl.pallas_call(
        functools.partial(_ffn_kernel, final_norm=final_norm, mixer=mixer is not None),
        out_shape=jax.ShapeDtypeStruct((n, D_MODEL), F32),
        grid=(n // tm,),
        in_specs=in_specs,
        out_specs=pl.BlockSpec((tm, D_MODEL), lambda i: (i, 0)),
        scratch_shapes=[pltpu.VMEM((tm, D_MODEL), F32)],
        compiler_params=_cparams(1),
        name="ffn_final" if final_norm else "ffn",
    )(*args)


def _inproj_kernel(x_ref, sh_ref, sc_ref, gain_ref, w_ref,
                   qx_o, k_o, kb_o, v_o, vb_o, iq_o, ikx_o, misc_o, gqkv_o, gz_o):
    h = _rms_mod(x_ref[...], gain_ref[...], sh_ref[...], sc_ref[...]).astype(BF16)

    def mm(off, width):
        return jnp.dot(h, w_ref[:, off:off + width], preferred_element_type=F32)

    qx_o[...] = mm(OFF_QX, A_WIDTH).astype(BF16)
    k = mm(OFF_K, OFF_V - OFF_K)
    k_o[...] = k
    kb_o[...] = k.astype(BF16)
    v = mm(OFF_V, OFF_IQ - OFF_V)
    v_o[...] = v
    vb_o[...] = v.astype(BF16)
    iq_o[...] = mm(OFF_IQ, OFF_IKX - OFF_IQ).astype(BF16)
    ikx_o[...] = mm(OFF_IKX, OFF_MISC - OFF_IKX).astype(BF16)
    misc_o[...] = mm(OFF_MISC, LANES)
    gqkv_o[...] = mm(OFF_GQKV, CONV_CH)
    gz_o[...] = mm(OFF_GZ, G_WIDTH)


def _inproj_call(x2, sh, sc, gain, w_packed, *, tm, tiles_per_batch, per_token):
    n = x2.shape[0]
    mspec = _mod_specs(per_token, tm, tiles_per_batch)
    widths = [(A_WIDTH, BF16), (256, F32), (256, BF16), (256, F32), (256, BF16), (512, BF16),
              (LANES, BF16), (LANES, F32), (CONV_CH, F32), (G_WIDTH, F32)]
    return pl.pallas_call(
        _inproj_kernel,
        out_shape=[jax.ShapeDtypeStruct((n, w), dt) for w, dt in widths],
        grid=(n // tm,),
        in_specs=[pl.BlockSpec((tm, D_MODEL), lambda i: (i, 0)), mspec, mspec,
                  _resident((1, D_MODEL)), _resident(w_packed.shape)],
        out_specs=[pl.BlockSpec((tm, w), lambda i: (i, 0)) for w, _ in widths],
        compiler_params=_cparams(1),
        name="inproj",
    )(x2, sh, sc, gain, w_packed)


def _rel_bucket_int(rel):
    n = jnp.abs(rel)
    large = jnp.full(rel.shape, 8, jnp.int32)
    for th in (12, 16, 23, 32, 46, 64, 91):
        large = large + jnp.where(n >= th, 1, 0)
    return jnp.where(rel > 0, REL_BUCKETS // 2, 0) + jnp.where(n < 8, n, large)


def _key_to_float(t):
    bits = jnp.where(t >= 0, t, t ^ jnp.int32(0x7FFFFFFF))
    return lax.bitcast_convert_type(bits, F32)


def _top16(x):
    bits = lax.bitcast_convert_type(x, jnp.int32) & jnp.int32(-65536)
    return lax.bitcast_convert_type(bits, F32).astype(BF16)


def _bias_kernel(relb_ref, o_ref, *, tq):
    h = pl.program_id(1)
    key_off = lax.broadcasted_iota(jnp.int32, (ATT_KB, tq), 0)
    qry_off = lax.broadcasted_iota(jnp.int32, (ATT_KB, tq), 1)
    bucket = _rel_bucket_int(key_off - qry_off - pl.program_id(0) * ATT_KB)
    t = jnp.zeros((ATT_KB, tq), F32)
    for b in range(REL_BUCKETS):
        t = jnp.where(bucket == b, relb_ref[b, h], t)
    o_ref[0, 0] = (t * LOG2E).astype(BF16)


def _bias_call(rel_bias, tq):
    return pl.pallas_call(
        functools.partial(_bias_kernel, tq=tq),
        out_shape=jax.ShapeDtypeStruct((3, A_HEADS, ATT_KB, tq), BF16),
        grid=(3, A_HEADS),
        in_specs=[pl.BlockSpec(memory_space=pltpu.SMEM)],
        out_specs=pl.BlockSpec((1, 1, ATT_KB, tq), lambda d, h: (d, h, 0, 0)),
        compiler_params=_cparams(2),
        name="rel_bias_tiles",
    )(rel_bias)


def _attn_kernel(q_ref, iq_ref, misc_ref, k_ref, vt_ref, ikx_ref, bias_ref, o_ref,
                 s_ref, shi_ref, sel_ref, lg_ref, cm_ref, acc_ref, cnt_ref, cnth_ref, qx_ref,
                 *, tq, l_true, offset, n_top):
    kb_sz = ATT_KB
    low_half_q = lax.broadcasted_iota(jnp.int32, (tq, LANES), 1) < A_HEAD_DIM
    low_half_k = lax.broadcasted_iota(jnp.int32, (kb_sz, LANES), 1) < A_HEAD_DIM

    for s, pair in enumerate(Q_PAIRS):
        slab = q_ref[0, :, s * LANES:(s + 1) * LANES]
        zero = jnp.zeros_like(slab)
        qx_ref[:, pair[0] * LANES:(pair[0] + 1) * LANES] = jnp.where(low_half_q, slab, zero)
        qx_ref[:, pair[1] * LANES:(pair[1] + 1) * LANES] = jnp.where(low_half_q, zero, slab)
    q0 = offset + pl.program_id(1) * tq
    kmax = jnp.minimum(q0 + tq, l_true)
    nkb = (kmax + kb_sz - 1) // kb_sz

    key_off = lax.broadcasted_iota(jnp.int32, (kb_sz, tq), 0)

    qpos = q0 + lax.broadcasted_iota(jnp.int32, (1, tq), 1)
    limit = jnp.minimum((qpos // CHUNK + 1) * CHUNK, l_true)

    iw_t = misc_ref[0].T[MISC_IW:MISC_IW + IDX_HEADS, :] * (IDX_HEADS ** -0.5)

    def score_body(kb, carry):
        base = pl.multiple_of(kb * kb_sz, kb_sz)
        ik2 = ikx_ref[0, pl.ds(base, kb_sz), :]
        ik_lo = jnp.where(low_half_k, ik2, jnp.zeros_like(ik2))
        ik_hi = jnp.where(low_half_k, jnp.zeros_like(ik2), ik2)
        s = jnp.zeros((kb_sz, tq), F32)
        for j in range(IDX_HEADS // 2):
            slab = iq_ref[0, :, j * LANES:(j + 1) * LANES]
            d0 = _dot_nt(ik_lo, slab)
            d1 = _dot_nt(ik_hi, slab)
            s = s + iw_t[2 * j:2 * j + 1, :] * jnp.maximum(d0, 0.0)
            s = s + iw_t[2 * j + 1:2 * j + 2, :] * jnp.maximum(d1, 0.0)
        s = jnp.where(base + key_off < limit, s, -jnp.inf)
        s_ref[pl.ds(base, kb_sz), :] = s
        shi_ref[pl.ds(base, kb_sz), :] = _top16(s)
        return carry

    def for_each_block(body):
        def quad(i, carry):
            for r in range(4):
                body(4 * i + r, carry)
            return carry
        lax.fori_loop(0, nkb // 4, quad, 0)
        done = (nkb // 4) * 4

        @pl.when(nkb % 4 >= 2)
        def _():
            body(done, 0)
            body(done + 1, 0)

        @pl.when(nkb % 2 == 1)
        def _():
            body(nkb - 1, 0)

    for_each_block(score_body)

    def sum_blocks(contrib, acc_ref, n=nkb):
        acc_ref[...] = jnp.zeros(acc_ref.shape, acc_ref.dtype)

        def quad(i, carry):
            acc_ref[...] += _tree_sum([contrib(4 * i + r) for r in range(4)])
            return carry
        lax.fori_loop(0, n // 4, quad, 0)
        done = (n // 4) * 4

        @pl.when(n % 4 >= 2)
        def _():
            acc_ref[...] += contrib(done) + contrib(done + 1)

        @pl.when(n % 2 == 1)
        def _():
            acc_ref[...] += contrib(n - 1)
        return acc_ref[...]

    def count(pred, n=nkb):
        def contrib(kb):
            base = pl.multiple_of(kb * kb_sz, kb_sz)
            hit = pred(s_ref[pl.ds(base, kb_sz), :], base + key_off)
            return _tree_sum([hit[r:r + 8] for r in range(0, kb_sz, 8)])
        return jnp.sum(sum_blocks(contrib, cnt_ref, n), axis=0, keepdims=True)

    def count_hi(thr_hi):
        one = jnp.ones((), BF16)
        zero = jnp.zeros((), BF16)

        def contrib(kb):
            base = pl.multiple_of(kb * kb_sz, kb_sz)
            hit = jnp.where(shi_ref[pl.ds(base, kb_sz), :] >= thr_hi, one, zero)
            return _tree_sum([hit[r:r + 16] for r in range(0, kb_sz, 16)])
        return jnp.sum(sum_blocks(contrib, cnth_ref).astype(F32), axis=0, keepdims=True)

    def hi_body(i, carry):
        t, c_t = carry
        cand = t + lax.shift_left(jnp.int32(1), 31 - i)
        c = count_hi(_top16(_key_to_float(cand)))
        return jnp.where(c >= n_top, cand, t), jnp.where(c >= n_top, c, c_t)

    def lo_body(i, carry):
        t, c_t = carry
        cand = t + lax.shift_left(jnp.int32(1), 15 - i)
        thr_c = _key_to_float(cand)
        c = count(lambda blk, _: jnp.where(blk >= thr_c, 1.0, 0.0))
        return jnp.where(c >= n_top, cand, t), jnp.where(c >= n_top, c, c_t)

    carry = (jnp.full((1, tq), INT_MIN, jnp.int32), jnp.full((1, tq), float(n_top), F32))
    carry = lax.fori_loop(0, 16, hi_body, carry)
    t_key, n_ge = lax.fori_loop(0, 16, lo_body, carry)
    thr = _key_to_float(t_key)

    take_all = limit <= n_top
    excess = jnp.where(take_all, 0.0, jnp.where(n_ge > n_top, 1.0, 0.0))
    any_excess = jnp.max(excess) > 0.0
    n_tie = jnp.where(any_excess, nkb, 0)
    need = n_top - count(lambda blk, _: jnp.where(blk > thr, 1.0, 0.0), n_tie)
    idx_bits = 14

    def tie_body(i, c):
        cand = c + lax.shift_left(jnp.int32(1), idx_bits - 1 - i)
        f = count(lambda blk, kidx: jnp.where(blk == thr, jnp.where(kidx < cand, 1.0, 0.0), 0.0))
        return jnp.where(f <= need, cand, c)

    cut0 = jnp.where(any_excess, jnp.zeros((1, tq), jnp.int32),
                     jnp.full((1, tq), 2 ** idx_bits, jnp.int32))
    cut = lax.fori_loop(0, jnp.where(any_excess, idx_bits, 0), tie_body, cut0)
    thr = jnp.where(take_all, -jnp.inf, thr)
    cut = jnp.where(take_all, limit, cut)

    def mask_body(kb, carry):
        base = pl.multiple_of(kb * kb_sz, kb_sz)
        blk = s_ref[pl.ds(base, kb_sz), :]
        tie = jnp.where(base + key_off < cut, 0.0, NEG_BIG)
        sel = jnp.where(blk > thr, 0.0, jnp.where(blk == thr, tie, NEG_BIG))
        sel_ref[pl.ds(base, kb_sz), :] = sel.astype(BF16)
        return carry

    lax.fori_loop(0, nkb, mask_body, 0)

    cm_ref[...] = jnp.full(cm_ref.shape, NEG_BIG, BF16)

    def logit_body(kb, carry):
        base = pl.multiple_of(kb * kb_sz, kb_sz)
        dsel = jnp.clip((q0 - base) // kb_sz, 0, 2)
        kblk = k_ref[0, pl.ds(base, kb_sz), :]
        sel = sel_ref[pl.ds(base, kb_sz), :]
        for h in range(A_HEADS):
            sl = h // 4
            lg = _dot_nt(kblk[:, sl * LANES:(sl + 1) * LANES], qx_ref[:, h * LANES:(h + 1) * LANES])
            lg = lg.astype(BF16) + sel + bias_ref[dsel, h]
            lg_ref[h, pl.ds(base, kb_sz), :] = lg
            cm_ref[h] = jnp.maximum(cm_ref[h], lg)
        return carry

    for_each_block(logit_body)

    acc_ref[...] = jnp.zeros(acc_ref.shape, F32)
    m_rows = [jnp.max(cm_ref[h].astype(F32), axis=0, keepdims=True).astype(BF16) for h in range(A_HEADS)]

    def pv_body(kb, carry):
        base = pl.multiple_of(kb * kb_sz, kb_sz)
        for h in range(A_HEADS):
            p = jnp.exp2(lg_ref[h, pl.ds(base, kb_sz), :] - m_rows[h])
            acc_ref[h] += jnp.dot(vt_ref[0, kb, h // 4], p, preferred_element_type=F32)
        return carry

    for_each_block(pv_body)

    outs = []
    for h in range(A_HEADS):
        pos = (h // 2) % 2
        outs.append(acc_ref[h, pos * A_HEAD_DIM:(pos + 1) * A_HEAD_DIM, :] / acc_ref[h, LANES:LANES + 1, :])
    o_ref[0] = jnp.concatenate(outs, axis=0).T.astype(BF16)


def _attn_call(rel_bias, qx, iq, misc, kb, vt, ikx, *, tq, l_true, offset, n_top):
    b, tq_total, _ = qx.shape
    lp = kb.shape[1]
    nq = tq_total // tq
    assert offset % ATT_KB == 0 and tq % CHUNK == 0 and ATT_KB % tq == 0 and lp % ATT_KB == 0
    assert lp < 2 ** 14 and l_true <= lp
    kern = functools.partial(_attn_kernel, tq=tq, l_true=l_true, offset=offset, n_top=n_top)
    bias = _bias_call(rel_bias, tq)
    return pl.pallas_call(
        kern,
        out_shape=jax.ShapeDtypeStruct((b, tq_total, A_WIDTH), BF16),
        grid=(b, nq),
        in_specs=[pl.BlockSpec((1, tq, A_WIDTH), lambda i, j: (i, j, 0)),
                  pl.BlockSpec((1, tq, IDX_HEADS * IDX_DIM), lambda i, j: (i, j, 0)),
                  pl.BlockSpec((1, tq, LANES), lambda i, j: (i, j, 0)),
                  pl.BlockSpec((1, lp, 256), lambda i, j: (i, 0, 0)),
                  pl.BlockSpec((1, lp // ATT_KB, 2, V_ROWS, ATT_KB), lambda i, j: (i, 0, 0, 0, 0)),
                  pl.BlockSpec((1, lp, LANES), lambda i, j: (i, 0, 0)),
                  _resident(bias.shape)],
        out_specs=pl.BlockSpec((1, tq, A_WIDTH), lambda i, j: (i, j, 0)),
        scratch_shapes=[pltpu.VMEM((lp, tq), F32),
                        pltpu.VMEM((lp, tq), BF16),
                        pltpu.VMEM((lp, tq), BF16),
                        pltpu.VMEM((A_HEADS, lp, tq), BF16),
                        pltpu.VMEM((A_HEADS, ATT_KB, tq), BF16),
                        pltpu.VMEM((A_HEADS, V_ROWS, tq), F32),
                        pltpu.VMEM((8, tq), F32),
                        pltpu.VMEM((16, tq), BF16),
                        pltpu.VMEM((tq, QX_W), BF16)],
        compiler_params=_cparams(2),
        name="sparse_attn",
    )(qx, iq, misc, kb, vt, ikx, bias)


def _split_bf16(a, n):
    parts = []
    r = a
    for i in range(n):
        p = r.astype(BF16)
        parts.append(p)
        if i + 1 < n:
            r = r - p.astype(F32)
    return parts


def _bdot(a, b, dims=(((1,), (0,)), ((), ()))):
    return lax.dot_general(a.astype(BF16), b.astype(BF16), dims, preferred_element_type=F32)


_NT = (((1,), (1,)), ((), ()))
_TN = (((0,), (0,)), ((), ()))


def _gdn_kernel(x_ref, misc_ref, gz_ref, convw_ref, hist_ref, s0_ref, gnw_ref, alane_ref, dlane_ref,
                o_ref, conv_o, s_o, xp_ref, y_ref, st_ref, vn_ref,
                u_ref, w_ref, qk_ref, qe_ref, kd_ref, el_ref, gate_ref, *, cb, t_true, nblk):
    c = CHUNK
    step = pl.program_id(1)
    blk = jnp.minimum(step, nblk - 1)
    carried = (u_ref, w_ref, qk_ref, qe_ref, kd_ref, el_ref, gate_ref)
    u_nx, w_nx, qk_nx, qe_nx, kd_nx, el_nx, gate_nx = (r.at[1] for r in carried)
    u_ref, w_ref, qk_ref, qe_ref, kd_ref, el_ref, gate_ref = (r.at[0] for r in carried)

    @pl.when(step == 0)
    def _():
        xp_ref[8 - (CONV_W - 1):8, :] = hist_ref[0]
        st_ref[...] = s0_ref[0]
        for r in carried:
            r[...] = jnp.zeros(r.shape, r.dtype)

    heads = range(G_HEADS)
    for r in carried:
        r[0] = r[1]

    def scan_stages():
        vn_ref[...] = jnp.zeros(vn_ref.shape, F32)
        for ck in range(cb // c):
            r0, r1 = ck * c, (ck + 1) * c
            s_prev = [st_ref[h] for h in heads]
            v_new = [u_ref[h, r0:r1, :] - _bdot(w_ref[h, r0:r1, :], s_prev[h]) for h in heads]
            o_state = [_bdot(qe_ref[h, r0:r1, :], s_prev[h]) for h in heads]
            yield
            for h in heads:
                vn_ref[h, r0:r1, :] = v_new[h]
            o = [o_state[h] + _bdot(qk_ref[h, r0:r1, :], vn_ref[h]) for h in heads]
            for h in heads:
                s_new = s_prev[h] * el_ref[h, ck, 0:1, :] + _bdot(kd_ref[h, r0:r1, :], v_new[h], _TN)
                st_ref[h] = jnp.where(step > 0, s_new, s_prev[h])
            yield
            for h in heads:
                on = o[h] * lax.rsqrt(jnp.mean(o[h] * o[h], axis=-1, keepdims=True) + EPS) * gnw_ref[...]
                gate = gate_ref[r0:r1, h * G_VAL_DIM:(h + 1) * G_VAL_DIM]
                o_ref[0, r0:r1, h * G_VAL_DIM:(h + 1) * G_VAL_DIM] = (on * gate).astype(BF16)
            yield
        s_o[0] = st_ref[...]

    scan = scan_stages()

    def tick():
        next(scan, None)

    tick()

    xp_ref[8:8 + cb, :] = x_ref[0]
    y = convw_ref[CONV_W - 1:CONV_W, :] * xp_ref[8:8 + cb, :]
    for j in range(CONV_W - 1):
        y = y + convw_ref[j:j + 1, :] * xp_ref[5 + j:5 + j + cb, :]
    y_ref[...] = _silu(y)

    last_row = (t_true - 1) % cb
    conv_o[0] = xp_ref[8 + last_row - (CONV_W - 2):8 + last_row + 1, :]

    xp_ref[8 - (CONV_W - 1):8, :] = xp_ref[8 + cb - (CONV_W - 1):8 + cb, :]

    ri = lax.broadcasted_iota(jnp.int32, (cb, cb), 0)
    ci = lax.broadcasted_iota(jnp.int32, (cb, cb), 1)
    lag = jnp.where((ri // c) == (ci // c), ri - ci, -1)
    tri = lag >= 0
    strict = lag > 0
    eye = ri == ci
    tril_bf = jnp.where(tri, 1.0, 0.0).astype(BF16)
    eye_f = jnp.where(eye, 1.0, 0.0)

    ms = misc_ref[0]
    tok = blk * cb + lax.broadcasted_iota(jnp.int32, (cb, 1), 0)
    live = tok < t_true
    beta_s = jnp.where(live, jax.nn.sigmoid(ms), 0.0)
    z = ms + dlane_ref[...]
    softplus = jnp.maximum(z, 0.0) + jnp.log(1.0 + jnp.exp(-jnp.abs(z)))
    g_s = jnp.where(live, -jnp.exp(alane_ref[...]) * softplus, 0.0)
    gc_s = None
    for piece in _split_bf16(g_s, 3):
        t = jnp.dot(tril_bf, piece, preferred_element_type=F32)
        gc_s = t if gc_s is None else gc_s + t

    tick()

    q, k, v, beta, gc, decay, kb, eg = ([None] * G_HEADS for _ in range(8))
    for h in heads:
        qh = y_ref[:, h * G_KEY_DIM:(h + 1) * G_KEY_DIM]
        kh = y_ref[:, G_WIDTH + h * G_KEY_DIM:G_WIDTH + (h + 1) * G_KEY_DIM]
        v[h] = y_ref[:, 2 * G_WIDTH + h * G_VAL_DIM:2 * G_WIDTH + (h + 1) * G_VAL_DIM]
        q[h] = qh * lax.rsqrt(jnp.sum(qh * qh, axis=-1, keepdims=True) + 1e-6) * (G_KEY_DIM ** -0.5)
        k[h] = kh * lax.rsqrt(jnp.sum(kh * kh, axis=-1, keepdims=True) + 1e-6)
        beta[h] = beta_s[:, MISC_GB + h:MISC_GB + h + 1]
        gc[h] = gc_s[:, MISC_GA + h:MISC_GA + h + 1]
        gc_b = jnp.broadcast_to(gc[h], (cb, cb))
        gc_row = jnp.sum(jnp.where(eye, gc_b, 0.0), axis=0, keepdims=True)
        decay[h] = jnp.exp(jnp.where(tri, gc_b - gc_row, NEG_BIG))
        kb[h] = k[h] * beta[h]
        eg[h] = jnp.exp(gc[h])
    tick()
    m = [jnp.where(strict, _bdot(kb[h], k[h], _NT) * decay[h], 0.0) for h in heads]
    rhs = [jnp.concatenate([v[h] * beta[h], kb[h] * eg[h]], axis=1) for h in heads]
    qk = [jnp.where(tri, _bdot(q[h], k[h], _NT) * decay[h], 0.0) for h in heads]
    tick()
    pw = [-m[h] for h in heads]
    inv = [eye_f + pw[h] for h in heads]
    for _ in range(5):
        pw = [_bdot(pw[h], pw[h]) for h in heads]
        tick()
        inv = [inv[h] + _bdot(inv[h], pw[h]) for h in heads]
    tick()
    sol = [_bdot(inv[h], rhs[h]) for h in heads]
    tick()
    m_sol = []
    for h in heads:
        m_hi, m_lo = _split_bf16(m[h], 2)
        s_hi, s_lo = _split_bf16(sol[h], 2)
        m_sol.append(_bdot(m_hi, s_hi) + _bdot(m_hi, s_lo) + _bdot(m_lo, s_hi))
    tick()
    sol = [sol[h] + _bdot(inv[h], rhs[h] - sol[h] - m_sol[h]) for h in heads]
    for _ in scan:
        pass
    gate_nx[...] = _silu(gz_ref[0])
    for h in heads:
        u_nx[h] = sol[h][:, :G_VAL_DIM]
        w_nx[h] = sol[h][:, G_VAL_DIM:].astype(BF16)
        qk_nx[h] = qk[h].astype(BF16)
        qe_nx[h] = (q[h] * eg[h]).astype(BF16)
        for ck in range(cb // c):
            r0, r1 = ck * c, (ck + 1) * c
            g_last = gc[h][r1 - 1:r1, :]
            kd_nx[h, r0:r1, :] = (k[h][r0:r1] * jnp.exp(g_last - gc[h][r0:r1])).astype(BF16)
            el_nx[h, ck] = jnp.broadcast_to(jnp.exp(g_last), (8, LANES))


def _gdn_call(gqkv, misc, gz, conv_w, hist, s0, gnorm_w, a_lane, d_lane, *, cb, t_true):
    b, t_pad, _ = gqkv.shape
    nblk = t_pad // cb
    assert (nblk - 1) * cb < t_true <= t_pad and (t_true - 1) % cb >= CONV_W - 2
    kern = functools.partial(_gdn_kernel, cb=cb, t_true=t_true, nblk=nblk)
    prep = lambda i, j: (i, jnp.minimum(j, nblk - 1), 0)
    scan = lambda i, j: (i, jnp.maximum(j - 1, 0), 0)
    nch = cb // CHUNK
    return pl.pallas_call(
        kern,
        out_shape=[jax.ShapeDtypeStruct((b, t_pad, G_WIDTH), BF16),
                   jax.ShapeDtypeStruct((b, CONV_W - 1, CONV_CH), F32),
                   jax.ShapeDtypeStruct((b, G_HEADS, G_KEY_DIM, G_VAL_DIM), F32)],
        grid=(b, nblk + 1),
        in_specs=[pl.BlockSpec((1, cb, CONV_CH), prep),
                  pl.BlockSpec((1, cb, LANES), prep),
                  pl.BlockSpec((1, cb, G_WIDTH), prep),
                  pl.BlockSpec((CONV_W, CONV_CH), lambda i, j: (0, 0)),
                  pl.BlockSpec((1, CONV_W - 1, CONV_CH), lambda i, j: (i, 0, 0)),
                  pl.BlockSpec((1, G_HEADS, G_KEY_DIM, G_VAL_DIM), lambda i, j: (i, 0, 0, 0)),
                  pl.BlockSpec((1, G_VAL_DIM), lambda i, j: (0, 0)),
                  pl.BlockSpec((1, LANES), lambda i, j: (0, 0)),
                  pl.BlockSpec((1, LANES), lambda i, j: (0, 0))],
        out_specs=[pl.BlockSpec((1, cb, G_WIDTH), scan),
                   pl.BlockSpec((1, CONV_W - 1, CONV_CH), lambda i, j: (i, 0, 0)),
                   pl.BlockSpec((1, G_HEADS, G_KEY_DIM, G_VAL_DIM), lambda i, j: (i, 0, 0, 0))],
        scratch_shapes=[pltpu.VMEM((cb + 8, CONV_CH), F32),
                        pltpu.VMEM((cb, CONV_CH), F32),
                        pltpu.VMEM((G_HEADS, G_KEY_DIM, G_VAL_DIM), F32),
                        pltpu.VMEM((G_HEADS, cb, G_VAL_DIM), F32),
                        pltpu.VMEM((2, G_HEADS, cb, G_VAL_DIM), F32),
                        pltpu.VMEM((2, G_HEADS, cb, G_KEY_DIM), BF16),
                        pltpu.VMEM((2, G_HEADS, cb, cb), BF16),
                        pltpu.VMEM((2, G_HEADS, cb, G_KEY_DIM), BF16),
                        pltpu.VMEM((2, G_HEADS, cb, G_KEY_DIM), BF16),
                        pltpu.VMEM((2, G_HEADS, nch, 8, LANES), F32),
                        pltpu.VMEM((2, cb, G_WIDTH), F32)],
        compiler_params=_cparams(2),
        name="gated_delta",
    )(gqkv, misc, gz, conv_w, hist, s0, gnorm_w, a_lane, d_lane)


def _pack_w_in(w_in):
    d = w_in.shape[0]
    splits = (512, 256, 256, 512, 64, 8, 512, 512, 512, 512, 4, 4)
    offs = np.concatenate([[0], np.cumsum(splits)])
    aq, ak, av, iq, ik, iw, gq, gk, gv, gz, gb, ga = [w_in[:, offs[i]:offs[i + 1]] for i in range(12)]
    qx = [aq[:, h * A_HEAD_DIM:(h + 1) * A_HEAD_DIM] * (A_HEAD_DIM ** -0.5 * LOG2E)
          for pair in Q_PAIRS for h in pair]
    ikx = [ik, ik]
    misc = [ik, iw, gb, ga, jnp.zeros((d, LANES - MISC_GA - G_HEADS), w_in.dtype)]
    cols = qx + [ak, av, iq * (IDX_DIM ** -0.5)] + ikx + misc + [gq, gk, gv, gz]
    packed = jnp.concatenate(cols, axis=1).astype(BF16)
    assert packed.shape[1] == IN_PACKED
    return packed


def _pack_ffn(w_gate, w_up, w_down):
    d = w_gate.shape[0]
    wg = w_gate.reshape(d, N_FF_CHUNKS, FF_CHUNK)
    wu = w_up.reshape(d, N_FF_CHUNKS, FF_CHUNK)
    wgu = jnp.transpose(jnp.concatenate([wg, wu], axis=2), (1, 0, 2)).astype(BF16)
    wd = w_down.reshape(N_FF_CHUNKS, FF_CHUNK, d).astype(BF16)
    return wgu, wd


def _ikx_layout(ik):
    return jnp.concatenate([ik, ik], axis=-1)


def _lane_vec(vals, lane0):
    return jnp.zeros((1, LANES), F32).at[0, lane0:lane0 + vals.shape[0]].set(vals.astype(F32))


def _run(x, mod, past, layers, rel_bias, norm_final):
    b, t, d = x.shape
    n = b * t
    per_token = t < TOKEN_TILE
    tm = n if per_token else TOKEN_TILE
    tiles_per_batch = None if per_token else t // tm
    tok = dict(tm=tm, tiles_per_batch=tiles_per_batch, per_token=per_token)

    x2 = x.reshape(n, d)
    states = []
    for li, lw in enumerate(layers):
        m = mod[li]

        def mvec(kidx):
            row = m[:, kidx]
            return jnp.repeat(row, t, axis=0) if per_token else row[:, None, :]

        sh1, sc1, gt1, sh2, sc2, gt2, sh3, sc3, gt3 = [mvec(i) for i in range(N_MOD)]
        x2 = _ffn_call(x2, sh1, sc1, gt1, lw["norm_ffn1"], lw["wgu1"], lw["wd1"], None, **tok)
        qx, k, kb, v, vb, iq, ikx, misc, gqkv, gz = _inproj_call(
            x2, sh2, sc2, lw["norm_mix"], lw["w_in"], **tok)

        if past is None:
            offset, l_true, tq = 0, t, ATT_TQ
            k_all, v_all, ikx_all = (a.reshape(b, t, -1) for a in (kb, vb, ikx))
            qx3, iq3, misc3 = (a.reshape(b, t, -1) for a in (qx, iq, misc))
            conv_hist = jnp.zeros((b, CONV_W - 1, CONV_CH), F32)
            s0 = jnp.zeros((b, G_HEADS, G_KEY_DIM, G_VAL_DIM), F32)
        else:
            k_hist, v_hist, ik_hist, conv_hist, s0 = (p[li] for p in past)
            offset = k_hist.shape[1]
            l_true = offset + t
            tq = LANES
            lp = -(-l_true // ATT_KB) * ATT_KB
            pad_k = lambda a: jnp.pad(a, ((0, 0), (0, lp - l_true), (0, 0)))
            k_all = pad_k(jnp.concatenate([k_hist.reshape(b, offset, -1).astype(BF16),
                                           kb.reshape(b, t, -1)], axis=1))
            v_all = pad_k(jnp.concatenate([v_hist.reshape(b, offset, -1).astype(BF16),
                                           vb.reshape(b, t, -1)], axis=1))
            ikx_all = pad_k(jnp.concatenate([_ikx_layout(ik_hist.astype(BF16)),
                                             ikx.reshape(b, t, -1)], axis=1))
            pad_q = lambda a: jnp.pad(a.reshape(b, t, -1), ((0, 0), (0, tq - t), (0, 0)))
            qx3, iq3, misc3 = pad_q(qx), pad_q(iq), pad_q(misc)
            s0 = s0.astype(F32)
            conv_hist = conv_hist.astype(F32)
        lp = k_all.shape[1]
        n_top = min(TOPK_MAX, l_true // 4)
        vt = jnp.transpose(v_all.reshape(b, lp // ATT_KB, ATT_KB, 2, LANES), (0, 1, 3, 4, 2))
        vt = jnp.concatenate([vt, jnp.ones((b, lp // ATT_KB, 2, V_ROWS - LANES, ATT_KB), BF16)], axis=3)
        attn = _attn_call(rel_bias, qx3, iq3, misc3, k_all, vt, ikx_all,
                          tq=tq, l_true=l_true, offset=offset, n_top=n_top)
        attn = attn[:, :t].reshape(n, A_WIDTH)

        cb = GDN_BLOCK if t % GDN_BLOCK == 0 else CHUNK
        t_pad = -(-t // cb) * cb
        pad_t = lambda a: jnp.pad(a.reshape(b, t, -1), ((0, 0), (0, t_pad - t), (0, 0)))
        gdn, conv_new, s_new = _gdn_call(pad_t(gqkv), pad_t(misc), pad_t(gz), lw["conv_w"], conv_hist, s0,
                                         lw["gnorm_w"], lw["a_lane"], lw["d_lane"], cb=cb, t_true=t)
        gdn = gdn[:, :t].reshape(n, G_WIDTH)

        nf = norm_final if li == len(layers) - 1 else None
        x2 = _ffn_call(x2, sh3, sc3, gt3, lw["norm_ffn2"], lw["wgu2"], lw["wd2"], nf,
                       mixer=(gt2, attn, gdn, lw["wo_a"], lw["wo_g"]), **tok)

        states.append((k.reshape(b, t, A_KV_HEADS, A_HEAD_DIM), v.reshape(b, t, A_KV_HEADS, A_HEAD_DIM),
                       misc.reshape(b, t, LANES)[..., :IDX_DIM], conv_new, s_new))
    stacked = [jnp.stack(s, axis=0) for s in zip(*states)]
    return x2.reshape(b, t, d), stacked


def kernel(x_prompt, x_sample, cache_k, cache_v, cache_idx_k, state_conv, state_delta, c_prompt, c_sample,
           w_mod, b_mod, norm_ffn1, norm_mix, norm_ffn2, ffn1_w_gate, ffn1_w_up, ffn1_w_down,
           ffn2_w_gate, ffn2_w_up, ffn2_w_down, w_in, w_out, rel_bias, conv_w, a_log, dt_bias, gnorm_w,
           norm_final):
    depth = w_mod.shape[0]
    bp = c_prompt.shape[0]
    c_all = jnp.concatenate([c_prompt, c_sample], axis=0)
    layers, mods_p, mods_s = [], [], []
    for l in range(depth):
        mod = _mod_call(c_all, w_mod[l], b_mod[l]).reshape(c_all.shape[0], N_MOD, D_MODEL)
        mods_p.append(mod[:bp])
        mods_s.append(mod[bp:])
        wgu1, wd1 = _pack_ffn(ffn1_w_gate[l], ffn1_w_up[l], ffn1_w_down[l])
        wgu2, wd2 = _pack_ffn(ffn2_w_gate[l], ffn2_w_up[l], ffn2_w_down[l])
        wo = w_out[l].astype(BF16)
        layers.append(dict(
            norm_ffn1=norm_ffn1[l].reshape(1, -1), norm_mix=norm_mix[l].reshape(1, -1),
            norm_ffn2=norm_ffn2[l].reshape(1, -1), wgu1=wgu1, wd1=wd1, wgu2=wgu2, wd2=wd2,
            w_in=_pack_w_in(w_in[l]), wo_a=wo[:A_WIDTH], wo_g=wo[A_WIDTH:],
            conv_w=conv_w[l], gnorm_w=gnorm_w[l].reshape(1, -1),
            a_lane=_lane_vec(a_log[l], MISC_GA), d_lane=_lane_vec(dt_bias[l], MISC_GA)))
    nf = norm_final.reshape(1, -1)
    y_p, (k_p, v_p, ik_p, conv_p, delta_p) = _run(x_prompt, mods_p, None, layers, rel_bias, nf)
    past = (cache_k, cache_v, cache_idx_k, state_conv, state_delta)
    y_s, (k_s, v_s, ik_s, conv_s, delta_s) = _run(x_sample, mods_s, past, layers, rel_bias, nf)
    return (y_p, y_s, k_p, v_p, ik_p, conv_p, delta_p, k_s, v_s, ik_s, conv_s, delta_s)
```

```python
import functools

import jax
import jax.numpy as jnp
import numpy as np
from jax import lax
from jax.experimental import pallas as pl
from jax.experimental.pallas import tpu as pltpu

F32 = jnp.float32
BF16 = jnp.bfloat16

D_MODEL = 1024
CHUNK = 64
A_HEAD_DIM = 64
A_HEADS = 8
A_KV_HEADS = 4
A_WIDTH = A_HEADS * A_HEAD_DIM
IDX_HEADS = 8
IDX_DIM = 64
TOPK_MAX = 256
REL_BUCKETS = 32
G_KEY_DIM = 128
G_VAL_DIM = 128
G_HEADS = 4
G_WIDTH = G_HEADS * G_VAL_DIM
CONV_W = 4
CONV_CH = 2 * G_HEADS * G_KEY_DIM + G_HEADS * G_VAL_DIM
D_FF = 2816
N_MOD = 9
EPS = 1e-6

LANES = 128
MXU_DIM = 256
VMEM_LIMIT_BYTES = 56 * 1024 * 1024

FF_CHUNK = MXU_DIM
N_FF_CHUNKS = D_FF // FF_CHUNK
TOKEN_TILE = 512
ATT_TQ = 256
ATT_KB = 256
GDN_BLOCK = 256

QX_W = A_HEADS * LANES
Q_PAIRS = ((0, 2), (1, 3), (4, 6), (5, 7))
OFF_QX = 0
OFF_K = OFF_QX + A_WIDTH
OFF_V = OFF_K + A_KV_HEADS * A_HEAD_DIM
OFF_IQ = OFF_V + A_KV_HEADS * A_HEAD_DIM
OFF_IKX = OFF_IQ + IDX_HEADS * IDX_DIM
OFF_MISC = OFF_IKX + LANES
OFF_GQKV = OFF_MISC + LANES
OFF_GZ = OFF_GQKV + CONV_CH
IN_PACKED = OFF_GZ + G_WIDTH
MISC_IW = IDX_DIM
MISC_GB = MISC_IW + IDX_HEADS
MISC_GA = MISC_GB + G_HEADS

NEG_BIG = -1e30
INT_MIN = -2 ** 31
LOG2E = 1.4426950408889634
V_ROWS = LANES + 16


def _cparams(n_axes):
    return pltpu.CompilerParams(dimension_semantics=("arbitrary",) * n_axes,
                                vmem_limit_bytes=VMEM_LIMIT_BYTES)


def _resident(shape):
    nd = len(shape)
    return pl.BlockSpec(shape, lambda *_: (0,) * nd, pipeline_mode=pl.Buffered(1))


def _dot_nt(a, b):
    return lax.dot_general(a, b, (((1,), (1,)), ((), ())), preferred_element_type=F32)


def _rms_mod(x, gain, shift, scale):
    ms = jnp.mean(x * x, axis=-1, keepdims=True)
    y = x * lax.rsqrt(ms + EPS) * gain
    return y * (1.0 + scale) + shift


def _silu(x):
    return x * jax.nn.sigmoid(x)


def _tree_sum(parts):
    while len(parts) > 1:
        parts = [a + b for a, b in zip(parts[0::2], parts[1::2])] + ([parts[-1]] if len(parts) % 2 else [])
    return parts[0]


def _mod_kernel(c_ref, w_ref, b_ref, o_ref):
    s = _silu(c_ref[...]).astype(BF16)
    o_ref[...] = jnp.dot(s, w_ref[...].astype(BF16), preferred_element_type=F32) + b_ref[...]


def _mod_call(c, w_mod, b_mod):
    rows, d = c.shape
    n = w_mod.shape[1]
    tn = D_MODEL
    return pl.pallas_call(
        _mod_kernel,
        out_shape=jax.ShapeDtypeStruct((rows, n), F32),
        grid=(n // tn,),
        in_specs=[pl.BlockSpec((rows, d), lambda j: (0, 0)),
                  pl.BlockSpec((d, tn), lambda j: (0, j)),
                  pl.BlockSpec((1, tn), lambda j: (0, j))],
        out_specs=pl.BlockSpec((rows, tn), lambda j: (0, j)),
        compiler_params=_cparams(1),
        name="mod",
    )(c, w_mod, b_mod.reshape(1, n))


def _mod_specs(per_token, tm, tiles_per_batch):
    if per_token:
        return pl.BlockSpec((tm, D_MODEL), lambda i: (i, 0))
    return pl.BlockSpec((None, 1, D_MODEL), lambda i: (i // tiles_per_batch, 0, 0))


def _ffn_kernel(x_ref, sh_ref, sc_ref, gt_ref, gain_ref, wgu_ref, wd_ref, *rest, final_norm, mixer):
    rest = list(rest)
    if mixer:
        gm_ref, a_ref, g_ref, wa_ref, wg_ref = rest[:5]
        rest = rest[5:]
    if final_norm:
        nf_ref = rest.pop(0)
    o_ref, acc_ref = rest
    x = x_ref[...]
    if mixer:
        x = x + gm_ref[...] * (jnp.dot(a_ref[...], wa_ref[...], preferred_element_type=F32)
                               + jnp.dot(g_ref[...], wg_ref[...], preferred_element_type=F32))
    h = _rms_mod(x, gain_ref[...], sh_ref[...], sc_ref[...]).astype(BF16)
    for j in range(N_FF_CHUNKS):
        ab = jnp.dot(h, wgu_ref[j], preferred_element_type=F32)
        g = (_silu(ab[:, :FF_CHUNK]) * ab[:, FF_CHUNK:]).astype(BF16)
        d = jnp.dot(g, wd_ref[j], preferred_element_type=F32)
        if j == 0:
            acc_ref[...] = d
        else:
            acc_ref[...] += d
    y = x + 0.5 * gt_ref[...] * acc_ref[...]
    if final_norm:
        ms = jnp.mean(y * y, axis=-1, keepdims=True)
        y = y * lax.rsqrt(ms + EPS) * nf_ref[...]
    o_ref[...] = y


def _ffn_call(x2, sh, sc, gt, gain, wgu, wd, norm_final, mixer=None, *, tm, tiles_per_batch, per_token):
    n = x2.shape[0]
    mspec = _mod_specs(per_token, tm, tiles_per_batch)
    in_specs = [pl.BlockSpec((tm, D_MODEL), lambda i: (i, 0)), mspec, mspec, mspec,
                _resident((1, D_MODEL)), _resident(wgu.shape), _resident(wd.shape)]
    args = [x2, sh, sc, gt, gain, wgu, wd]
    if mixer is not None:
        gm, attn, gdn, wa, wg = mixer
        in_specs += [mspec, pl.BlockSpec((tm, A_WIDTH), lambda i: (i, 0)),
                     pl.BlockSpec((tm, G_WIDTH), lambda i: (i, 0)), _resident(wa.shape), _resident(wg.shape)]
        args += [gm, attn, gdn, wa, wg]
    final_norm = norm_final is not None
    if final_norm:
        in_specs.append(_resident((1, D_MODEL)))
        args.append(norm_final)
    return pl.pallas_call(
        functools.partial(_ffn_kernel, final_norm=final_norm, mixer=mixer is not None),
        out_shape=jax.ShapeDtypeStruct((n, D_MODEL), F32),
        grid=(n // tm,),
        in_specs=in_specs,
        out_specs=pl.BlockSpec((tm, D_MODEL), lambda i: (i, 0)),
        scratch_shapes=[pltpu.VMEM((tm, D_MODEL), F32)],
        compiler_params=_cparams(1),
        name="ffn_final" if final_norm else "ffn",
    )(*args)


def _inproj_kernel(x_ref, sh_ref, sc_ref, gain_ref, w_ref,
                   qx_o, k_o, kb_o, v_o, vb_o, iq_o, ikx_o, misc_o, gqkv_o, gz_o):
    h = _rms_mod(x_ref[...], gain_ref[...], sh_ref[...], sc_ref[...]).astype(BF16)

    def mm(off, width):
        return jnp.dot(h, w_ref[:, off:off + width], preferred_element_type=F32)

    qx_o[...] = mm(OFF_QX, A_WIDTH).astype(BF16)
    k = mm(OFF_K, OFF_V - OFF_K)
    k_o[...] = k
    kb_o[...] = k.astype(BF16)
    v = mm(OFF_V, OFF_IQ - OFF_V)
    v_o[...] = v
    vb_o[...] = v.astype(BF16)
    iq_o[...] = mm(OFF_IQ, OFF_IKX - OFF_IQ).astype(BF16)
    ikx_o[...] = mm(OFF_IKX, OFF_MISC - OFF_IKX).astype(BF16)
    misc_o[...] = mm(OFF_MISC, LANES)
    gqkv_o[...] = mm(OFF_GQKV, CONV_CH)
    gz_o[...] = mm(OFF_GZ, G_WIDTH)


def _inproj_call(x2, sh, sc, gain, w_packed, *, tm, tiles_per_batch, per_token):
    n = x2.shape[0]
    mspec = _mod_specs(per_token, tm, tiles_per_batch)
    widths = [(A_WIDTH, BF16), (256, F32), (256, BF16), (256, F32), (256, BF16), (512, BF16),
              (LANES, BF16), (LANES, F32), (CONV_CH, F32), (G_WIDTH, F32)]
    return pl.pallas_call(
        _inproj_kernel,
        out_shape=[jax.ShapeDtypeStruct((n, w), dt) for w, dt in widths],
        grid=(n // tm,),
        in_specs=[pl.BlockSpec((tm, D_MODEL), lambda i: (i, 0)), mspec, mspec,
                  _resident((1, D_MODEL)), _resident(w_packed.shape)],
        out_specs=[pl.BlockSpec((tm, w), lambda i: (i, 0)) for w, _ in widths],
        compiler_params=_cparams(1),
        name="inproj",
    )(x2, sh, sc, gain, w_packed)


def _rel_bucket_int(rel):
    n = jnp.abs(rel)
    large = jnp.full(rel.shape, 8, jnp.int32)
    for th in (12, 16, 23, 32, 46, 64, 91):
        large = large + jnp.where(n >= th, 1, 0)
    return jnp.where(rel > 0, REL_BUCKETS // 2, 0) + jnp.where(n < 8, n, large)


def _key_to_float(t):
    bits = jnp.where(t >= 0, t, t ^ jnp.int32(0x7FFFFFFF))
    return lax.bitcast_convert_type(bits, F32)


def _top16(x):
    bits = lax.bitcast_convert_type(x, jnp.int32) & jnp.int32(-65536)
    return lax.bitcast_convert_type(bits, F32).astype(BF16)


def _bias_kernel(relb_ref, o_ref, *, tq):
    h = pl.program_id(1)
    key_off = lax.broadcasted_iota(jnp.int32, (ATT_KB, tq), 0)
    qry_off = lax.broadcasted_iota(jnp.int32, (ATT_KB, tq), 1)
    bucket = _rel_bucket_int(key_off - qry_off - pl.program_id(0) * ATT_KB)
    t = jnp.zeros((ATT_KB, tq), F32)
    for b in range(REL_BUCKETS):
        t = jnp.where(bucket == b, relb_ref[b, h], t)
    o_ref[0, 0] = (t * LOG2E).astype(BF16)


def _bias_call(rel_bias, tq):
    return pl.pallas_call(
        functools.partial(_bias_kernel, tq=tq),
        out_shape=jax.ShapeDtypeStruct((3, A_HEADS, ATT_KB, tq), BF16),
        grid=(3, A_HEADS),
        in_specs=[pl.BlockSpec(memory_space=pltpu.SMEM)],
        out_specs=pl.BlockSpec((1, 1, ATT_KB, tq), lambda d, h: (d, h, 0, 0)),
        compiler_params=_cparams(2),
        name="rel_bias_tiles",
    )(rel_bias)


def _attn_kernel(q_ref, iq_ref, misc_ref, k_ref, vt_ref, ikx_ref, bias_ref, o_ref,
                 s_ref, shi_ref, sel_ref, lg_ref, cm_ref, acc_ref, cnt_ref, cnth_ref, qx_ref,
                 *, tq, l_true, offset, n_top):
    kb_sz = ATT_KB
    low_half_q = lax.broadcasted_iota(jnp.int32, (tq, LANES), 1) < A_HEAD_DIM
    low_half_k = lax.broadcasted_iota(jnp.int32, (kb_sz, LANES), 1) < A_HEAD_DIM

    for s, pair in enumerate(Q_PAIRS):
        slab = q_ref[0, :, s * LANES:(s + 1) * LANES]
        zero = jnp.zeros_like(slab)
        qx_ref[:, pair[0] * LANES:(pair[0] + 1) * LANES] = jnp.where(low_half_q, slab, zero)
        qx_ref[:, pair[1] * LANES:(pair[1] + 1) * LANES] = jnp.where(low_half_q, zero, slab)
    q0 = offset + pl.program_id(1) * tq
    kmax = jnp.minimum(q0 + tq, l_true)
    nkb = (kmax + kb_sz - 1) // kb_sz

    key_off = lax.broadcasted_iota(jnp.int32, (kb_sz, tq), 0)

    qpos = q0 + lax.broadcasted_iota(jnp.int32, (1, tq), 1)
    limit = jnp.minimum((qpos // CHUNK + 1) * CHUNK, l_true)

    iw_t = misc_ref[0].T[MISC_IW:MISC_IW + IDX_HEADS, :] * (IDX_HEADS ** -0.5)

    def score_body(kb, carry):
        base = pl.multiple_of(kb * kb_sz, kb_sz)
        ik2 = ikx_ref[0, pl.ds(base, kb_sz), :]
        ik_lo = jnp.where(low_half_k, ik2, jnp.zeros_like(ik2))
        ik_hi = jnp.where(low_half_k, jnp.zeros_like(ik2), ik2)
        s = jnp.zeros((kb_sz, tq), F32)
        for j in range(IDX_HEADS // 2):
            slab = iq_ref[0, :, j * LANES:(j + 1) * LANES]
            d0 = _dot_nt(ik_lo, slab)
            d1 = _dot_nt(ik_hi, slab)
            s = s + iw_t[2 * j:2 * j + 1, :] * jnp.maximum(d0, 0.0)
            s = s + iw_t[2 * j + 1:2 * j + 2, :] * jnp.maximum(d1, 0.0)
        s = jnp.where(base + key_off < limit, s, -jnp.inf)
        s_ref[pl.ds(base, kb_sz), :] = s
        shi_ref[pl.ds(base, kb_sz), :] = _top16(s)
        return carry

    def for_each_block(body):
        def quad(i, carry):
            for r in range(4):
                body(4 * i + r, carry)
            return carry
        lax.fori_loop(0, nkb // 4, quad, 0)
        done = (nkb // 4) * 4

        @pl.when(nkb % 4 >= 2)
        def _():
            body(done, 0)
            body(done + 1, 0)

        @pl.when(nkb % 2 == 1)
        def _():
            body(nkb - 1, 0)

    for_each_block(score_body)

    def sum_blocks(contrib, acc_ref, n=nkb):
        acc_ref[...] = jnp.zeros(acc_ref.shape, acc_ref.dtype)

        def quad(i, carry):
            acc_ref[...] += _tree_sum([contrib(4 * i + r) for r in range(4)])
            return carry
        lax.fori_loop(0, n // 4, quad, 0)
        done = (n // 4) * 4

        @pl.when(n % 4 >= 2)
        def _():
            acc_ref[...] += contrib(done) + contrib(done + 1)

        @pl.when(n % 2 == 1)
        def _():
            acc_ref[...] += contrib(n - 1)
        return acc_ref[...]

    def count(pred, n=nkb):
        def contrib(kb):
            base = pl.multiple_of(kb * kb_sz, kb_sz)
            hit = pred(s_ref[pl.ds(base, kb_sz), :], base + key_off)
            return _tree_sum([hit[r:r + 8] for r in range(0, kb_sz, 8)])
        return jnp.sum(sum_blocks(contrib, cnt_ref, n), axis=0, keepdims=True)

    def count_hi(thr_hi):
        one = jnp.ones((), BF16)
        zero = jnp.zeros((), BF16)

        def contrib(kb):
            base = pl.multiple_of(kb * kb_sz, kb_sz)
            hit = jnp.where(shi_ref[pl.ds(base, kb_sz), :] >= thr_hi, one, zero)
            return _tree_sum([hit[r:r + 16] for r in range(0, kb_sz, 16)])
        return jnp.sum(sum_blocks(contrib, cnth_ref).astype(F32), axis=0, keepdims=True)

    def hi_body(i, carry):
        t, c_t = carry
        cand = t + lax.shift_left(jnp.int32(1), 31 - i)
        c = count_hi(_top16(_key_to_float(cand)))
        return jnp.where(c >= n_top, cand, t), jnp.where(c >= n_top, c, c_t)

    def lo_body(i, carry):
        t, c_t = carry
        cand = t + lax.shift_left(jnp.int32(1), 15 - i)
        thr_c = _key_to_float(cand)
        c = count(lambda blk, _: jnp.where(blk >= thr_c, 1.0, 0.0))
        return jnp.where(c >= n_top, cand, t), jnp.where(c >= n_top, c, c_t)

    carry = (jnp.full((1, tq), INT_MIN, jnp.int32), jnp.full((1, tq), float(n_top), F32))
    carry = lax.fori_loop(0, 16, hi_body, carry)
    t_key, n_ge = lax.fori_loop(0, 16, lo_body, carry)
    thr = _key_to_float(t_key)

    take_all = limit <= n_top
    excess = jnp.where(take_all, 0.0, jnp.where(n_ge > n_top, 1.0, 0.0))
    any_excess = jnp.max(excess) > 0.0
    n_tie = jnp.where(any_excess, nkb, 0)
    need = n_top - count(lambda blk, _: jnp.where(blk > thr, 1.0, 0.0), n_tie)
    idx_bits = 14

    def tie_body(i, c):
        cand = c + lax.shift_left(jnp.int32(1), idx_bits - 1 - i)
        f = count(lambda blk, kidx: jnp.where(blk == thr, jnp.where(kidx < cand, 1.0, 0.0), 0.0))
        return jnp.where(f <= need, cand, c)

    cut0 = jnp.where(any_excess, jnp.zeros((1, tq), jnp.int32),
                     jnp.full((1, tq), 2 ** idx_bits, jnp.int32))
    cut = lax.fori_loop(0, jnp.where(any_excess, idx_bits, 0), tie_body, cut0)
    thr = jnp.where(take_all, -jnp.inf, thr)
    cut = jnp.where(take_all, limit, cut)

    def mask_body(kb, carry):
        base = pl.multiple_of(kb * kb_sz, kb_sz)
        blk = s_ref[pl.ds(base, kb_sz), :]
        tie = jnp.where(base + key_off < cut, 0.0, NEG_BIG)
        sel = jnp.where(blk > thr, 0.0, jnp.where(blk == thr, tie, NEG_BIG))
        sel_ref[pl.ds(base, kb_sz), :] = sel.astype(BF16)
        return carry

    lax.fori_loop(0, nkb, mask_body, 0)

    cm_ref[...] = jnp.full(cm_ref.shape, NEG_BIG, BF16)

    def logit_body(kb, carry):
        base = pl.multiple_of(kb * kb_sz, kb_sz)
        dsel = jnp.clip((q0 - base) // kb_sz, 0, 2)
        kblk = k_ref[0, pl.ds(base, kb_sz), :]
        sel = sel_ref[pl.ds(base, kb_sz), :]
        for h in range(A_HEADS):
            sl = h // 4
            lg = _dot_nt(kblk[:, sl * LANES:(sl + 1) * LANES], qx_ref[:, h * LANES:(h + 1) * LANES])
            lg = lg.astype(BF16) + sel + bias_ref[dsel, h]
            lg_ref[h, pl.ds(base, kb_sz), :] = lg
            cm_ref[h] = jnp.maximum(cm_ref[h], lg)
        return carry

    for_each_block(logit_body)

    acc_ref[...] = jnp.zeros(acc_ref.shape, F32)
    m_rows = [jnp.max(cm_ref[h].astype(F32), axis=0, keepdims=True) for h in range(A_HEADS)]

    def pv_body(kb, carry):
        base = pl.multiple_of(kb * kb_sz, kb_sz)
        for h in range(A_HEADS):
            p = jnp.exp2(lg_ref[h, pl.ds(base, kb_sz), :].astype(F32) - m_rows[h]).astype(BF16)
            acc_ref[h] += jnp.dot(vt_ref[0, kb, h // 4], p, preferred_element_type=F32)
        return carry

    for_each_block(pv_body)

    outs = []
    for h in range(A_HEADS):
        pos = (h // 2) % 2
        outs.append(acc_ref[h, pos * A_HEAD_DIM:(pos + 1) * A_HEAD_DIM, :] / acc_ref[h, LANES:LANES + 1, :])
    o_ref[0] = jnp.concatenate(outs, axis=0).T.astype(BF16)


def _attn_call(rel_bias, qx, iq, misc, kb, vt, ikx, *, tq, l_true, offset, n_top):
    b, tq_total, _ = qx.shape
    lp = kb.shape[1]
    nq = tq_total // tq
    assert offset % ATT_KB == 0 and tq % CHUNK == 0 and ATT_KB % tq == 0 and lp % ATT_KB == 0
    assert lp < 2 ** 14 and l_true <= lp
    kern = functools.partial(_attn_kernel, tq=tq, l_true=l_true, offset=offset, n_top=n_top)
    bias = _bias_call(rel_bias, tq)
    return pl.pallas_call(
        kern,
        out_shape=jax.ShapeDtypeStruct((b, tq_total, A_WIDTH), BF16),
        grid=(b, nq),
        in_specs=[pl.BlockSpec((1, tq, A_WIDTH), lambda i, j: (i, j, 0)),
                  pl.BlockSpec((1, tq, IDX_HEADS * IDX_DIM), lambda i, j: (i, j, 0)),
                  pl.BlockSpec((1, tq, LANES), lambda i, j: (i, j, 0)),
                  pl.BlockSpec((1, lp, 256), lambda i, j: (i, 0, 0)),
                  pl.BlockSpec((1, lp // ATT_KB, 2, V_ROWS, ATT_KB), lambda i, j: (i, 0, 0, 0, 0)),
                  pl.BlockSpec((1, lp, LANES), lambda i, j: (i, 0, 0)),
                  _resident(bias.shape)],
        out_specs=pl.BlockSpec((1, tq, A_WIDTH), lambda i, j: (i, j, 0)),
        scratch_shapes=[pltpu.VMEM((lp, tq), F32),
                        pltpu.VMEM((lp, tq), BF16),
                        pltpu.VMEM((lp, tq), BF16),
                        pltpu.VMEM((A_HEADS, lp, tq), BF16),
                        pltpu.VMEM((A_HEADS, ATT_KB, tq), BF16),
                        pltpu.VMEM((A_HEADS, V_ROWS, tq), F32),
                        pltpu.VMEM((8, tq), F32),
                        pltpu.VMEM((16, tq), BF16),
                        pltpu.VMEM((tq, QX_W), BF16)],
        compiler_params=_cparams(2),
        name="sparse_attn",
    )(qx, iq, misc, kb, vt, ikx, bias)


def _split_bf16(a, n):
    parts = []
    r = a
    for i in range(n):
        p = r.astype(BF16)
        parts.append(p)
        if i + 1 < n:
            r = r - p.astype(F32)
    return parts


def _bdot(a, b, dims=(((1,), (0,)), ((), ()))):
    return lax.dot_general(a.astype(BF16), b.astype(BF16), dims, preferred_element_type=F32)


_NT = (((1,), (1,)), ((), ()))
_TN = (((0,), (0,)), ((), ()))


def _gdn_kernel(x_ref, misc_ref, gz_ref, convw_ref, hist_ref, s0_ref, gnw_ref, alane_ref, dlane_ref,
                o_ref, conv_o, s_o, xp_ref, y_ref, st_ref, vn_ref,
                u_ref, w_ref, qk_ref, qe_ref, kd_ref, el_ref, gate_ref, *, cb, t_true, nblk):
    c = CHUNK
    step = pl.program_id(1)
    blk = jnp.minimum(step, nblk - 1)
    carried = (u_ref, w_ref, qk_ref, qe_ref, kd_ref, el_ref, gate_ref)
    u_nx, w_nx, qk_nx, qe_nx, kd_nx, el_nx, gate_nx = (r.at[1] for r in carried)
    u_ref, w_ref, qk_ref, qe_ref, kd_ref, el_ref, gate_ref = (r.at[0] for r in carried)

    @pl.when(step == 0)
    def _():
        xp_ref[8 - (CONV_W - 1):8, :] = hist_ref[0]
        st_ref[...] = s0_ref[0]
        for r in carried:
            r[...] = jnp.zeros(r.shape, r.dtype)

    heads = range(G_HEADS)
    for r in carried:
        r[0] = r[1]

    def scan_stages():
        vn_ref[...] = jnp.zeros(vn_ref.shape, F32)
        for ck in range(cb // c):
            r0, r1 = ck * c, (ck + 1) * c
            s_prev = [st_ref[h] for h in heads]
            v_new = [u_ref[h, r0:r1, :] - _bdot(w_ref[h, r0:r1, :], s_prev[h]) for h in heads]
            o_state = [_bdot(qe_ref[h, r0:r1, :], s_prev[h]) for h in heads]
            yield
            for h in heads:
                vn_ref[h, r0:r1, :] = v_new[h]
            o = [o_state[h] + _bdot(qk_ref[h, r0:r1, :], vn_ref[h]) for h in heads]
            for h in heads:
                s_new = s_prev[h] * el_ref[h, ck, 0:1, :] + _bdot(kd_ref[h, r0:r1, :], v_new[h], _TN)
                st_ref[h] = jnp.where(step > 0, s_new, s_prev[h])
            yield
            for h in heads:
                on = o[h] * lax.rsqrt(jnp.mean(o[h] * o[h], axis=-1, keepdims=True) + EPS) * gnw_ref[...]
                gate = gate_ref[r0:r1, h * G_VAL_DIM:(h + 1) * G_VAL_DIM]
                o_ref[0, r0:r1, h * G_VAL_DIM:(h + 1) * G_VAL_DIM] = (on * gate).astype(BF16)
            yield
        s_o[0] = st_ref[...]

    scan = scan_stages()

    def tick():
        next(scan, None)

    tick()

    xp_ref[8:8 + cb, :] = x_ref[0]
    y = convw_ref[CONV_W - 1:CONV_W, :] * xp_ref[8:8 + cb, :]
    for j in range(CONV_W - 1):
        y = y + convw_ref[j:j + 1, :] * xp_ref[5 + j:5 + j + cb, :]
    y_ref[...] = _silu(y)

    last_row = (t_true - 1) % cb
    conv_o[0] = xp_ref[8 + last_row - (CONV_W - 2):8 + last_row + 1, :]

    xp_ref[8 - (CONV_W - 1):8, :] = xp_ref[8 + cb - (CONV_W - 1):8 + cb, :]

    ri = lax.broadcasted_iota(jnp.int32, (cb, cb), 0)
    ci = lax.broadcasted_iota(jnp.int32, (cb, cb), 1)
    lag = jnp.where((ri // c) == (ci // c), ri - ci, -1)
    tri = lag >= 0
    strict = lag > 0
    eye = ri == ci
    tril_bf = jnp.where(tri, 1.0, 0.0).astype(BF16)
    eye_f = jnp.where(eye, 1.0, 0.0)

    ms = misc_ref[0]
    tok = blk * cb + lax.broadcasted_iota(jnp.int32, (cb, 1), 0)
    live = tok < t_true
    beta_s = jnp.where(live, jax.nn.sigmoid(ms), 0.0)
    z = ms + dlane_ref[...]
    softplus = jnp.maximum(z, 0.0) + jnp.log(1.0 + jnp.exp(-jnp.abs(z)))
    g_s = jnp.where(live, -jnp.exp(alane_ref[...]) * softplus, 0.0)
    gc_s = None
    for piece in _split_bf16(g_s, 3):
        t = jnp.dot(tril_bf, piece, preferred_element_type=F32)
        gc_s = t if gc_s is None else gc_s + t

    tick()

    q, k, v, beta, gc, decay, kb, eg = ([None] * G_HEADS for _ in range(8))
    for h in heads:
        qh = y_ref[:, h * G_KEY_DIM:(h + 1) * G_KEY_DIM]
        kh = y_ref[:, G_WIDTH + h * G_KEY_DIM:G_WIDTH + (h + 1) * G_KEY_DIM]
        v[h] = y_ref[:, 2 * G_WIDTH + h * G_VAL_DIM:2 * G_WIDTH + (h + 1) * G_VAL_DIM]
        q[h] = qh * lax.rsqrt(jnp.sum(qh * qh, axis=-1, keepdims=True) + 1e-6) * (G_KEY_DIM ** -0.5)
        k[h] = kh * lax.rsqrt(jnp.sum(kh * kh, axis=-1, keepdims=True) + 1e-6)
        beta[h] = beta_s[:, MISC_GB + h:MISC_GB + h + 1]
        gc[h] = gc_s[:, MISC_GA + h:MISC_GA + h + 1]
        gc_b = jnp.broadcast_to(gc[h], (cb, cb))
        gc_row = jnp.sum(jnp.where(eye, gc_b, 0.0), axis=0, keepdims=True)
        decay[h] = jnp.exp(jnp.where(tri, gc_b - gc_row, NEG_BIG))
        kb[h] = k[h] * beta[h]
        eg[h] = jnp.exp(gc[h])
    tick()
    m = [jnp.where(strict, _bdot(kb[h], k[h], _NT) * decay[h], 0.0) for h in heads]
    rhs = [jnp.concatenate([v[h] * beta[h], kb[h] * eg[h]], axis=1) for h in heads]
    qk = [jnp.where(tri, _bdot(q[h], k[h], _NT) * decay[h], 0.0) for h in heads]
    tick()
    pw = [-m[h] for h in heads]
    inv = [eye_f + pw[h] for h in heads]
    for _ in range(5):
        pw = [_bdot(pw[h], pw[h]) for h in heads]
        tick()
        inv = [inv[h] + _bdot(inv[h], pw[h]) for h in heads]
    tick()
    sol = [_bdot(inv[h], rhs[h]) for h in heads]
    tick()
    m_sol = []
    for h in heads:
        m_hi, m_lo = _split_bf16(m[h], 2)
        s_hi, s_lo = _split_bf16(sol[h], 2)
        m_sol.append(_bdot(m_hi, s_hi) + _bdot(m_hi, s_lo) + _bdot(m_lo, s_hi))
    tick()
    sol = [sol[h] + _bdot(inv[h], rhs[h] - sol[h] - m_sol[h]) for h in heads]
    for _ in scan:
        pass
    gate_nx[...] = _silu(gz_ref[0])
    for h in heads:
        u_nx[h] = sol[h][:, :G_VAL_DIM]
        w_nx[h] = sol[h][:, G_VAL_DIM:].astype(BF16)
        qk_nx[h] = qk[h].astype(BF16)
        qe_nx[h] = (q[h] * eg[h]).astype(BF16)
        for ck in range(cb // c):
            r0, r1 = ck * c, (ck + 1) * c
            g_last = gc[h][r1 - 1:r1, :]
            kd_nx[h, r0:r1, :] = (k[h][r0:r1] * jnp.exp(g_last - gc[h][r0:r1])).astype(BF16)
            el_nx[h, ck] = jnp.broadcast_to(jnp.exp(g_last), (8, LANES))


def _gdn_call(gqkv, misc, gz, conv_w, hist, s0, gnorm_w, a_lane, d_lane, *, cb, t_true):
    b, t_pad, _ = gqkv.shape
    nblk = t_pad // cb
    assert (nblk - 1) * cb < t_true <= t_pad and (t_true - 1) % cb >= CONV_W - 2
    kern = functools.partial(_gdn_kernel, cb=cb, t_true=t_true, nblk=nblk)
    prep = lambda i, j: (i, jnp.minimum(j, nblk - 1), 0)
    scan = lambda i, j: (i, jnp.maximum(j - 1, 0), 0)
    nch = cb // CHUNK
    return pl.pallas_call(
        kern,
        out_shape=[jax.ShapeDtypeStruct((b, t_pad, G_WIDTH), BF16),
                   jax.ShapeDtypeStruct((b, CONV_W - 1, CONV_CH), F32),
                   jax.ShapeDtypeStruct((b, G_HEADS, G_KEY_DIM, G_VAL_DIM), F32)],
        grid=(b, nblk + 1),
        in_specs=[pl.BlockSpec((1, cb, CONV_CH), prep),
                  pl.BlockSpec((1, cb, LANES), prep),
                  pl.BlockSpec((1, cb, G_WIDTH), prep),
                  pl.BlockSpec((CONV_W, CONV_CH), lambda i, j: (0, 0)),
                  pl.BlockSpec((1, CONV_W - 1, CONV_CH), lambda i, j: (i, 0, 0)),
                  pl.BlockSpec((1, G_HEADS, G_KEY_DIM, G_VAL_DIM), lambda i, j: (i, 0, 0, 0)),
                  pl.BlockSpec((1, G_VAL_DIM), lambda i, j: (0, 0)),
                  pl.BlockSpec((1, LANES), lambda i, j: (0, 0)),
                  pl.BlockSpec((1, LANES), lambda i, j: (0, 0))],
        out_specs=[pl.BlockSpec((1, cb, G_WIDTH), scan),
                   pl.BlockSpec((1, CONV_W - 1, CONV_CH), lambda i, j: (i, 0, 0)),
                   pl.BlockSpec((1, G_HEADS, G_KEY_DIM, G_VAL_DIM), lambda i, j: (i, 0, 0, 0))],
        scratch_shapes=[pltpu.VMEM((cb + 8, CONV_CH), F32),
                        pltpu.VMEM((cb, CONV_CH), F32),
                        pltpu.VMEM((G_HEADS, G_KEY_DIM, G_VAL_DIM), F32),
                        pltpu.VMEM((G_HEADS, cb, G_VAL_DIM), F32),
                        pltpu.VMEM((2, G_HEADS, cb, G_VAL_DIM), F32),
                        pltpu.VMEM((2, G_HEADS, cb, G_KEY_DIM), BF16),
                        pltpu.VMEM((2, G_HEADS, cb, cb), BF16),
                        pltpu.VMEM((2, G_HEADS, cb, G_KEY_DIM), BF16),
                        pltpu.VMEM((2, G_HEADS, cb, G_KEY_DIM), BF16),
                        pltpu.VMEM((2, G_HEADS, nch, 8, LANES), F32),
                        pltpu.VMEM((2, cb, G_WIDTH), F32)],
        compiler_params=_cparams(2),
        name="gated_delta",
    )(gqkv, misc, gz, conv_w, hist, s0, gnorm_w, a_lane, d_lane)


def _pack_w_in(w_in):
    d = w_in.shape[0]
    splits = (512, 256, 256, 512, 64, 8, 512, 512, 512, 512, 4, 4)
    offs = np.concatenate([[0], np.cumsum(splits)])
    aq, ak, av, iq, ik, iw, gq, gk, gv, gz, gb, ga = [w_in[:, offs[i]:offs[i + 1]] for i in range(12)]
    qx = [aq[:, h * A_HEAD_DIM:(h + 1) * A_HEAD_DIM] * (A_HEAD_DIM ** -0.5 * LOG2E)
          for pair in Q_PAIRS for h in pair]
    ikx = [ik, ik]
    misc = [ik, iw, gb, ga, jnp.zeros((d, LANES - MISC_GA - G_HEADS), w_in.dtype)]
    cols = qx + [ak, av, iq * (IDX_DIM ** -0.5)] + ikx + misc + [gq, gk, gv, gz]
    packed = jnp.concatenate(cols, axis=1).astype(BF16)
    assert packed.shape[1] == IN_PACKED
    return packed


def _pack_ffn(w_gate, w_up, w_down):
    d = w_gate.shape[0]
    wg = w_gate.reshape(d, N_FF_CHUNKS, FF_CHUNK)
    wu = w_up.reshape(d, N_FF_CHUNKS, FF_CHUNK)
    wgu = jnp.transpose(jnp.concatenate([wg, wu], axis=2), (1, 0, 2)).astype(BF16)
    wd = w_down.reshape(N_FF_CHUNKS, FF_CHUNK, d).astype(BF16)
    return wgu, wd


def _ikx_layout(ik):
    return jnp.concatenate([ik, ik], axis=-1)


def _lane_vec(vals, lane0):
    return jnp.zeros((1, LANES), F32).at[0, lane0:lane0 + vals.shape[0]].set(vals.astype(F32))


def _run(x, mod, past, layers, rel_bias, norm_final):
    b, t, d = x.shape
    n = b * t
    per_token = t < TOKEN_TILE
    tm = n if per_token else TOKEN_TILE
    tiles_per_batch = None if per_token else t // tm
    tok = dict(tm=tm, tiles_per_batch=tiles_per_batch, per_token=per_token)

    x2 = x.reshape(n, d)
    states = []
    for li, lw in enumerate(layers):
        m = mod[li]

        def mvec(kidx):
            row = m[:, kidx]
            return jnp.repeat(row, t, axis=0) if per_token else row[:, None, :]

        sh1, sc1, gt1, sh2, sc2, gt2, sh3, sc3, gt3 = [mvec(i) for i in range(N_MOD)]
        x2 = _ffn_call(x2, sh1, sc1, gt1, lw["norm_ffn1"], lw["wgu1"], lw["wd1"], None, **tok)
        qx, k, kb, v, vb, iq, ikx, misc, gqkv, gz = _inproj_call(
            x2, sh2, sc2, lw["norm_mix"], lw["w_in"], **tok)

        if past is None:
            offset, l_true, tq = 0, t, ATT_TQ
            k_all, v_all, ikx_all = (a.reshape(b, t, -1) for a in (kb, vb, ikx))
            qx3, iq3, misc3 = (a.reshape(b, t, -1) for a in (qx, iq, misc))
            conv_hist = jnp.zeros((b, CONV_W - 1, CONV_CH), F32)
            s0 = jnp.zeros((b, G_HEADS, G_KEY_DIM, G_VAL_DIM), F32)
        else:
            k_hist, v_hist, ik_hist, conv_hist, s0 = (p[li] for p in past)
            offset = k_hist.shape[1]
            l_true = offset + t
            tq = LANES
            lp = -(-l_true // ATT_KB) * ATT_KB
            pad_k = lambda a: jnp.pad(a, ((0, 0), (0, lp - l_true), (0, 0)))
            k_all = pad_k(jnp.concatenate([k_hist.reshape(b, offset, -1).astype(BF16),
                                           kb.reshape(b, t, -1)], axis=1))
            v_all = pad_k(jnp.concatenate([v_hist.reshape(b, offset, -1).astype(BF16),
                                           vb.reshape(b, t, -1)], axis=1))
            ikx_all = pad_k(jnp.concatenate([_ikx_layout(ik_hist.astype(BF16)),
                                             ikx.reshape(b, t, -1)], axis=1))
            pad_q = lambda a: jnp.pad(a.reshape(b, t, -1), ((0, 0), (0, tq - t), (0, 0)))
            qx3, iq3, misc3 = pad_q(qx), pad_q(iq), pad_q(misc)
            s0 = s0.astype(F32)
            conv_hist = conv_hist.astype(F32)
        lp = k_all.shape[1]
        n_top = min(TOPK_MAX, l_true // 4)
        vt = jnp.transpose(v_all.reshape(b, lp // ATT_KB, ATT_KB, 2, LANES), (0, 1, 3, 4, 2))
        vt = jnp.concatenate([vt, jnp.ones((b, lp // ATT_KB, 2, V_ROWS - LANES, ATT_KB), BF16)], axis=3)
        attn = _attn_call(rel_bias, qx3, iq3, misc3, k_all, vt, ikx_all,
                          tq=tq, l_true=l_true, offset=offset, n_top=n_top)
        attn = attn[:, :t].reshape(n, A_WIDTH)

        cb = GDN_BLOCK if t % GDN_BLOCK == 0 else CHUNK
        t_pad = -(-t // cb) * cb
        pad_t = lambda a: jnp.pad(a.reshape(b, t, -1), ((0, 0), (0, t_pad - t), (0, 0)))
        gdn, conv_new, s_new = _gdn_call(pad_t(gqkv), pad_t(misc), pad_t(gz), lw["conv_w"], conv_hist, s0,
                                         lw["gnorm_w"], lw["a_lane"], lw["d_lane"], cb=cb, t_true=t)
        gdn = gdn[:, :t].reshape(n, G_WIDTH)

        nf = norm_final if li == len(layers) - 1 else None
        x2 = _ffn_call(x2, sh3, sc3, gt3, lw["norm_ffn2"], lw["wgu2"], lw["wd2"], nf,
                       mixer=(gt2, attn, gdn, lw["wo_a"], lw["wo_g"]), **tok)

        states.append((k.reshape(b, t, A_KV_HEADS, A_HEAD_DIM), v.reshape(b, t, A_KV_HEADS, A_HEAD_DIM),
                       misc.reshape(b, t, LANES)[..., :IDX_DIM], conv_new, s_new))
    stacked = [jnp.stack(s, axis=0) for s in zip(*states)]
    return x2.reshape(b, t, d), stacked


def kernel(x_prompt, x_sample, cache_k, cache_v, cache_idx_k, state_conv, state_delta, c_prompt, c_sample,
           w_mod, b_mod, norm_ffn1, norm_mix, norm_ffn2, ffn1_w_gate, ffn1_w_up, ffn1_w_down,
           ffn2_w_gate, ffn2_w_up, ffn2_w_down, w_in, w_out, rel_bias, conv_w, a_log, dt_bias, gnorm_w,
           norm_final):
    depth = w_mod.shape[0]
    bp = c_prompt.shape[0]
    c_all = jnp.concatenate([c_prompt, c_sample], axis=0)
    layers, mods_p, mods_s = [], [], []
    for l in range(depth):
        mod = _mod_call(c_all, w_mod[l], b_mod[l]).reshape(c_all.shape[0], N_MOD, D_MODEL)
        mods_p.append(mod[:bp])
        mods_s.append(mod[bp:])
        wgu1, wd1 = _pack_ffn(ffn1_w_gate[l], ffn1_w_up[l], ffn1_w_down[l])
        wgu2, wd2 = _pack_ffn(ffn2_w_gate[l], ffn2_w_up[l], ffn2_w_down[l])
        wo = w_out[l].astype(BF16)
        layers.append(dict(
            norm_ffn1=norm_ffn1[l].reshape(1, -1), norm_mix=norm_mix[l].reshape(1, -1),
            norm_ffn2=norm_ffn2[l].reshape(1, -1), wgu1=wgu1, wd1=wd1, wgu2=wgu2, wd2=wd2,
            w_in=_pack_w_in(w_in[l]), wo_a=wo[:A_WIDTH], wo_g=wo[A_WIDTH:],
            conv_w=conv_w[l], gnorm_w=gnorm_w[l].reshape(1, -1),
            a_lane=_lane_vec(a_log[l], MISC_GA), d_lane=_lane_vec(dt_bias[l], MISC_GA)))
    nf = norm_final.reshape(1, -1)
    y_p, (k_p, v_p, ik_p, conv_p, delta_p) = _run(x_prompt, mods_p, None, layers, rel_bias, nf)
    past = (cache_k, cache_v, cache_idx_k, state_conv, state_delta)
    y_s, (k_s, v_s, ik_s, conv_s, delta_s) = _run(x_sample, mods_s, past, layers, rel_bias, nf)
    return (y_p, y_s, k_p, v_p, ik_p, conv_p, delta_p, k_s, v_s, ik_s, conv_s, delta_s)
```

```python
import functools

import jax
import jax.numpy as jnp
import numpy as np
from jax import lax
from jax.experimental import pallas as pl
from jax.experimental.pallas import tpu as pltpu

F32 = jnp.float32
BF16 = jnp.bfloat16

D_MODEL = 1024
CHUNK = 64
A_HEAD_DIM = 64
A_HEADS = 8
A_KV_HEADS = 4
A_WIDTH = A_HEADS * A_HEAD_DIM
KV_WIDTH = A_KV_HEADS * A_HEAD_DIM
IDX_HEADS = 8
IDX_DIM = 64
TOPK_MAX = 256
REL_BUCKETS = 32
G_KEY_DIM = 128
G_VAL_DIM = 128
G_HEADS = 4
G_WIDTH = G_HEADS * G_VAL_DIM
CONV_W = 4
CONV_CH = 2 * G_HEADS * G_KEY_DIM + G_HEADS * G_VAL_DIM
D_FF = 2816
N_MOD = 9
EPS = 1e-6

LANES = 128
SUBLANES = 8
BF16_ROWS = 16
MXU_DIM = 256
VMEM_LIMIT_BYTES = 56 * 1024 * 1024

FF_CHUNK = MXU_DIM
N_FF_CHUNKS = D_FF // FF_CHUNK
TOKEN_TILE = 512
ATT_TQ = 256
ATT_KB = 256
GDN_BLOCK = 256
XP_PAD = SUBLANES

QX_W = A_HEADS * LANES
Q_PAIRS = ((0, 2), (1, 3), (4, 6), (5, 7))
OFF_QX = 0
OFF_K = OFF_QX + A_WIDTH
OFF_V = OFF_K + A_KV_HEADS * A_HEAD_DIM
OFF_IQ = OFF_V + A_KV_HEADS * A_HEAD_DIM
OFF_IKX = OFF_IQ + IDX_HEADS * IDX_DIM
OFF_MISC = OFF_IKX + LANES
OFF_GQKV = OFF_MISC + LANES
OFF_GZ = OFF_GQKV + CONV_CH
IN_PACKED = OFF_GZ + G_WIDTH
MISC_IW = IDX_DIM
MISC_GB = MISC_IW + IDX_HEADS
MISC_GA = MISC_GB + G_HEADS

NEG_BIG = -1e30
INT_MIN = -2 ** 31
LOG2E = 1.4426950408889634
V_ROWS = LANES + 16


def _cparams(n_axes):
    return pltpu.CompilerParams(dimension_semantics=("arbitrary",) * n_axes,
                                vmem_limit_bytes=VMEM_LIMIT_BYTES)


def _resident(shape):
    nd = len(shape)
    return pl.BlockSpec(shape, lambda *_: (0,) * nd, pipeline_mode=pl.Buffered(1))


def _dot_nt(a, b):
    return lax.dot_general(a, b, (((1,), (1,)), ((), ())), preferred_element_type=F32)


def _rms_mod(x, gain, shift, scale):
    ms = jnp.mean(x * x, axis=-1, keepdims=True)
    y = x * lax.rsqrt(ms + EPS) * gain
    return y * (1.0 + scale) + shift


def _silu(x):
    return x * jax.nn.sigmoid(x)


def _tree_sum(parts):
    while len(parts) > 1:
        parts = [a + b for a, b in zip(parts[0::2], parts[1::2])] + ([parts[-1]] if len(parts) % 2 else [])
    return parts[0]


def _mod_kernel(c_ref, w_ref, b_ref, o_ref):
    s = _silu(c_ref[...]).astype(BF16)
    o_ref[...] = jnp.dot(s, w_ref[...].astype(BF16), preferred_element_type=F32) + b_ref[...]


def _mod_call(c, w_mod, b_mod):
    rows, d = c.shape
    n = w_mod.shape[1]
    tn = D_MODEL
    return pl.pallas_call(
        _mod_kernel,
        out_shape=jax.ShapeDtypeStruct((rows, n), F32),
        grid=(n // tn,),
        in_specs=[pl.BlockSpec((rows, d), lambda j: (0, 0)),
                  pl.BlockSpec((d, tn), lambda j: (0, j)),
                  pl.BlockSpec((1, tn), lambda j: (0, j))],
        out_specs=pl.BlockSpec((rows, tn), lambda j: (0, j)),
        compiler_params=_cparams(1),
        name="mod",
    )(c, w_mod, b_mod.reshape(1, n))


def _mod_specs(per_token, tm, tiles_per_batch):
    if per_token:
        return pl.BlockSpec((tm, D_MODEL), lambda i: (i, 0))
    return pl.BlockSpec((None, 1, D_MODEL), lambda i: (i // tiles_per_batch, 0, 0))


def _ffn_kernel(x_ref, sh_ref, sc_ref, gt_ref, gain_ref, wgu_ref, wd_ref, *rest, final_norm, mixer):
    rest = list(rest)
    if mixer:
        gm_ref, a_ref, g_ref, wa_ref, wg_ref = rest[:5]
        rest = rest[5:]
    if final_norm:
        nf_ref = rest.pop(0)
    o_ref, acc_ref = rest
    x = x_ref[...]
    if mixer:
        x = x + gm_ref[...] * (jnp.dot(a_ref[...], wa_ref[...], preferred_element_type=F32)
                               + jnp.dot(g_ref[...], wg_ref[...], preferred_element_type=F32))
    h = _rms_mod(x, gain_ref[...], sh_ref[...], sc_ref[...]).astype(BF16)
    for j in range(N_FF_CHUNKS):
        ab = jnp.dot(h, wgu_ref[j], preferred_element_type=F32)
        g = (_silu(ab[:, :FF_CHUNK]) * ab[:, FF_CHUNK:]).astype(BF16)
        d = jnp.dot(g, wd_ref[j], preferred_element_type=F32)
        if j == 0:
            acc_ref[...] = d
        else:
            acc_ref[...] += d
    y = x + 0.5 * gt_ref[...] * acc_ref[...]
    if final_norm:
        ms = jnp.mean(y * y, axis=-1, keepdims=True)
        y = y * lax.rsqrt(ms + EPS) * nf_ref[...]
    o_ref[...] = y


def _ffn_call(x2, sh, sc, gt, gain, wgu, wd, norm_final, mixer=None, *, tm, tiles_per_batch, per_token):
    n = x2.shape[0]
    mspec = _mod_specs(per_token, tm, tiles_per_batch)
    in_specs = [pl.BlockSpec((tm, D_MODEL), lambda i: (i, 0)), mspec, mspec, mspec,
                _resident((1, D_MODEL)), _resident(wgu.shape), _resident(wd.shape)]
    args = [x2, sh, sc, gt, gain, wgu, wd]
    if mixer is not None:
        gm, attn, gdn, wa, wg = mixer
        in_specs += [mspec, pl.BlockSpec((tm, A_WIDTH), lambda i: (i, 0)),
                     pl.BlockSpec((tm, G_WIDTH), lambda i: (i, 0)), _resident(wa.shape), _resident(wg.shape)]
        args += [gm, attn, gdn, wa, wg]
    final_norm = norm_final is not None
    if final_norm:
        in_specs.append(_resident((1, D_MODEL)))
        args.append(norm_final)
    return pl.pallas_call(
        functools.partial(_ffn_kernel, final_norm=final_norm, mixer=mixer is not None),
        out_shape=jax.ShapeDtypeStruct((n, D_MODEL), F32),
        grid=(n // tm,),
        in_specs=in_specs,
        out_specs=pl.BlockSpec((tm, D_MODEL), lambda i: (i, 0)),
        scratch_shapes=[pltpu.VMEM((tm, D_MODEL), F32)],
        compiler_params=_cparams(1),
        name="ffn_final" if final_norm else "ffn",
    )(*args)


def _inproj_kernel(x_ref, sh_ref, sc_ref, gain_ref, w_ref,
                   qx_o, k_o, kb_o, v_o, vb_o, iq_o, ikx_o, misc_o, gqkv_o, gz_o):
    h = _rms_mod(x_ref[...], gain_ref[...], sh_ref[...], sc_ref[...]).astype(BF16)

    def mm(off, width):
        return jnp.dot(h, w_ref[:, off:off + width], preferred_element_type=F32)

    qx_o[...] = mm(OFF_QX, A_WIDTH).astype(BF16)
    k = mm(OFF_K, OFF_V - OFF_K)
    k_o[...] = k
    kb_o[...] = k.astype(BF16)
    v = mm(OFF_V, OFF_IQ - OFF_V)
    v_o[...] = v
    vb_o[...] = v.astype(BF16)
    iq_o[...] = mm(OFF_IQ, OFF_IKX - OFF_IQ).astype(BF16)
    ikx_o[...] = mm(OFF_IKX, OFF_MISC - OFF_IKX).astype(BF16)
    misc_o[...] = mm(OFF_MISC, LANES)
    gqkv_o[...] = mm(OFF_GQKV, CONV_CH)
    gz_o[...] = mm(OFF_GZ, G_WIDTH)


def _inproj_call(x2, sh, sc, gain, w_packed, *, tm, tiles_per_batch, per_token):
    n = x2.shape[0]
    mspec = _mod_specs(per_token, tm, tiles_per_batch)
    widths = [(A_WIDTH, BF16), (KV_WIDTH, F32), (KV_WIDTH, BF16), (KV_WIDTH, F32), (KV_WIDTH, BF16),
              (IDX_HEADS * IDX_DIM, BF16), (LANES, BF16), (LANES, F32), (CONV_CH, F32), (G_WIDTH, F32)]
    return pl.pallas_call(
        _inproj_kernel,
        out_shape=[jax.ShapeDtypeStruct((n, w), dt) for w, dt in widths],
        grid=(n // tm,),
        in_specs=[pl.BlockSpec((tm, D_MODEL), lambda i: (i, 0)), mspec, mspec,
                  _resident((1, D_MODEL)), _resident(w_packed.shape)],
        out_specs=[pl.BlockSpec((tm, w), lambda i: (i, 0)) for w, _ in widths],
        compiler_params=_cparams(1),
        name="inproj",
    )(x2, sh, sc, gain, w_packed)


def _rel_bucket_int(rel):
    n = jnp.abs(rel)
    large = jnp.full(rel.shape, 8, jnp.int32)
    for th in (12, 16, 23, 32, 46, 64, 91):
        large = large + jnp.where(n >= th, 1, 0)
    return jnp.where(rel > 0, REL_BUCKETS // 2, 0) + jnp.where(n < 8, n, large)


def _key_to_float(t):
    bits = jnp.where(t >= 0, t, t ^ jnp.int32(0x7FFFFFFF))
    return lax.bitcast_convert_type(bits, F32)


def _top16(x):
    bits = lax.bitcast_convert_type(x, jnp.int32) & jnp.int32(-65536)
    return lax.bitcast_convert_type(bits, F32).astype(BF16)


def _bias_kernel(relb_ref, o_ref, *, tq):
    h = pl.program_id(1)
    key_off = lax.broadcasted_iota(jnp.int32, (ATT_KB, tq), 0)
    qry_off = lax.broadcasted_iota(jnp.int32, (ATT_KB, tq), 1)
    bucket = _rel_bucket_int(key_off - qry_off - pl.program_id(0) * ATT_KB)
    t = jnp.zeros((ATT_KB, tq), F32)
    for b in range(REL_BUCKETS):
        t = jnp.where(bucket == b, relb_ref[b, h], t)
    o_ref[0, 0] = (t * LOG2E).astype(BF16)


def _bias_call(rel_bias, tq):
    return pl.pallas_call(
        functools.partial(_bias_kernel, tq=tq),
        out_shape=jax.ShapeDtypeStruct((3, A_HEADS, ATT_KB, tq), BF16),
        grid=(3, A_HEADS),
        in_specs=[pl.BlockSpec(memory_space=pltpu.SMEM)],
        out_specs=pl.BlockSpec((1, 1, ATT_KB, tq), lambda d, h: (d, h, 0, 0)),
        compiler_params=_cparams(2),
        name="rel_bias_tiles",
    )(rel_bias)


def _attn_kernel(q_ref, iq_ref, misc_ref, k_ref, vt_ref, ikx_ref, bias_ref, o_ref,
                 s_ref, shi_ref, sel_ref, lg_ref, cm_ref, acc_ref, cnt_ref, cnth_ref, qx_ref,
                 *, tq, l_true, offset, n_top):
    kb_sz = ATT_KB
    low_half_q = lax.broadcasted_iota(jnp.int32, (tq, LANES), 1) < A_HEAD_DIM
    low_half_k = lax.broadcasted_iota(jnp.int32, (kb_sz, LANES), 1) < A_HEAD_DIM

    for s, pair in enumerate(Q_PAIRS):
        slab = q_ref[0, :, s * LANES:(s + 1) * LANES]
        zero = jnp.zeros_like(slab)
        qx_ref[:, pair[0] * LANES:(pair[0] + 1) * LANES] = jnp.where(low_half_q, slab, zero)
        qx_ref[:, pair[1] * LANES:(pair[1] + 1) * LANES] = jnp.where(low_half_q, zero, slab)
    q0 = offset + pl.program_id(1) * tq
    kmax = jnp.minimum(q0 + tq, l_true)
    nkb = (kmax + kb_sz - 1) // kb_sz

    key_off = lax.broadcasted_iota(jnp.int32, (kb_sz, tq), 0)

    qpos = q0 + lax.broadcasted_iota(jnp.int32, (1, tq), 1)
    limit = jnp.minimum((qpos // CHUNK + 1) * CHUNK, l_true)

    iw_t = misc_ref[0].T[MISC_IW:MISC_IW + IDX_HEADS, :] * (IDX_HEADS ** -0.5)

    def score_body(kb, carry):
        base = pl.multiple_of(kb * kb_sz, kb_sz)
        ik2 = ikx_ref[0, pl.ds(base, kb_sz), :]
        ik_lo = jnp.where(low_half_k, ik2, jnp.zeros_like(ik2))
        ik_hi = jnp.where(low_half_k, jnp.zeros_like(ik2), ik2)
        s = jnp.zeros((kb_sz, tq), F32)
        for j in range(IDX_HEADS // 2):
            slab = iq_ref[0, :, j * LANES:(j + 1) * LANES]
            d0 = _dot_nt(ik_lo, slab)
            d1 = _dot_nt(ik_hi, slab)
            s = s + iw_t[2 * j:2 * j + 1, :] * jnp.maximum(d0, 0.0)
            s = s + iw_t[2 * j + 1:2 * j + 2, :] * jnp.maximum(d1, 0.0)
        s = jnp.where(base + key_off < limit, s, -jnp.inf)
        s_ref[pl.ds(base, kb_sz), :] = s
        shi_ref[pl.ds(base, kb_sz), :] = _top16(s)
        return carry

    def for_each_block(body):
        def oct_(i, carry):
            for r in range(8):
                body(8 * i + r, carry)
            return carry
        lax.fori_loop(0, nkb // 8, oct_, 0)

        @pl.when(nkb % 8 >= 4)
        def _():
            for r in range(4):
                body((nkb // 8) * 8 + r, 0)
        done = (nkb // 4) * 4

        @pl.when(nkb % 4 >= 2)
        def _():
            body(done, 0)
            body(done + 1, 0)

        @pl.when(nkb % 2 == 1)
        def _():
            body(nkb - 1, 0)

    for_each_block(score_body)

    def sum_blocks(contrib, acc_ref, n=nkb):
        acc_ref[...] = jnp.zeros(acc_ref.shape, acc_ref.dtype)

        def quad(i, carry):
            acc_ref[...] += _tree_sum([contrib(4 * i + r) for r in range(4)])
            return carry
        lax.fori_loop(0, n // 4, quad, 0)
        done = (n // 4) * 4

        @pl.when(n % 4 >= 2)
        def _():
            acc_ref[...] += contrib(done) + contrib(done + 1)

        @pl.when(n % 2 == 1)
        def _():
            acc_ref[...] += contrib(n - 1)
        return acc_ref[...]

    def count(pred, n=nkb):
        def contrib(kb):
            base = pl.multiple_of(kb * kb_sz, kb_sz)
            hit = pred(s_ref[pl.ds(base, kb_sz), :], base + key_off)
            return _tree_sum([hit[r:r + SUBLANES] for r in range(0, kb_sz, SUBLANES)])
        return jnp.sum(sum_blocks(contrib, cnt_ref, n), axis=0, keepdims=True)

    def count_hi(thr_hi):
        one = jnp.ones((), BF16)
        zero = jnp.zeros((), BF16)

        def contrib(kb):
            base = pl.multiple_of(kb * kb_sz, kb_sz)
            hit = jnp.where(shi_ref[pl.ds(base, kb_sz), :] >= thr_hi, one, zero)
            return _tree_sum([hit[r:r + BF16_ROWS] for r in range(0, kb_sz, BF16_ROWS)])
        return jnp.sum(sum_blocks(contrib, cnth_ref).astype(F32), axis=0, keepdims=True)

    def hi_body(i, carry):
        t, c_t = carry
        cand = t + lax.shift_left(jnp.int32(1), 31 - i)
        c = count_hi(_top16(_key_to_float(cand)))
        return jnp.where(c >= n_top, cand, t), jnp.where(c >= n_top, c, c_t)

    def lo_body(i, carry):
        t, c_t = carry
        cand = t + lax.shift_left(jnp.int32(1), 15 - i)
        thr_c = _key_to_float(cand)
        c = count(lambda blk, _: jnp.where(blk >= thr_c, 1.0, 0.0))
        return jnp.where(c >= n_top, cand, t), jnp.where(c >= n_top, c, c_t)

    carry = (jnp.full((1, tq), INT_MIN, jnp.int32), jnp.full((1, tq), float(n_top), F32))
    carry = lax.fori_loop(0, 16, hi_body, carry)
    t_key, n_ge = lax.fori_loop(0, 16, lo_body, carry)
    thr = _key_to_float(t_key)

    take_all = limit <= n_top
    excess = jnp.where(take_all, 0.0, jnp.where(n_ge > n_top, 1.0, 0.0))
    any_excess = jnp.max(excess) > 0.0
    n_tie = jnp.where(any_excess, nkb, 0)
    need = n_top - count(lambda blk, _: jnp.where(blk > thr, 1.0, 0.0), n_tie)
    idx_bits = 14

    def tie_body(i, c):
        cand = c + lax.shift_left(jnp.int32(1), idx_bits - 1 - i)
        f = count(lambda blk, kidx: jnp.where(blk == thr, jnp.where(kidx < cand, 1.0, 0.0), 0.0))
        return jnp.where(f <= need, cand, c)

    cut0 = jnp.where(any_excess, jnp.zeros((1, tq), jnp.int32),
                     jnp.full((1, tq), 2 ** idx_bits, jnp.int32))
    cut = lax.fori_loop(0, jnp.where(any_excess, idx_bits, 0), tie_body, cut0)
    thr = jnp.where(take_all, -jnp.inf, thr)
    cut = jnp.where(take_all, limit, cut)

    def mask_body(kb, carry):
        base = pl.multiple_of(kb * kb_sz, kb_sz)
        blk = s_ref[pl.ds(base, kb_sz), :]
        tie = jnp.where(base + key_off < cut, 0.0, NEG_BIG)
        sel = jnp.where(blk > thr, 0.0, jnp.where(blk == thr, tie, NEG_BIG))
        sel_ref[pl.ds(base, kb_sz), :] = sel.astype(BF16)
        return carry

    lax.fori_loop(0, nkb, mask_body, 0)

    cm_ref[...] = jnp.full(cm_ref.shape, NEG_BIG, BF16)

    def logit_body(kb, carry):
        base = pl.multiple_of(kb * kb_sz, kb_sz)
        dsel = jnp.clip((q0 - base) // kb_sz, 0, 2)
        kblk = k_ref[0, pl.ds(base, kb_sz), :]
        sel = sel_ref[pl.ds(base, kb_sz), :]
        for h in range(A_HEADS):
            sl = h // 4
            lg = _dot_nt(kblk[:, sl * LANES:(sl + 1) * LANES], qx_ref[:, h * LANES:(h + 1) * LANES])
            lg = lg.astype(BF16) + sel + bias_ref[dsel, h]
            lg_ref[h, pl.ds(base, kb_sz), :] = lg
            cm_ref[h] = jnp.maximum(cm_ref[h], lg)
        return carry

    for_each_block(logit_body)

    acc_ref[...] = jnp.zeros(acc_ref.shape, F32)
    m_rows = [jnp.max(cm_ref[h].astype(F32), axis=0, keepdims=True).astype(BF16) for h in range(A_HEADS)]

    def pv_body(kb, carry):
        base = pl.multiple_of(kb * kb_sz, kb_sz)
        for h in range(A_HEADS):
            p = jnp.exp2(lg_ref[h, pl.ds(base, kb_sz), :] - m_rows[h])
            acc_ref[h] += jnp.dot(vt_ref[0, kb, h // 4], p, preferred_element_type=F32)
        return carry

    for_each_block(pv_body)

    outs = []
    for h in range(A_HEADS):
        pos = (h // 2) % 2
        outs.append(acc_ref[h, pos * A_HEAD_DIM:(pos + 1) * A_HEAD_DIM, :] / acc_ref[h, LANES:LANES + 1, :])
    o_ref[0] = jnp.concatenate(outs, axis=0).T.astype(BF16)


def _attn_call(rel_bias, qx, iq, misc, kb, vt, ikx, *, tq, l_true, offset, n_top):
    b, tq_total, _ = qx.shape
    lp = kb.shape[1]
    nq = tq_total // tq
    assert offset % ATT_KB == 0 and tq % CHUNK == 0 and ATT_KB % tq == 0 and lp % ATT_KB == 0
    assert lp < 2 ** 14 and l_true <= lp
    kern = functools.partial(_attn_kernel, tq=tq, l_true=l_true, offset=offset, n_top=n_top)
    bias = _bias_call(rel_bias, tq)
    return pl.pallas_call(
        kern,
        out_shape=jax.ShapeDtypeStruct((b, tq_total, A_WIDTH), BF16),
        grid=(b, nq),
        in_specs=[pl.BlockSpec((1, tq, A_WIDTH), lambda i, j: (i, j, 0)),
                  pl.BlockSpec((1, tq, IDX_HEADS * IDX_DIM), lambda i, j: (i, j, 0)),
                  pl.BlockSpec((1, tq, LANES), lambda i, j: (i, j, 0)),
                  pl.BlockSpec((1, lp, KV_WIDTH), lambda i, j: (i, 0, 0)),
                  pl.BlockSpec((1, lp // ATT_KB, 2, V_ROWS, ATT_KB), lambda i, j: (i, 0, 0, 0, 0)),
                  pl.BlockSpec((1, lp, LANES), lambda i, j: (i, 0, 0)),
                  _resident(bias.shape)],
        out_specs=pl.BlockSpec((1, tq, A_WIDTH), lambda i, j: (i, j, 0)),
        scratch_shapes=[pltpu.VMEM((lp, tq), F32),
                        pltpu.VMEM((lp, tq), BF16),
                        pltpu.VMEM((lp, tq), BF16),
                        pltpu.VMEM((A_HEADS, lp, tq), BF16),
                        pltpu.VMEM((A_HEADS, ATT_KB, tq), BF16),
                        pltpu.VMEM((A_HEADS, V_ROWS, tq), F32),
                        pltpu.VMEM((SUBLANES, tq), F32),
                        pltpu.VMEM((BF16_ROWS, tq), BF16),
                        pltpu.VMEM((tq, QX_W), BF16)],
        compiler_params=_cparams(2),
        name="sparse_attn",
    )(qx, iq, misc, kb, vt, ikx, bias)


def _split_bf16(a, n):
    parts = []
    r = a
    for i in range(n):
        p = r.astype(BF16)
        parts.append(p)
        if i + 1 < n:
            r = r - p.astype(F32)
    return parts


def _bdot(a, b, dims=(((1,), (0,)), ((), ()))):
    return lax.dot_general(a.astype(BF16), b.astype(BF16), dims, preferred_element_type=F32)


_NT = (((1,), (1,)), ((), ()))
_TN = (((0,), (0,)), ((), ()))


def _gdn_kernel(x_ref, misc_ref, gz_ref, convw_ref, hist_ref, s0_ref, gnw_ref, alane_ref, dlane_ref,
                o_ref, conv_o, s_o, xp_ref, y_ref, st_ref, vn_ref,
                u_ref, w_ref, qk_ref, qe_ref, kd_ref, el_ref, gate_ref, *, cb, t_true, nblk):
    c = CHUNK
    step = pl.program_id(1)
    blk = jnp.minimum(step, nblk - 1)
    carried = (u_ref, w_ref, qk_ref, qe_ref, kd_ref, el_ref, gate_ref)
    u_nx, w_nx, qk_nx, qe_nx, kd_nx, el_nx, gate_nx = (r.at[1] for r in carried)
    u_ref, w_ref, qk_ref, qe_ref, kd_ref, el_ref, gate_ref = (r.at[0] for r in carried)

    @pl.when(step == 0)
    def _():
        xp_ref[XP_PAD - (CONV_W - 1):XP_PAD, :] = hist_ref[0]
        st_ref[...] = s0_ref[0]
        for r in carried:
            r[...] = jnp.zeros(r.shape, r.dtype)

    heads = range(G_HEADS)
    for r in carried:
        r[0] = r[1]

    def scan_stages():
        vn_ref[...] = jnp.zeros(vn_ref.shape, F32)
        for ck in range(cb // c):
            r0, r1 = ck * c, (ck + 1) * c
            s_prev = [st_ref[h] for h in heads]
            v_new = [u_ref[h, r0:r1, :] - _bdot(w_ref[h, r0:r1, :], s_prev[h]) for h in heads]
            o_state = [_bdot(qe_ref[h, r0:r1, :], s_prev[h]) for h in heads]
            yield
            for h in heads:
                vn_ref[h, r0:r1, :] = v_new[h]
            o = [o_state[h] + _bdot(qk_ref[h, r0:r1, :], vn_ref[h]) for h in heads]
            for h in heads:
                s_new = s_prev[h] * el_ref[h, ck, 0:1, :] + _bdot(kd_ref[h, r0:r1, :], v_new[h], _TN)
                st_ref[h] = jnp.where(step > 0, s_new, s_prev[h])
            yield
            for h in heads:
                on = o[h] * lax.rsqrt(jnp.mean(o[h] * o[h], axis=-1, keepdims=True) + EPS) * gnw_ref[...]
                gate = gate_ref[r0:r1, h * G_VAL_DIM:(h + 1) * G_VAL_DIM]
                o_ref[0, r0:r1, h * G_VAL_DIM:(h + 1) * G_VAL_DIM] = (on * gate).astype(BF16)
            yield
        s_o[0] = st_ref[...]

    scan = scan_stages()

    def tick():
        next(scan, None)

    tick()

    xp_ref[XP_PAD:XP_PAD + cb, :] = x_ref[0]
    y = convw_ref[CONV_W - 1:CONV_W, :] * xp_ref[XP_PAD:XP_PAD + cb, :]
    for j in range(CONV_W - 1):
        lag_rows = XP_PAD - (CONV_W - 1) + j
        y = y + convw_ref[j:j + 1, :] * xp_ref[lag_rows:lag_rows + cb, :]
    y_ref[...] = _silu(y)

    last_row = XP_PAD + (t_true - 1) % cb
    conv_o[0] = xp_ref[last_row - (CONV_W - 2):last_row + 1, :]

    xp_ref[XP_PAD - (CONV_W - 1):XP_PAD, :] = xp_ref[XP_PAD + cb - (CONV_W - 1):XP_PAD + cb, :]

    ri = lax.broadcasted_iota(jnp.int32, (cb, cb), 0)
    ci = lax.broadcasted_iota(jnp.int32, (cb, cb), 1)
    lag = jnp.where((ri // c) == (ci // c), ri - ci, -1)
    tri = lag >= 0
    strict = lag > 0
    eye = ri == ci
    tril_bf = jnp.where(tri, 1.0, 0.0).astype(BF16)
    eye_f = jnp.where(eye, 1.0, 0.0)

    ms = misc_ref[0]
    tok = blk * cb + lax.broadcasted_iota(jnp.int32, (cb, 1), 0)
    live = tok < t_true
    beta_s = jnp.where(live, jax.nn.sigmoid(ms), 0.0)
    z = ms + dlane_ref[...]
    softplus = jnp.maximum(z, 0.0) + jnp.log(1.0 + jnp.exp(-jnp.abs(z)))
    g_s = jnp.where(live, -jnp.exp(alane_ref[...]) * softplus, 0.0)
    gc_s = None
    for piece in _split_bf16(g_s, 3):
        t = jnp.dot(tril_bf, piece, preferred_element_type=F32)
        gc_s = t if gc_s is None else gc_s + t

    tick()

    q, k, v, beta, gc, decay, kb, eg = ([None] * G_HEADS for _ in range(8))
    for h in heads:
        qh = y_ref[:, h * G_KEY_DIM:(h + 1) * G_KEY_DIM]
        kh = y_ref[:, G_WIDTH + h * G_KEY_DIM:G_WIDTH + (h + 1) * G_KEY_DIM]
        v[h] = y_ref[:, 2 * G_WIDTH + h * G_VAL_DIM:2 * G_WIDTH + (h + 1) * G_VAL_DIM]
        q[h] = qh * lax.rsqrt(jnp.sum(qh * qh, axis=-1, keepdims=True) + 1e-6) * (G_KEY_DIM ** -0.5)
        k[h] = kh * lax.rsqrt(jnp.sum(kh * kh, axis=-1, keepdims=True) + 1e-6)
        beta[h] = beta_s[:, MISC_GB + h:MISC_GB + h + 1]
        gc[h] = gc_s[:, MISC_GA + h:MISC_GA + h + 1]
        gc_b = jnp.broadcast_to(gc[h], (cb, cb))
        gc_row = jnp.sum(jnp.where(eye, gc_b, 0.0), axis=0, keepdims=True)
        decay[h] = jnp.exp(jnp.where(tri, gc_b - gc_row, NEG_BIG))
        kb[h] = k[h] * beta[h]
        eg[h] = jnp.exp(gc[h])
    tick()
    m = [jnp.where(strict, _bdot(kb[h], k[h], _NT) * decay[h], 0.0) for h in heads]
    rhs = [jnp.concatenate([v[h] * beta[h], kb[h] * eg[h]], axis=1) for h in heads]
    qk = [jnp.where(tri, _bdot(q[h], k[h], _NT) * decay[h], 0.0) for h in heads]
    tick()
    pw = [-m[h] for h in heads]
    inv = [eye_f + pw[h] for h in heads]
    for _ in range(5):
        pw = [_bdot(pw[h], pw[h]) for h in heads]
        tick()
        inv = [inv[h] + _bdot(inv[h], pw[h]) for h in heads]
    tick()
    sol = [_bdot(inv[h], rhs[h]) for h in heads]
    tick()
    m_sol = []
    for h in heads:
        m_hi, m_lo = _split_bf16(m[h], 2)
        s_hi, s_lo = _split_bf16(sol[h], 2)
        m_sol.append(_bdot(m_hi, s_hi) + _bdot(m_hi, s_lo) + _bdot(m_lo, s_hi))
    tick()
    sol = [sol[h] + _bdot(inv[h], rhs[h] - sol[h] - m_sol[h]) for h in heads]
    for _ in scan:
        pass
    gate_nx[...] = _silu(gz_ref[0])
    for h in heads:
        u_nx[h] = sol[h][:, :G_VAL_DIM]
        w_nx[h] = sol[h][:, G_VAL_DIM:].astype(BF16)
        qk_nx[h] = qk[h].astype(BF16)
        qe_nx[h] = (q[h] * eg[h]).astype(BF16)
        for ck in range(cb // c):
            r0, r1 = ck * c, (ck + 1) * c
            g_last = gc[h][r1 - 1:r1, :]
            kd_nx[h, r0:r1, :] = (k[h][r0:r1] * jnp.exp(g_last - gc[h][r0:r1])).astype(BF16)
            el_nx[h, ck] = jnp.broadcast_to(jnp.exp(g_last), (SUBLANES, LANES))


def _gdn_call(gqkv, misc, gz, conv_w, hist, s0, gnorm_w, a_lane, d_lane, *, cb, t_true):
    b, t_pad, _ = gqkv.shape
    nblk = t_pad // cb
    assert (nblk - 1) * cb < t_true <= t_pad and (t_true - 1) % cb >= CONV_W - 2
    kern = functools.partial(_gdn_kernel, cb=cb, t_true=t_true, nblk=nblk)
    prep = lambda i, j: (i, jnp.minimum(j, nblk - 1), 0)
    scan = lambda i, j: (i, jnp.maximum(j - 1, 0), 0)
    nch = cb // CHUNK
    return pl.pallas_call(
        kern,
        out_shape=[jax.ShapeDtypeStruct((b, t_pad, G_WIDTH), BF16),
                   jax.ShapeDtypeStruct((b, CONV_W - 1, CONV_CH), F32),
                   jax.ShapeDtypeStruct((b, G_HEADS, G_KEY_DIM, G_VAL_DIM), F32)],
        grid=(b, nblk + 1),
        in_specs=[pl.BlockSpec((1, cb, CONV_CH), prep),
                  pl.BlockSpec((1, cb, LANES), prep),
                  pl.BlockSpec((1, cb, G_WIDTH), prep),
                  pl.BlockSpec((CONV_W, CONV_CH), lambda i, j: (0, 0)),
                  pl.BlockSpec((1, CONV_W - 1, CONV_CH), lambda i, j: (i, 0, 0)),
                  pl.BlockSpec((1, G_HEADS, G_KEY_DIM, G_VAL_DIM), lambda i, j: (i, 0, 0, 0)),
                  pl.BlockSpec((1, G_VAL_DIM), lambda i, j: (0, 0)),
                  pl.BlockSpec((1, LANES), lambda i, j: (0, 0)),
                  pl.BlockSpec((1, LANES), lambda i, j: (0, 0))],
        out_specs=[pl.BlockSpec((1, cb, G_WIDTH), scan),
                   pl.BlockSpec((1, CONV_W - 1, CONV_CH), lambda i, j: (i, 0, 0)),
                   pl.BlockSpec((1, G_HEADS, G_KEY_DIM, G_VAL_DIM), lambda i, j: (i, 0, 0, 0))],
        scratch_shapes=[pltpu.VMEM((cb + XP_PAD, CONV_CH), F32),
                        pltpu.VMEM((cb, CONV_CH), F32),
                        pltpu.VMEM((G_HEADS, G_KEY_DIM, G_VAL_DIM), F32),
                        pltpu.VMEM((G_HEADS, cb, G_VAL_DIM), F32),
                        pltpu.VMEM((2, G_HEADS, cb, G_VAL_DIM), F32),
                        pltpu.VMEM((2, G_HEADS, cb, G_KEY_DIM), BF16),
                        pltpu.VMEM((2, G_HEADS, cb, cb), BF16),
                        pltpu.VMEM((2, G_HEADS, cb, G_KEY_DIM), BF16),
                        pltpu.VMEM((2, G_HEADS, cb, G_KEY_DIM), BF16),
                        pltpu.VMEM((2, G_HEADS, nch, SUBLANES, LANES), F32),
                        pltpu.VMEM((2, cb, G_WIDTH), F32)],
        compiler_params=_cparams(2),
        name="gated_delta",
    )(gqkv, misc, gz, conv_w, hist, s0, gnorm_w, a_lane, d_lane)


def _pack_w_in(w_in):
    d = w_in.shape[0]
    splits = (512, 256, 256, 512, 64, 8, 512, 512, 512, 512, 4, 4)
    offs = np.concatenate([[0], np.cumsum(splits)])
    aq, ak, av, iq, ik, iw, gq, gk, gv, gz, gb, ga = [w_in[:, offs[i]:offs[i + 1]] for i in range(12)]
    qx = [aq[:, h * A_HEAD_DIM:(h + 1) * A_HEAD_DIM] * (A_HEAD_DIM ** -0.5 * LOG2E)
          for pair in Q_PAIRS for h in pair]
    ikx = [ik, ik]
    misc = [ik, iw, gb, ga, jnp.zeros((d, LANES - MISC_GA - G_HEADS), w_in.dtype)]
    cols = qx + [ak, av, iq * (IDX_DIM ** -0.5)] + ikx + misc + [gq, gk, gv, gz]
    packed = jnp.concatenate(cols, axis=1).astype(BF16)
    assert packed.shape[1] == IN_PACKED
    return packed


def _pack_ffn(w_gate, w_up, w_down):
    d = w_gate.shape[0]
    wg = w_gate.reshape(d, N_FF_CHUNKS, FF_CHUNK)
    wu = w_up.reshape(d, N_FF_CHUNKS, FF_CHUNK)
    wgu = jnp.transpose(jnp.concatenate([wg, wu], axis=2), (1, 0, 2)).astype(BF16)
    wd = w_down.reshape(N_FF_CHUNKS, FF_CHUNK, d).astype(BF16)
    return wgu, wd


def _ikx_layout(ik):
    return jnp.concatenate([ik, ik], axis=-1)


def _lane_vec(vals, lane0):
    return jnp.zeros((1, LANES), F32).at[0, lane0:lane0 + vals.shape[0]].set(vals.astype(F32))


def _run(x, mod, past, layers, rel_bias, norm_final):
    b, t, d = x.shape
    n = b * t
    per_token = t < TOKEN_TILE
    tm = n if per_token else TOKEN_TILE
    tiles_per_batch = None if per_token else t // tm
    tok = dict(tm=tm, tiles_per_batch=tiles_per_batch, per_token=per_token)

    x2 = x.reshape(n, d)
    states = []
    for li, lw in enumerate(layers):
        m = mod[li]

        def mvec(kidx):
            row = m[:, kidx]
            return jnp.repeat(row, t, axis=0) if per_token else row[:, None, :]

        sh1, sc1, gt1, sh2, sc2, gt2, sh3, sc3, gt3 = [mvec(i) for i in range(N_MOD)]
        x2 = _ffn_call(x2, sh1, sc1, gt1, lw["norm_ffn1"], lw["wgu1"], lw["wd1"], None, **tok)
        qx, k, kb, v, vb, iq, ikx, misc, gqkv, gz = _inproj_call(
            x2, sh2, sc2, lw["norm_mix"], lw["w_in"], **tok)

        if past is None:
            offset, l_true, tq = 0, t, ATT_TQ
            k_all, v_all, ikx_all = (a.reshape(b, t, -1) for a in (kb, vb, ikx))
            qx3, iq3, misc3 = (a.reshape(b, t, -1) for a in (qx, iq, misc))
            conv_hist = jnp.zeros((b, CONV_W - 1, CONV_CH), F32)
            s0 = jnp.zeros((b, G_HEADS, G_KEY_DIM, G_VAL_DIM), F32)
        else:
            k_hist, v_hist, ik_hist, conv_hist, s0 = (p[li] for p in past)
            offset = k_hist.shape[1]
            l_true = offset + t
            tq = LANES
            lp = -(-l_true // ATT_KB) * ATT_KB
            pad_k = lambda a: jnp.pad(a, ((0, 0), (0, lp - l_true), (0, 0)))
            k_all = pad_k(jnp.concatenate([k_hist.reshape(b, offset, -1).astype(BF16),
                                           kb.reshape(b, t, -1)], axis=1))
            v_all = pad_k(jnp.concatenate([v_hist.reshape(b, offset, -1).astype(BF16),
                                           vb.reshape(b, t, -1)], axis=1))
            ikx_all = pad_k(jnp.concatenate([_ikx_layout(ik_hist.astype(BF16)),
                                             ikx.reshape(b, t, -1)], axis=1))
            pad_q = lambda a: jnp.pad(a.reshape(b, t, -1), ((0, 0), (0, tq - t), (0, 0)))
            qx3, iq3, misc3 = pad_q(qx), pad_q(iq), pad_q(misc)
            s0 = s0.astype(F32)
            conv_hist = conv_hist.astype(F32)
        lp = k_all.shape[1]
        n_top = min(TOPK_MAX, l_true // 4)
        vt = jnp.transpose(v_all.reshape(b, lp // ATT_KB, ATT_KB, 2, LANES), (0, 1, 3, 4, 2))
        vt = jnp.concatenate([vt, jnp.ones((b, lp // ATT_KB, 2, V_ROWS - LANES, ATT_KB), BF16)], axis=3)
        attn = _attn_call(rel_bias, qx3, iq3, misc3, k_all, vt, ikx_all,
                          tq=tq, l_true=l_true, offset=offset, n_top=n_top)
        attn = attn[:, :t].reshape(n, A_WIDTH)

        cb = GDN_BLOCK if t % GDN_BLOCK == 0 else CHUNK
        t_pad = -(-t // cb) * cb
        pad_t = lambda a: jnp.pad(a.reshape(b, t, -1), ((0, 0), (0, t_pad - t), (0, 0)))
        gdn, conv_new, s_new = _gdn_call(pad_t(gqkv), pad_t(misc), pad_t(gz), lw["conv_w"], conv_hist, s0,
                                         lw["gnorm_w"], lw["a_lane"], lw["d_lane"], cb=cb, t_true=t)
        gdn = gdn[:, :t].reshape(n, G_WIDTH)

        nf = norm_final if li == len(layers) - 1 else None
        x2 = _ffn_call(x2, sh3, sc3, gt3, lw["norm_ffn2"], lw["wgu2"], lw["wd2"], nf,
                       mixer=(gt2, attn, gdn, lw["wo_a"], lw["wo_g"]), **tok)

        states.append((k.reshape(b, t, A_KV_HEADS, A_HEAD_DIM), v.reshape(b, t, A_KV_HEADS, A_HEAD_DIM),
                       misc.reshape(b, t, LANES)[..., :IDX_DIM], conv_new, s_new))
    stacked = [jnp.stack(s, axis=0) for s in zip(*states)]
    return x2.reshape(b, t, d), stacked


def kernel(x_prompt, x_sample, cache_k, cache_v, cache_idx_k, state_conv, state_delta, c_prompt, c_sample,
           w_mod, b_mod, norm_ffn1, norm_mix, norm_ffn2, ffn1_w_gate, ffn1_w_up, ffn1_w_down,
           ffn2_w_gate, ffn2_w_up, ffn2_w_down, w_in, w_out, rel_bias, conv_w, a_log, dt_bias, gnorm_w,
           norm_final):
    depth = w_mod.shape[0]
    bp = c_prompt.shape[0]
    c_all = jnp.concatenate([c_prompt, c_sample], axis=0)
    layers, mods_p, mods_s = [], [], []
    for l in range(depth):
        mod = _mod_call(c_all, w_mod[l], b_mod[l]).reshape(c_all.shape[0], N_MOD, D_MODEL)
        mods_p.append(mod[:bp])
        mods_s.append(mod[bp:])
        wgu1, wd1 = _pack_ffn(ffn1_w_gate[l], ffn1_w_up[l], ffn1_w_down[l])
        wgu2, wd2 = _pack_ffn(ffn2_w_gate[l], ffn2_w_up[l], ffn2_w_down[l])
        wo = w_out[l].astype(BF16)
        layers.append(dict(
            norm_ffn1=norm_ffn1[l].reshape(1, -1), norm_mix=norm_mix[l].reshape(1, -1),
            norm_ffn2=norm_ffn2[l].reshape(1, -1), wgu1=wgu1, wd1=wd1, wgu2=wgu2, wd2=wd2,
            w_in=_pack_w_in(w_in[l]), wo_a=wo[:A_WIDTH], wo_g=wo[A_WIDTH:],
            conv_w=conv_w[l], gnorm_w=gnorm_w[l].reshape(1, -1),
            a_lane=_lane_vec(a_log[l], MISC_GA), d_lane=_lane_vec(dt_bias[l], MISC_GA)))
    nf = norm_final.reshape(1, -1)
    y_p, (k_p, v_p, ik_p, conv_p, delta_p) = _run(x_prompt, mods_p, None, layers, rel_bias, nf)
    past = (cache_k, cache_v, cache_idx_k, state_conv, state_delta)
    y_s, (k_s, v_s, ik_s, conv_s, delta_s) = _run(x_sample, mods_s, past, layers, rel_bias, nf)
    return (y_p, y_s, k_p, v_p, ik_p, conv_p, delta_p, k_s, v_s, ik_s, conv_s, delta_s)
```

```python
import functools

import jax
import jax.numpy as jnp
import numpy as np
from jax import lax
from jax.experimental import pallas as pl
from jax.experimental.pallas import tpu as pltpu

F32 = jnp.float32
BF16 = jnp.bfloat16

D_MODEL = 1024
CHUNK = 64
A_HEAD_DIM = 64
A_HEADS = 8
A_KV_HEADS = 4
A_WIDTH = A_HEADS * A_HEAD_DIM
KV_WIDTH = A_KV_HEADS * A_HEAD_DIM
IDX_HEADS = 8
IDX_DIM = 64
TOPK_MAX = 256
REL_BUCKETS = 32
G_KEY_DIM = 128
G_VAL_DIM = 128
G_HEADS = 4
G_WIDTH = G_HEADS * G_VAL_DIM
CONV_W = 4
CONV_CH = 2 * G_HEADS * G_KEY_DIM + G_HEADS * G_VAL_DIM
D_FF = 2816
N_MOD = 9
EPS = 1e-6

LANES = 128
SUBLANES = 8
BF16_ROWS = 16
MXU_DIM = 256
VMEM_LIMIT_BYTES = 56 * 1024 * 1024

FF_CHUNK = MXU_DIM
N_FF_CHUNKS = D_FF // FF_CHUNK
TOKEN_TILE = 512
ATT_TQ = 256
ATT_KB = 256
GDN_BLOCK = 256
XP_PAD = SUBLANES

QX_W = A_HEADS * LANES
Q_PAIRS = ((0, 2), (1, 3), (4, 6), (5, 7))
OFF_QX = 0
OFF_K = OFF_QX + A_WIDTH
OFF_V = OFF_K + A_KV_HEADS * A_HEAD_DIM
OFF_IQ = OFF_V + A_KV_HEADS * A_HEAD_DIM
OFF_IKX = OFF_IQ + IDX_HEADS * IDX_DIM
OFF_MISC = OFF_IKX + LANES
OFF_GQKV = OFF_MISC + LANES
OFF_GZ = OFF_GQKV + CONV_CH
IN_PACKED = OFF_GZ + G_WIDTH
MISC_IW = IDX_DIM
MISC_GB = MISC_IW + IDX_HEADS
MISC_GA = MISC_GB + G_HEADS

NEG_BIG = -1e30
INT_MIN = -2 ** 31
LOG2E = 1.4426950408889634
V_ROWS = LANES + 16


def _cparams(n_axes):
    return pltpu.CompilerParams(dimension_semantics=("arbitrary",) * n_axes,
                                vmem_limit_bytes=VMEM_LIMIT_BYTES)


def _resident(shape):
    nd = len(shape)
    return pl.BlockSpec(shape, lambda *_: (0,) * nd, pipeline_mode=pl.Buffered(1))


def _dot_nt(a, b):
    return lax.dot_general(a, b, (((1,), (1,)), ((), ())), preferred_element_type=F32)


def _rms_mod(x, gain, shift, scale):
    ms = jnp.mean(x * x, axis=-1, keepdims=True)
    y = x * lax.rsqrt(ms + EPS) * gain
    return y * (1.0 + scale) + shift


def _silu(x):
    return x * jax.nn.sigmoid(x)


def _tree_sum(parts):
    while len(parts) > 1:
        parts = [a + b for a, b in zip(parts[0::2], parts[1::2])] + ([parts[-1]] if len(parts) % 2 else [])
    return parts[0]


def _mod_kernel(c_ref, w_ref, b_ref, o_ref):
    s = _silu(c_ref[...]).astype(BF16)
    o_ref[...] = jnp.dot(s, w_ref[...].astype(BF16), preferred_element_type=F32) + b_ref[...]


def _mod_call(c, w_mod, b_mod):
    rows, d = c.shape
    n = w_mod.shape[1]
    tn = D_MODEL
    return pl.pallas_call(
        _mod_kernel,
        out_shape=jax.ShapeDtypeStruct((rows, n), F32),
        grid=(n // tn,),
        in_specs=[pl.BlockSpec((rows, d), lambda j: (0, 0)),
                  pl.BlockSpec((d, tn), lambda j: (0, j)),
                  pl.BlockSpec((1, tn), lambda j: (0, j))],
        out_specs=pl.BlockSpec((rows, tn), lambda j: (0, j)),
        compiler_params=_cparams(1),
        name="mod",
    )(c, w_mod, b_mod.reshape(1, n))


def _mod_specs(per_token, tm, tiles_per_batch):
    if per_token:
        return pl.BlockSpec((tm, D_MODEL), lambda i: (i, 0))
    return pl.BlockSpec((None, 1, D_MODEL), lambda i: (i // tiles_per_batch, 0, 0))


def _ffn_kernel(x_ref, sh_ref, sc_ref, gt_ref, gain_ref, wgu_ref, wd_ref, *rest, final_norm, mixer):
    rest = list(rest)
    if mixer:
        gm_ref, a_ref, g_ref, wa_ref, wg_ref = rest[:5]
        rest = rest[5:]
    if final_norm:
        nf_ref = rest.pop(0)
    o_ref, acc_ref = rest
    x = x_ref[...]
    if mixer:
        x = x + gm_ref[...] * (jnp.dot(a_ref[...], wa_ref[...], preferred_element_type=F32)
                               + jnp.dot(g_ref[...], wg_ref[...], preferred_element_type=F32))
    h = _rms_mod(x, gain_ref[...], sh_ref[...], sc_ref[...]).astype(BF16)
    for j in range(N_FF_CHUNKS):
        ab = jnp.dot(h, wgu_ref[j], preferred_element_type=F32)
        g = (_silu(ab[:, :FF_CHUNK]) * ab[:, FF_CHUNK:]).astype(BF16)
        d = jnp.dot(g, wd_ref[j], preferred_element_type=F32)
        if j == 0:
            acc_ref[...] = d
        else:
            acc_ref[...] += d
    y = x + 0.5 * gt_ref[...] * acc_ref[...]
    if final_norm:
        ms = jnp.mean(y * y, axis=-1, keepdims=True)
        y = y * lax.rsqrt(ms + EPS) * nf_ref[...]
    o_ref[...] = y


def _ffn_call(x2, sh, sc, gt, gain, wgu, wd, norm_final, mixer=None, *, tm, tiles_per_batch, per_token):
    n = x2.shape[0]
    mspec = _mod_specs(per_token, tm, tiles_per_batch)
    in_specs = [pl.BlockSpec((tm, D_MODEL), lambda i: (i, 0)), mspec, mspec, mspec,
                _resident((1, D_MODEL)), _resident(wgu.shape), _resident(wd.shape)]
    args = [x2, sh, sc, gt, gain, wgu, wd]
    if mixer is not None:
        gm, attn, gdn, wa, wg = mixer
        in_specs += [mspec, pl.BlockSpec((tm, A_WIDTH), lambda i: (i, 0)),
                     pl.BlockSpec((tm, G_WIDTH), lambda i: (i, 0)), _resident(wa.shape), _resident(wg.shape)]
        args += [gm, attn, gdn, wa, wg]
    final_norm = norm_final is not None
    if final_norm:
        in_specs.append(_resident((1, D_MODEL)))
        args.append(norm_final)
    return pl.pallas_call(
        functools.partial(_ffn_kernel, final_norm=final_norm, mixer=mixer is not None),
        out_shape=jax.ShapeDtypeStruct((n, D_MODEL), F32),
        grid=(n // tm,),
        in_specs=in_specs,
        out_specs=pl.BlockSpec((tm, D_MODEL), lambda i: (i, 0)),
        scratch_shapes=[pltpu.VMEM((tm, D_MODEL), F32)],
        compiler_params=_cparams(1),
        name="ffn_final" if final_norm else "ffn",
    )(*args)


def _inproj_kernel(x_ref, sh_ref, sc_ref, gain_ref, w_ref,
                   qx_o, k_o, kb_o, v_o, vb_o, iq_o, ikx_o, misc_o, gqkv_o, gz_o):
    h = _rms_mod(x_ref[...], gain_ref[...], sh_ref[...], sc_ref[...]).astype(BF16)

    def mm(off, width):
        return jnp.dot(h, w_ref[:, off:off + width], preferred_element_type=F32)

    qx_o[...] = mm(OFF_QX, A_WIDTH).astype(BF16)
    k = mm(OFF_K, OFF_V - OFF_K)
    k_o[...] = k
    kb_o[...] = k.astype(BF16)
    v = mm(OFF_V, OFF_IQ - OFF_V)
    v_o[...] = v
    vb_o[...] = v.astype(BF16)
    iq_o[...] = mm(OFF_IQ, OFF_IKX - OFF_IQ).astype(BF16)
    ikx_o[...] = mm(OFF_IKX, OFF_MISC - OFF_IKX).astype(BF16)
    misc_o[...] = mm(OFF_MISC, LANES)
    gqkv_o[...] = mm(OFF_GQKV, CONV_CH)
    gz_o[...] = mm(OFF_GZ, G_WIDTH)


def _inproj_call(x2, sh, sc, gain, w_packed, *, tm, tiles_per_batch, per_token):
    n = x2.shape[0]
    mspec = _mod_specs(per_token, tm, tiles_per_batch)
    widths = [(A_WIDTH, BF16), (KV_WIDTH, F32), (KV_WIDTH, BF16), (KV_WIDTH, F32), (KV_WIDTH, BF16),
              (IDX_HEADS * IDX_DIM, BF16), (LANES, BF16), (LANES, F32), (CONV_CH, F32), (G_WIDTH, F32)]
    return pl.pallas_call(
        _inproj_kernel,
        out_shape=[jax.ShapeDtypeStruct((n, w), dt) for w, dt in widths],
        grid=(n // tm,),
        in_specs=[pl.BlockSpec((tm, D_MODEL), lambda i: (i, 0)), mspec, mspec,
                  _resident((1, D_MODEL)), _resident(w_packed.shape)],
        out_specs=[pl.BlockSpec((tm, w), lambda i: (i, 0)) for w, _ in widths],
        compiler_params=_cparams(1),
        name="inproj",
    )(x2, sh, sc, gain, w_packed)


def _rel_bucket_int(rel):
    n = jnp.abs(rel)
    large = jnp.full(rel.shape, 8, jnp.int32)
    for th in (12, 16, 23, 32, 46, 64, 91):
        large = large + jnp.where(n >= th, 1, 0)
    return jnp.where(rel > 0, REL_BUCKETS // 2, 0) + jnp.where(n < 8, n, large)


def _key_to_float(t):
    bits = jnp.where(t >= 0, t, t ^ jnp.int32(0x7FFFFFFF))
    return lax.bitcast_convert_type(bits, F32)


def _top16(x):
    bits = lax.bitcast_convert_type(x, jnp.int32) & jnp.int32(-65536)
    return lax.bitcast_convert_type(bits, F32).astype(BF16)


def _bias_kernel(relb_ref, o_ref, *, tq):
    h = pl.program_id(1)
    key_off = lax.broadcasted_iota(jnp.int32, (ATT_KB, tq), 0)
    qry_off = lax.broadcasted_iota(jnp.int32, (ATT_KB, tq), 1)
    bucket = _rel_bucket_int(key_off - qry_off - pl.program_id(0) * ATT_KB)
    t = jnp.zeros((ATT_KB, tq), F32)
    for b in range(REL_BUCKETS):
        t = jnp.where(bucket == b, relb_ref[b, h], t)
    o_ref[0, 0] = (t * LOG2E).astype(BF16)


def _bias_call(rel_bias, tq):
    return pl.pallas_call(
        functools.partial(_bias_kernel, tq=tq),
        out_shape=jax.ShapeDtypeStruct((3, A_HEADS, ATT_KB, tq), BF16),
        grid=(3, A_HEADS),
        in_specs=[pl.BlockSpec(memory_space=pltpu.SMEM)],
        out_specs=pl.BlockSpec((1, 1, ATT_KB, tq), lambda d, h: (d, h, 0, 0)),
        compiler_params=_cparams(2),
        name="rel_bias_tiles",
    )(rel_bias)


def _attn_kernel(q_ref, iq_ref, misc_ref, k_ref, vt_ref, ikx_ref, bias_ref, o_ref,
                 s_ref, shi_ref, lg_ref, cm_ref, acc_ref, cnt_ref, cnth_ref, qx_ref,
                 *, tq, l_true, offset, n_top):
    kb_sz = ATT_KB
    low_half_q = lax.broadcasted_iota(jnp.int32, (tq, LANES), 1) < A_HEAD_DIM
    low_half_k = lax.broadcasted_iota(jnp.int32, (kb_sz, LANES), 1) < A_HEAD_DIM

    for s, pair in enumerate(Q_PAIRS):
        slab = q_ref[0, :, s * LANES:(s + 1) * LANES]
        zero = jnp.zeros_like(slab)
        qx_ref[:, pair[0] * LANES:(pair[0] + 1) * LANES] = jnp.where(low_half_q, slab, zero)
        qx_ref[:, pair[1] * LANES:(pair[1] + 1) * LANES] = jnp.where(low_half_q, zero, slab)
    q0 = offset + pl.program_id(1) * tq
    kmax = jnp.minimum(q0 + tq, l_true)
    nkb = (kmax + kb_sz - 1) // kb_sz

    key_off = lax.broadcasted_iota(jnp.int32, (kb_sz, tq), 0)

    qpos = q0 + lax.broadcasted_iota(jnp.int32, (1, tq), 1)
    limit = jnp.minimum((qpos // CHUNK + 1) * CHUNK, l_true)

    iw_t = misc_ref[0].T[MISC_IW:MISC_IW + IDX_HEADS, :] * (IDX_HEADS ** -0.5)

    def score_body(kb, carry):
        base = pl.multiple_of(kb * kb_sz, kb_sz)
        ik2 = ikx_ref[0, pl.ds(base, kb_sz), :]
        ik_lo = jnp.where(low_half_k, ik2, jnp.zeros_like(ik2))
        ik_hi = jnp.where(low_half_k, jnp.zeros_like(ik2), ik2)
        s = jnp.zeros((kb_sz, tq), F32)
        for j in range(IDX_HEADS // 2):
            slab = iq_ref[0, :, j * LANES:(j + 1) * LANES]
            d0 = _dot_nt(ik_lo, slab)
            d1 = _dot_nt(ik_hi, slab)
            s = s + iw_t[2 * j:2 * j + 1, :] * jnp.maximum(d0, 0.0)
            s = s + iw_t[2 * j + 1:2 * j + 2, :] * jnp.maximum(d1, 0.0)
        s = jnp.where(base + key_off < limit, s, -jnp.inf)
        s_ref[pl.ds(base, kb_sz), :] = s
        shi_ref[pl.ds(base, kb_sz), :] = _top16(s)
        return carry

    def for_each_block(body):
        def oct_(i, carry):
            for r in range(8):
                body(8 * i + r, carry)
            return carry
        lax.fori_loop(0, nkb // 8, oct_, 0)

        @pl.when(nkb % 8 >= 4)
        def _():
            for r in range(4):
                body((nkb // 8) * 8 + r, 0)
        done = (nkb // 4) * 4

        @pl.when(nkb % 4 >= 2)
        def _():
            body(done, 0)
            body(done + 1, 0)

        @pl.when(nkb % 2 == 1)
        def _():
            body(nkb - 1, 0)

    for_each_block(score_body)

    def sum_blocks(contrib, acc_ref, n=nkb):
        acc_ref[...] = jnp.zeros(acc_ref.shape, acc_ref.dtype)

        def quad(i, carry):
            acc_ref[...] += _tree_sum([contrib(4 * i + r) for r in range(4)])
            return carry
        lax.fori_loop(0, n // 4, quad, 0)
        done = (n // 4) * 4

        @pl.when(n % 4 >= 2)
        def _():
            acc_ref[...] += contrib(done) + contrib(done + 1)

        @pl.when(n % 2 == 1)
        def _():
            acc_ref[...] += contrib(n - 1)
        return acc_ref[...]

    def count(pred, n=nkb):
        def contrib(kb):
            base = pl.multiple_of(kb * kb_sz, kb_sz)
            hit = pred(s_ref[pl.ds(base, kb_sz), :], base + key_off)
            return _tree_sum([hit[r:r + SUBLANES] for r in range(0, kb_sz, SUBLANES)])
        return jnp.sum(sum_blocks(contrib, cnt_ref, n), axis=0, keepdims=True)

    def count_hi(thr_hi):
        one = jnp.ones((), BF16)
        zero = jnp.zeros((), BF16)

        def contrib(kb):
            base = pl.multiple_of(kb * kb_sz, kb_sz)
            hit = jnp.where(shi_ref[pl.ds(base, kb_sz), :] >= thr_hi, one, zero)
            return _tree_sum([hit[r:r + BF16_ROWS] for r in range(0, kb_sz, BF16_ROWS)])
        return jnp.sum(sum_blocks(contrib, cnth_ref).astype(F32), axis=0, keepdims=True)

    def hi_body(i, carry):
        t, c_t = carry
        cand = t + lax.shift_left(jnp.int32(1), 31 - i)
        c = count_hi(_top16(_key_to_float(cand)))
        return jnp.where(c >= n_top, cand, t), jnp.where(c >= n_top, c, c_t)

    def lo_body(i, carry):
        t, c_t = carry
        cand = t + lax.shift_left(jnp.int32(1), 15 - i)
        thr_c = _key_to_float(cand)
        c = count(lambda blk, _: jnp.where(blk >= thr_c, 1.0, 0.0))
        return jnp.where(c >= n_top, cand, t), jnp.where(c >= n_top, c, c_t)

    carry = (jnp.full((1, tq), INT_MIN, jnp.int32), jnp.full((1, tq), float(n_top), F32))
    carry = lax.fori_loop(0, 16, hi_body, carry)
    t_key, n_ge = lax.fori_loop(0, 16, lo_body, carry)
    thr = _key_to_float(t_key)

    take_all = limit <= n_top
    excess = jnp.where(take_all, 0.0, jnp.where(n_ge > n_top, 1.0, 0.0))
    any_excess = jnp.max(excess) > 0.0
    n_tie = jnp.where(any_excess, nkb, 0)
    need = n_top - count(lambda blk, _: jnp.where(blk > thr, 1.0, 0.0), n_tie)
    idx_bits = 14

    def tie_body(i, c):
        cand = c + lax.shift_left(jnp.int32(1), idx_bits - 1 - i)
        f = count(lambda blk, kidx: jnp.where(blk == thr, jnp.where(kidx < cand, 1.0, 0.0), 0.0))
        return jnp.where(f <= need, cand, c)

    cut0 = jnp.where(any_excess, jnp.zeros((1, tq), jnp.int32),
                     jnp.full((1, tq), 2 ** idx_bits, jnp.int32))
    cut = lax.fori_loop(0, jnp.where(any_excess, idx_bits, 0), tie_body, cut0)
    thr = jnp.where(take_all, -jnp.inf, thr)
    cut = jnp.where(take_all, limit, cut)

    cm_ref[...] = jnp.full(cm_ref.shape, NEG_BIG, BF16)

    def logit_body(kb, carry):
        base = pl.multiple_of(kb * kb_sz, kb_sz)
        dsel = jnp.clip((q0 - base) // kb_sz, 0, 2)
        kblk = k_ref[0, pl.ds(base, kb_sz), :]
        blk = s_ref[pl.ds(base, kb_sz), :]
        tie = jnp.where(base + key_off < cut, 0.0, NEG_BIG)
        sel = jnp.where(blk > thr, 0.0, jnp.where(blk == thr, tie, NEG_BIG)).astype(BF16)
        for h in range(A_HEADS):
            sl = h // 4
            lg = _dot_nt(kblk[:, sl * LANES:(sl + 1) * LANES], qx_ref[:, h * LANES:(h + 1) * LANES])
            lg = lg.astype(BF16) + sel + bias_ref[dsel, h]
            lg_ref[h, pl.ds(base, kb_sz), :] = lg
            rows = [lg[r:r + BF16_ROWS] for r in range(0, kb_sz, BF16_ROWS)]
            while len(rows) > 1:
                rows = [jnp.maximum(a, b) for a, b in zip(rows[0::2], rows[1::2])]
            cm_ref[h] = jnp.maximum(cm_ref[h], rows[0])
        return carry

    for_each_block(logit_body)

    acc_ref[...] = jnp.zeros(acc_ref.shape, F32)
    m_rows = [jnp.max(cm_ref[h].astype(F32), axis=0, keepdims=True).astype(BF16) for h in range(A_HEADS)]

    def pv_body(kb, carry):
        base = pl.multiple_of(kb * kb_sz, kb_sz)
        for h in range(A_HEADS):
            p = jnp.exp2(lg_ref[h, pl.ds(base, kb_sz), :] - m_rows[h])
            acc_ref[h] += jnp.dot(vt_ref[0, kb, h // 4], p, preferred_element_type=F32)
        return carry

    for_each_block(pv_body)

    outs = []
    for h in range(A_HEADS):
        pos = (h // 2) % 2
        outs.append(acc_ref[h, pos * A_HEAD_DIM:(pos + 1) * A_HEAD_DIM, :] / acc_ref[h, LANES:LANES + 1, :])
    o_ref[0] = jnp.concatenate(outs, axis=0).T.astype(BF16)


def _attn_call(rel_bias, qx, iq, misc, kb, vt, ikx, *, tq, l_true, offset, n_top):
    b, tq_total, _ = qx.shape
    lp = kb.shape[1]
    nq = tq_total // tq
    assert offset % ATT_KB == 0 and tq % CHUNK == 0 and ATT_KB % tq == 0 and lp % ATT_KB == 0
    assert lp < 2 ** 14 and l_true <= lp
    kern = functools.partial(_attn_kernel, tq=tq, l_true=l_true, offset=offset, n_top=n_top)
    bias = _bias_call(rel_bias, tq)
    return pl.pallas_call(
        kern,
        out_shape=jax.ShapeDtypeStruct((b, tq_total, A_WIDTH), BF16),
        grid=(b, nq),
        in_specs=[pl.BlockSpec((1, tq, A_WIDTH), lambda i, j: (i, j, 0)),
                  pl.BlockSpec((1, tq, IDX_HEADS * IDX_DIM), lambda i, j: (i, j, 0)),
                  pl.BlockSpec((1, tq, LANES), lambda i, j: (i, j, 0)),
                  pl.BlockSpec((1, lp, KV_WIDTH), lambda i, j: (i, 0, 0)),
                  pl.BlockSpec((1, lp // ATT_KB, 2, V_ROWS, ATT_KB), lambda i, j: (i, 0, 0, 0, 0)),
                  pl.BlockSpec((1, lp, LANES), lambda i, j: (i, 0, 0)),
                  _resident(bias.shape)],
        out_specs=pl.BlockSpec((1, tq, A_WIDTH), lambda i, j: (i, j, 0)),
        scratch_shapes=[pltpu.VMEM((lp, tq), F32),
                        pltpu.VMEM((lp, tq), BF16),
                        pltpu.VMEM((A_HEADS, lp, tq), BF16),
                        pltpu.VMEM((A_HEADS, BF16_ROWS, tq), BF16),
                        pltpu.VMEM((A_HEADS, V_ROWS, tq), F32),
                        pltpu.VMEM((SUBLANES, tq), F32),
                        pltpu.VMEM((BF16_ROWS, tq), BF16),
                        pltpu.VMEM((tq, QX_W), BF16)],
        compiler_params=_cparams(2),
        name="sparse_attn",
    )(qx, iq, misc, kb, vt, ikx, bias)


def _split_bf16(a, n):
    parts = []
    r = a
    for i in range(n):
        p = r.astype(BF16)
        parts.append(p)
        if i + 1 < n:
            r = r - p.astype(F32)
    return parts


def _bdot(a, b, dims=(((1,), (0,)), ((), ()))):
    return lax.dot_general(a.astype(BF16), b.astype(BF16), dims, preferred_element_type=F32)


_NT = (((1,), (1,)), ((), ()))
_TN = (((0,), (0,)), ((), ()))


def _gdn_kernel(x_ref, misc_ref, gz_ref, convw_ref, hist_ref, s0_ref, gnw_ref, alane_ref, dlane_ref,
                o_ref, conv_o, s_o, xp_ref, y_ref, st_ref, vn_ref,
                u_ref, w_ref, qk_ref, qe_ref, kd_ref, el_ref, gate_ref, *, cb, t_true, nblk):
    c = CHUNK
    step = pl.program_id(1)
    blk = jnp.minimum(step, nblk - 1)
    carried = (u_ref, w_ref, qk_ref, qe_ref, kd_ref, el_ref, gate_ref)
    u_nx, w_nx, qk_nx, qe_nx, kd_nx, el_nx, gate_nx = (r.at[1] for r in carried)
    u_ref, w_ref, qk_ref, qe_ref, kd_ref, el_ref, gate_ref = (r.at[0] for r in carried)

    @pl.when(step == 0)
    def _():
        xp_ref[XP_PAD - (CONV_W - 1):XP_PAD, :] = hist_ref[0]
        st_ref[...] = s0_ref[0]
        for r in carried:
            r[...] = jnp.zeros(r.shape, r.dtype)

    heads = range(G_HEADS)
    for r in carried:
        r[0] = r[1]

    def scan_stages():
        vn_ref[...] = jnp.zeros(vn_ref.shape, F32)
        for ck in range(cb // c):
            r0, r1 = ck * c, (ck + 1) * c
            s_prev = [st_ref[h] for h in heads]
            v_new = [u_ref[h, r0:r1, :] - _bdot(w_ref[h, r0:r1, :], s_prev[h]) for h in heads]
            o_state = [_bdot(qe_ref[h, r0:r1, :], s_prev[h]) for h in heads]
            yield
            for h in heads:
                vn_ref[h, r0:r1, :] = v_new[h]
            o = [o_state[h] + _bdot(qk_ref[h, r0:r1, :], vn_ref[h]) for h in heads]
            for h in heads:
                s_new = s_prev[h] * el_ref[h, ck, 0:1, :] + _bdot(kd_ref[h, r0:r1, :], v_new[h], _TN)
                st_ref[h] = jnp.where(step > 0, s_new, s_prev[h])
            yield
            for h in heads:
                on = o[h] * lax.rsqrt(jnp.mean(o[h] * o[h], axis=-1, keepdims=True) + EPS) * gnw_ref[...]
                gate = gate_ref[r0:r1, h * G_VAL_DIM:(h + 1) * G_VAL_DIM]
                o_ref[0, r0:r1, h * G_VAL_DIM:(h + 1) * G_VAL_DIM] = (on * gate).astype(BF16)
            yield
        s_o[0] = st_ref[...]

    scan = scan_stages()

    def tick():
        next(scan, None)

    tick()

    xp_ref[XP_PAD:XP_PAD + cb, :] = x_ref[0]
    y = convw_ref[CONV_W - 1:CONV_W, :] * xp_ref[XP_PAD:XP_PAD + cb, :]
    for j in range(CONV_W - 1):
        lag_rows = XP_PAD - (CONV_W - 1) + j
        y = y + convw_ref[j:j + 1, :] * xp_ref[lag_rows:lag_rows + cb, :]
    y_ref[...] = _silu(y)

    last_row = XP_PAD + (t_true - 1) % cb
    conv_o[0] = xp_ref[last_row - (CONV_W - 2):last_row + 1, :]

    xp_ref[XP_PAD - (CONV_W - 1):XP_PAD, :] = xp_ref[XP_PAD + cb - (CONV_W - 1):XP_PAD + cb, :]

    ri = lax.broadcasted_iota(jnp.int32, (cb, cb), 0)
    ci = lax.broadcasted_iota(jnp.int32, (cb, cb), 1)
    lag = jnp.where((ri // c) == (ci // c), ri - ci, -1)
    tri = lag >= 0
    strict = lag > 0
    eye = ri == ci
    tril_bf = jnp.where(tri, 1.0, 0.0).astype(BF16)
    eye_f = jnp.where(eye, 1.0, 0.0)

    ms = misc_ref[0]
    tok = blk * cb + lax.broadcasted_iota(jnp.int32, (cb, 1), 0)
    live = tok < t_true
    beta_s = jnp.where(live, jax.nn.sigmoid(ms), 0.0)
    z = ms + dlane_ref[...]
    softplus = jnp.maximum(z, 0.0) + jnp.log(1.0 + jnp.exp(-jnp.abs(z)))
    g_s = jnp.where(live, -jnp.exp(alane_ref[...]) * softplus, 0.0)
    gc_s = None
    for piece in _split_bf16(g_s, 3):
        t = jnp.dot(tril_bf, piece, preferred_element_type=F32)
        gc_s = t if gc_s is None else gc_s + t

    tick()

    q, k, v, beta, gc, decay, kb, eg = ([None] * G_HEADS for _ in range(8))
    for h in heads:
        qh = y_ref[:, h * G_KEY_DIM:(h + 1) * G_KEY_DIM]
        kh = y_ref[:, G_WIDTH + h * G_KEY_DIM:G_WIDTH + (h + 1) * G_KEY_DIM]
        v[h] = y_ref[:, 2 * G_WIDTH + h * G_VAL_DIM:2 * G_WIDTH + (h + 1) * G_VAL_DIM]
        q[h] = qh * lax.rsqrt(jnp.sum(qh * qh, axis=-1, keepdims=True) + 1e-6) * (G_KEY_DIM ** -0.5)
        k[h] = kh * lax.rsqrt(jnp.sum(kh * kh, axis=-1, keepdims=True) + 1e-6)
        beta[h] = beta_s[:, MISC_GB + h:MISC_GB + h + 1]
        gc[h] = gc_s[:, MISC_GA + h:MISC_GA + h + 1]
        gc_b = jnp.broadcast_to(gc[h], (cb, cb))
        gc_row = jnp.sum(jnp.where(eye, gc_b, 0.0), axis=0, keepdims=True)
        decay[h] = jnp.exp(jnp.where(tri, gc_b - gc_row, NEG_BIG))
        kb[h] = k[h] * beta[h]
        eg[h] = jnp.exp(gc[h])
    tick()
    m = [jnp.where(strict, _bdot(kb[h], k[h], _NT) * decay[h], 0.0) for h in heads]
    rhs = [jnp.concatenate([v[h] * beta[h], kb[h] * eg[h]], axis=1) for h in heads]
    qk = [jnp.where(tri, _bdot(q[h], k[h], _NT) * decay[h], 0.0) for h in heads]
    tick()
    pw = [-m[h] for h in heads]
    inv = [eye_f + pw[h] for h in heads]
    for _ in range(5):
        pw = [_bdot(pw[h], pw[h]) for h in heads]
        tick()
        inv = [inv[h] + _bdot(inv[h], pw[h]) for h in heads]
    tick()
    sol = [_bdot(inv[h], rhs[h]) for h in heads]
    tick()
    m_sol = []
    for h in heads:
        m_hi, m_lo = _split_bf16(m[h], 2)
        s_hi, s_lo = _split_bf16(sol[h], 2)
        m_sol.append(_bdot(m_hi, s_hi) + _bdot(m_hi, s_lo) + _bdot(m_lo, s_hi))
    tick()
    sol = [sol[h] + _bdot(inv[h], rhs[h] - sol[h] - m_sol[h]) for h in heads]
    for _ in scan:
        pass
    gate_nx[...] = _silu(gz_ref[0])
    for h in heads:
        u_nx[h] = sol[h][:, :G_VAL_DIM]
        w_nx[h] = sol[h][:, G_VAL_DIM:].astype(BF16)
        qk_nx[h] = qk[h].astype(BF16)
        qe_nx[h] = (q[h] * eg[h]).astype(BF16)
        for ck in range(cb // c):
            r0, r1 = ck * c, (ck + 1) * c
            g_last = gc[h][r1 - 1:r1, :]
            kd_nx[h, r0:r1, :] = (k[h][r0:r1] * jnp.exp(g_last - gc[h][r0:r1])).astype(BF16)
            el_nx[h, ck] = jnp.broadcast_to(jnp.exp(g_last), (SUBLANES, LANES))


def _gdn_call(gqkv, misc, gz, conv_w, hist, s0, gnorm_w, a_lane, d_lane, *, cb, t_true):
    b, t_pad, _ = gqkv.shape
    nblk = t_pad // cb
    assert (nblk - 1) * cb < t_true <= t_pad and (t_true - 1) % cb >= CONV_W - 2
    kern = functools.partial(_gdn_kernel, cb=cb, t_true=t_true, nblk=nblk)
    prep = lambda i, j: (i, jnp.minimum(j, nblk - 1), 0)
    scan = lambda i, j: (i, jnp.maximum(j - 1, 0), 0)
    nch = cb // CHUNK
    return pl.pallas_call(
        kern,
        out_shape=[jax.ShapeDtypeStruct((b, t_pad, G_WIDTH), BF16),
                   jax.ShapeDtypeStruct((b, CONV_W - 1, CONV_CH), F32),
                   jax.ShapeDtypeStruct((b, G_HEADS, G_KEY_DIM, G_VAL_DIM), F32)],
        grid=(b, nblk + 1),
        in_specs=[pl.BlockSpec((1, cb, CONV_CH), prep),
                  pl.BlockSpec((1, cb, LANES), prep),
                  pl.BlockSpec((1, cb, G_WIDTH), prep),
                  pl.BlockSpec((CONV_W, CONV_CH), lambda i, j: (0, 0)),
                  pl.BlockSpec((1, CONV_W - 1, CONV_CH), lambda i, j: (i, 0, 0)),
                  pl.BlockSpec((1, G_HEADS, G_KEY_DIM, G_VAL_DIM), lambda i, j: (i, 0, 0, 0)),
                  pl.BlockSpec((1, G_VAL_DIM), lambda i, j: (0, 0)),
                  pl.BlockSpec((1, LANES), lambda i, j: (0, 0)),
                  pl.BlockSpec((1, LANES), lambda i, j: (0, 0))],
        out_specs=[pl.BlockSpec((1, cb, G_WIDTH), scan),
                   pl.BlockSpec((1, CONV_W - 1, CONV_CH), lambda i, j: (i, 0, 0)),
                   pl.BlockSpec((1, G_HEADS, G_KEY_DIM, G_VAL_DIM), lambda i, j: (i, 0, 0, 0))],
        scratch_shapes=[pltpu.VMEM((cb + XP_PAD, CONV_CH), F32),
                        pltpu.VMEM((cb, CONV_CH), F32),
                        pltpu.VMEM((G_HEADS, G_KEY_DIM, G_VAL_DIM), F32),
                        pltpu.VMEM((G_HEADS, cb, G_VAL_DIM), F32),
                        pltpu.VMEM((2, G_HEADS, cb, G_VAL_DIM), F32),
                        pltpu.VMEM((2, G_HEADS, cb, G_KEY_DIM), BF16),
                        pltpu.VMEM((2, G_HEADS, cb, cb), BF16),
                        pltpu.VMEM((2, G_HEADS, cb, G_KEY_DIM), BF16),
                        pltpu.VMEM((2, G_HEADS, cb, G_KEY_DIM), BF16),
                        pltpu.VMEM((2, G_HEADS, nch, SUBLANES, LANES), F32),
                        pltpu.VMEM((2, cb, G_WIDTH), F32)],
        compiler_params=_cparams(2),
        name="gated_delta",
    )(gqkv, misc, gz, conv_w, hist, s0, gnorm_w, a_lane, d_lane)


def _pack_w_in(w_in):
    d = w_in.shape[0]
    splits = (512, 256, 256, 512, 64, 8, 512, 512, 512, 512, 4, 4)
    offs = np.concatenate([[0], np.cumsum(splits)])
    aq, ak, av, iq, ik, iw, gq, gk, gv, gz, gb, ga = [w_in[:, offs[i]:offs[i + 1]] for i in range(12)]
    qx = [aq[:, h * A_HEAD_DIM:(h + 1) * A_HEAD_DIM] * (A_HEAD_DIM ** -0.5 * LOG2E)
          for pair in Q_PAIRS for h in pair]
    ikx = [ik, ik]
    misc = [ik, iw, gb, ga, jnp.zeros((d, LANES - MISC_GA - G_HEADS), w_in.dtype)]
    cols = qx + [ak, av, iq * (IDX_DIM ** -0.5)] + ikx + misc + [gq, gk, gv, gz]
    packed = jnp.concatenate(cols, axis=1).astype(BF16)
    assert packed.shape[1] == IN_PACKED
    return packed


def _pack_ffn(w_gate, w_up, w_down):
    d = w_gate.shape[0]
    wg = w_gate.reshape(d, N_FF_CHUNKS, FF_CHUNK)
    wu = w_up.reshape(d, N_FF_CHUNKS, FF_CHUNK)
    wgu = jnp.transpose(jnp.concatenate([wg, wu], axis=2), (1, 0, 2)).astype(BF16)
    wd = w_down.reshape(N_FF_CHUNKS, FF_CHUNK, d).astype(BF16)
    return wgu, wd


def _ikx_layout(ik):
    return jnp.concatenate([ik, ik], axis=-1)


def _lane_vec(vals, lane0):
    return jnp.zeros((1, LANES), F32).at[0, lane0:lane0 + vals.shape[0]].set(vals.astype(F32))


def _run(x, mod, past, layers, rel_bias, norm_final):
    b, t, d = x.shape
    n = b * t
    per_token = t < TOKEN_TILE
    tm = n if per_token else TOKEN_TILE
    tiles_per_batch = None if per_token else t // tm
    tok = dict(tm=tm, tiles_per_batch=tiles_per_batch, per_token=per_token)

    x2 = x.reshape(n, d)
    states = []
    for li, lw in enumerate(layers):
        m = mod[li]

        def mvec(kidx):
            row = m[:, kidx]
            return jnp.repeat(row, t, axis=0) if per_token else row[:, None, :]

        sh1, sc1, gt1, sh2, sc2, gt2, sh3, sc3, gt3 = [mvec(i) for i in range(N_MOD)]
        x2 = _ffn_call(x2, sh1, sc1, gt1, lw["norm_ffn1"], lw["wgu1"], lw["wd1"], None, **tok)
        qx, k, kb, v, vb, iq, ikx, misc, gqkv, gz = _inproj_call(
            x2, sh2, sc2, lw["norm_mix"], lw["w_in"], **tok)

        if past is None:
            offset, l_true, tq = 0, t, ATT_TQ
            k_all, v_all, ikx_all = (a.reshape(b, t, -1) for a in (kb, vb, ikx))
            qx3, iq3, misc3 = (a.reshape(b, t, -1) for a in (qx, iq, misc))
            conv_hist = jnp.zeros((b, CONV_W - 1, CONV_CH), F32)
            s0 = jnp.zeros((b, G_HEADS, G_KEY_DIM, G_VAL_DIM), F32)
        else:
            k_hist, v_hist, ik_hist, conv_hist, s0 = (p[li] for p in past)
            offset = k_hist.shape[1]
            l_true = offset + t
            tq = LANES
            lp = -(-l_true // ATT_KB) * ATT_KB
            pad_k = lambda a: jnp.pad(a, ((0, 0), (0, lp - l_true), (0, 0)))
            k_all = pad_k(jnp.concatenate([k_hist.reshape(b, offset, -1).astype(BF16),
                                           kb.reshape(b, t, -1)], axis=1))
            v_all = pad_k(jnp.concatenate([v_hist.reshape(b, offset, -1).astype(BF16),
                                           vb.reshape(b, t, -1)], axis=1))
            ikx_all = pad_k(jnp.concatenate([_ikx_layout(ik_hist.astype(BF16)),
                                             ikx.reshape(b, t, -1)], axis=1))
            pad_q = lambda a: jnp.pad(a.reshape(b, t, -1), ((0, 0), (0, tq - t), (0, 0)))
            qx3, iq3, misc3 = pad_q(qx), pad_q(iq), pad_q(misc)
            s0 = s0.astype(F32)
            conv_hist = conv_hist.astype(F32)
        lp = k_all.shape[1]
        n_top = min(TOPK_MAX, l_true // 4)
        vt = jnp.transpose(v_all.reshape(b, lp // ATT_KB, ATT_KB, 2, LANES), (0, 1, 3, 4, 2))
        vt = jnp.concatenate([vt, jnp.ones((b, lp // ATT_KB, 2, V_ROWS - LANES, ATT_KB), BF16)], axis=3)
        attn = _attn_call(rel_bias, qx3, iq3, misc3, k_all, vt, ikx_all,
                          tq=tq, l_true=l_true, offset=offset, n_top=n_top)
        attn = attn[:, :t].reshape(n, A_WIDTH)

        cb = GDN_BLOCK if t % GDN_BLOCK == 0 else CHUNK
        t_pad = -(-t // cb) * cb
        pad_t = lambda a: jnp.pad(a.reshape(b, t, -1), ((0, 0), (0, t_pad - t), (0, 0)))
        gdn, conv_new, s_new = _gdn_call(pad_t(gqkv), pad_t(misc), pad_t(gz), lw["conv_w"], conv_hist, s0,
                                         lw["gnorm_w"], lw["a_lane"], lw["d_lane"], cb=cb, t_true=t)
        gdn = gdn[:, :t].reshape(n, G_WIDTH)

        nf = norm_final if li == len(layers) - 1 else None
        x2 = _ffn_call(x2, sh3, sc3, gt3, lw["norm_ffn2"], lw["wgu2"], lw["wd2"], nf,
                       mixer=(gt2, attn, gdn, lw["wo_a"], lw["wo_g"]), **tok)

        states.append((k.reshape(b, t, A_KV_HEADS, A_HEAD_DIM), v.reshape(b, t, A_KV_HEADS, A_HEAD_DIM),
                       misc.reshape(b, t, LANES)[..., :IDX_DIM], conv_new, s_new))
    stacked = [jnp.stack(s, axis=0) for s in zip(*states)]
    return x2.reshape(b, t, d), stacked


def kernel(x_prompt, x_sample, cache_k, cache_v, cache_idx_k, state_conv, state_delta, c_prompt, c_sample,
           w_mod, b_mod, norm_ffn1, norm_mix, norm_ffn2, ffn1_w_gate, ffn1_w_up, ffn1_w_down,
           ffn2_w_gate, ffn2_w_up, ffn2_w_down, w_in, w_out, rel_bias, conv_w, a_log, dt_bias, gnorm_w,
           norm_final):
    depth = w_mod.shape[0]
    bp = c_prompt.shape[0]
    c_all = jnp.concatenate([c_prompt, c_sample], axis=0)
    layers, mods_p, mods_s = [], [], []
    for l in range(depth):
        mod = _mod_call(c_all, w_mod[l], b_mod[l]).reshape(c_all.shape[0], N_MOD, D_MODEL)
        mods_p.append(mod[:bp])
        mods_s.append(mod[bp:])
        wgu1, wd1 = _pack_ffn(ffn1_w_gate[l], ffn1_w_up[l], ffn1_w_down[l])
        wgu2, wd2 = _pack_ffn(ffn2_w_gate[l], ffn2_w_up[l], ffn2_w_down[l])
        wo = w_out[l].astype(BF16)
        layers.append(dict(
            norm_ffn1=norm_ffn1[l].reshape(1, -1), norm_mix=norm_mix[l].reshape(1, -1),
            norm_ffn2=norm_ffn2[l].reshape(1, -1), wgu1=wgu1, wd1=wd1, wgu2=wgu2, wd2=wd2,
            w_in=_pack_w_in(w_in[l]), wo_a=wo[:A_WIDTH], wo_g=wo[A_WIDTH:],
            conv_w=conv_w[l], gnorm_w=gnorm_w[l].reshape(1, -1),
            a_lane=_lane_vec(a_log[l], MISC_GA), d_lane=_lane_vec(dt_bias[l], MISC_GA)))
    nf = norm_final.reshape(1, -1)
    y_p, (k_p, v_p, ik_p, conv_p, delta_p) = _run(x_prompt, mods_p, None, layers, rel_bias, nf)
    past = (cache_k, cache_v, cache_idx_k, state_conv, state_delta)
    y_s, (k_s, v_s, ik_s, conv_s, delta_s) = _run(x_sample, mods_s, past, layers, rel_bias, nf)
    return (y_p, y_s, k_p, v_p, ik_p, conv_p, delta_p, k_s, v_s, ik_s, conv_s, delta_s)
```

```python
import functools

import jax
import jax.numpy as jnp
import numpy as np
from jax import lax
from jax.experimental import pallas as pl
from jax.experimental.pallas import tpu as pltpu

F32 = jnp.float32
BF16 = jnp.bfloat16

D_MODEL = 1024
CHUNK = 64
A_HEAD_DIM = 64
A_HEADS = 8
A_KV_HEADS = 4
A_WIDTH = A_HEADS * A_HEAD_DIM
KV_WIDTH = A_KV_HEADS * A_HEAD_DIM
IDX_HEADS = 8
IDX_DIM = 64
TOPK_MAX = 256
REL_BUCKETS = 32
G_KEY_DIM = 128
G_VAL_DIM = 128
G_HEADS = 4
G_WIDTH = G_HEADS * G_VAL_DIM
CONV_W = 4
CONV_CH = 2 * G_HEADS * G_KEY_DIM + G_HEADS * G_VAL_DIM
D_FF = 2816
N_MOD = 9
EPS = 1e-6

LANES = 128
SUBLANES = 8
BF16_ROWS = 16
MXU_DIM = 256
VMEM_LIMIT_BYTES = 56 * 1024 * 1024

FF_CHUNK = MXU_DIM
N_FF_CHUNKS = D_FF // FF_CHUNK
TOKEN_TILE = 1024
ATT_TQ = 256
ATT_KB = 256
GDN_BLOCK = 256
XP_PAD = SUBLANES

QX_W = A_HEADS * LANES
Q_PAIRS = ((0, 2), (1, 3), (4, 6), (5, 7))
OFF_QX = 0
OFF_K = OFF_QX + A_WIDTH
OFF_V = OFF_K + A_KV_HEADS * A_HEAD_DIM
OFF_IQ = OFF_V + A_KV_HEADS * A_HEAD_DIM
OFF_IKX = OFF_IQ + IDX_HEADS * IDX_DIM
OFF_MISC = OFF_IKX + LANES
OFF_GQKV = OFF_MISC + LANES
OFF_GZ = OFF_GQKV + CONV_CH
IN_PACKED = OFF_GZ + G_WIDTH
MISC_IW = IDX_DIM
MISC_GB = MISC_IW + IDX_HEADS
MISC_GA = MISC_GB + G_HEADS

NEG_BIG = -1e30
INT_MIN = -2 ** 31
LOG2E = 1.4426950408889634
V_ROWS = LANES + 16


def _cparams(n_axes):
    return pltpu.CompilerParams(dimension_semantics=("arbitrary",) * n_axes,
                                vmem_limit_bytes=VMEM_LIMIT_BYTES)


def _resident(shape):
    nd = len(shape)
    return pl.BlockSpec(shape, lambda *_: (0,) * nd, pipeline_mode=pl.Buffered(1))


def _dot_nt(a, b):
    return lax.dot_general(a, b, (((1,), (1,)), ((), ())), preferred_element_type=F32)


def _rms_mod(x, gain, shift, scale):
    ms = jnp.mean(x * x, axis=-1, keepdims=True)
    y = x * lax.rsqrt(ms + EPS) * gain
    return y * (1.0 + scale) + shift


def _silu(x):
    return x * jax.nn.sigmoid(x)


def _tree_sum(parts):
    while len(parts) > 1:
        parts = [a + b for a, b in zip(parts[0::2], parts[1::2])] + ([parts[-1]] if len(parts) % 2 else [])
    return parts[0]


def _mod_kernel(c_ref, w_ref, b_ref, o_ref):
    s = _silu(c_ref[...]).astype(BF16)
    o_ref[...] = jnp.dot(s, w_ref[...].astype(BF16), preferred_element_type=F32) + b_ref[...]


def _mod_call(c, w_mod, b_mod):
    rows, d = c.shape
    n = w_mod.shape[1]
    tn = D_MODEL
    return pl.pallas_call(
        _mod_kernel,
        out_shape=jax.ShapeDtypeStruct((rows, n), F32),
        grid=(n // tn,),
        in_specs=[pl.BlockSpec((rows, d), lambda j: (0, 0)),
                  pl.BlockSpec((d, tn), lambda j: (0, j)),
                  pl.BlockSpec((1, tn), lambda j: (0, j))],
        out_specs=pl.BlockSpec((rows, tn), lambda j: (0, j)),
        compiler_params=_cparams(1),
        name="mod",
    )(c, w_mod, b_mod.reshape(1, n))


def _mod_specs(per_token, tm, tiles_per_batch):
    if per_token:
        return pl.BlockSpec((tm, D_MODEL), lambda i: (i, 0))
    return pl.BlockSpec((None, 1, D_MODEL), lambda i: (i // tiles_per_batch, 0, 0))


def _ffn_kernel(x_ref, sh_ref, sc_ref, gt_ref, gain_ref, wgu_ref, wd_ref, *rest, final_norm, mixer):
    rest = list(rest)
    if mixer:
        gm_ref, a_ref, g_ref, wa_ref, wg_ref = rest[:5]
        rest = rest[5:]
    if final_norm:
        nf_ref = rest.pop(0)
    o_ref, acc_ref = rest
    x = x_ref[...]
    if mixer:
        x = x + gm_ref[...] * (jnp.dot(a_ref[...], wa_ref[...], preferred_element_type=F32)
                               + jnp.dot(g_ref[...], wg_ref[...], preferred_element_type=F32))
    h = _rms_mod(x, gain_ref[...], sh_ref[...], sc_ref[...]).astype(BF16)
    for j in range(N_FF_CHUNKS):
        ab = jnp.dot(h, wgu_ref[j], preferred_element_type=F32)
        g = (_silu(ab[:, :FF_CHUNK]) * ab[:, FF_CHUNK:]).astype(BF16)
        d = jnp.dot(g, wd_ref[j], preferred_element_type=F32)
        if j == 0:
            acc_ref[...] = d
        else:
            acc_ref[...] += d
    y = x + 0.5 * gt_ref[...] * acc_ref[...]
    if final_norm:
        ms = jnp.mean(y * y, axis=-1, keepdims=True)
        y = y * lax.rsqrt(ms + EPS) * nf_ref[...]
    o_ref[...] = y


def _ffn_call(x2, sh, sc, gt, gain, wgu, wd, norm_final, mixer=None, *, tm, tiles_per_batch, per_token):
    n = x2.shape[0]
    mspec = _mod_specs(per_token, tm, tiles_per_batch)
    in_specs = [pl.BlockSpec((tm, D_MODEL), lambda i: (i, 0)), mspec, mspec, mspec,
                _resident((1, D_MODEL)), _resident(wgu.shape), _resident(wd.shape)]
    args = [x2, sh, sc, gt, gain, wgu, wd]
    if mixer is not None:
        gm, attn, gdn, wa, wg = mixer
        in_specs += [mspec, pl.BlockSpec((tm, A_WIDTH), lambda i: (i, 0)),
                     pl.BlockSpec((tm, G_WIDTH), lambda i: (i, 0)), _resident(wa.shape), _resident(wg.shape)]
        args += [gm, attn, gdn, wa, wg]
    final_norm = norm_final is not None
    if final_norm:
        in_specs.append(_resident((1, D_MODEL)))
        args.append(norm_final)
    return pl.pallas_call(
        functools.partial(_ffn_kernel, final_norm=final_norm, mixer=mixer is not None),
        out_shape=jax.ShapeDtypeStruct((n, D_MODEL), F32),
        grid=(n // tm,),
        in_specs=in_specs,
        out_specs=pl.BlockSpec((tm, D_MODEL), lambda i: (i, 0)),
        scratch_shapes=[pltpu.VMEM((tm, D_MODEL), F32)],
        compiler_params=_cparams(1),
        name="ffn_final" if final_norm else "ffn",
    )(*args)


def _inproj_kernel(x_ref, sh_ref, sc_ref, gain_ref, w_ref,
                   qx_o, k_o, kb_o, v_o, vb_o, iq_o, ikx_o, misc_o, gqkv_o, gz_o):
    h = _rms_mod(x_ref[...], gain_ref[...], sh_ref[...], sc_ref[...]).astype(BF16)

    def mm(off, width):
        return jnp.dot(h, w_ref[:, off:off + width], preferred_element_type=F32)

    qx_o[...] = mm(OFF_QX, A_WIDTH).astype(BF16)
    k = mm(OFF_K, OFF_V - OFF_K)
    k_o[...] = k
    kb_o[...] = k.astype(BF16)
    v = mm(OFF_V, OFF_IQ - OFF_V)
    v_o[...] = v
    vb_o[...] = v.astype(BF16)
    iq_o[...] = mm(OFF_IQ, OFF_IKX - OFF_IQ).astype(BF16)
    ikx_o[...] = mm(OFF_IKX, OFF_MISC - OFF_IKX).astype(BF16)
    misc_o[...] = mm(OFF_MISC, LANES)
    gqkv_o[...] = mm(OFF_GQKV, CONV_CH)
    gz_o[...] = mm(OFF_GZ, G_WIDTH)


def _inproj_call(x2, sh, sc, gain, w_packed, *, tm, tiles_per_batch, per_token):
    n = x2.shape[0]
    mspec = _mod_specs(per_token, tm, tiles_per_batch)
    widths = [(A_WIDTH, BF16), (KV_WIDTH, F32), (KV_WIDTH, BF16), (KV_WIDTH, F32), (KV_WIDTH, BF16),
              (IDX_HEADS * IDX_DIM, BF16), (LANES, BF16), (LANES, F32), (CONV_CH, F32), (G_WIDTH, F32)]
    return pl.pallas_call(
        _inproj_kernel,
        out_shape=[jax.ShapeDtypeStruct((n, w), dt) for w, dt in widths],
        grid=(n // tm,),
        in_specs=[pl.BlockSpec((tm, D_MODEL), lambda i: (i, 0)), mspec, mspec,
                  _resident((1, D_MODEL)), _resident(w_packed.shape)],
        out_specs=[pl.BlockSpec((tm, w), lambda i: (i, 0)) for w, _ in widths],
        compiler_params=_cparams(1),
        name="inproj",
    )(x2, sh, sc, gain, w_packed)


def _rel_bucket_int(rel):
    n = jnp.abs(rel)
    large = jnp.full(rel.shape, 8, jnp.int32)
    for th in (12, 16, 23, 32, 46, 64, 91):
        large = large + jnp.where(n >= th, 1, 0)
    return jnp.where(rel > 0, REL_BUCKETS // 2, 0) + jnp.where(n < 8, n, large)


def _key_to_float(t):
    bits = jnp.where(t >= 0, t, t ^ jnp.int32(0x7FFFFFFF))
    return lax.bitcast_convert_type(bits, F32)


def _top16(x):
    bits = lax.bitcast_convert_type(x, jnp.int32) & jnp.int32(-65536)
    return lax.bitcast_convert_type(bits, F32).astype(BF16)


def _bias_kernel(relb_ref, o_ref, *, tq):
    h = pl.program_id(1)
    key_off = lax.broadcasted_iota(jnp.int32, (ATT_KB, tq), 0)
    qry_off = lax.broadcasted_iota(jnp.int32, (ATT_KB, tq), 1)
    bucket = _rel_bucket_int(key_off - qry_off - pl.program_id(0) * ATT_KB)
    t = jnp.zeros((ATT_KB, tq), F32)
    for b in range(REL_BUCKETS):
        t = jnp.where(bucket == b, relb_ref[b, h], t)
    o_ref[0, 0] = (t * LOG2E).astype(BF16)


def _bias_call(rel_bias, tq):
    return pl.pallas_call(
        functools.partial(_bias_kernel, tq=tq),
        out_shape=jax.ShapeDtypeStruct((3, A_HEADS, ATT_KB, tq), BF16),
        grid=(3, A_HEADS),
        in_specs=[pl.BlockSpec(memory_space=pltpu.SMEM)],
        out_specs=pl.BlockSpec((1, 1, ATT_KB, tq), lambda d, h: (d, h, 0, 0)),
        compiler_params=_cparams(2),
        name="rel_bias_tiles",
    )(rel_bias)


def _attn_kernel(q_ref, iq_ref, misc_ref, k_ref, vt_ref, ikx_ref, bias_ref, o_ref,
                 s_ref, shi_ref, lg_ref, cm_ref, acc_ref, cnt_ref, cnth_ref, qx_ref,
                 *, tq, l_true, offset, n_top):
    kb_sz = ATT_KB
    low_half_q = lax.broadcasted_iota(jnp.int32, (tq, LANES), 1) < A_HEAD_DIM
    low_half_k = lax.broadcasted_iota(jnp.int32, (kb_sz, LANES), 1) < A_HEAD_DIM

    for s, pair in enumerate(Q_PAIRS):
        slab = q_ref[0, :, s * LANES:(s + 1) * LANES]
        zero = jnp.zeros_like(slab)
        qx_ref[:, pair[0] * LANES:(pair[0] + 1) * LANES] = jnp.where(low_half_q, slab, zero)
        qx_ref[:, pair[1] * LANES:(pair[1] + 1) * LANES] = jnp.where(low_half_q, zero, slab)
    q0 = offset + pl.program_id(1) * tq
    kmax = jnp.minimum(q0 + tq, l_true)
    nkb = (kmax + kb_sz - 1) // kb_sz

    key_off = lax.broadcasted_iota(jnp.int32, (kb_sz, tq), 0)

    qpos = q0 + lax.broadcasted_iota(jnp.int32, (1, tq), 1)
    limit = jnp.minimum((qpos // CHUNK + 1) * CHUNK, l_true)

    iw_t = misc_ref[0].T[MISC_IW:MISC_IW + IDX_HEADS, :] * (IDX_HEADS ** -0.5)

    def score_body(kb, carry):
        base = pl.multiple_of(kb * kb_sz, kb_sz)
        ik2 = ikx_ref[0, pl.ds(base, kb_sz), :]
        ik_lo = jnp.where(low_half_k, ik2, jnp.zeros_like(ik2))
        ik_hi = jnp.where(low_half_k, jnp.zeros_like(ik2), ik2)
        s = jnp.zeros((kb_sz, tq), F32)
        for j in range(IDX_HEADS // 2):
            slab = iq_ref[0, :, j * LANES:(j + 1) * LANES]
            d0 = _dot_nt(ik_lo, slab)
            d1 = _dot_nt(ik_hi, slab)
            s = s + iw_t[2 * j:2 * j + 1, :] * jnp.maximum(d0, 0.0)
            s = s + iw_t[2 * j + 1:2 * j + 2, :] * jnp.maximum(d1, 0.0)
        s = jnp.where(base + key_off < limit, s, -jnp.inf)
        s_ref[pl.ds(base, kb_sz), :] = s
        shi_ref[pl.ds(base, kb_sz), :] = _top16(s)
        return carry

    def for_each_block(body):
        def oct_(i, carry):
            for r in range(8):
                body(8 * i + r, carry)
            return carry
        lax.fori_loop(0, nkb // 8, oct_, 0)

        @pl.when(nkb % 8 >= 4)
        def _():
            for r in range(4):
                body((nkb // 8) * 8 + r, 0)
        done = (nkb // 4) * 4

        @pl.when(nkb % 4 >= 2)
        def _():
            body(done, 0)
            body(done + 1, 0)

        @pl.when(nkb % 2 == 1)
        def _():
            body(nkb - 1, 0)

    for_each_block(score_body)

    def sum_blocks(contrib, acc_ref, n=nkb):
        acc_ref[...] = jnp.zeros(acc_ref.shape, acc_ref.dtype)

        def quad(i, carry):
            acc_ref[...] += _tree_sum([contrib(4 * i + r) for r in range(4)])
            return carry
        lax.fori_loop(0, n // 4, quad, 0)
        done = (n // 4) * 4

        @pl.when(n % 4 >= 2)
        def _():
            acc_ref[...] += contrib(done) + contrib(done + 1)

        @pl.when(n % 2 == 1)
        def _():
            acc_ref[...] += contrib(n - 1)
        return acc_ref[...]

    def count(pred, n=nkb):
        def contrib(kb):
            base = pl.multiple_of(kb * kb_sz, kb_sz)
            hit = pred(s_ref[pl.ds(base, kb_sz), :], base + key_off)
            return _tree_sum([hit[r:r + SUBLANES] for r in range(0, kb_sz, SUBLANES)])
        return jnp.sum(sum_blocks(contrib, cnt_ref, n), axis=0, keepdims=True)

    def count_hi(thr_hi):
        one = jnp.ones((), BF16)
        zero = jnp.zeros((), BF16)

        def contrib(kb):
            base = pl.multiple_of(kb * kb_sz, kb_sz)
            hit = jnp.where(shi_ref[pl.ds(base, kb_sz), :] >= thr_hi, one, zero)
            return _tree_sum([hit[r:r + BF16_ROWS] for r in range(0, kb_sz, BF16_ROWS)])
        return jnp.sum(sum_blocks(contrib, cnth_ref).astype(F32), axis=0, keepdims=True)

    def hi_body(i, carry):
        t, c_t = carry
        cand = t + lax.shift_left(jnp.int32(1), 31 - i)
        c = count_hi(_top16(_key_to_float(cand)))
        return jnp.where(c >= n_top, cand, t), jnp.where(c >= n_top, c, c_t)

    def lo_body(i, carry):
        t, c_t = carry
        cand = t + lax.shift_left(jnp.int32(1), 15 - i)
        thr_c = _key_to_float(cand)
        c = count(lambda blk, _: jnp.where(blk >= thr_c, 1.0, 0.0))
        return jnp.where(c >= n_top, cand, t), jnp.where(c >= n_top, c, c_t)

    carry = (jnp.full((1, tq), INT_MIN, jnp.int32), jnp.full((1, tq), float(n_top), F32))
    carry = lax.fori_loop(0, 16, hi_body, carry)
    t_key, n_ge = lax.fori_loop(0, 16, lo_body, carry)
    thr = _key_to_float(t_key)

    take_all = limit <= n_top
    excess = jnp.where(take_all, 0.0, jnp.where(n_ge > n_top, 1.0, 0.0))
    any_excess = jnp.max(excess) > 0.0
    n_tie = jnp.where(any_excess, nkb, 0)
    need = n_top - count(lambda blk, _: jnp.where(blk > thr, 1.0, 0.0), n_tie)
    idx_bits = 14

    def tie_body(i, c):
        cand = c + lax.shift_left(jnp.int32(1), idx_bits - 1 - i)
        f = count(lambda blk, kidx: jnp.where(blk == thr, jnp.where(kidx < cand, 1.0, 0.0), 0.0))
        return jnp.where(f <= need, cand, c)

    cut0 = jnp.where(any_excess, jnp.zeros((1, tq), jnp.int32),
                     jnp.full((1, tq), 2 ** idx_bits, jnp.int32))
    cut = lax.fori_loop(0, jnp.where(any_excess, idx_bits, 0), tie_body, cut0)
    thr = jnp.where(take_all, -jnp.inf, thr)
    cut = jnp.where(take_all, limit, cut)

    cm_ref[...] = jnp.full(cm_ref.shape, NEG_BIG, BF16)

    def logit_body(kb, carry):
        base = pl.multiple_of(kb * kb_sz, kb_sz)
        dsel = jnp.clip((q0 - base) // kb_sz, 0, 2)
        kblk = k_ref[0, pl.ds(base, kb_sz), :]
        blk = s_ref[pl.ds(base, kb_sz), :]
        tie = jnp.where(base + key_off < cut, 0.0, NEG_BIG)
        sel = jnp.where(blk > thr, 0.0, jnp.where(blk == thr, tie, NEG_BIG)).astype(BF16)
        for h in range(A_HEADS):
            sl = h // 4
            lg = _dot_nt(kblk[:, sl * LANES:(sl + 1) * LANES], qx_ref[:, h * LANES:(h + 1) * LANES])
            lg = lg.astype(BF16) + sel + bias_ref[dsel, h]
            lg_ref[h, pl.ds(base, kb_sz), :] = lg
            rows = [lg[r:r + BF16_ROWS] for r in range(0, kb_sz, BF16_ROWS)]
            while len(rows) > 1:
                rows = [jnp.maximum(a, b) for a, b in zip(rows[0::2], rows[1::2])]
            cm_ref[h] = jnp.maximum(cm_ref[h], rows[0])
        return carry

    for_each_block(logit_body)

    acc_ref[...] = jnp.zeros(acc_ref.shape, F32)
    m_rows = [jnp.max(cm_ref[h].astype(F32), axis=0, keepdims=True).astype(BF16) for h in range(A_HEADS)]

    def pv_body(kb, carry):
        base = pl.multiple_of(kb * kb_sz, kb_sz)
        for h in range(A_HEADS):
            p = jnp.exp2(lg_ref[h, pl.ds(base, kb_sz), :] - m_rows[h])
            acc_ref[h] += jnp.dot(vt_ref[0, kb, h // 4], p, preferred_element_type=F32)
        return carry

    for_each_block(pv_body)

    outs = []
    for h in range(A_HEADS):
        pos = (h // 2) % 2
        outs.append(acc_ref[h, pos * A_HEAD_DIM:(pos + 1) * A_HEAD_DIM, :] / acc_ref[h, LANES:LANES + 1, :])
    o_ref[0] = jnp.concatenate(outs, axis=0).T.astype(BF16)


def _attn_call(rel_bias, qx, iq, misc, kb, vt, ikx, *, tq, l_true, offset, n_top):
    b, tq_total, _ = qx.shape
    lp = kb.shape[1]
    nq = tq_total // tq
    assert offset % ATT_KB == 0 and tq % CHUNK == 0 and ATT_KB % tq == 0 and lp % ATT_KB == 0
    assert lp < 2 ** 14 and l_true <= lp
    kern = functools.partial(_attn_kernel, tq=tq, l_true=l_true, offset=offset, n_top=n_top)
    bias = _bias_call(rel_bias, tq)
    return pl.pallas_call(
        kern,
        out_shape=jax.ShapeDtypeStruct((b, tq_total, A_WIDTH), BF16),
        grid=(b, nq),
        in_specs=[pl.BlockSpec((1, tq, A_WIDTH), lambda i, j: (i, j, 0)),
                  pl.BlockSpec((1, tq, IDX_HEADS * IDX_DIM), lambda i, j: (i, j, 0)),
                  pl.BlockSpec((1, tq, LANES), lambda i, j: (i, j, 0)),
                  pl.BlockSpec((1, lp, KV_WIDTH), lambda i, j: (i, 0, 0)),
                  pl.BlockSpec((1, lp // ATT_KB, 2, V_ROWS, ATT_KB), lambda i, j: (i, 0, 0, 0, 0)),
                  pl.BlockSpec((1, lp, LANES), lambda i, j: (i, 0, 0)),
                  _resident(bias.shape)],
        out_specs=pl.BlockSpec((1, tq, A_WIDTH), lambda i, j: (i, j, 0)),
        scratch_shapes=[pltpu.VMEM((lp, tq), F32),
                        pltpu.VMEM((lp, tq), BF16),
                        pltpu.VMEM((A_HEADS, lp, tq), BF16),
                        pltpu.VMEM((A_HEADS, BF16_ROWS, tq), BF16),
                        pltpu.VMEM((A_HEADS, V_ROWS, tq), F32),
                        pltpu.VMEM((SUBLANES, tq), F32),
                        pltpu.VMEM((BF16_ROWS, tq), BF16),
                        pltpu.VMEM((tq, QX_W), BF16)],
        compiler_params=_cparams(2),
        name="sparse_attn",
    )(qx, iq, misc, kb, vt, ikx, bias)


def _split_bf16(a, n):
    parts = []
    r = a
    for i in range(n):
        p = r.astype(BF16)
        parts.append(p)
        if i + 1 < n:
            r = r - p.astype(F32)
    return parts


def _bdot(a, b, dims=(((1,), (0,)), ((), ()))):
    return lax.dot_general(a.astype(BF16), b.astype(BF16), dims, preferred_element_type=F32)


_NT = (((1,), (1,)), ((), ()))
_TN = (((0,), (0,)), ((), ()))


def _gdn_kernel(x_ref, misc_ref, gz_ref, convw_ref, hist_ref, s0_ref, gnw_ref, alane_ref, dlane_ref,
                o_ref, conv_o, s_o, xp_ref, y_ref, st_ref, vn_ref,
                u_ref, w_ref, qk_ref, qe_ref, kd_ref, el_ref, gate_ref, *, cb, t_true, nblk):
    c = CHUNK
    step = pl.program_id(1)
    blk = jnp.minimum(step, nblk - 1)
    carried = (u_ref, w_ref, qk_ref, qe_ref, kd_ref, el_ref, gate_ref)
    u_nx, w_nx, qk_nx, qe_nx, kd_nx, el_nx, gate_nx = (r.at[1] for r in carried)
    u_ref, w_ref, qk_ref, qe_ref, kd_ref, el_ref, gate_ref = (r.at[0] for r in carried)

    @pl.when(step == 0)
    def _():
        xp_ref[XP_PAD - (CONV_W - 1):XP_PAD, :] = hist_ref[0]
        st_ref[...] = s0_ref[0]
        for r in carried:
            r[...] = jnp.zeros(r.shape, r.dtype)

    heads = range(G_HEADS)
    for r in carried:
        r[0] = r[1]

    def scan_stages():
        vn_ref[...] = jnp.zeros(vn_ref.shape, F32)
        for ck in range(cb // c):
            r0, r1 = ck * c, (ck + 1) * c
            s_prev = [st_ref[h] for h in heads]
            v_new = [u_ref[h, r0:r1, :] - _bdot(w_ref[h, r0:r1, :], s_prev[h]) for h in heads]
            o_state = [_bdot(qe_ref[h, r0:r1, :], s_prev[h]) for h in heads]
            yield
            for h in heads:
                vn_ref[h, r0:r1, :] = v_new[h]
            o = [o_state[h] + _bdot(qk_ref[h, r0:r1, :], vn_ref[h]) for h in heads]
            for h in heads:
                s_new = s_prev[h] * el_ref[h, ck, 0:1, :] + _bdot(kd_ref[h, r0:r1, :], v_new[h], _TN)
                st_ref[h] = jnp.where(step > 0, s_new, s_prev[h])
            yield
            for h in heads:
                on = o[h] * lax.rsqrt(jnp.mean(o[h] * o[h], axis=-1, keepdims=True) + EPS) * gnw_ref[...]
                gate = gate_ref[r0:r1, h * G_VAL_DIM:(h + 1) * G_VAL_DIM]
                o_ref[0, r0:r1, h * G_VAL_DIM:(h + 1) * G_VAL_DIM] = (on * gate).astype(BF16)
            yield
        s_o[0] = st_ref[...]

    scan = scan_stages()

    def tick():
        next(scan, None)

    tick()

    xp_ref[XP_PAD:XP_PAD + cb, :] = x_ref[0]
    y = convw_ref[CONV_W - 1:CONV_W, :] * xp_ref[XP_PAD:XP_PAD + cb, :]
    for j in range(CONV_W - 1):
        lag_rows = XP_PAD - (CONV_W - 1) + j
        y = y + convw_ref[j:j + 1, :] * xp_ref[lag_rows:lag_rows + cb, :]
    y_ref[...] = _silu(y)

    last_row = XP_PAD + (t_true - 1) % cb
    conv_o[0] = xp_ref[last_row - (CONV_W - 2):last_row + 1, :]

    xp_ref[XP_PAD - (CONV_W - 1):XP_PAD, :] = xp_ref[XP_PAD + cb - (CONV_W - 1):XP_PAD + cb, :]

    ri = lax.broadcasted_iota(jnp.int32, (cb, cb), 0)
    ci = lax.broadcasted_iota(jnp.int32, (cb, cb), 1)
    lag = jnp.where((ri // c) == (ci // c), ri - ci, -1)
    tri = lag >= 0
    strict = lag > 0
    eye = ri == ci
    tril_bf = jnp.where(tri, 1.0, 0.0).astype(BF16)
    eye_f = jnp.where(eye, 1.0, 0.0)

    ms = misc_ref[0]
    tok = blk * cb + lax.broadcasted_iota(jnp.int32, (cb, 1), 0)
    live = tok < t_true
    beta_s = jnp.where(live, jax.nn.sigmoid(ms), 0.0)
    z = ms + dlane_ref[...]
    softplus = jnp.maximum(z, 0.0) + jnp.log(1.0 + jnp.exp(-jnp.abs(z)))
    g_s = jnp.where(live, -jnp.exp(alane_ref[...]) * softplus, 0.0)
    gc_s = None
    for piece in _split_bf16(g_s, 3):
        t = jnp.dot(tril_bf, piece, preferred_element_type=F32)
        gc_s = t if gc_s is None else gc_s + t

    tick()

    q, k, v, beta, gc, decay, kb, eg = ([None] * G_HEADS for _ in range(8))
    for h in heads:
        qh = y_ref[:, h * G_KEY_DIM:(h + 1) * G_KEY_DIM]
        kh = y_ref[:, G_WIDTH + h * G_KEY_DIM:G_WIDTH + (h + 1) * G_KEY_DIM]
        v[h] = y_ref[:, 2 * G_WIDTH + h * G_VAL_DIM:2 * G_WIDTH + (h + 1) * G_VAL_DIM]
        q[h] = qh * lax.rsqrt(jnp.sum(qh * qh, axis=-1, keepdims=True) + 1e-6) * (G_KEY_DIM ** -0.5)
        k[h] = kh * lax.rsqrt(jnp.sum(kh * kh, axis=-1, keepdims=True) + 1e-6)
        beta[h] = beta_s[:, MISC_GB + h:MISC_GB + h + 1]
        gc[h] = gc_s[:, MISC_GA + h:MISC_GA + h + 1]
        gc_b = jnp.broadcast_to(gc[h], (cb, cb))
        gc_row = jnp.sum(jnp.where(eye, gc_b, 0.0), axis=0, keepdims=True)
        decay[h] = jnp.exp(jnp.where(tri, gc_b - gc_row, NEG_BIG))
        kb[h] = k[h] * beta[h]
        eg[h] = jnp.exp(gc[h])
    tick()
    m = [jnp.where(strict, _bdot(kb[h], k[h], _NT) * decay[h], 0.0) for h in heads]
    rhs = [jnp.concatenate([v[h] * beta[h], kb[h] * eg[h]], axis=1) for h in heads]
    qk = [jnp.where(tri, _bdot(q[h], k[h], _NT) * decay[h], 0.0) for h in heads]
    tick()
    pw = [-m[h] for h in heads]
    inv = [eye_f + pw[h] for h in heads]
    for _ in range(5):
        pw = [_bdot(pw[h], pw[h]) for h in heads]
        tick()
        inv = [inv[h] + _bdot(inv[h], pw[h]) for h in heads]
    tick()
    sol = [_bdot(inv[h], rhs[h]) for h in heads]
    tick()
    m_sol = []
    for h in heads:
        m_hi, m_lo = _split_bf16(m[h], 2)
        s_hi, s_lo = _split_bf16(sol[h], 2)
        m_sol.append(_bdot(m_hi, s_hi) + _bdot(m_hi, s_lo) + _bdot(m_lo, s_hi))
    tick()
    sol = [sol[h] + _bdot(inv[h], rhs[h] - sol[h] - m_sol[h]) for h in heads]
    for _ in scan:
        pass
    gate_nx[...] = _silu(gz_ref[0])
    for h in heads:
        u_nx[h] = sol[h][:, :G_VAL_DIM]
        w_nx[h] = sol[h][:, G_VAL_DIM:].astype(BF16)
        qk_nx[h] = qk[h].astype(BF16)
        qe_nx[h] = (q[h] * eg[h]).astype(BF16)
        for ck in range(cb // c):
            r0, r1 = ck * c, (ck + 1) * c
            g_last = gc[h][r1 - 1:r1, :]
            kd_nx[h, r0:r1, :] = (k[h][r0:r1] * jnp.exp(g_last - gc[h][r0:r1])).astype(BF16)
            el_nx[h, ck] = jnp.broadcast_to(jnp.exp(g_last), (SUBLANES, LANES))


def _gdn_call(gqkv, misc, gz, conv_w, hist, s0, gnorm_w, a_lane, d_lane, *, cb, t_true):
    b, t_pad, _ = gqkv.shape
    nblk = t_pad // cb
    assert (nblk - 1) * cb < t_true <= t_pad and (t_true - 1) % cb >= CONV_W - 2
    kern = functools.partial(_gdn_kernel, cb=cb, t_true=t_true, nblk=nblk)
    prep = lambda i, j: (i, jnp.minimum(j, nblk - 1), 0)
    scan = lambda i, j: (i, jnp.maximum(j - 1, 0), 0)
    nch = cb // CHUNK
    return pl.pallas_call(
        kern,
        out_shape=[jax.ShapeDtypeStruct((b, t_pad, G_WIDTH), BF16),
                   jax.ShapeDtypeStruct((b, CONV_W - 1, CONV_CH), F32),
                   jax.ShapeDtypeStruct((b, G_HEADS, G_KEY_DIM, G_VAL_DIM), F32)],
        grid=(b, nblk + 1),
        in_specs=[pl.BlockSpec((1, cb, CONV_CH), prep),
                  pl.BlockSpec((1, cb, LANES), prep),
                  pl.BlockSpec((1, cb, G_WIDTH), prep),
                  pl.BlockSpec((CONV_W, CONV_CH), lambda i, j: (0, 0)),
                  pl.BlockSpec((1, CONV_W - 1, CONV_CH), lambda i, j: (i, 0, 0)),
                  pl.BlockSpec((1, G_HEADS, G_KEY_DIM, G_VAL_DIM), lambda i, j: (i, 0, 0, 0)),
                  pl.BlockSpec((1, G_VAL_DIM), lambda i, j: (0, 0)),
                  pl.BlockSpec((1, LANES), lambda i, j: (0, 0)),
                  pl.BlockSpec((1, LANES), lambda i, j: (0, 0))],
        out_specs=[pl.BlockSpec((1, cb, G_WIDTH), scan),
                   pl.BlockSpec((1, CONV_W - 1, CONV_CH), lambda i, j: (i, 0, 0)),
                   pl.BlockSpec((1, G_HEADS, G_KEY_DIM, G_VAL_DIM), lambda i, j: (i, 0, 0, 0))],
        scratch_shapes=[pltpu.VMEM((cb + XP_PAD, CONV_CH), F32),
                        pltpu.VMEM((cb, CONV_CH), F32),
                        pltpu.VMEM((G_HEADS, G_KEY_DIM, G_VAL_DIM), F32),
                        pltpu.VMEM((G_HEADS, cb, G_VAL_DIM), F32),
                        pltpu.VMEM((2, G_HEADS, cb, G_VAL_DIM), F32),
                        pltpu.VMEM((2, G_HEADS, cb, G_KEY_DIM), BF16),
                        pltpu.VMEM((2, G_HEADS, cb, cb), BF16),
                        pltpu.VMEM((2, G_HEADS, cb, G_KEY_DIM), BF16),
                        pltpu.VMEM((2, G_HEADS, cb, G_KEY_DIM), BF16),
                        pltpu.VMEM((2, G_HEADS, nch, SUBLANES, LANES), F32),
                        pltpu.VMEM((2, cb, G_WIDTH), F32)],
        compiler_params=_cparams(2),
        name="gated_delta",
    )(gqkv, misc, gz, conv_w, hist, s0, gnorm_w, a_lane, d_lane)


def _pack_w_in(w_in):
    d = w_in.shape[0]
    splits = (512, 256, 256, 512, 64, 8, 512, 512, 512, 512, 4, 4)
    offs = np.concatenate([[0], np.cumsum(splits)])
    aq, ak, av, iq, ik, iw, gq, gk, gv, gz, gb, ga = [w_in[:, offs[i]:offs[i + 1]] for i in range(12)]
    qx = [aq[:, h * A_HEAD_DIM:(h + 1) * A_HEAD_DIM] * (A_HEAD_DIM ** -0.5 * LOG2E)
          for pair in Q_PAIRS for h in pair]
    ikx = [ik, ik]
    misc = [ik, iw, gb, ga, jnp.zeros((d, LANES - MISC_GA - G_HEADS), w_in.dtype)]
    cols = qx + [ak, av, iq * (IDX_DIM ** -0.5)] + ikx + misc + [gq, gk, gv, gz]
    packed = jnp.concatenate(cols, axis=1).astype(BF16)
    assert packed.shape[1] == IN_PACKED
    return packed


def _pack_ffn(w_gate, w_up, w_down):
    d = w_gate.shape[0]
    wg = w_gate.reshape(d, N_FF_CHUNKS, FF_CHUNK)
    wu = w_up.reshape(d, N_FF_CHUNKS, FF_CHUNK)
    wgu = jnp.transpose(jnp.concatenate([wg, wu], axis=2), (1, 0, 2)).astype(BF16)
    wd = w_down.reshape(N_FF_CHUNKS, FF_CHUNK, d).astype(BF16)
    return wgu, wd


def _ikx_layout(ik):
    return jnp.concatenate([ik, ik], axis=-1)


def _lane_vec(vals, lane0):
    return jnp.zeros((1, LANES), F32).at[0, lane0:lane0 + vals.shape[0]].set(vals.astype(F32))


def _run(x, mod, past, layers, rel_bias, norm_final):
    b, t, d = x.shape
    n = b * t
    per_token = t < TOKEN_TILE
    tm = n if per_token else TOKEN_TILE
    tiles_per_batch = None if per_token else t // tm
    tok = dict(tm=tm, tiles_per_batch=tiles_per_batch, per_token=per_token)

    x2 = x.reshape(n, d)
    states = []
    for li, lw in enumerate(layers):
        m = mod[li]

        def mvec(kidx):
            row = m[:, kidx]
            return jnp.repeat(row, t, axis=0) if per_token else row[:, None, :]

        sh1, sc1, gt1, sh2, sc2, gt2, sh3, sc3, gt3 = [mvec(i) for i in range(N_MOD)]
        x2 = _ffn_call(x2, sh1, sc1, gt1, lw["norm_ffn1"], lw["wgu1"], lw["wd1"], None, **tok)
        qx, k, kb, v, vb, iq, ikx, misc, gqkv, gz = _inproj_call(
            x2, sh2, sc2, lw["norm_mix"], lw["w_in"], **tok)

        if past is None:
            offset, l_true, tq = 0, t, ATT_TQ
            k_all, v_all, ikx_all = (a.reshape(b, t, -1) for a in (kb, vb, ikx))
            qx3, iq3, misc3 = (a.reshape(b, t, -1) for a in (qx, iq, misc))
            conv_hist = jnp.zeros((b, CONV_W - 1, CONV_CH), F32)
            s0 = jnp.zeros((b, G_HEADS, G_KEY_DIM, G_VAL_DIM), F32)
        else:
            k_hist, v_hist, ik_hist, conv_hist, s0 = (p[li] for p in past)
            offset = k_hist.shape[1]
            l_true = offset + t
            tq = LANES
            lp = -(-l_true // ATT_KB) * ATT_KB
            pad_k = lambda a: jnp.pad(a, ((0, 0), (0, lp - l_true), (0, 0)))
            k_all = pad_k(jnp.concatenate([k_hist.reshape(b, offset, -1).astype(BF16),
                                           kb.reshape(b, t, -1)], axis=1))
            v_all = pad_k(jnp.concatenate([v_hist.reshape(b, offset, -1).astype(BF16),
                                           vb.reshape(b, t, -1)], axis=1))
            ikx_all = pad_k(jnp.concatenate([_ikx_layout(ik_hist.astype(BF16)),
                                             ikx.reshape(b, t, -1)], axis=1))
            pad_q = lambda a: jnp.pad(a.reshape(b, t, -1), ((0, 0), (0, tq - t), (0, 0)))
            qx3, iq3, misc3 = pad_q(qx), pad_q(iq), pad_q(misc)
            s0 = s0.astype(F32)
            conv_hist = conv_hist.astype(F32)
        lp = k_all.shape[1]
        n_top = min(TOPK_MAX, l_true // 4)
        vt = jnp.transpose(v_all.reshape(b, lp // ATT_KB, ATT_KB, 2, LANES), (0, 1, 3, 4, 2))
        vt = jnp.concatenate([vt, jnp.ones((b, lp // ATT_KB, 2, V_ROWS - LANES, ATT_KB), BF16)], axis=3)
        attn = _attn_call(rel_bias, qx3, iq3, misc3, k_all, vt, ikx_all,
                          tq=tq, l_true=l_true, offset=offset, n_top=n_top)
        attn = attn[:, :t].reshape(n, A_WIDTH)

        cb = GDN_BLOCK if t % GDN_BLOCK == 0 else CHUNK
        t_pad = -(-t // cb) * cb
        pad_t = lambda a: jnp.pad(a.reshape(b, t, -1), ((0, 0), (0, t_pad - t), (0, 0)))
        gdn, conv_new, s_new = _gdn_call(pad_t(gqkv), pad_t(misc), pad_t(gz), lw["conv_w"], conv_hist, s0,
                                         lw["gnorm_w"], lw["a_lane"], lw["d_lane"], cb=cb, t_true=t)
        gdn = gdn[:, :t].reshape(n, G_WIDTH)

        nf = norm_final if li == len(layers) - 1 else None
        x2 = _ffn_call(x2, sh3, sc3, gt3, lw["norm_ffn2"], lw["wgu2"], lw["wd2"], nf,
                       mixer=(gt2, attn, gdn, lw["wo_a"], lw["wo_g"]), **tok)

        states.append((k.reshape(b, t, A_KV_HEADS, A_HEAD_DIM), v.reshape(b, t, A_KV_HEADS, A_HEAD_DIM),
                       misc.reshape(b, t, LANES)[..., :IDX_DIM], conv_new, s_new))
    stacked = [jnp.stack(s, axis=0) for s in zip(*states)]
    return x2.reshape(b, t, d), stacked


def kernel(x_prompt, x_sample, cache_k, cache_v, cache_idx_k, state_conv, state_delta, c_prompt, c_sample,
           w_mod, b_mod, norm_ffn1, norm_mix, norm_ffn2, ffn1_w_gate, ffn1_w_up, ffn1_w_down,
           ffn2_w_gate, ffn2_w_up, ffn2_w_down, w_in, w_out, rel_bias, conv_w, a_log, dt_bias, gnorm_w,
           norm_final):
    depth = w_mod.shape[0]
    bp = c_prompt.shape[0]
    c_all = jnp.concatenate([c_prompt, c_sample], axis=0)
    layers, mods_p, mods_s = [], [], []
    for l in range(depth):
        mod = _mod_call(c_all, w_mod[l], b_mod[l]).reshape(c_all.shape[0], N_MOD, D_MODEL)
        mods_p.append(mod[:bp])
        mods_s.append(mod[bp:])
        wgu1, wd1 = _pack_ffn(ffn1_w_gate[l], ffn1_w_up[l], ffn1_w_down[l])
        wgu2, wd2 = _pack_ffn(ffn2_w_gate[l], ffn2_w_up[l], ffn2_w_down[l])
        wo = w_out[l].astype(BF16)
        layers.append(dict(
            norm_ffn1=norm_ffn1[l].reshape(1, -1), norm_mix=norm_mix[l].reshape(1, -1),
            norm_ffn2=norm_ffn2[l].reshape(1, -1), wgu1=wgu1, wd1=wd1, wgu2=wgu2, wd2=wd2,
            w_in=_pack_w_in(w_in[l]), wo_a=wo[:A_WIDTH], wo_g=wo[A_WIDTH:],
            conv_w=conv_w[l], gnorm_w=gnorm_w[l].reshape(1, -1),
            a_lane=_lane_vec(a_log[l], MISC_GA), d_lane=_lane_vec(dt_bias[l], MISC_GA)))
    nf = norm_final.reshape(1, -1)
    y_p, (k_p, v_p, ik_p, conv_p, delta_p) = _run(x_prompt, mods_p, None, layers, rel_bias, nf)
    past = (cache_k, cache_v, cache_idx_k, state_conv, state_delta)
    y_s, (k_s, v_s, ik_s, conv_s, delta_s) = _run(x_sample, mods_s, past, layers, rel_bias, nf)
    return (y_p, y_s, k_p, v_p, ik_p, conv_p, delta_p, k_s, v_s, ik_s, conv_s, delta_s)
```

```python
import functools

import jax
import jax.numpy as jnp
import numpy as np
from jax import lax
from jax.experimental import pallas as pl
from jax.experimental.pallas import tpu as pltpu

F32 = jnp.float32
BF16 = jnp.bfloat16

D_MODEL = 1024
CHUNK = 64
A_HEAD_DIM = 64
A_HEADS = 8
A_KV_HEADS = 4
A_WIDTH = A_HEADS * A_HEAD_DIM
KV_WIDTH = A_KV_HEADS * A_HEAD_DIM
IDX_HEADS = 8
IDX_DIM = 64
TOPK_MAX = 256
REL_BUCKETS = 32
G_KEY_DIM = 128
G_VAL_DIM = 128
G_HEADS = 4
G_WIDTH = G_HEADS * G_VAL_DIM
CONV_W = 4
CONV_CH = 2 * G_HEADS * G_KEY_DIM + G_HEADS * G_VAL_DIM
D_FF = 2816
N_MOD = 9
EPS = 1e-6

LANES = 128
SUBLANES = 8
BF16_ROWS = 16
MXU_DIM = 256
VMEM_LIMIT_BYTES = 56 * 1024 * 1024

FF_CHUNK = MXU_DIM
N_FF_CHUNKS = D_FF // FF_CHUNK
TOKEN_TILE = 1024
ATT_TQ = 256
ATT_KB = 256
GDN_BLOCK = 256
XP_PAD = SUBLANES

QX_W = A_HEADS * LANES
Q_PAIRS = ((0, 2), (1, 3), (4, 6), (5, 7))
OFF_QX = 0
OFF_K = OFF_QX + A_WIDTH
OFF_V = OFF_K + A_KV_HEADS * A_HEAD_DIM
OFF_IQ = OFF_V + A_KV_HEADS * A_HEAD_DIM
OFF_IKX = OFF_IQ + IDX_HEADS * IDX_DIM
OFF_MISC = OFF_IKX + LANES
OFF_GQKV = OFF_MISC + LANES
OFF_GZ = OFF_GQKV + CONV_CH
IN_PACKED = OFF_GZ + G_WIDTH
MISC_IW = IDX_DIM
MISC_GB = MISC_IW + IDX_HEADS
MISC_GA = MISC_GB + G_HEADS

NEG_BIG = -1e30
INT_MIN = -2 ** 31
LOG2E = 1.4426950408889634
V_ROWS = LANES + 16


def _cparams(n_axes):
    return pltpu.CompilerParams(dimension_semantics=("arbitrary",) * n_axes,
                                vmem_limit_bytes=VMEM_LIMIT_BYTES)


def _resident(shape):
    nd = len(shape)
    return pl.BlockSpec(shape, lambda *_: (0,) * nd, pipeline_mode=pl.Buffered(1))


def _dot_nt(a, b):
    return lax.dot_general(a, b, (((1,), (1,)), ((), ())), preferred_element_type=F32)


def _rms_mod(x, gain, shift, scale):
    ms = jnp.mean(x * x, axis=-1, keepdims=True)
    y = x * lax.rsqrt(ms + EPS) * gain
    return y * (1.0 + scale) + shift


def _silu(x):
    return x * jax.nn.sigmoid(x)


def _tree_sum(parts):
    while len(parts) > 1:
        parts = [a + b for a, b in zip(parts[0::2], parts[1::2])] + ([parts[-1]] if len(parts) % 2 else [])
    return parts[0]


def _mod_kernel(c_ref, w_ref, b_ref, o_ref):
    s = _silu(c_ref[...]).astype(BF16)
    o_ref[...] = jnp.dot(s, w_ref[...].astype(BF16), preferred_element_type=F32) + b_ref[...]


def _mod_call(c, w_mod, b_mod):
    rows, d = c.shape
    n = w_mod.shape[1]
    tn = D_MODEL
    return pl.pallas_call(
        _mod_kernel,
        out_shape=jax.ShapeDtypeStruct((rows, n), F32),
        grid=(n // tn,),
        in_specs=[pl.BlockSpec((rows, d), lambda j: (0, 0)),
                  pl.BlockSpec((d, tn), lambda j: (0, j)),
                  pl.BlockSpec((1, tn), lambda j: (0, j))],
        out_specs=pl.BlockSpec((rows, tn), lambda j: (0, j)),
        compiler_params=_cparams(1),
        name="mod",
    )(c, w_mod, b_mod.reshape(1, n))


def _mod_specs(per_token, tm, tiles_per_batch):
    if per_token:
        return pl.BlockSpec((tm, D_MODEL), lambda i: (i, 0))
    return pl.BlockSpec((None, 1, D_MODEL), lambda i: (i // tiles_per_batch, 0, 0))


def _ffn_kernel(x_ref, sh_ref, sc_ref, gt_ref, gain_ref, wgu_ref, wd_ref, *rest, final_norm, mixer):
    rest = list(rest)
    if mixer:
        gm_ref, a_ref, g_ref, wa_ref, wg_ref = rest[:5]
        rest = rest[5:]
    if final_norm:
        nf_ref = rest.pop(0)
    o_ref, acc_ref = rest
    x = x_ref[...]
    if mixer:
        x = x + gm_ref[...] * (jnp.dot(a_ref[...], wa_ref[...], preferred_element_type=F32)
                               + jnp.dot(g_ref[...], wg_ref[...], preferred_element_type=F32))
    h = _rms_mod(x, gain_ref[...], sh_ref[...], sc_ref[...]).astype(BF16)
    for j in range(N_FF_CHUNKS):
        ab = jnp.dot(h, wgu_ref[j], preferred_element_type=F32)
        g = (_silu(ab[:, :FF_CHUNK]) * ab[:, FF_CHUNK:]).astype(BF16)
        d = jnp.dot(g, wd_ref[j], preferred_element_type=F32)
        if j == 0:
            acc_ref[...] = d
        else:
            acc_ref[...] += d
    y = x + 0.5 * gt_ref[...] * acc_ref[...]
    if final_norm:
        ms = jnp.mean(y * y, axis=-1, keepdims=True)
        y = y * lax.rsqrt(ms + EPS) * nf_ref[...]
    o_ref[...] = y


def _ffn_call(x2, sh, sc, gt, gain, wgu, wd, norm_final, mixer=None, *, tm, tiles_per_batch, per_token):
    n = x2.shape[0]
    mspec = _mod_specs(per_token, tm, tiles_per_batch)
    in_specs = [pl.BlockSpec((tm, D_MODEL), lambda i: (i, 0)), mspec, mspec, mspec,
                _resident((1, D_MODEL)), _resident(wgu.shape), _resident(wd.shape)]
    args = [x2, sh, sc, gt, gain, wgu, wd]
    if mixer is not None:
        gm, attn, gdn, wa, wg = mixer
        in_specs += [mspec, pl.BlockSpec((tm, A_WIDTH), lambda i: (i, 0)),
                     pl.BlockSpec((tm, G_WIDTH), lambda i: (i, 0)), _resident(wa.shape), _resident(wg.shape)]
        args += [gm, attn, gdn, wa, wg]
    final_norm = norm_final is not None
    if final_norm:
        in_specs.append(_resident((1, D_MODEL)))
        args.append(norm_final)
    return pl.pallas_call(
        functools.partial(_ffn_kernel, final_norm=final_norm, mixer=mixer is not None),
        out_shape=jax.ShapeDtypeStruct((n, D_MODEL), F32),
        grid=(n // tm,),
        in_specs=in_specs,
        out_specs=pl.BlockSpec((tm, D_MODEL), lambda i: (i, 0)),
        scratch_shapes=[pltpu.VMEM((tm, D_MODEL), F32)],
        compiler_params=_cparams(1),
        name="ffn_final" if final_norm else "ffn",
    )(*args)


def _inproj_kernel(x_ref, sh_ref, sc_ref, gain_ref, w_ref,
                   qx_o, k_o, kb_o, v_o, vb_o, iq_o, ikx_o, misc_o, gqkv_o, gz_o):
    h = _rms_mod(x_ref[...], gain_ref[...], sh_ref[...], sc_ref[...]).astype(BF16)

    def mm(off, width):
        return jnp.dot(h, w_ref[:, off:off + width], preferred_element_type=F32)

    qx_o[...] = mm(OFF_QX, A_WIDTH).astype(BF16)
    k = mm(OFF_K, OFF_V - OFF_K)
    k_o[...] = k
    kb_o[...] = k.astype(BF16)
    v = mm(OFF_V, OFF_IQ - OFF_V)
    v_o[...] = v
    vb_o[...] = v.astype(BF16)
    iq_o[...] = mm(OFF_IQ, OFF_IKX - OFF_IQ).astype(BF16)
    ikx_o[...] = mm(OFF_IKX, OFF_MISC - OFF_IKX).astype(BF16)
    misc_o[...] = mm(OFF_MISC, LANES)
    gqkv_o[...] = mm(OFF_GQKV, CONV_CH)
    gz_o[...] = mm(OFF_GZ, G_WIDTH)


def _inproj_call(x2, sh, sc, gain, w_packed, *, tm, tiles_per_batch, per_token):
    n = x2.shape[0]
    mspec = _mod_specs(per_token, tm, tiles_per_batch)
    widths = [(A_WIDTH, BF16), (KV_WIDTH, F32), (KV_WIDTH, BF16), (KV_WIDTH, F32), (KV_WIDTH, BF16),
              (IDX_HEADS * IDX_DIM, BF16), (LANES, BF16), (LANES, F32), (CONV_CH, F32), (G_WIDTH, F32)]
    return pl.pallas_call(
        _inproj_kernel,
        out_shape=[jax.ShapeDtypeStruct((n, w), dt) for w, dt in widths],
        grid=(n // tm,),
        in_specs=[pl.BlockSpec((tm, D_MODEL), lambda i: (i, 0)), mspec, mspec,
                  _resident((1, D_MODEL)), _resident(w_packed.shape)],
        out_specs=[pl.BlockSpec((tm, w), lambda i: (i, 0)) for w, _ in widths],
        compiler_params=_cparams(1),
        name="inproj",
    )(x2, sh, sc, gain, w_packed)


def _rel_bucket_int(rel):
    n = jnp.abs(rel)
    large = jnp.full(rel.shape, 8, jnp.int32)
    for th in (12, 16, 23, 32, 46, 64, 91):
        large = large + jnp.where(n >= th, 1, 0)
    return jnp.where(rel > 0, REL_BUCKETS // 2, 0) + jnp.where(n < 8, n, large)


def _key_to_float(t):
    bits = jnp.where(t >= 0, t, t ^ jnp.int32(0x7FFFFFFF))
    return lax.bitcast_convert_type(bits, F32)


def _top16(x):
    bits = lax.bitcast_convert_type(x, jnp.int32) & jnp.int32(-65536)
    return lax.bitcast_convert_type(bits, F32).astype(BF16)


def _bias_kernel(relb_ref, o_ref, *, tq):
    h = pl.program_id(1)
    key_off = lax.broadcasted_iota(jnp.int32, (ATT_KB, tq), 0)
    qry_off = lax.broadcasted_iota(jnp.int32, (ATT_KB, tq), 1)
    bucket = _rel_bucket_int(key_off - qry_off - pl.program_id(0) * ATT_KB)
    t = jnp.zeros((ATT_KB, tq), F32)
    for b in range(REL_BUCKETS):
        t = jnp.where(bucket == b, relb_ref[b, h], t)
    o_ref[0, 0] = (t * LOG2E).astype(BF16)


def _bias_call(rel_bias, tq):
    return pl.pallas_call(
        functools.partial(_bias_kernel, tq=tq),
        out_shape=jax.ShapeDtypeStruct((3, A_HEADS, ATT_KB, tq), BF16),
        grid=(3, A_HEADS),
        in_specs=[pl.BlockSpec(memory_space=pltpu.SMEM)],
        out_specs=pl.BlockSpec((1, 1, ATT_KB, tq), lambda d, h: (d, h, 0, 0)),
        compiler_params=_cparams(2),
        name="rel_bias_tiles",
    )(rel_bias)


def _attn_kernel(q_ref, iq_ref, misc_ref, k_ref, vt_ref, ikx_ref, bias_ref, o_ref,
                 s_ref, shi_ref, lg_ref, cm_ref, acc_ref, cnt_ref, cnth_ref, qx_ref,
                 *, tq, l_true, offset, n_top):
    kb_sz = ATT_KB
    low_half_q = lax.broadcasted_iota(jnp.int32, (tq, LANES), 1) < A_HEAD_DIM
    low_half_k = lax.broadcasted_iota(jnp.int32, (kb_sz, LANES), 1) < A_HEAD_DIM

    for s, pair in enumerate(Q_PAIRS):
        slab = q_ref[0, :, s * LANES:(s + 1) * LANES]
        zero = jnp.zeros_like(slab)
        qx_ref[:, pair[0] * LANES:(pair[0] + 1) * LANES] = jnp.where(low_half_q, slab, zero)
        qx_ref[:, pair[1] * LANES:(pair[1] + 1) * LANES] = jnp.where(low_half_q, zero, slab)
    q0 = offset + pl.program_id(1) * tq
    kmax = jnp.minimum(q0 + tq, l_true)
    nkb = (kmax + kb_sz - 1) // kb_sz

    key_off = lax.broadcasted_iota(jnp.int32, (kb_sz, tq), 0)

    qpos = q0 + lax.broadcasted_iota(jnp.int32, (1, tq), 1)
    limit = jnp.minimum((qpos // CHUNK + 1) * CHUNK, l_true)

    iw_t = misc_ref[0].T[MISC_IW:MISC_IW + IDX_HEADS, :] * (IDX_HEADS ** -0.5)

    def score_body(kb, carry):
        base = pl.multiple_of(kb * kb_sz, kb_sz)
        ik2 = ikx_ref[0, pl.ds(base, kb_sz), :]
        ik_lo = jnp.where(low_half_k, ik2, jnp.zeros_like(ik2))
        ik_hi = jnp.where(low_half_k, jnp.zeros_like(ik2), ik2)
        s = jnp.zeros((kb_sz, tq), F32)
        for j in range(IDX_HEADS // 2):
            slab = iq_ref[0, :, j * LANES:(j + 1) * LANES]
            d0 = _dot_nt(ik_lo, slab)
            d1 = _dot_nt(ik_hi, slab)
            s = s + iw_t[2 * j:2 * j + 1, :] * jnp.maximum(d0, 0.0)
            s = s + iw_t[2 * j + 1:2 * j + 2, :] * jnp.maximum(d1, 0.0)
        s = jnp.where(base + key_off < limit, s, -jnp.inf)
        s_ref[pl.ds(base, kb_sz), :] = s
        shi_ref[pl.ds(base, kb_sz), :] = _top16(s)
        return carry

    def for_each_block(body):
        def oct_(i, carry):
            for r in range(8):
                body(8 * i + r, carry)
            return carry
        lax.fori_loop(0, nkb // 8, oct_, 0)

        @pl.when(nkb % 8 >= 4)
        def _():
            for r in range(4):
                body((nkb // 8) * 8 + r, 0)
        done = (nkb // 4) * 4

        @pl.when(nkb % 4 >= 2)
        def _():
            body(done, 0)
            body(done + 1, 0)

        @pl.when(nkb % 2 == 1)
        def _():
            body(nkb - 1, 0)

    for_each_block(score_body)

    def sum_blocks(contrib, acc_ref, n=nkb):
        acc_ref[...] = jnp.zeros(acc_ref.shape, acc_ref.dtype)

        def quad(i, carry):
            acc_ref[...] += _tree_sum([contrib(4 * i + r) for r in range(4)])
            return carry
        lax.fori_loop(0, n // 4, quad, 0)
        done = (n // 4) * 4

        @pl.when(n % 4 >= 2)
        def _():
            acc_ref[...] += contrib(done) + contrib(done + 1)

        @pl.when(n % 2 == 1)
        def _():
            acc_ref[...] += contrib(n - 1)
        return acc_ref[...]

    def count(pred, n=nkb):
        def contrib(kb):
            base = pl.multiple_of(kb * kb_sz, kb_sz)
            hit = pred(s_ref[pl.ds(base, kb_sz), :], base + key_off)
            return _tree_sum([hit[r:r + SUBLANES] for r in range(0, kb_sz, SUBLANES)])
        return jnp.sum(sum_blocks(contrib, cnt_ref, n), axis=0, keepdims=True)

    def count_hi(thr_hi):
        one = jnp.ones((), BF16)
        zero = jnp.zeros((), BF16)

        def contrib(kb):
            base = pl.multiple_of(kb * kb_sz, kb_sz)
            hit = jnp.where(shi_ref[pl.ds(base, kb_sz), :] >= thr_hi, one, zero)
            return _tree_sum([hit[r:r + BF16_ROWS] for r in range(0, kb_sz, BF16_ROWS)])
        return jnp.sum(sum_blocks(contrib, cnth_ref).astype(F32), axis=0, keepdims=True)

    def hi_body(i, carry):
        t, c_t = carry
        cand = t + lax.shift_left(jnp.int32(1), 31 - i)
        c = count_hi(_top16(_key_to_float(cand)))
        return jnp.where(c >= n_top, cand, t), jnp.where(c >= n_top, c, c_t)

    def lo_body(i, carry):
        t, c_t = carry
        cand = t + lax.shift_left(jnp.int32(1), 15 - i)
        thr_c = _key_to_float(cand)
        c = count(lambda blk, _: jnp.where(blk >= thr_c, 1.0, 0.0))
        return jnp.where(c >= n_top, cand, t), jnp.where(c >= n_top, c, c_t)

    carry = (jnp.full((1, tq), INT_MIN, jnp.int32), jnp.full((1, tq), float(n_top), F32))
    carry = lax.fori_loop(0, 16, hi_body, carry)
    t_key, n_ge = lax.fori_loop(0, 16, lo_body, carry)
    thr = _key_to_float(t_key)

    take_all = limit <= n_top
    excess = jnp.where(take_all, 0.0, jnp.where(n_ge > n_top, 1.0, 0.0))
    any_excess = jnp.max(excess) > 0.0
    n_tie = jnp.where(any_excess, nkb, 0)
    need = n_top - count(lambda blk, _: jnp.where(blk > thr, 1.0, 0.0), n_tie)
    idx_bits = 14

    def tie_body(i, c):
        cand = c + lax.shift_left(jnp.int32(1), idx_bits - 1 - i)
        f = count(lambda blk, kidx: jnp.where(blk == thr, jnp.where(kidx < cand, 1.0, 0.0), 0.0))
        return jnp.where(f <= need, cand, c)

    cut0 = jnp.where(any_excess, jnp.zeros((1, tq), jnp.int32),
                     jnp.full((1, tq), 2 ** idx_bits, jnp.int32))
    cut = lax.fori_loop(0, jnp.where(any_excess, idx_bits, 0), tie_body, cut0)
    thr = jnp.where(take_all, -jnp.inf, thr)
    cut = jnp.where(take_all, limit, cut)

    cm_ref[...] = jnp.full(cm_ref.shape, NEG_BIG, BF16)

    def logit_body(kb, carry):
        base = pl.multiple_of(kb * kb_sz, kb_sz)
        dsel = jnp.clip((q0 - base) // kb_sz, 0, 2)
        kblk = k_ref[0, pl.ds(base, kb_sz), :]
        blk = s_ref[pl.ds(base, kb_sz), :]
        tie = jnp.where(base + key_off < cut, 0.0, NEG_BIG)
        sel = jnp.where(blk > thr, 0.0, jnp.where(blk == thr, tie, NEG_BIG)).astype(BF16)
        for h in range(A_HEADS):
            sl = h // 4
            lg = _dot_nt(kblk[:, sl * LANES:(sl + 1) * LANES], qx_ref[:, h * LANES:(h + 1) * LANES])
            lg = lg.astype(BF16) + sel + bias_ref[dsel, h]
            lg_ref[h, pl.ds(base, kb_sz), :] = lg
            rows = [lg[r:r + BF16_ROWS] for r in range(0, kb_sz, BF16_ROWS)]
            while len(rows) > 1:
                rows = [jnp.maximum(a, b) for a, b in zip(rows[0::2], rows[1::2])]
            cm_ref[h] = jnp.maximum(cm_ref[h], rows[0])
        return carry

    for_each_block(logit_body)

    acc_ref[...] = jnp.zeros(acc_ref.shape, F32)
    m_rows = [jnp.max(cm_ref[h].astype(F32), axis=0, keepdims=True).astype(BF16) for h in range(A_HEADS)]

    def pv_body(kb, carry):
        base = pl.multiple_of(kb * kb_sz, kb_sz)
        for h in range(A_HEADS):
            p = jnp.exp2(lg_ref[h, pl.ds(base, kb_sz), :] - m_rows[h])
            acc_ref[h] += jnp.dot(vt_ref[0, kb, h // 4], p, preferred_element_type=F32)
        return carry

    for_each_block(pv_body)

    outs = []
    for h in range(A_HEADS):
        pos = (h // 2) % 2
        outs.append(acc_ref[h, pos * A_HEAD_DIM:(pos + 1) * A_HEAD_DIM, :] / acc_ref[h, LANES:LANES + 1, :])
    o_ref[0] = jnp.concatenate(outs, axis=0).T.astype(BF16)


def _attn_call(rel_bias, qx, iq, misc, kb, vt, ikx, *, tq, l_true, offset, n_top):
    b, tq_total, _ = qx.shape
    lp = kb.shape[1]
    nq = tq_total // tq
    assert offset % ATT_KB == 0 and tq % CHUNK == 0 and ATT_KB % tq == 0 and lp % ATT_KB == 0
    assert lp < 2 ** 14 and l_true <= lp
    kern = functools.partial(_attn_kernel, tq=tq, l_true=l_true, offset=offset, n_top=n_top)
    bias = _bias_call(rel_bias, tq)
    return pl.pallas_call(
        kern,
        out_shape=jax.ShapeDtypeStruct((b, tq_total, A_WIDTH), BF16),
        grid=(b, nq),
        in_specs=[pl.BlockSpec((1, tq, A_WIDTH), lambda i, j: (i, j, 0)),
                  pl.BlockSpec((1, tq, IDX_HEADS * IDX_DIM), lambda i, j: (i, j, 0)),
                  pl.BlockSpec((1, tq, LANES), lambda i, j: (i, j, 0)),
                  pl.BlockSpec((1, lp, KV_WIDTH), lambda i, j: (i, 0, 0)),
                  pl.BlockSpec((1, lp // ATT_KB, 2, V_ROWS, ATT_KB), lambda i, j: (i, 0, 0, 0, 0)),
                  pl.BlockSpec((1, lp, LANES), lambda i, j: (i, 0, 0)),
                  _resident(bias.shape)],
        out_specs=pl.BlockSpec((1, tq, A_WIDTH), lambda i, j: (i, j, 0)),
        scratch_shapes=[pltpu.VMEM((lp, tq), F32),
                        pltpu.VMEM((lp, tq), BF16),
                        pltpu.VMEM((A_HEADS, lp, tq), BF16),
                        pltpu.VMEM((A_HEADS, BF16_ROWS, tq), BF16),
                        pltpu.VMEM((A_HEADS, V_ROWS, tq), F32),
                        pltpu.VMEM((SUBLANES, tq), F32),
                        pltpu.VMEM((BF16_ROWS, tq), BF16),
                        pltpu.VMEM((tq, QX_W), BF16)],
        compiler_params=_cparams(2),
        name="sparse_attn",
    )(qx, iq, misc, kb, vt, ikx, bias)


def _split_bf16(a, n):
    parts = []
    r = a
    for i in range(n):
        p = r.astype(BF16)
        parts.append(p)
        if i + 1 < n:
            r = r - p.astype(F32)
    return parts


def _bdot(a, b, dims=(((1,), (0,)), ((), ()))):
    return lax.dot_general(a.astype(BF16), b.astype(BF16), dims, preferred_element_type=F32)


_NT = (((1,), (1,)), ((), ()))
_TN = (((0,), (0,)), ((), ()))


def _gdn_kernel(x_ref, misc_ref, gz_ref, convw_ref, hist_ref, s0_ref, gnw_ref, alane_ref, dlane_ref,
                o_ref, conv_o, s_o, xp_ref, y_ref, st_ref, vn_ref,
                u_ref, w_ref, qk_ref, qe_ref, kd_ref, el_ref, gate_ref, *, cb, t_true, nblk):
    c = CHUNK
    step = pl.program_id(1)
    blk = jnp.minimum(step, nblk - 1)
    carried = (u_ref, w_ref, qk_ref, qe_ref, kd_ref, el_ref, gate_ref)
    u_nx, w_nx, qk_nx, qe_nx, kd_nx, el_nx, gate_nx = (r.at[1] for r in carried)
    u_ref, w_ref, qk_ref, qe_ref, kd_ref, el_ref, gate_ref = (r.at[0] for r in carried)

    @pl.when(step == 0)
    def _():
        xp_ref[XP_PAD - (CONV_W - 1):XP_PAD, :] = hist_ref[0]
        st_ref[...] = s0_ref[0]
        for r in carried:
            r[...] = jnp.zeros(r.shape, r.dtype)

    heads = range(G_HEADS)
    for r in carried:
        r[0] = r[1]

    def scan_stages():
        vn_ref[...] = jnp.zeros(vn_ref.shape, F32)
        for ck in range(cb // c):
            r0, r1 = ck * c, (ck + 1) * c
            s_prev = [st_ref[h] for h in heads]
            v_new = [u_ref[h, r0:r1, :] - _bdot(w_ref[h, r0:r1, :], s_prev[h]) for h in heads]
            o_state = [_bdot(qe_ref[h, r0:r1, :], s_prev[h]) for h in heads]
            yield
            for h in heads:
                vn_ref[h, r0:r1, :] = v_new[h]
            o = [o_state[h] + _bdot(qk_ref[h, r0:r1, :], vn_ref[h]) for h in heads]
            for h in heads:
                s_new = s_prev[h] * el_ref[h, ck, 0:1, :] + _bdot(kd_ref[h, r0:r1, :], v_new[h], _TN)
                st_ref[h] = jnp.where(step > 0, s_new, s_prev[h])
            yield
            for h in heads:
                on = o[h] * lax.rsqrt(jnp.mean(o[h] * o[h], axis=-1, keepdims=True) + EPS) * gnw_ref[...]
                gate = gate_ref[r0:r1, h * G_VAL_DIM:(h + 1) * G_VAL_DIM]
                o_ref[0, r0:r1, h * G_VAL_DIM:(h + 1) * G_VAL_DIM] = (on * gate).astype(BF16)
            yield
        s_o[0] = st_ref[...]

    scan = scan_stages()

    def tick():
        next(scan, None)

    tick()

    xp_ref[XP_PAD:XP_PAD + cb, :] = x_ref[0]
    y = convw_ref[CONV_W - 1:CONV_W, :] * xp_ref[XP_PAD:XP_PAD + cb, :]
    for j in range(CONV_W - 1):
        lag_rows = XP_PAD - (CONV_W - 1) + j
        y = y + convw_ref[j:j + 1, :] * xp_ref[lag_rows:lag_rows + cb, :]
    y_ref[...] = _silu(y)

    last_row = XP_PAD + (t_true - 1) % cb
    conv_o[0] = xp_ref[last_row - (CONV_W - 2):last_row + 1, :]

    xp_ref[XP_PAD - (CONV_W - 1):XP_PAD, :] = xp_ref[XP_PAD + cb - (CONV_W - 1):XP_PAD + cb, :]

    ri = lax.broadcasted_iota(jnp.int32, (cb, cb), 0)
    ci = lax.broadcasted_iota(jnp.int32, (cb, cb), 1)
    lag = jnp.where((ri // c) == (ci // c), ri - ci, -1)
    tri = lag >= 0
    strict = lag > 0
    eye = ri == ci
    tril_bf = jnp.where(tri, 1.0, 0.0).astype(BF16)
    eye_f = jnp.where(eye, 1.0, 0.0)

    ms = misc_ref[0]
    tok = blk * cb + lax.broadcasted_iota(jnp.int32, (cb, 1), 0)
    live = tok < t_true
    beta_s = jnp.where(live, jax.nn.sigmoid(ms), 0.0)
    z = ms + dlane_ref[...]
    softplus = jnp.maximum(z, 0.0) + jnp.log(1.0 + jnp.exp(-jnp.abs(z)))
    g_s = jnp.where(live, -jnp.exp(alane_ref[...]) * softplus, 0.0)
    gc_s = None
    for piece in _split_bf16(g_s, 3):
        t = jnp.dot(tril_bf, piece, preferred_element_type=F32)
        gc_s = t if gc_s is None else gc_s + t

    tick()

    q, k, v, beta, gc, decay, kb, eg = ([None] * G_HEADS for _ in range(8))
    for h in heads:
        qh = y_ref[:, h * G_KEY_DIM:(h + 1) * G_KEY_DIM]
        kh = y_ref[:, G_WIDTH + h * G_KEY_DIM:G_WIDTH + (h + 1) * G_KEY_DIM]
        v[h] = y_ref[:, 2 * G_WIDTH + h * G_VAL_DIM:2 * G_WIDTH + (h + 1) * G_VAL_DIM]
        q[h] = qh * lax.rsqrt(jnp.sum(qh * qh, axis=-1, keepdims=True) + 1e-6) * (G_KEY_DIM ** -0.5)
        k[h] = kh * lax.rsqrt(jnp.sum(kh * kh, axis=-1, keepdims=True) + 1e-6)
        beta[h] = beta_s[:, MISC_GB + h:MISC_GB + h + 1]
        gc[h] = gc_s[:, MISC_GA + h:MISC_GA + h + 1]
        gc_b = jnp.broadcast_to(gc[h], (cb, cb))
        gc_row = jnp.sum(jnp.where(eye, gc_b, 0.0), axis=0, keepdims=True)
        decay[h] = jnp.exp(jnp.where(tri, gc_b - gc_row, NEG_BIG))
        kb[h] = k[h] * beta[h]
        eg[h] = jnp.exp(gc[h])
    tick()
    m = [jnp.where(strict, _bdot(kb[h], k[h], _NT) * decay[h], 0.0) for h in heads]
    rhs = [jnp.concatenate([v[h] * beta[h], kb[h] * eg[h]], axis=1) for h in heads]
    qk = [jnp.where(tri, _bdot(q[h], k[h], _NT) * decay[h], 0.0) for h in heads]
    tick()
    pw = [-m[h] for h in heads]
    inv = [eye_f + pw[h] for h in heads]
    for _ in range(5):
        pw = [_bdot(pw[h], pw[h]) for h in heads]
        tick()
        inv = [inv[h] + _bdot(inv[h], pw[h]) for h in heads]
    tick()
    sol = [_bdot(inv[h], rhs[h]) for h in heads]
    tick()
    m_sol = []
    for h in heads:
        m_hi, m_lo = _split_bf16(m[h], 2)
        s_hi, s_lo = _split_bf16(sol[h], 2)
        m_sol.append(_bdot(m_hi, s_hi) + _bdot(m_hi, s_lo) + _bdot(m_lo, s_hi))
    tick()
    sol = [sol[h] + _bdot(inv[h], rhs[h] - sol[h] - m_sol[h]) for h in heads]
    for _ in scan:
        pass
    gate_nx[...] = _silu(gz_ref[0])
    for h in heads:
        u_nx[h] = sol[h][:, :G_VAL_DIM]
        w_nx[h] = sol[h][:, G_VAL_DIM:].astype(BF16)
        qk_nx[h] = qk[h].astype(BF16)
        qe_nx[h] = (q[h] * eg[h]).astype(BF16)
        for ck in range(cb // c):
            r0, r1 = ck * c, (ck + 1) * c
            g_last = gc[h][r1 - 1:r1, :]
            kd_nx[h, r0:r1, :] = (k[h][r0:r1] * jnp.exp(g_last - gc[h][r0:r1])).astype(BF16)
            el_nx[h, ck] = jnp.broadcast_to(jnp.exp(g_last), (SUBLANES, LANES))


def _gdn_call(gqkv, misc, gz, conv_w, hist, s0, gnorm_w, a_lane, d_lane, *, cb, t_true):
    b, t_pad, _ = gqkv.shape
    nblk = t_pad // cb
    assert (nblk - 1) * cb < t_true <= t_pad and (t_true - 1) % cb >= CONV_W - 2
    kern = functools.partial(_gdn_kernel, cb=cb, t_true=t_true, nblk=nblk)
    prep = lambda i, j: (i, jnp.minimum(j, nblk - 1), 0)
    scan = lambda i, j: (i, jnp.maximum(j - 1, 0), 0)
    nch = cb // CHUNK
    return pl.pallas_call(
        kern,
        out_shape=[jax.ShapeDtypeStruct((b, t_pad, G_WIDTH), BF16),
                   jax.ShapeDtypeStruct((b, CONV_W - 1, CONV_CH), F32),
                   jax.ShapeDtypeStruct((b, G_HEADS, G_KEY_DIM, G_VAL_DIM), F32)],
        grid=(b, nblk + 1),
        in_specs=[pl.BlockSpec((1, cb, CONV_CH), prep),
                  pl.BlockSpec((1, cb, LANES), prep),
                  pl.BlockSpec((1, cb, G_WIDTH), prep),
                  pl.BlockSpec((CONV_W, CONV_CH), lambda i, j: (0, 0)),
                  pl.BlockSpec((1, CONV_W - 1, CONV_CH), lambda i, j: (i, 0, 0)),
                  pl.BlockSpec((1, G_HEADS, G_KEY_DIM, G_VAL_DIM), lambda i, j: (i, 0, 0, 0)),
                  pl.BlockSpec((1, G_VAL_DIM), lambda i, j: (0, 0)),
                  pl.BlockSpec((1, LANES), lambda i, j: (0, 0)),
                  pl.BlockSpec((1, LANES), lambda i, j: (0, 0))],
        out_specs=[pl.BlockSpec((1, cb, G_WIDTH), scan),
                   pl.BlockSpec((1, CONV_W - 1, CONV_CH), lambda i, j: (i, 0, 0)),
                   pl.BlockSpec((1, G_HEADS, G_KEY_DIM, G_VAL_DIM), lambda i, j: (i, 0, 0, 0))],
        scratch_shapes=[pltpu.VMEM((cb + XP_PAD, CONV_CH), F32),
                        pltpu.VMEM((cb, CONV_CH), F32),
                        pltpu.VMEM((G_HEADS, G_KEY_DIM, G_VAL_DIM), F32),
                        pltpu.VMEM((G_HEADS, cb, G_VAL_DIM), F32),
                        pltpu.VMEM((2, G_HEADS, cb, G_VAL_DIM), F32),
                        pltpu.VMEM((2, G_HEADS, cb, G_KEY_DIM), BF16),
                        pltpu.VMEM((2, G_HEADS, cb, cb), BF16),
                        pltpu.VMEM((2, G_HEADS, cb, G_KEY_DIM), BF16),
                        pltpu.VMEM((2, G_HEADS, cb, G_KEY_DIM), BF16),
                        pltpu.VMEM((2, G_HEADS, nch, SUBLANES, LANES), F32),
                        pltpu.VMEM((2, cb, G_WIDTH), F32)],
        compiler_params=_cparams(2),
        name="gated_delta",
    )(gqkv, misc, gz, conv_w, hist, s0, gnorm_w, a_lane, d_lane)


def _pack_w_in(w_in):
    d = w_in.shape[0]
    splits = (512, 256, 256, 512, 64, 8, 512, 512, 512, 512, 4, 4)
    offs = np.concatenate([[0], np.cumsum(splits)])
    aq, ak, av, iq, ik, iw, gq, gk, gv, gz, gb, ga = [w_in[:, offs[i]:offs[i + 1]] for i in range(12)]
    qx = [aq[:, h * A_HEAD_DIM:(h + 1) * A_HEAD_DIM] * (A_HEAD_DIM ** -0.5 * LOG2E)
          for pair in Q_PAIRS for h in pair]
    ikx = [ik, ik]
    misc = [ik, iw, gb, ga, jnp.zeros((d, LANES - MISC_GA - G_HEADS), w_in.dtype)]
    cols = qx + [ak, av, iq * (IDX_DIM ** -0.5)] + ikx + misc + [gq, gk, gv, gz]
    packed = jnp.concatenate(cols, axis=1).astype(BF16)
    assert packed.shape[1] == IN_PACKED
    return packed


def _pack_ffn(w_gate, w_up, w_down):
    d = w_gate.shape[0]
    wg = w_gate.reshape(d, N_FF_CHUNKS, FF_CHUNK)
    wu = w_up.reshape(d, N_FF_CHUNKS, FF_CHUNK)
    wgu = jnp.transpose(jnp.concatenate([wg, wu], axis=2), (1, 0, 2)).astype(BF16)
    wd = w_down.reshape(N_FF_CHUNKS, FF_CHUNK, d).astype(BF16)
    return wgu, wd


def _ikx_layout(ik):
    return jnp.concatenate([ik, ik], axis=-1)


def _lane_vec(vals, lane0):
    return jnp.zeros((1, LANES), F32).at[0, lane0:lane0 + vals.shape[0]].set(vals.astype(F32))


def _run(x, mod, past, layers, rel_bias, norm_final):
    b, t, d = x.shape
    n = b * t
    per_token = t < TOKEN_TILE
    tm = n if per_token else TOKEN_TILE
    tiles_per_batch = None if per_token else t // tm
    tok = dict(tm=tm, tiles_per_batch=tiles_per_batch, per_token=per_token)

    x2 = x.reshape(n, d)
    states = []
    for li, lw in enumerate(layers):
        m = mod[li]

        def mvec(kidx):
            row = m[:, kidx]
            return jnp.repeat(row, t, axis=0) if per_token else row[:, None, :]

        sh1, sc1, gt1, sh2, sc2, gt2, sh3, sc3, gt3 = [mvec(i) for i in range(N_MOD)]
        x2 = _ffn_call(x2, sh1, sc1, gt1, lw["norm_ffn1"], lw["wgu1"], lw["wd1"], None, **tok)
        qx, k, kb, v, vb, iq, ikx, misc, gqkv, gz = _inproj_call(
            x2, sh2, sc2, lw["norm_mix"], lw["w_in"], **tok)

        if past is None:
            offset, l_true, tq = 0, t, ATT_TQ
            k_all, v_all, ikx_all = (a.reshape(b, t, -1) for a in (kb, vb, ikx))
            qx3, iq3, misc3 = (a.reshape(b, t, -1) for a in (qx, iq, misc))
            conv_hist = jnp.zeros((b, CONV_W - 1, CONV_CH), F32)
            s0 = jnp.zeros((b, G_HEADS, G_KEY_DIM, G_VAL_DIM), F32)
        else:
            k_hist, v_hist, ik_hist, conv_hist, s0 = (p[li] for p in past)
            offset = k_hist.shape[1]
            l_true = offset + t
            tq = LANES
            lp = -(-l_true // ATT_KB) * ATT_KB
            pad_k = lambda a: jnp.pad(a, ((0, 0), (0, lp - l_true), (0, 0)))
            k_all = pad_k(jnp.concatenate([k_hist.reshape(b, offset, -1).astype(BF16),
                                           kb.reshape(b, t, -1)], axis=1))
            v_all = pad_k(jnp.concatenate([v_hist.reshape(b, offset, -1).astype(BF16),
                                           vb.reshape(b, t, -1)], axis=1))
            ikx_all = pad_k(jnp.concatenate([_ikx_layout(ik_hist.astype(BF16)),
                                             ikx.reshape(b, t, -1)], axis=1))
            pad_q = lambda a: jnp.pad(a.reshape(b, t, -1), ((0, 0), (0, tq - t), (0, 0)))
            qx3, iq3, misc3 = pad_q(qx), pad_q(iq), pad_q(misc)
            s0 = s0.astype(F32)
            conv_hist = conv_hist.astype(F32)
        lp = k_all.shape[1]
        n_top = min(TOPK_MAX, l_true // 4)
        vt = jnp.transpose(v_all.reshape(b, lp // ATT_KB, ATT_KB, 2, LANES), (0, 1, 3, 4, 2))
        vt = jnp.pad(vt, ((0, 0), (0, 0), (0, 0), (0, V_ROWS - LANES), (0, 0)), constant_values=1.0)
        attn = _attn_call(rel_bias, qx3, iq3, misc3, k_all, vt, ikx_all,
                          tq=tq, l_true=l_true, offset=offset, n_top=n_top)
        attn = attn[:, :t].reshape(n, A_WIDTH)

        cb = GDN_BLOCK if t % GDN_BLOCK == 0 else CHUNK
        t_pad = -(-t // cb) * cb
        pad_t = lambda a: jnp.pad(a.reshape(b, t, -1), ((0, 0), (0, t_pad - t), (0, 0)))
        gdn, conv_new, s_new = _gdn_call(pad_t(gqkv), pad_t(misc), pad_t(gz), lw["conv_w"], conv_hist, s0,
                                         lw["gnorm_w"], lw["a_lane"], lw["d_lane"], cb=cb, t_true=t)
        gdn = gdn[:, :t].reshape(n, G_WIDTH)

        nf = norm_final if li == len(layers) - 1 else None
        x2 = _ffn_call(x2, sh3, sc3, gt3, lw["norm_ffn2"], lw["wgu2"], lw["wd2"], nf,
                       mixer=(gt2, attn, gdn, lw["wo_a"], lw["wo_g"]), **tok)

        states.append((k.reshape(b, t, A_KV_HEADS, A_HEAD_DIM), v.reshape(b, t, A_KV_HEADS, A_HEAD_DIM),
                       misc.reshape(b, t, LANES)[..., :IDX_DIM], conv_new, s_new))
    stacked = [jnp.stack(s, axis=0) for s in zip(*states)]
    return x2.reshape(b, t, d), stacked


def kernel(x_prompt, x_sample, cache_k, cache_v, cache_idx_k, state_conv, state_delta, c_prompt, c_sample,
           w_mod, b_mod, norm_ffn1, norm_mix, norm_ffn2, ffn1_w_gate, ffn1_w_up, ffn1_w_down,
           ffn2_w_gate, ffn2_w_up, ffn2_w_down, w_in, w_out, rel_bias, conv_w, a_log, dt_bias, gnorm_w,
           norm_final):
    depth = w_mod.shape[0]
    bp = c_prompt.shape[0]
    c_all = jnp.concatenate([c_prompt, c_sample], axis=0)
    layers, mods_p, mods_s = [], [], []
    for l in range(depth):
        mod = _mod_call(c_all, w_mod[l], b_mod[l]).reshape(c_all.shape[0], N_MOD, D_MODEL)
        mods_p.append(mod[:bp])
        mods_s.append(mod[bp:])
        wgu1, wd1 = _pack_ffn(ffn1_w_gate[l], ffn1_w_up[l], ffn1_w_down[l])
        wgu2, wd2 = _pack_ffn(ffn2_w_gate[l], ffn2_w_up[l], ffn2_w_down[l])
        wo = w_out[l].astype(BF16)
        layers.append(dict(
            norm_ffn1=norm_ffn1[l].reshape(1, -1), norm_mix=norm_mix[l].reshape(1, -1),
            norm_ffn2=norm_ffn2[l].reshape(1, -1), wgu1=wgu1, wd1=wd1, wgu2=wgu2, wd2=wd2,
            w_in=_pack_w_in(w_in[l]), wo_a=wo[:A_WIDTH], wo_g=wo[A_WIDTH:],
            conv_w=conv_w[l], gnorm_w=gnorm_w[l].reshape(1, -1),
            a_lane=_lane_vec(a_log[l], MISC_GA), d_lane=_lane_vec(dt_bias[l], MISC_GA)))
    nf = norm_final.reshape(1, -1)
    y_p, (k_p, v_p, ik_p, conv_p, delta_p) = _run(x_prompt, mods_p, None, layers, rel_bias, nf)
    past = (cache_k, cache_v, cache_idx_k, state_conv, state_delta)
    y_s, (k_s, v_s, ik_s, conv_s, delta_s) = _run(x_sample, mods_s, past, layers, rel_bias, nf)
    return (y_p, y_s, k_p, v_p, ik_p, conv_p, delta_p, k_s, v_s, ik_s, conv_s, delta_s)
```

```python
import functools

import jax
import jax.numpy as jnp
import numpy as np
from jax import lax
from jax.experimental import pallas as pl
from jax.experimental.pallas import tpu as pltpu

F32 = jnp.float32
BF16 = jnp.bfloat16

D_MODEL = 1024
CHUNK = 64
A_HEAD_DIM = 64
A_HEADS = 8
A_KV_HEADS = 4
A_WIDTH = A_HEADS * A_HEAD_DIM
KV_WIDTH = A_KV_HEADS * A_HEAD_DIM
IDX_HEADS = 8
IDX_DIM = 64
TOPK_MAX = 256
REL_BUCKETS = 32
G_KEY_DIM = 128
G_VAL_DIM = 128
G_HEADS = 4
G_WIDTH = G_HEADS * G_VAL_DIM
CONV_W = 4
CONV_CH = 2 * G_HEADS * G_KEY_DIM + G_HEADS * G_VAL_DIM
D_FF = 2816
N_MOD = 9
EPS = 1e-6

LANES = 128
SUBLANES = 8
BF16_ROWS = 16
MXU_DIM = 256
VMEM_LIMIT_BYTES = 56 * 1024 * 1024

FF_CHUNK = MXU_DIM
N_FF_CHUNKS = D_FF // FF_CHUNK
TOKEN_TILE = 1024
ATT_TQ = 256
ATT_KB = 256
GDN_BLOCK = 256
XP_PAD = SUBLANES

QX_W = A_HEADS * LANES
Q_PAIRS = ((0, 2), (1, 3), (4, 6), (5, 7))
OFF_QX = 0
OFF_K = OFF_QX + A_WIDTH
OFF_V = OFF_K + A_KV_HEADS * A_HEAD_DIM
OFF_IQ = OFF_V + A_KV_HEADS * A_HEAD_DIM
OFF_IKX = OFF_IQ + IDX_HEADS * IDX_DIM
OFF_MISC = OFF_IKX + LANES
OFF_GQKV = OFF_MISC + LANES
OFF_GZ = OFF_GQKV + CONV_CH
IN_PACKED = OFF_GZ + G_WIDTH
MISC_IW = IDX_DIM
MISC_GB = MISC_IW + IDX_HEADS
MISC_GA = MISC_GB + G_HEADS

NEG_BIG = -1e30
INT_MIN = -2 ** 31
LOG2E = 1.4426950408889634
V_ROWS = LANES + 16


def _cparams(n_axes):
    return pltpu.CompilerParams(dimension_semantics=("arbitrary",) * n_axes,
                                vmem_limit_bytes=VMEM_LIMIT_BYTES)


def _resident(shape):
    nd = len(shape)
    return pl.BlockSpec(shape, lambda *_: (0,) * nd, pipeline_mode=pl.Buffered(1))


def _dot_nt(a, b):
    return lax.dot_general(a, b, (((1,), (1,)), ((), ())), preferred_element_type=F32)


def _rms_mod(x, gain, shift, scale):
    ms = jnp.mean(x * x, axis=-1, keepdims=True)
    y = x * lax.rsqrt(ms + EPS) * gain
    return y * (1.0 + scale) + shift


def _silu(x):
    return x * jax.nn.sigmoid(x)


def _tree_sum(parts):
    while len(parts) > 1:
        parts = [a + b for a, b in zip(parts[0::2], parts[1::2])] + ([parts[-1]] if len(parts) % 2 else [])
    return parts[0]


def _mod_kernel(c_ref, w_ref, b_ref, o_ref):
    s = _silu(c_ref[...]).astype(BF16)
    o_ref[...] = jnp.dot(s, w_ref[...].astype(BF16), preferred_element_type=F32) + b_ref[...]


def _mod_call(c, w_mod, b_mod):
    rows, d = c.shape
    n = w_mod.shape[1]
    tn = D_MODEL
    return pl.pallas_call(
        _mod_kernel,
        out_shape=jax.ShapeDtypeStruct((rows, n), F32),
        grid=(n // tn,),
        in_specs=[pl.BlockSpec((rows, d), lambda j: (0, 0)),
                  pl.BlockSpec((d, tn), lambda j: (0, j)),
                  pl.BlockSpec((1, tn), lambda j: (0, j))],
        out_specs=pl.BlockSpec((rows, tn), lambda j: (0, j)),
        compiler_params=_cparams(1),
        name="mod",
    )(c, w_mod, b_mod.reshape(1, n))


def _mod_specs(per_token, tm, tiles_per_batch):
    if per_token:
        return pl.BlockSpec((tm, D_MODEL), lambda i: (i, 0))
    return pl.BlockSpec((None, 1, D_MODEL), lambda i: (i // tiles_per_batch, 0, 0))


def _ffn_kernel(x_ref, sh_ref, sc_ref, gt_ref, gain_ref, wgu_ref, wd_ref, *rest, final_norm, mixer):
    rest = list(rest)
    if mixer:
        gm_ref, a_ref, g_ref, wa_ref, wg_ref = rest[:5]
        rest = rest[5:]
    if final_norm:
        nf_ref = rest.pop(0)
    o_ref, acc_ref = rest
    x = x_ref[...]
    if mixer:
        x = x + gm_ref[...] * (jnp.dot(a_ref[...], wa_ref[...], preferred_element_type=F32)
                               + jnp.dot(g_ref[...], wg_ref[...], preferred_element_type=F32))
    h = _rms_mod(x, gain_ref[...], sh_ref[...], sc_ref[...]).astype(BF16)
    for j in range(N_FF_CHUNKS):
        ab = jnp.dot(h, wgu_ref[j], preferred_element_type=F32)
        g = (_silu(ab[:, :FF_CHUNK]) * ab[:, FF_CHUNK:]).astype(BF16)
        d = jnp.dot(g, wd_ref[j], preferred_element_type=F32)
        if j == 0:
            acc_ref[...] = d
        else:
            acc_ref[...] += d
    y = x + 0.5 * gt_ref[...] * acc_ref[...]
    if final_norm:
        ms = jnp.mean(y * y, axis=-1, keepdims=True)
        y = y * lax.rsqrt(ms + EPS) * nf_ref[...]
    o_ref[...] = y


def _ffn_call(x2, sh, sc, gt, gain, wgu, wd, norm_final, mixer=None, *, tm, tiles_per_batch, per_token):
    n = x2.shape[0]
    mspec = _mod_specs(per_token, tm, tiles_per_batch)
    in_specs = [pl.BlockSpec((tm, D_MODEL), lambda i: (i, 0)), mspec, mspec, mspec,
                _resident((1, D_MODEL)), _resident(wgu.shape), _resident(wd.shape)]
    args = [x2, sh, sc, gt, gain, wgu, wd]
    if mixer is not None:
        gm, attn, gdn, wa, wg = mixer
        in_specs += [mspec, pl.BlockSpec((tm, A_WIDTH), lambda i: (i, 0)),
                     pl.BlockSpec((tm, G_WIDTH), lambda i: (i, 0)), _resident(wa.shape), _resident(wg.shape)]
        args += [gm, attn, gdn, wa, wg]
    final_norm = norm_final is not None
    if final_norm:
        in_specs.append(_resident((1, D_MODEL)))
        args.append(norm_final)
    return pl.pallas_call(
        functools.partial(_ffn_kernel, final_norm=final_norm, mixer=mixer is not None),
        out_shape=jax.ShapeDtypeStruct((n, D_MODEL), F32),
        grid=(n // tm,),
        in_specs=in_specs,
        out_specs=pl.BlockSpec((tm, D_MODEL), lambda i: (i, 0)),
        scratch_shapes=[pltpu.VMEM((tm, D_MODEL), F32)],
        compiler_params=_cparams(1),
        name="ffn_final" if final_norm else "ffn",
    )(*args)


def _inproj_kernel(x_ref, sh_ref, sc_ref, gain_ref, w_ref,
                   qx_o, k_o, kb_o, v_o, vb_o, iq_o, ikx_o, misc_o, gqkv_o, gz_o):
    h = _rms_mod(x_ref[...], gain_ref[...], sh_ref[...], sc_ref[...]).astype(BF16)

    def mm(off, width):
        return jnp.dot(h, w_ref[:, off:off + width], preferred_element_type=F32)

    qx_o[...] = mm(OFF_QX, A_WIDTH).astype(BF16)
    k = mm(OFF_K, OFF_V - OFF_K)
    k_o[...] = k
    kb_o[...] = k.astype(BF16)
    v = mm(OFF_V, OFF_IQ - OFF_V)
    v_o[...] = v
    vb_o[...] = v.astype(BF16)
    iq_o[...] = mm(OFF_IQ, OFF_IKX - OFF_IQ).astype(BF16)
    ikx_o[...] = mm(OFF_IKX, OFF_MISC - OFF_IKX).astype(BF16)
    misc_o[...] = mm(OFF_MISC, LANES)
    gqkv_o[...] = mm(OFF_GQKV, CONV_CH)
    gz_o[...] = mm(OFF_GZ, G_WIDTH)


def _inproj_call(x2, sh, sc, gain, w_packed, *, tm, tiles_per_batch, per_token):
    n = x2.shape[0]
    mspec = _mod_specs(per_token, tm, tiles_per_batch)
    widths = [(A_WIDTH, BF16), (KV_WIDTH, F32), (KV_WIDTH, BF16), (KV_WIDTH, F32), (KV_WIDTH, BF16),
              (IDX_HEADS * IDX_DIM, BF16), (LANES, BF16), (LANES, F32), (CONV_CH, F32), (G_WIDTH, F32)]
    return pl.pallas_call(
        _inproj_kernel,
        out_shape=[jax.ShapeDtypeStruct((n, w), dt) for w, dt in widths],
        grid=(n // tm,),
        in_specs=[pl.BlockSpec((tm, D_MODEL), lambda i: (i, 0)), mspec, mspec,
                  _resident((1, D_MODEL)), _resident(w_packed.shape)],
        out_specs=[pl.BlockSpec((tm, w), lambda i: (i, 0)) for w, _ in widths],
        compiler_params=_cparams(1),
        name="inproj",
    )(x2, sh, sc, gain, w_packed)


def _rel_bucket_int(rel):
    n = jnp.abs(rel)
    large = jnp.full(rel.shape, 8, jnp.int32)
    for th in (12, 16, 23, 32, 46, 64, 91):
        large = large + jnp.where(n >= th, 1, 0)
    return jnp.where(rel > 0, REL_BUCKETS // 2, 0) + jnp.where(n < 8, n, large)


def _key_to_float(t):
    bits = jnp.where(t >= 0, t, t ^ jnp.int32(0x7FFFFFFF))
    return lax.bitcast_convert_type(bits, F32)


def _top16(x):
    bits = lax.bitcast_convert_type(x, jnp.int32) & jnp.int32(-65536)
    return lax.bitcast_convert_type(bits, F32).astype(BF16)


def _bias_kernel(relb_ref, o_ref, *, tq):
    h = pl.program_id(1)
    key_off = lax.broadcasted_iota(jnp.int32, (ATT_KB, tq), 0)
    qry_off = lax.broadcasted_iota(jnp.int32, (ATT_KB, tq), 1)
    bucket = _rel_bucket_int(key_off - qry_off - pl.program_id(0) * ATT_KB)
    t = jnp.zeros((ATT_KB, tq), F32)
    for b in range(REL_BUCKETS):
        t = jnp.where(bucket == b, relb_ref[b, h], t)
    o_ref[0, 0] = (t * LOG2E).astype(BF16)


def _bias_call(rel_bias, tq):
    return pl.pallas_call(
        functools.partial(_bias_kernel, tq=tq),
        out_shape=jax.ShapeDtypeStruct((3, A_HEADS, ATT_KB, tq), BF16),
        grid=(3, A_HEADS),
        in_specs=[pl.BlockSpec(memory_space=pltpu.SMEM)],
        out_specs=pl.BlockSpec((1, 1, ATT_KB, tq), lambda d, h: (d, h, 0, 0)),
        compiler_params=_cparams(2),
        name="rel_bias_tiles",
    )(rel_bias)


def _attn_kernel(q_ref, iq_ref, misc_ref, k_ref, vt_ref, ikx_ref, bias_ref, o_ref,
                 s_ref, shi_ref, lg_ref, cm_ref, acc_ref, cnt_ref, cnth_ref, qx_ref,
                 *, tq, l_true, offset, n_top):
    kb_sz = ATT_KB
    low_half_q = lax.broadcasted_iota(jnp.int32, (tq, LANES), 1) < A_HEAD_DIM
    low_half_k = lax.broadcasted_iota(jnp.int32, (kb_sz, LANES), 1) < A_HEAD_DIM

    for s, pair in enumerate(Q_PAIRS):
        slab = q_ref[0, :, s * LANES:(s + 1) * LANES]
        zero = jnp.zeros_like(slab)
        qx_ref[:, pair[0] * LANES:(pair[0] + 1) * LANES] = jnp.where(low_half_q, slab, zero)
        qx_ref[:, pair[1] * LANES:(pair[1] + 1) * LANES] = jnp.where(low_half_q, zero, slab)
    q0 = offset + pl.program_id(1) * tq
    kmax = jnp.minimum(q0 + tq, l_true)
    nkb = (kmax + kb_sz - 1) // kb_sz

    key_off = lax.broadcasted_iota(jnp.int32, (kb_sz, tq), 0)

    qpos = q0 + lax.broadcasted_iota(jnp.int32, (1, tq), 1)
    limit = jnp.minimum((qpos // CHUNK + 1) * CHUNK, l_true)

    iw_t = misc_ref[0].T[MISC_IW:MISC_IW + IDX_HEADS, :] * (IDX_HEADS ** -0.5)

    def score_body(kb, carry):
        base = pl.multiple_of(kb * kb_sz, kb_sz)
        ik2 = ikx_ref[0, pl.ds(base, kb_sz), :]
        ik_lo = jnp.where(low_half_k, ik2, jnp.zeros_like(ik2))
        ik_hi = jnp.where(low_half_k, jnp.zeros_like(ik2), ik2)
        s = jnp.zeros((kb_sz, tq), F32)
        for j in range(IDX_HEADS // 2):
            slab = iq_ref[0, :, j * LANES:(j + 1) * LANES]
            d0 = _dot_nt(ik_lo, slab)
            d1 = _dot_nt(ik_hi, slab)
            s = s + iw_t[2 * j:2 * j + 1, :] * jnp.maximum(d0, 0.0)
            s = s + iw_t[2 * j + 1:2 * j + 2, :] * jnp.maximum(d1, 0.0)
        s = jnp.where(base + key_off < limit, s, -jnp.inf)
        s_ref[pl.ds(base, kb_sz), :] = s
        shi_ref[pl.ds(base, kb_sz), :] = _top16(s)
        return carry

    def for_each_block(body):
        def oct_(i, carry):
            for r in range(8):
                body(8 * i + r, carry)
            return carry
        lax.fori_loop(0, nkb // 8, oct_, 0)

        @pl.when(nkb % 8 >= 4)
        def _():
            for r in range(4):
                body((nkb // 8) * 8 + r, 0)
        done = (nkb // 4) * 4

        @pl.when(nkb % 4 >= 2)
        def _():
            body(done, 0)
            body(done + 1, 0)

        @pl.when(nkb % 2 == 1)
        def _():
            body(nkb - 1, 0)

    for_each_block(score_body)

    def sum_blocks(contrib, acc_ref, n=nkb):
        acc_ref[...] = jnp.zeros(acc_ref.shape, acc_ref.dtype)

        def quad(i, carry):
            acc_ref[...] += _tree_sum([contrib(4 * i + r) for r in range(4)])
            return carry
        lax.fori_loop(0, n // 4, quad, 0)
        done = (n // 4) * 4

        @pl.when(n % 4 >= 2)
        def _():
            acc_ref[...] += contrib(done) + contrib(done + 1)

        @pl.when(n % 2 == 1)
        def _():
            acc_ref[...] += contrib(n - 1)
        return acc_ref[...]

    def count(pred, n=nkb):
        def contrib(kb):
            base = pl.multiple_of(kb * kb_sz, kb_sz)
            hit = pred(s_ref[pl.ds(base, kb_sz), :], base + key_off)
            return _tree_sum([hit[r:r + SUBLANES] for r in range(0, kb_sz, SUBLANES)])
        return jnp.sum(sum_blocks(contrib, cnt_ref, n), axis=0, keepdims=True)

    def count_hi(thr_hi):
        one = jnp.ones((), BF16)
        zero = jnp.zeros((), BF16)

        def contrib(kb):
            base = pl.multiple_of(kb * kb_sz, kb_sz)
            hit = jnp.where(shi_ref[pl.ds(base, kb_sz), :] >= thr_hi, one, zero)
            return _tree_sum([hit[r:r + BF16_ROWS] for r in range(0, kb_sz, BF16_ROWS)])
        return jnp.sum(sum_blocks(contrib, cnth_ref).astype(F32), axis=0, keepdims=True)

    def hi_body(i, carry):
        t, c_t = carry
        cand = t + lax.shift_left(jnp.int32(1), 31 - i)
        c = count_hi(_top16(_key_to_float(cand)))
        return jnp.where(c >= n_top, cand, t), jnp.where(c >= n_top, c, c_t)

    def lo_body(i, carry):
        t, c_t = carry
        cand = t + lax.shift_left(jnp.int32(1), 15 - i)
        thr_c = _key_to_float(cand)
        c = count(lambda blk, _: jnp.where(blk >= thr_c, 1.0, 0.0))
        return jnp.where(c >= n_top, cand, t), jnp.where(c >= n_top, c, c_t)

    carry = (jnp.full((1, tq), INT_MIN, jnp.int32), jnp.full((1, tq), float(n_top), F32))
    carry = lax.fori_loop(0, 16, hi_body, carry)
    t_key, n_ge = lax.fori_loop(0, 16, lo_body, carry)
    thr = _key_to_float(t_key)

    take_all = limit <= n_top
    excess = jnp.where(take_all, 0.0, jnp.where(n_ge > n_top, 1.0, 0.0))
    any_excess = jnp.max(excess) > 0.0
    n_tie = jnp.where(any_excess, nkb, 0)
    need = n_top - count(lambda blk, _: jnp.where(blk > thr, 1.0, 0.0), n_tie)
    idx_bits = 14

    def tie_body(i, c):
        cand = c + lax.shift_left(jnp.int32(1), idx_bits - 1 - i)
        f = count(lambda blk, kidx: jnp.where(blk == thr, jnp.where(kidx < cand, 1.0, 0.0), 0.0))
        return jnp.where(f <= need, cand, c)

    cut0 = jnp.where(any_excess, jnp.zeros((1, tq), jnp.int32),
                     jnp.full((1, tq), 2 ** idx_bits, jnp.int32))
    cut = lax.fori_loop(0, jnp.where(any_excess, idx_bits, 0), tie_body, cut0)
    thr = jnp.where(take_all, -jnp.inf, thr)
    cut = jnp.where(take_all, limit, cut)

    cm_ref[...] = jnp.full(cm_ref.shape, NEG_BIG, BF16)

    half = A_HEADS // 2
    acc_ref[...] = jnp.zeros(acc_ref.shape, F32)

    def logit_heads(kb, hs):
        base = pl.multiple_of(kb * kb_sz, kb_sz)
        dsel = jnp.clip((q0 - base) // kb_sz, 0, 2)
        kblk = k_ref[0, pl.ds(base, kb_sz), :]
        blk = s_ref[pl.ds(base, kb_sz), :]
        tie = jnp.where(base + key_off < cut, 0.0, NEG_BIG)
        sel = jnp.where(blk > thr, 0.0, jnp.where(blk == thr, tie, NEG_BIG)).astype(BF16)
        for h in hs:
            sl = h // 4
            lg = _dot_nt(kblk[:, sl * LANES:(sl + 1) * LANES], qx_ref[:, h * LANES:(h + 1) * LANES])
            lg = lg.astype(BF16) + sel + bias_ref[dsel, h]
            lg_ref[h, pl.ds(base, kb_sz), :] = lg
            rows = [lg[r:r + BF16_ROWS] for r in range(0, kb_sz, BF16_ROWS)]
            while len(rows) > 1:
                rows = [jnp.maximum(a, b) for a, b in zip(rows[0::2], rows[1::2])]
            cm_ref[h] = jnp.maximum(cm_ref[h], rows[0])

    def pv_heads(kb, hs, m_rows):
        base = pl.multiple_of(kb * kb_sz, kb_sz)
        for h in hs:
            p = jnp.exp2(lg_ref[h, pl.ds(base, kb_sz), :] - m_rows[h])
            acc_ref[h] += jnp.dot(vt_ref[0, kb, h // 4], p, preferred_element_type=F32)

    def row_max(hs):
        return {h: jnp.max(cm_ref[h].astype(F32), axis=0, keepdims=True).astype(BF16) for h in hs}

    lo, hi = range(half), range(half, A_HEADS)

    def stage_a(kb, carry):
        logit_heads(kb, lo)
        return carry

    for_each_block(stage_a)
    m_lo = row_max(lo)

    def stage_b(kb, carry):
        logit_heads(kb, hi)
        pv_heads(kb, lo, m_lo)
        return carry

    for_each_block(stage_b)
    m_hi = row_max(hi)

    def stage_c(kb, carry):
        pv_heads(kb, hi, m_hi)
        return carry

    for_each_block(stage_c)

    outs = []
    for h in range(A_HEADS):
        pos = (h // 2) % 2
        outs.append(acc_ref[h, pos * A_HEAD_DIM:(pos + 1) * A_HEAD_DIM, :] / acc_ref[h, LANES:LANES + 1, :])
    o_ref[0] = jnp.concatenate(outs, axis=0).T.astype(BF16)


def _attn_call(rel_bias, qx, iq, misc, kb, vt, ikx, *, tq, l_true, offset, n_top):
    b, tq_total, _ = qx.shape
    lp = kb.shape[1]
    nq = tq_total // tq
    assert offset % ATT_KB == 0 and tq % CHUNK == 0 and ATT_KB % tq == 0 and lp % ATT_KB == 0
    assert lp < 2 ** 14 and l_true <= lp
    kern = functools.partial(_attn_kernel, tq=tq, l_true=l_true, offset=offset, n_top=n_top)
    bias = _bias_call(rel_bias, tq)
    return pl.pallas_call(
        kern,
        out_shape=jax.ShapeDtypeStruct((b, tq_total, A_WIDTH), BF16),
        grid=(b, nq),
        in_specs=[pl.BlockSpec((1, tq, A_WIDTH), lambda i, j: (i, j, 0)),
                  pl.BlockSpec((1, tq, IDX_HEADS * IDX_DIM), lambda i, j: (i, j, 0)),
                  pl.BlockSpec((1, tq, LANES), lambda i, j: (i, j, 0)),
                  pl.BlockSpec((1, lp, KV_WIDTH), lambda i, j: (i, 0, 0)),
                  pl.BlockSpec((1, lp // ATT_KB, 2, V_ROWS, ATT_KB), lambda i, j: (i, 0, 0, 0, 0)),
                  pl.BlockSpec((1, lp, LANES), lambda i, j: (i, 0, 0)),
                  _resident(bias.shape)],
        out_specs=pl.BlockSpec((1, tq, A_WIDTH), lambda i, j: (i, j, 0)),
        scratch_shapes=[pltpu.VMEM((lp, tq), F32),
                        pltpu.VMEM((lp, tq), BF16),
                        pltpu.VMEM((A_HEADS, lp, tq), BF16),
                        pltpu.VMEM((A_HEADS, BF16_ROWS, tq), BF16),
                        pltpu.VMEM((A_HEADS, V_ROWS, tq), F32),
                        pltpu.VMEM((SUBLANES, tq), F32),
                        pltpu.VMEM((BF16_ROWS, tq), BF16),
                        pltpu.VMEM((tq, QX_W), BF16)],
        compiler_params=_cparams(2),
        name="sparse_attn",
    )(qx, iq, misc, kb, vt, ikx, bias)


def _split_bf16(a, n):
    parts = []
    r = a
    for i in range(n):
        p = r.astype(BF16)
        parts.append(p)
        if i + 1 < n:
            r = r - p.astype(F32)
    return parts


def _bdot(a, b, dims=(((1,), (0,)), ((), ()))):
    return lax.dot_general(a.astype(BF16), b.astype(BF16), dims, preferred_element_type=F32)


_NT = (((1,), (1,)), ((), ()))
_TN = (((0,), (0,)), ((), ()))


def _gdn_kernel(x_ref, misc_ref, gz_ref, convw_ref, hist_ref, s0_ref, gnw_ref, alane_ref, dlane_ref,
                o_ref, conv_o, s_o, xp_ref, y_ref, st_ref, vn_ref,
                u_ref, w_ref, qk_ref, qe_ref, kd_ref, el_ref, gate_ref, *, cb, t_true, nblk):
    c = CHUNK
    step = pl.program_id(1)
    blk = jnp.minimum(step, nblk - 1)
    carried = (u_ref, w_ref, qk_ref, qe_ref, kd_ref, el_ref, gate_ref)
    u_nx, w_nx, qk_nx, qe_nx, kd_nx, el_nx, gate_nx = (r.at[1] for r in carried)
    u_ref, w_ref, qk_ref, qe_ref, kd_ref, el_ref, gate_ref = (r.at[0] for r in carried)

    @pl.when(step == 0)
    def _():
        xp_ref[XP_PAD - (CONV_W - 1):XP_PAD, :] = hist_ref[0]
        st_ref[...] = s0_ref[0]
        for r in carried:
            r[...] = jnp.zeros(r.shape, r.dtype)

    heads = range(G_HEADS)
    for r in carried:
        r[0] = r[1]

    def scan_stages():
        vn_ref[...] = jnp.zeros(vn_ref.shape, F32)
        for ck in range(cb // c):
            r0, r1 = ck * c, (ck + 1) * c
            s_prev = [st_ref[h] for h in heads]
            v_new = [u_ref[h, r0:r1, :] - _bdot(w_ref[h, r0:r1, :], s_prev[h]) for h in heads]
            o_state = [_bdot(qe_ref[h, r0:r1, :], s_prev[h]) for h in heads]
            yield
            for h in heads:
                vn_ref[h, r0:r1, :] = v_new[h]
            o = [o_state[h] + _bdot(qk_ref[h, r0:r1, :], vn_ref[h]) for h in heads]
            for h in heads:
                s_new = s_prev[h] * el_ref[h, ck, 0:1, :] + _bdot(kd_ref[h, r0:r1, :], v_new[h], _TN)
                st_ref[h] = jnp.where(step > 0, s_new, s_prev[h])
            yield
            for h in heads:
                on = o[h] * lax.rsqrt(jnp.mean(o[h] * o[h], axis=-1, keepdims=True) + EPS) * gnw_ref[...]
                gate = gate_ref[r0:r1, h * G_VAL_DIM:(h + 1) * G_VAL_DIM]
                o_ref[0, r0:r1, h * G_VAL_DIM:(h + 1) * G_VAL_DIM] = (on * gate).astype(BF16)
            yield
        s_o[0] = st_ref[...]

    scan = scan_stages()

    def tick():
        next(scan, None)

    tick()

    xp_ref[XP_PAD:XP_PAD + cb, :] = x_ref[0]
    y = convw_ref[CONV_W - 1:CONV_W, :] * xp_ref[XP_PAD:XP_PAD + cb, :]
    for j in range(CONV_W - 1):
        lag_rows = XP_PAD - (CONV_W - 1) + j
        y = y + convw_ref[j:j + 1, :] * xp_ref[lag_rows:lag_rows + cb, :]
    y_ref[...] = _silu(y)

    last_row = XP_PAD + (t_true - 1) % cb
    conv_o[0] = xp_ref[last_row - (CONV_W - 2):last_row + 1, :]

    xp_ref[XP_PAD - (CONV_W - 1):XP_PAD, :] = xp_ref[XP_PAD + cb - (CONV_W - 1):XP_PAD + cb, :]

    ri = lax.broadcasted_iota(jnp.int32, (cb, cb), 0)
    ci = lax.broadcasted_iota(jnp.int32, (cb, cb), 1)
    lag = jnp.where((ri // c) == (ci // c), ri - ci, -1)
    tri = lag >= 0
    strict = lag > 0
    eye = ri == ci
    tril_bf = jnp.where(tri, 1.0, 0.0).astype(BF16)
    eye_f = jnp.where(eye, 1.0, 0.0)

    ms = misc_ref[0]
    tok = blk * cb + lax.broadcasted_iota(jnp.int32, (cb, 1), 0)
    live = tok < t_true
    beta_s = jnp.where(live, jax.nn.sigmoid(ms), 0.0)
    z = ms + dlane_ref[...]
    softplus = jnp.maximum(z, 0.0) + jnp.log(1.0 + jnp.exp(-jnp.abs(z)))
    g_s = jnp.where(live, -jnp.exp(alane_ref[...]) * softplus, 0.0)
    gc_s = None
    for piece in _split_bf16(g_s, 3):
        t = jnp.dot(tril_bf, piece, preferred_element_type=F32)
        gc_s = t if gc_s is None else gc_s + t

    tick()

    q, k, v, beta, gc, decay, kb, eg = ([None] * G_HEADS for _ in range(8))
    for h in heads:
        qh = y_ref[:, h * G_KEY_DIM:(h + 1) * G_KEY_DIM]
        kh = y_ref[:, G_WIDTH + h * G_KEY_DIM:G_WIDTH + (h + 1) * G_KEY_DIM]
        v[h] = y_ref[:, 2 * G_WIDTH + h * G_VAL_DIM:2 * G_WIDTH + (h + 1) * G_VAL_DIM]
        q[h] = qh * lax.rsqrt(jnp.sum(qh * qh, axis=-1, keepdims=True) + 1e-6) * (G_KEY_DIM ** -0.5)
        k[h] = kh * lax.rsqrt(jnp.sum(kh * kh, axis=-1, keepdims=True) + 1e-6)
        beta[h] = beta_s[:, MISC_GB + h:MISC_GB + h + 1]
        gc[h] = gc_s[:, MISC_GA + h:MISC_GA + h + 1]
        gc_b = jnp.broadcast_to(gc[h], (cb, cb))
        gc_row = jnp.sum(jnp.where(eye, gc_b, 0.0), axis=0, keepdims=True)
        decay[h] = jnp.exp(jnp.where(tri, gc_b - gc_row, NEG_BIG))
        kb[h] = k[h] * beta[h]
        eg[h] = jnp.exp(gc[h])
    tick()
    m = [jnp.where(strict, _bdot(kb[h], k[h], _NT) * decay[h], 0.0) for h in heads]
    rhs = [jnp.concatenate([v[h] * beta[h], kb[h] * eg[h]], axis=1) for h in heads]
    qk = [jnp.where(tri, _bdot(q[h], k[h], _NT) * decay[h], 0.0) for h in heads]
    tick()
    pw = [-m[h] for h in heads]
    inv = [eye_f + pw[h] for h in heads]
    for _ in range(5):
        pw = [_bdot(pw[h], pw[h]) for h in heads]
        tick()
        inv = [inv[h] + _bdot(inv[h], pw[h]) for h in heads]
    tick()
    sol = [_bdot(inv[h], rhs[h]) for h in heads]
    tick()
    m_sol = []
    for h in heads:
        m_hi, m_lo = _split_bf16(m[h], 2)
        s_hi, s_lo = _split_bf16(sol[h], 2)
        m_sol.append(_bdot(m_hi, s_hi) + _bdot(m_hi, s_lo) + _bdot(m_lo, s_hi))
    tick()
    sol = [sol[h] + _bdot(inv[h], rhs[h] - sol[h] - m_sol[h]) for h in heads]
    for _ in scan:
        pass
    gate_nx[...] = _silu(gz_ref[0])
    for h in heads:
        u_nx[h] = sol[h][:, :G_VAL_DIM]
        w_nx[h] = sol[h][:, G_VAL_DIM:].astype(BF16)
        qk_nx[h] = qk[h].astype(BF16)
        qe_nx[h] = (q[h] * eg[h]).astype(BF16)
        for ck in range(cb // c):
            r0, r1 = ck * c, (ck + 1) * c
            g_last = gc[h][r1 - 1:r1, :]
            kd_nx[h, r0:r1, :] = (k[h][r0:r1] * jnp.exp(g_last - gc[h][r0:r1])).astype(BF16)
            el_nx[h, ck] = jnp.broadcast_to(jnp.exp(g_last), (SUBLANES, LANES))


def _gdn_call(gqkv, misc, gz, conv_w, hist, s0, gnorm_w, a_lane, d_lane, *, cb, t_true):
    b, t_pad, _ = gqkv.shape
    nblk = t_pad // cb
    assert (nblk - 1) * cb < t_true <= t_pad and (t_true - 1) % cb >= CONV_W - 2
    kern = functools.partial(_gdn_kernel, cb=cb, t_true=t_true, nblk=nblk)
    prep = lambda i, j: (i, jnp.minimum(j, nblk - 1), 0)
    scan = lambda i, j: (i, jnp.maximum(j - 1, 0), 0)
    nch = cb // CHUNK
    return pl.pallas_call(
        kern,
        out_shape=[jax.ShapeDtypeStruct((b, t_pad, G_WIDTH), BF16),
                   jax.ShapeDtypeStruct((b, CONV_W - 1, CONV_CH), F32),
                   jax.ShapeDtypeStruct((b, G_HEADS, G_KEY_DIM, G_VAL_DIM), F32)],
        grid=(b, nblk + 1),
        in_specs=[pl.BlockSpec((1, cb, CONV_CH), prep),
                  pl.BlockSpec((1, cb, LANES), prep),
                  pl.BlockSpec((1, cb, G_WIDTH), prep),
                  pl.BlockSpec((CONV_W, CONV_CH), lambda i, j: (0, 0)),
                  pl.BlockSpec((1, CONV_W - 1, CONV_CH), lambda i, j: (i, 0, 0)),
                  pl.BlockSpec((1, G_HEADS, G_KEY_DIM, G_VAL_DIM), lambda i, j: (i, 0, 0, 0)),
                  pl.BlockSpec((1, G_VAL_DIM), lambda i, j: (0, 0)),
                  pl.BlockSpec((1, LANES), lambda i, j: (0, 0)),
                  pl.BlockSpec((1, LANES), lambda i, j: (0, 0))],
        out_specs=[pl.BlockSpec((1, cb, G_WIDTH), scan),
                   pl.BlockSpec((1, CONV_W - 1, CONV_CH), lambda i, j: (i, 0, 0)),
                   pl.BlockSpec((1, G_HEADS, G_KEY_DIM, G_VAL_DIM), lambda i, j: (i, 0, 0, 0))],
        scratch_shapes=[pltpu.VMEM((cb + XP_PAD, CONV_CH), F32),
                        pltpu.VMEM((cb, CONV_CH), F32),
                        pltpu.VMEM((G_HEADS, G_KEY_DIM, G_VAL_DIM), F32),
                        pltpu.VMEM((G_HEADS, cb, G_VAL_DIM), F32),
                        pltpu.VMEM((2, G_HEADS, cb, G_VAL_DIM), F32),
                        pltpu.VMEM((2, G_HEADS, cb, G_KEY_DIM), BF16),
                        pltpu.VMEM((2, G_HEADS, cb, cb), BF16),
                        pltpu.VMEM((2, G_HEADS, cb, G_KEY_DIM), BF16),
                        pltpu.VMEM((2, G_HEADS, cb, G_KEY_DIM), BF16),
                        pltpu.VMEM((2, G_HEADS, nch, SUBLANES, LANES), F32),
                        pltpu.VMEM((2, cb, G_WIDTH), F32)],
        compiler_params=_cparams(2),
        name="gated_delta",
    )(gqkv, misc, gz, conv_w, hist, s0, gnorm_w, a_lane, d_lane)


def _pack_w_in(w_in):
    d = w_in.shape[0]
    splits = (512, 256, 256, 512, 64, 8, 512, 512, 512, 512, 4, 4)
    offs = np.concatenate([[0], np.cumsum(splits)])
    aq, ak, av, iq, ik, iw, gq, gk, gv, gz, gb, ga = [w_in[:, offs[i]:offs[i + 1]] for i in range(12)]
    qx = [aq[:, h * A_HEAD_DIM:(h + 1) * A_HEAD_DIM] * (A_HEAD_DIM ** -0.5 * LOG2E)
          for pair in Q_PAIRS for h in pair]
    ikx = [ik, ik]
    misc = [ik, iw, gb, ga, jnp.zeros((d, LANES - MISC_GA - G_HEADS), w_in.dtype)]
    cols = qx + [ak, av, iq * (IDX_DIM ** -0.5)] + ikx + misc + [gq, gk, gv, gz]
    packed = jnp.concatenate(cols, axis=1).astype(BF16)
    assert packed.shape[1] == IN_PACKED
    return packed


def _pack_ffn(w_gate, w_up, w_down):
    d = w_gate.shape[0]
    wg = w_gate.reshape(d, N_FF_CHUNKS, FF_CHUNK)
    wu = w_up.reshape(d, N_FF_CHUNKS, FF_CHUNK)
    wgu = jnp.transpose(jnp.concatenate([wg, wu], axis=2), (1, 0, 2)).astype(BF16)
    wd = w_down.reshape(N_FF_CHUNKS, FF_CHUNK, d).astype(BF16)
    return wgu, wd


def _ikx_layout(ik):
    return jnp.concatenate([ik, ik], axis=-1)


def _lane_vec(vals, lane0):
    return jnp.zeros((1, LANES), F32).at[0, lane0:lane0 + vals.shape[0]].set(vals.astype(F32))


def _run(x, mod, past, layers, rel_bias, norm_final):
    b, t, d = x.shape
    n = b * t
    per_token = t < TOKEN_TILE
    tm = n if per_token else TOKEN_TILE
    tiles_per_batch = None if per_token else t // tm
    tok = dict(tm=tm, tiles_per_batch=tiles_per_batch, per_token=per_token)

    x2 = x.reshape(n, d)
    states = []
    for li, lw in enumerate(layers):
        m = mod[li]

        def mvec(kidx):
            row = m[:, kidx]
            return jnp.repeat(row, t, axis=0) if per_token else row[:, None, :]

        sh1, sc1, gt1, sh2, sc2, gt2, sh3, sc3, gt3 = [mvec(i) for i in range(N_MOD)]
        x2 = _ffn_call(x2, sh1, sc1, gt1, lw["norm_ffn1"], lw["wgu1"], lw["wd1"], None, **tok)
        qx, k, kb, v, vb, iq, ikx, misc, gqkv, gz = _inproj_call(
            x2, sh2, sc2, lw["norm_mix"], lw["w_in"], **tok)

        if past is None:
            offset, l_true, tq = 0, t, ATT_TQ
            k_all, v_all, ikx_all = (a.reshape(b, t, -1) for a in (kb, vb, ikx))
            qx3, iq3, misc3 = (a.reshape(b, t, -1) for a in (qx, iq, misc))
            conv_hist = jnp.zeros((b, CONV_W - 1, CONV_CH), F32)
            s0 = jnp.zeros((b, G_HEADS, G_KEY_DIM, G_VAL_DIM), F32)
        else:
            k_hist, v_hist, ik_hist, conv_hist, s0 = (p[li] for p in past)
            offset = k_hist.shape[1]
            l_true = offset + t
            tq = LANES
            lp = -(-l_true // ATT_KB) * ATT_KB
            pad_k = lambda a: jnp.pad(a, ((0, 0), (0, lp - l_true), (0, 0)))
            k_all = pad_k(jnp.concatenate([k_hist.reshape(b, offset, -1).astype(BF16),
                                           kb.reshape(b, t, -1)], axis=1))
            v_all = pad_k(jnp.concatenate([v_hist.reshape(b, offset, -1).astype(BF16),
                                           vb.reshape(b, t, -1)], axis=1))
            ikx_all = pad_k(jnp.concatenate([_ikx_layout(ik_hist.astype(BF16)),
                                             ikx.reshape(b, t, -1)], axis=1))
            pad_q = lambda a: jnp.pad(a.reshape(b, t, -1), ((0, 0), (0, tq - t), (0, 0)))
            qx3, iq3, misc3 = pad_q(qx), pad_q(iq), pad_q(misc)
            s0 = s0.astype(F32)
            conv_hist = conv_hist.astype(F32)
        lp = k_all.shape[1]
        n_top = min(TOPK_MAX, l_true // 4)
        vt = jnp.transpose(v_all.reshape(b, lp // ATT_KB, ATT_KB, 2, LANES), (0, 1, 3, 4, 2))
        vt = jnp.concatenate([vt, jnp.ones((b, lp // ATT_KB, 2, V_ROWS - LANES, ATT_KB), BF16)], axis=3)
        attn = _attn_call(rel_bias, qx3, iq3, misc3, k_all, vt, ikx_all,
                          tq=tq, l_true=l_true, offset=offset, n_top=n_top)
        attn = attn[:, :t].reshape(n, A_WIDTH)

        cb = GDN_BLOCK if t % GDN_BLOCK == 0 else CHUNK
        t_pad = -(-t // cb) * cb
        pad_t = lambda a: jnp.pad(a.reshape(b, t, -1), ((0, 0), (0, t_pad - t), (0, 0)))
        gdn, conv_new, s_new = _gdn_call(pad_t(gqkv), pad_t(misc), pad_t(gz), lw["conv_w"], conv_hist, s0,
                                         lw["gnorm_w"], lw["a_lane"], lw["d_lane"], cb=cb, t_true=t)
        gdn = gdn[:, :t].reshape(n, G_WIDTH)

        nf = norm_final if li == len(layers) - 1 else None
        x2 = _ffn_call(x2, sh3, sc3, gt3, lw["norm_ffn2"], lw["wgu2"], lw["wd2"], nf,
                       mixer=(gt2, attn, gdn, lw["wo_a"], lw["wo_g"]), **tok)

        states.append((k.reshape(b, t, A_KV_HEADS, A_HEAD_DIM), v.reshape(b, t, A_KV_HEADS, A_HEAD_DIM),
                       misc.reshape(b, t, LANES)[..., :IDX_DIM], conv_new, s_new))
    stacked = [jnp.stack(s, axis=0) for s in zip(*states)]
    return x2.reshape(b, t, d), stacked


def kernel(x_prompt, x_sample, cache_k, cache_v, cache_idx_k, state_conv, state_delta, c_prompt, c_sample,
           w_mod, b_mod, norm_ffn1, norm_mix, norm_ffn2, ffn1_w_gate, ffn1_w_up, ffn1_w_down,
           ffn2_w_gate, ffn2_w_up, ffn2_w_down, w_in, w_out, rel_bias, conv_w, a_log, dt_bias, gnorm_w,
           norm_final):
    depth = w_mod.shape[0]
    bp = c_prompt.shape[0]
    c_all = jnp.concatenate([c_prompt, c_sample], axis=0)
    layers, mods_p, mods_s = [], [], []
    for l in range(depth):
        mod = _mod_call(c_all, w_mod[l], b_mod[l]).reshape(c_all.shape[0], N_MOD, D_MODEL)
        mods_p.append(mod[:bp])
        mods_s.append(mod[bp:])
        wgu1, wd1 = _pack_ffn(ffn1_w_gate[l], ffn1_w_up[l], ffn1_w_down[l])
        wgu2, wd2 = _pack_ffn(ffn2_w_gate[l], ffn2_w_up[l], ffn2_w_down[l])
        wo = w_out[l].astype(BF16)
        layers.append(dict(
            norm_ffn1=norm_ffn1[l].reshape(1, -1), norm_mix=norm_mix[l].reshape(1, -1),
            norm_ffn2=norm_ffn2[l].reshape(1, -1), wgu1=wgu1, wd1=wd1, wgu2=wgu2, wd2=wd2,
            w_in=_pack_w_in(w_in[l]), wo_a=wo[:A_WIDTH], wo_g=wo[A_WIDTH:],
            conv_w=conv_w[l], gnorm_w=gnorm_w[l].reshape(1, -1),
            a_lane=_lane_vec(a_log[l], MISC_GA), d_lane=_lane_vec(dt_bias[l], MISC_GA)))
    nf = norm_final.reshape(1, -1)
    y_p, (k_p, v_p, ik_p, conv_p, delta_p) = _run(x_prompt, mods_p, None, layers, rel_bias, nf)
    past = (cache_k, cache_v, cache_idx_k, state_conv, state_delta)
    y_s, (k_s, v_s, ik_s, conv_s, delta_s) = _run(x_sample, mods_s, past, layers, rel_bias, nf)
    return (y_p, y_s, k_p, v_p, ik_p, conv_p, delta_p, k_s, v_s, ik_s, conv_s, delta_s)
```
